```python
import jax, jax.numpy as jnp
from jax import lax
import numpy as np

D_MODEL = 1024
BATCH = 16
SEQ = 256
DEPTH = 2
DEC_BATCH = 4
DEC_SEQ = 1024
PAST_LEN = 256

GRID_W = 64
H_A = 8
DK_A = 128
DV_A = 128
H_B = 4
DK_B = 256
DV_B = 512
D_FF = 2816
CONV_W = 3
CHUNK = 16
ROPE_BASE = 10000.0
EPS = 1e-6
RET_DECAY_OFFSET_BWD = 0.5
SPLIT_SIZES = (H_A * DK_A, H_A * DK_A, H_A * DK_A, H_A * DV_A, H_A * DV_A,
               H_B * DK_B, H_B * DK_B, H_B * DV_B, H_B * DV_B, D_MODEL, D_MODEL)
IN_WIDTH = sum(SPLIT_SIZES)

kernel_name = "hgrn2_retention_diffusion_step"


def rmsnorm(x, g):
    xf = x.astype(jnp.float32)
    y = xf * lax.rsqrt(jnp.mean(xf * xf, axis=-1, keepdims=True) + EPS)
    return (y * g.astype(jnp.float32)).astype(x.dtype)


def to_heads(t, n_heads):
    b, s, _ = t.shape
    return t.reshape(b, s, n_heads, -1).transpose(0, 2, 1, 3)


def chunk_scan(q, k, v, log_a, s0):
    b_, h_, t_, dk = q.shape
    dv = v.shape[-1]
    n = t_ // CHUNK
    r = lambda t: t.reshape(b_, h_, n, CHUNK, t.shape[-1])
    q, k, v, log_a = r(q), r(k), r(v), r(log_a)
    cum = jnp.cumsum(log_a, axis=-2)
    cum_end = cum[..., -1:, :]
    diff = cum[..., :, None, :] - cum[..., None, :, :]
    lower = jnp.tril(jnp.ones((CHUNK, CHUNK), dtype=bool))[:, :, None]
    dec = jnp.where(lower, jnp.exp(jnp.minimum(diff, 0.0)), 0.0)
    scores = jnp.einsum('bhntd,bhnsd,bhntsd->bhnts', q, k, dec)
    o_intra = jnp.einsum('bhnts,bhnsv->bhntv', scores, v)
    q_dec = q * jnp.exp(cum)
    k_dec = k * jnp.exp(cum_end - cum)
    a_end = jnp.exp(cum_end[..., 0, :])

    def step(S, xs):
        qd, kd, vv, ae = xs
        o = jnp.einsum('bhtd,bhdv->bhtv', qd, S)
        S = ae[..., None] * S + jnp.einsum('bhtd,bhtv->bhdv', kd, vv)
        return S, o

    mv = lambda t: jnp.moveaxis(t, 2, 0)
    s_fin, o_inter = lax.scan(step, s0, (mv(q_dec), mv(k_dec), mv(v), mv(a_end)))
    o = o_intra + jnp.moveaxis(o_inter, 0, 2)
    return o.reshape(b_, h_, t_, dv), s_fin


def bidir_scan(q, k_f, k_b, v, la_f, la_b, s0_f, s0_b):
    flip = lambda t: jnp.flip(t, axis=2)
    o_f, s_f = chunk_scan(q, k_f, v, la_f, s0_f)
    o_b, s_b = chunk_scan(flip(q), flip(k_b), flip(v), flip(la_b), s0_b)
    return o_f + flip(o_b), s_f, s_b


def axial_rope(x):
    t_ = x.shape[2]
    rows = t_ // GRID_W
    r_idx = jnp.repeat(jnp.arange(rows), GRID_W).astype(jnp.float32)
    c_idx = jnp.tile(jnp.arange(GRID_W), rows).astype(jnp.float32)
    half = x.shape[-1] // 2
    quarter = half // 2
    inv = 1.0 / (ROPE_BASE ** (jnp.arange(quarter, dtype=jnp.float32) / quarter))

    def rot(xh, pos):
        ang = pos[:, None] * inv[None, :]
        cos, sin = jnp.cos(ang), jnp.sin(ang)
        x1, x2 = xh[..., :quarter], xh[..., quarter:]
        return jnp.concatenate([x1 * cos - x2 * sin, x1 * sin + x2 * cos], axis=-1)

    xf = x.astype(jnp.float32)
    return jnp.concatenate([rot(xf[..., :half], r_idx), rot(xf[..., half:], c_idx)], axis=-1)


def retention_log_decay(offset):
    return jnp.log1p(-jnp.exp2(-(5.0 + offset) - jnp.arange(H_B, dtype=jnp.float32)))


def gated_head_norm(o, g, dtype):
    o = o * lax.rsqrt(jnp.mean(o * o, axis=-1, keepdims=True) + EPS)
    b_, h_, t_, dv = o.shape
    o = o.transpose(0, 2, 1, 3).reshape(b_, t_, h_ * dv)
    return (o * jax.nn.silu(g.astype(jnp.float32))).astype(dtype)


def token_mixer(h, w_in, lb, p_a, p_b, w_out, s_hgrn, s_ret, latent):
    f32 = jnp.float32
    b_, t_, _ = h.shape
    proj = jnp.einsum('btd,de->bte', h, w_in)
    points = np.cumsum(SPLIT_SIZES)[:-1].tolist()
    q_a, z_f, z_b, i_a, g_a, q_b, k_b, v_b, g_b, gate_a, gate_b = jnp.split(proj, points, axis=-1)

    def forget(z, lb_dir):
        z = to_heads(z, H_A).astype(f32)
        lbd = lb_dir.reshape(H_A, 1, DK_A).astype(f32)
        log_f = jnp.logaddexp(jnp.log(lbd), jnp.log1p(-lbd) + jax.nn.log_sigmoid(z))
        key = (1.0 - lbd) * jax.nn.sigmoid(-z)
        return log_f, key

    la_f, kh_f = forget(z_f, lb[0])
    la_b, kh_b = forget(z_b, lb[1])
    qa = to_heads(q_a, H_A).astype(f32)
    va = to_heads(i_a, H_A).astype(f32)
    o_a, sa_f, sa_b = bidir_scan(qa, kh_f, kh_b, va, la_f, la_b,
                                 s_hgrn[:, 0].astype(f32), s_hgrn[:, 1].astype(f32))
    y_a = jnp.einsum('bte,ed->btd', gated_head_norm(o_a, g_a, h.dtype), p_a)

    qb = to_heads(q_b, H_B).astype(f32)
    kb = to_heads(k_b, H_B).astype(f32) * (DK_B ** -0.5)
    if latent:
        qb, kb = axial_rope(qb), axial_rope(kb)
    vb = to_heads(v_b, H_B).astype(f32)
    shape_b = (b_, H_B, t_, DK_B)
    lg_f = jnp.broadcast_to(retention_log_decay(0.0)[None, :, None, None], shape_b)
    lg_b = jnp.broadcast_to(retention_log_decay(RET_DECAY_OFFSET_BWD)[None, :, None, None], shape_b)
    o_b, sr_f, sr_b = bidir_scan(qb, kb, kb, vb, lg_f, lg_b,
                                 s_ret[:, 0].astype(f32), s_ret[:, 1].astype(f32))
    y_b = jnp.einsum('bte,ed->btd', gated_head_norm(o_b, g_b, h.dtype), p_b)

    merged = jax.nn.sigmoid(gate_a) * y_a + jax.nn.sigmoid(gate_b) * y_b
    y = jnp.einsum('btd,de->bte', merged, w_out)
    return y, jnp.stack([sa_f, sa_b], axis=1), jnp.stack([sr_f, sr_b], axis=1)


def conv_ffn(h, w_up, w_conv, b_conv, w_down):
    u = jnp.einsum('btd,df->btf', h, w_up)
    up = jnp.pad(u, ((0, 0), (1, 1), (0, 0)))
    u = up[:, :-2] * w_conv[0] + up[:, 1:-1] * w_conv[1] + up[:, 2:] * w_conv[2] + b_conv
    a, g = jnp.split(u, 2, axis=-1)
    return jnp.einsum('btf,fd->btd', jax.nn.silu(g) * a, w_down)


def modulation(cvec, w_mod, b_mod):
    m = jnp.einsum('bd,de->be', jax.nn.silu(cvec), w_mod) + b_mod
    return jnp.split(m[:, None, :], 6, axis=-1)


def trunk_layer(x, mods, n1, n2, w_in, lb, p_a, p_b, w_out, w_up, w_conv, b_conv, w_down, s_hgrn, s_ret, latent):
    sh1, sc1, gt1, sh2, sc2, gt2 = mods
    h = rmsnorm(x, n1) * (1.0 + sc1) + sh1
    y, st_hgrn, st_ret = token_mixer(h, w_in, lb, p_a, p_b, w_out, s_hgrn, s_ret, latent)
    x = x + gt1 * y
    h = rmsnorm(x, n2) * (1.0 + sc2) + sh2
    x = x + gt2 * conv_ffn(h, w_up, w_conv, b_conv, w_down)
    return x, st_hgrn, st_ret


def setup_inputs(seed: int = 0) -> dict:
    key = jax.random.key(seed)
    ks = jax.random.split(key, 20)
    nrm = lambda k, shape, s: jax.random.normal(k, shape, jnp.float32) * s
    return {
        "x_prompt": nrm(ks[0], (BATCH, SEQ, D_MODEL), 1.0),
        "x_sample": nrm(ks[1], (DEC_BATCH, DEC_SEQ, D_MODEL), 1.0),
        "state_hgrn": nrm(ks[2], (DEC_BATCH, DEPTH, 2, H_A, DK_A, DV_A), 0.5),
        "state_ret": nrm(ks[3], (DEC_BATCH, DEPTH, 2, H_B, DK_B, DV_B), 0.5),
        "c": nrm(ks[4], (DEC_BATCH, D_MODEL), 1.0),
        "c_ctx": nrm(ks[5], (D_MODEL,), 1.0),
        "norm1": 1.0 + nrm(ks[6], (DEPTH, D_MODEL), 0.02),
        "norm2": 1.0 + nrm(ks[7], (DEPTH, D_MODEL), 0.02),
        "final_norm": 1.0 + nrm(ks[8], (D_MODEL,), 0.02),
        "w_mod": nrm(ks[9], (DEPTH, D_MODEL, 6 * D_MODEL), 0.5 * D_MODEL ** -0.5),
        "b_mod": nrm(ks[10], (DEPTH, 6 * D_MODEL), 0.01),
        "w_in": nrm(ks[11], (DEPTH, D_MODEL, IN_WIDTH), D_MODEL ** -0.5),
        "hgrn_lb_raw": nrm(ks[12], (DEPTH, 2, H_A * DK_A), 1.0),
        "p_a": nrm(ks[13], (DEPTH, H_A * DV_A, D_MODEL), (H_A * DV_A) ** -0.5),
        "p_b": nrm(ks[14], (DEPTH, H_B * DV_B, D_MODEL), (H_B * DV_B) ** -0.5),
        "w_out": nrm(ks[15], (DEPTH, D_MODEL, D_MODEL), D_MODEL ** -0.5),
        "w_up": nrm(ks[16], (DEPTH, D_MODEL, 2 * D_FF), D_MODEL ** -0.5),
        "w_conv": nrm(ks[17], (DEPTH, CONV_W, 2 * D_FF), CONV_W ** -0.5),
        "b_conv": nrm(ks[18], (DEPTH, 2 * D_FF), 0.01),
        "w_down": nrm(ks[19], (DEPTH, D_FF, D_MODEL), D_FF ** -0.5),
    }


def reference(x_prompt, x_sample, state_hgrn, state_ret, c, c_ctx, norm1, norm2, final_norm,
              w_mod, b_mod, w_in, hgrn_lb_raw, p_a, p_b, w_out, w_up, w_conv, b_conv, w_down):
    sm = jax.nn.softmax(hgrn_lb_raw.astype(jnp.float32), axis=0)
    cum = jnp.cumsum(sm, axis=0)
    lower_bounds = cum - cum[0:1]

    x = x_prompt
    b_ctx = x_prompt.shape[0]
    zero_hgrn = jnp.zeros((b_ctx, 2, H_A, DK_A, DV_A), jnp.float32)
    zero_ret = jnp.zeros((b_ctx, 2, H_B, DK_B, DV_B), jnp.float32)
    hgrn_states, ret_states = [], []
    for l in range(DEPTH):
        mods = modulation(c_ctx[None, :], w_mod[l], b_mod[l])
        x, st_h, st_r = trunk_layer(x, mods, norm1[l], norm2[l], w_in[l], lower_bounds[l], p_a[l], p_b[l],
                                    w_out[l], w_up[l], w_conv[l], b_conv[l], w_down[l],
                                    zero_hgrn, zero_ret, False)
        hgrn_states.append(st_h)
        ret_states.append(st_r)
    y_prompt = rmsnorm(x, final_norm)
    new_state_hgrn = jnp.stack(hgrn_states, axis=1).astype(x_prompt.dtype)
    new_state_ret = jnp.stack(ret_states, axis=1).astype(x_prompt.dtype)

    x = x_sample
    for l in range(DEPTH):
        mods = modulation(c, w_mod[l], b_mod[l])
        x, _, _ = trunk_layer(x, mods, norm1[l], norm2[l], w_in[l], lower_bounds[l], p_a[l], p_b[l],
                              w_out[l], w_up[l], w_conv[l], b_conv[l], w_down[l],
                              state_hgrn[:, l], state_ret[:, l], True)
    y_sample = rmsnorm(x, final_norm)
    return (y_prompt, y_sample, new_state_hgrn, new_state_ret)
```

```python
import functools

import numpy as np
import jax
import jax.numpy as jnp
from jax import lax
from jax.experimental import pallas as pl
from jax.experimental.pallas import tpu as pltpu

F32 = jnp.float32
BF16 = jnp.bfloat16

D_MODEL = 1024
DEPTH = 2
GRID_W = 64
H_A, DK_A, DV_A = 8, 128, 128
H_B, DK_B, DV_B = 4, 256, 512
D_FF = 2816
ROPE_BASE = 10000.0
EPS = 1e-6
RET_DECAY_OFFSET_BWD = 0.5
IN_WIDTH = 13312

OFF_QA, OFF_ZF, OFF_ZB, OFF_IA, OFF_GA = 0, 1024, 2048, 3072, 4096
OFF_QB, OFF_KB, OFF_VB, OFF_GB = 5120, 6144, 7168, 9216
OFF_GATE_A, OFF_GATE_B = 11264, 12288

MOD_ROWS = 8
TOKEN_TILE = 1024
POST_TILE = 256
FF_CHUNK = 256
HGRN_CHUNK = 128
RET_ROWS = 256
VMEM_LIMIT = 52 * 1024 * 1024

_NT = (((1,), (1,)), ((), ()))
_TN = (((0,), (0,)), ((), ()))


def _params(sem):
    return pltpu.CompilerParams(dimension_semantics=sem, vmem_limit_bytes=VMEM_LIMIT)


def _sigmoid(x):
    return jax.nn.sigmoid(x)


def _silu(x):
    return x * jax.nn.sigmoid(x)


def _mod_kernel(c_ref, w_ref, b_ref, o_ref):
    s = _silu(c_ref[...])
    o_ref[...] = jnp.dot(s, w_ref[...], precision=lax.Precision.HIGHEST,
                         preferred_element_type=F32) + b_ref[...]


def _mod_call(cvec, w_mod, b_mod):
    n_col = 6 * D_MODEL // 1024
    return pl.pallas_call(
        _mod_kernel,
        grid=(DEPTH, n_col),
        in_specs=[
            pl.BlockSpec((MOD_ROWS, D_MODEL), lambda l, j: (0, 0)),
            pl.BlockSpec((None, D_MODEL, 1024), lambda l, j: (l, 0, j)),
            pl.BlockSpec((None, 1, 1024), lambda l, j: (l, 0, j)),
        ],
        out_specs=pl.BlockSpec((None, MOD_ROWS, 1024), lambda l, j: (l, 0, j)),
        out_shape=jax.ShapeDtypeStruct((DEPTH, MOD_ROWS, 6 * D_MODEL), F32),
        compiler_params=_params(("arbitrary", "arbitrary")),
        name="modulation",
    )(cvec, w_mod, b_mod.reshape(DEPTH, 1, 6 * D_MODEL))


def _mod_spec(row_fn, which, tile=TOKEN_TILE):
    return pl.BlockSpec((None, 1, D_MODEL), lambda i, j: (row_fn(i * tile), 0, which))


def _norm_mod(x, g, sc, sh):
    y = x * lax.rsqrt(jnp.mean(x * x, axis=-1, keepdims=True) + EPS) * g
    return y * (1.0 + sc) + sh


def _proj_kernel(x_ref, sh_ref, sc_ref, n_ref, w_ref, o_ref, h_ref):
    @pl.when(pl.program_id(1) == 0)
    def _():
        h_ref[...] = _norm_mod(x_ref[...], n_ref[...], sc_ref[...], sh_ref[...]).astype(BF16)

    o_ref[...] = jnp.dot(h_ref[...], w_ref[...], preferred_element_type=F32)


def _proj_call(x, mods, n1, w_in, row_fn):
    n_tok = x.shape[0]
    tn = 1024
    return pl.pallas_call(
        _proj_kernel,
        grid=(n_tok // TOKEN_TILE, IN_WIDTH // tn),
        in_specs=[
            pl.BlockSpec((TOKEN_TILE, D_MODEL), lambda i, j: (i, 0)),
            _mod_spec(row_fn, 0),
            _mod_spec(row_fn, 1),
            pl.BlockSpec((1, D_MODEL), lambda i, j: (0, 0)),
            pl.BlockSpec((D_MODEL, tn), lambda i, j: (0, j)),
        ],
        out_specs=pl.BlockSpec((TOKEN_TILE, tn), lambda i, j: (i, j)),
        out_shape=jax.ShapeDtypeStruct((n_tok, IN_WIDTH), F32),
        scratch_shapes=[pltpu.VMEM((TOKEN_TILE, D_MODEL), BF16)],
        compiler_params=_params(("arbitrary", "arbitrary")),
        name="in_projection",
    )(x, mods, mods, n1.reshape(1, D_MODEL), w_in)


def _hgrn_consts(c):
    nl = int(np.log2(c))
    t = np.arange(c)
    tt, rr = t[:, None], t[None, :]
    nf = np.zeros((nl + 1, c, c), np.float32)
    nb = np.zeros((nl + 1, c, c), np.float32)
    nf[0] = rr <= tt
    nb[0] = rr >= tt
    for lev in range(1, nl + 1):
        blk, half = 1 << lev, 1 << (lev - 1)
        base = (t // blk) * blk
        m_f = (base + half - 1)[:, None]
        nf[lev] = np.where(tt > m_f, (rr > m_f) & (rr <= tt), (rr > tt) & (rr <= m_f))
        m_b = (base + half)[:, None]
        nb[lev] = np.where(tt < m_b, (rr >= tt) & (rr < m_b), (rr >= m_b) & (rr < tt))
    x = tt ^ rr
    lev_of = np.where(x > 0, np.floor(np.log2(np.maximum(x, 1))) + 1, 0).astype(np.int32)
    lv_f = np.where(tt >= rr, lev_of, -1).astype(np.int32)
    lv_b = lv_f.T.copy()
    return (nf.reshape(-1, c), nb.reshape(-1, c), lv_f, lv_b, nl)


def _hgrn_forget(z, lb):
    log_lb = jnp.log(lb)
    b = jnp.log1p(-lb) + (jnp.minimum(z, 0.0) - jnp.log1p(jnp.exp(-jnp.abs(z))))
    amax = jnp.maximum(log_lb, b)
    delta = log_lb - b
    log_f = jnp.where(jnp.isnan(delta), log_lb + b, amax + jnp.log1p(jnp.exp(-jnp.abs(delta))))
    key = (1.0 - lb) * _sigmoid(-z)
    return log_f, key


def _hgrn_chunk(q, k, v, la, st, nall, lv, n_lev, end_row):
    c, dk = q.shape
    la_hi = la.astype(BF16)
    la_lo = (la - la_hi.astype(F32)).astype(BF16)
    x2 = jnp.dot(nall, jnp.concatenate([la_hi, la_lo], axis=1), preferred_element_type=F32)
    xs = x2[:, :dk] + x2[:, dk:]
    cum = xs[0:c]
    zero = jnp.zeros((c, dk), BF16)
    scores = jnp.zeros((c, c), F32)
    for a in range(0, n_lev + 1, 2):
        e_b = jnp.exp(xs[(a + 1) * c:(a + 2) * c])
        if a == 0:
            q_a, k_a = q, k
        else:
            e_a = jnp.exp(xs[a * c:(a + 1) * c])
            q_a, k_a = q * e_a, k * e_a
        lhs = jnp.concatenate([q_a, q * e_b], axis=1).astype(BF16)
        rhs = jnp.concatenate([
            jnp.concatenate([k_a.astype(BF16), zero], axis=1),
            jnp.concatenate([zero, (k * e_b).astype(BF16)], axis=1)], axis=0)
        p = lax.dot_general(lhs, rhs, _NT, preferred_element_type=F32)
        scores = jnp.where(lv == a, p[:, :c], scores)
        scores = jnp.where(lv == a + 1, p[:, c:], scores)
    v_b = v.astype(BF16)
    o = jnp.dot(scores.astype(BF16), v_b, preferred_element_type=F32)
    o = o + lax.dot_general((q * jnp.exp(cum)).astype(BF16), st.astype(BF16), _NT,
                            preferred_element_type=F32)
    cum_end = cum[end_row:end_row + 1]
    kd = (k * jnp.exp(cum_end - cum)).astype(BF16)
    st = st * jnp.exp(cum_end) + lax.dot_general(v_b, kd, _TN, preferred_element_type=F32)
    return o, st


def _hgrn_kernel(*refs, seq_len, chunk, n_lev, has_s0, emit_state):
    it = iter(refs)
    q_ref, zf_ref, zb_ref, i_ref, lb_ref, nf_ref, nb_ref, lvf_ref, lvb_ref = (
        next(it) for _ in range(9))
    s0_ref = next(it) if has_s0 else None
    o_ref = next(it)
    st_ref = next(it) if emit_state else None
    n_chunks = seq_len // chunk

    def run(z_ref, lb_d, nall_ref, lv_ref, reverse, st0):
        def body(ci, st):
            cidx = (n_chunks - 1 - ci) if reverse else ci
            rows = pl.ds(pl.multiple_of(cidx * chunk, chunk), chunk)
            q = q_ref[rows, :]
            v = i_ref[rows, :]
            la, k = _hgrn_forget(z_ref[rows, :], lb_d)
            o, st = _hgrn_chunk(q, k, v, la, st, nall_ref[...], lv_ref[...], n_lev,
                                0 if reverse else chunk - 1)
            if reverse:
                o_ref[rows, :] += o
            else:
                o_ref[rows, :] = o
            return st
        return lax.fori_loop(0, n_chunks, body, st0)

    lb = lb_ref[...]
    if has_s0:
        st_f0, st_b0 = s0_ref[0].T, s0_ref[1].T
    else:
        st_f0 = st_b0 = jnp.zeros((DV_A, DK_A), F32)
    st_f = run(zf_ref, lb[0:1], nf_ref, lvf_ref, False, st_f0)
    st_b = run(zb_ref, lb[1:2], nb_ref, lvb_ref, True, st_b0)
    o = o_ref[...]
    o_ref[...] = o * lax.rsqrt(jnp.mean(o * o, axis=-1, keepdims=True) + EPS)
    if emit_state:
        st_ref[0] = st_f.T
        st_ref[1] = st_b.T


def _hgrn_call(proj, lb, state, layer):
    n_b, seq_len, _ = proj.shape
    nf, nb, lv_f, lv_b, n_lev = _hgrn_consts(HGRN_CHUNK)
    has_s0 = state is not None
    col = lambda off: (lambda b, h: (b, 0, off // DK_A + h))
    const = lambda a: pl.BlockSpec(a.shape, lambda b, h: (0, 0))
    in_specs = [pl.BlockSpec((None, seq_len, DK_A), col(o))
                for o in (OFF_QA, OFF_ZF, OFF_ZB, OFF_IA)]
    in_specs += [pl.BlockSpec((2, DK_A), lambda b, h: (0, h)),
                 const(nf), const(nb), const(lv_f), const(lv_b)]
    args = [proj, proj, proj, proj, lb, jnp.asarray(nf, BF16), jnp.asarray(nb, BF16),
            jnp.asarray(lv_f), jnp.asarray(lv_b)]
    out_specs = [pl.BlockSpec((None, seq_len, DV_A), lambda b, h: (b, 0, h))]
    out_shape = [jax.ShapeDtypeStruct((n_b, seq_len, H_A * DV_A), F32)]
    if has_s0:
        in_specs.append(pl.BlockSpec((None, None, 2, None, DK_A, DV_A),
                                     lambda b, h: (b, layer, 0, h, 0, 0)))
        args.append(state)
    else:
        out_specs.append(pl.BlockSpec((None, 2, None, DK_A, DV_A), lambda b, h: (b, 0, h, 0, 0)))
        out_shape.append(jax.ShapeDtypeStruct((n_b, 2, H_A, DK_A, DV_A), F32))
    res = pl.pallas_call(
        functools.partial(_hgrn_kernel, seq_len=seq_len, chunk=HGRN_CHUNK, n_lev=n_lev,
                          has_s0=has_s0, emit_state=not has_s0),
        grid=(n_b, H_A),
        in_specs=in_specs, out_specs=out_specs, out_shape=out_shape,
        compiler_params=_params(("arbitrary", "arbitrary")),
        name="hgrn2_scan",
    )(*args)
    return (res[0], None) if has_s0 else (res[0], res[1])


def _rope_tables(seq_len):
    rows = seq_len // GRID_W
    r_idx = jnp.repeat(jnp.arange(rows), GRID_W).astype(F32)
    c_idx = jnp.tile(jnp.arange(GRID_W), rows).astype(F32)
    quarter = DK_B // 4
    inv = 1.0 / (ROPE_BASE ** (jnp.arange(quarter, dtype=F32) / quarter))
    ang_r = r_idx[:, None] * inv[None, :]
    ang_c = c_idx[:, None] * inv[None, :]
    cos = jnp.concatenate([jnp.cos(ang_r)] * 2 + [jnp.cos(ang_c)] * 2, axis=1)
    sin = jnp.concatenate([-jnp.sin(ang_r), jnp.sin(ang_r), -jnp.sin(ang_c), jnp.sin(ang_c)],
                          axis=1)
    return cos, sin


def _rope(x, cos, sin):
    half = DK_B // 2
    swapped = jnp.concatenate([pltpu.roll(x[:, :half], half // 2, axis=1),
                               pltpu.roll(x[:, half:], half // 2, axis=1)], axis=1)
    return x * cos + swapped * sin


def _ret_kernel(*refs, seq_len, latent, has_s0, emit_state):
    it = iter(refs)
    lg_ref, q_ref, k_ref, v_ref = (next(it) for _ in range(4))
    cos_ref, sin_ref = (next(it), next(it)) if latent else (None, None)
    s0_ref = next(it) if has_s0 else None
    o_ref = next(it)
    st_ref = next(it) if emit_state else None
    dm_ref = next(it)
    h = pl.program_id(0)
    lg_f = lg_ref[h, 0]
    lg_b = lg_ref[h, 1]

    @pl.when(pl.program_id(1) == 0)
    def _():
        t = lax.broadcasted_iota(jnp.int32, (seq_len, seq_len), 0)
        s = lax.broadcasted_iota(jnp.int32, (seq_len, seq_len), 1)
        d = (t - s).astype(F32)
        dm_ref[...] = (jnp.where(d >= 0.0, jnp.exp(lg_f * jnp.maximum(d, 0.0)), 0.0)
                       + jnp.where(d <= 0.0, jnp.exp(lg_b * jnp.maximum(-d, 0.0)), 0.0))

    q = q_ref[...]
    k = k_ref[...] * (DK_B ** -0.5)
    if latent:
        cos, sin = cos_ref[...], sin_ref[...]
        q, k = _rope(q, cos, sin), _rope(k, cos, sin)
    k_b = k.astype(BF16)
    v_b = v_ref[...].astype(BF16)
    if has_s0:
        s0 = jnp.concatenate([s0_ref[0], s0_ref[1]], axis=0).astype(BF16)
    for r in range(seq_len // RET_ROWS):
        rows = slice(r * RET_ROWS, (r + 1) * RET_ROWS)
        q_r = q[rows]
        p = lax.dot_general(q_r.astype(BF16), k_b, _NT, preferred_element_type=F32)
        o = jnp.dot((p * dm_ref[rows, :]).astype(BF16), v_b, preferred_element_type=F32)
        if has_s0:
            pos = (lax.broadcasted_iota(jnp.int32, (RET_ROWS, 1), 0) + r * RET_ROWS).astype(F32)
            q_s = jnp.concatenate([q_r * jnp.exp(lg_f * (pos + 1.0)),
                                   q_r * jnp.exp(lg_b * (seq_len - pos))], axis=1)
            o = o + jnp.dot(q_s.astype(BF16), s0, preferred_element_type=F32)
        o_ref[rows, :] = o * lax.rsqrt(jnp.mean(o * o, axis=-1, keepdims=True) + EPS)
    if emit_state:
        pos = lax.broadcasted_iota(jnp.int32, (seq_len, 1), 0).astype(F32)
        k_f = (k * jnp.exp(lg_f * (seq_len - 1.0 - pos))).astype(BF16)
        k_r = (k * jnp.exp(lg_b * pos)).astype(BF16)
        st_ref[0] = lax.dot_general(k_f, v_b, _TN, preferred_element_type=F32)
        st_ref[1] = lax.dot_general(k_r, v_b, _TN, preferred_element_type=F32)


def _ret_log_decay():
    heads = jnp.arange(H_B, dtype=F32)
    lg_f = jnp.log1p(-jnp.exp2(-5.0 - heads))
    lg_b = jnp.log1p(-jnp.exp2(-(5.0 + RET_DECAY_OFFSET_BWD) - heads))
    return jnp.stack([lg_f, lg_b], axis=1)


def _ret_call(proj, state, layer, latent):
    n_b, seq_len, _ = proj.shape
    has_s0 = state is not None
    in_specs = [
        pl.BlockSpec(memory_space=pltpu.SMEM),
        pl.BlockSpec((None, seq_len, DK_B), lambda h, b: (b, 0, OFF_QB // DK_B + h)),
        pl.BlockSpec((None, seq_len, DK_B), lambda h, b: (b, 0, OFF_KB // DK_B + h)),
        pl.BlockSpec((None, seq_len, DV_B), lambda h, b: (b, 0, OFF_VB // DV_B + h)),
    ]
    args = [_ret_log_decay(), proj, proj, proj]
    if latent:
        cos, sin = _rope_tables(seq_len)
        in_specs += [pl.BlockSpec((seq_len, DK_B), lambda h, b: (0, 0))] * 2
        args += [cos, sin]
    out_specs = [pl.BlockSpec((None, seq_len, DV_B), lambda h, b: (b, 0, h))]
    out_shape = [jax.ShapeDtypeStruct((n_b, seq_len, H_B * DV_B), F32)]
    if has_s0:
        in_specs.append(pl.BlockSpec((None, None, 2, None, DK_B, DV_B),
                                     lambda h, b: (b, layer, 0, h, 0, 0)))
        args.append(state)
    else:
        out_specs.append(pl.BlockSpec((None, 2, None, DK_B, DV_B), lambda h, b: (b, 0, h, 0, 0)))
        out_shape.append(jax.ShapeDtypeStruct((n_b, 2, H_B, DK_B, DV_B), F32))
    res = pl.pallas_call(
        functools.partial(_ret_kernel, seq_len=seq_len, latent=latent, has_s0=has_s0,
                          emit_state=not has_s0),
        grid=(H_B, n_b),
        in_specs=in_specs, out_specs=out_specs, out_shape=out_shape,
        scratch_shapes=[pltpu.VMEM((seq_len, seq_len), F32)],
        compiler_params=_params(("arbitrary", "arbitrary")),
        name="retention_scan",
    )(*args)
    return (res[0], None) if has_s0 else (res[0], res[1])


def _post_kernel(oa_ref, ga_ref, ob_ref, gb0_ref, gb1_ref, gta_ref, gtb_ref, x_ref, gt1_ref,
                 pa_ref, pb_ref, wo_ref, o_ref):
    a = (oa_ref[...] * _silu(ga_ref[...])).astype(BF16)
    g_b = jnp.concatenate([gb0_ref[...], gb1_ref[...]], axis=1)
    b = (ob_ref[...] * _silu(g_b)).astype(BF16)
    y_a = jnp.dot(a, pa_ref[...], preferred_element_type=F32)
    y_b = jnp.dot(b, pb_ref[...], preferred_element_type=F32)
    merged = _sigmoid(gta_ref[...]) * y_a + _sigmoid(gtb_ref[...]) * y_b
    y = jnp.dot(merged.astype(BF16), wo_ref[...], preferred_element_type=F32)
    o_ref[...] = x_ref[...] + gt1_ref[...] * y


def _post_call(o_a, o_b, proj, x, mods, p_a, p_b, w_out, row_fn):
    n_tok = x.shape[0]
    tile = lambda w: (lambda i, j: (i, w))
    pcol = lambda off: pl.BlockSpec((POST_TILE, 1024), tile(off // 1024))
    const = lambda a: pl.BlockSpec(a.shape, lambda i, j: (0, 0))
    return pl.pallas_call(
        _post_kernel,
        grid=(n_tok // POST_TILE, 1),
        in_specs=[
            pl.BlockSpec((POST_TILE, H_A * DV_A), tile(0)),
            pcol(OFF_GA),
            pl.BlockSpec((POST_TILE, H_B * DV_B), tile(0)),
            pcol(OFF_GB), pcol(OFF_GB + 1024), pcol(OFF_GATE_A), pcol(OFF_GATE_B),
            pl.BlockSpec((POST_TILE, D_MODEL), tile(0)),
            _mod_spec(row_fn, 2, POST_TILE),
            const(p_a), const(p_b), const(w_out),
        ],
        out_specs=pl.BlockSpec((POST_TILE, D_MODEL), tile(0)),
        out_shape=jax.ShapeDtypeStruct((n_tok, D_MODEL), F32),
        compiler_params=_params(("arbitrary", "arbitrary")),
        name="mixer_output",
    )(o_a, proj, o_b, proj, proj, proj, proj, x, mods, p_a, p_b, w_out)


def _conv3(u, wc, bc, first, last):
    n = u.shape[0]
    prev = jnp.where(first, 0.0, pltpu.roll(u, 1, axis=0))
    nxt = jnp.where(last, 0.0, pltpu.roll(u, n - 1, axis=0))
    return prev * wc[0:1] + u * wc[1:2] + nxt * wc[2:3] + bc


def _ffn_kernel(*refs, seq_len, final):
    it = iter(refs)
    (x_ref, sh_ref, sc_ref, gt_ref, n2_ref, wa_ref, wg_ref, wca_ref, wcg_ref, bca_ref, bcg_ref,
     wd_ref) = (next(it) for _ in range(12))
    fn_ref = next(it) if final else None
    o_ref, h_ref, acc_ref = next(it), next(it), next(it)
    j = pl.program_id(1)

    @pl.when(j == 0)
    def _():
        h_ref[...] = _norm_mod(x_ref[...], n2_ref[...], sc_ref[...], sh_ref[...]).astype(BF16)
        acc_ref[...] = jnp.zeros_like(acc_ref)

    h = h_ref[...]
    pos = lax.broadcasted_iota(jnp.int32, (TOKEN_TILE, 1), 0) & (seq_len - 1)
    first, last = pos == 0, pos == seq_len - 1
    u_a = jnp.dot(h, wa_ref[...], preferred_element_type=F32)
    u_g = jnp.dot(h, wg_ref[...], preferred_element_type=F32)
    a = _conv3(u_a, wca_ref[...], bca_ref[...], first, last)
    g = _conv3(u_g, wcg_ref[...], bcg_ref[...], first, last)
    acc_ref[...] += jnp.dot((_silu(g) * a).astype(BF16), wd_ref[...], preferred_element_type=F32)

    @pl.when(j == pl.num_programs(1) - 1)
    def _():
        x = x_ref[...] + gt_ref[...] * acc_ref[...]
        if final:
            x = x * lax.rsqrt(jnp.mean(x * x, axis=-1, keepdims=True) + EPS) * fn_ref[...]
        o_ref[...] = x


def _ffn_call(x, mods, n2, w_up, w_conv, b_conv, w_down, final_norm, row_fn, seq_len):
    n_tok = x.shape[0]
    n_ff = D_FF // FF_CHUNK
    b_conv = b_conv.reshape(1, 2 * D_FF)
    in_specs = [
        pl.BlockSpec((TOKEN_TILE, D_MODEL), lambda i, j: (i, 0)),
        _mod_spec(row_fn, 3), _mod_spec(row_fn, 4), _mod_spec(row_fn, 5),
        pl.BlockSpec((1, D_MODEL), lambda i, j: (0, 0)),
        pl.BlockSpec((D_MODEL, FF_CHUNK), lambda i, j: (0, j)),
        pl.BlockSpec((D_MODEL, FF_CHUNK), lambda i, j: (0, n_ff + j)),
        pl.BlockSpec((3, FF_CHUNK), lambda i, j: (0, j)),
        pl.BlockSpec((3, FF_CHUNK), lambda i, j: (0, n_ff + j)),
        pl.BlockSpec((1, FF_CHUNK), lambda i, j: (0, j)),
        pl.BlockSpec((1, FF_CHUNK), lambda i, j: (0, n_ff + j)),
        pl.BlockSpec((FF_CHUNK, D_MODEL), lambda i, j: (j, 0)),
    ]
    args = [x, mods, mods, mods, n2.reshape(1, D_MODEL), w_up, w_up, w_conv, w_conv,
            b_conv, b_conv, w_down]
    final = final_norm is not None
    if final:
        in_specs.append(pl.BlockSpec((1, D_MODEL), lambda i, j: (0, 0)))
        args.append(final_norm.reshape(1, D_MODEL))
    return pl.pallas_call(
        functools.partial(_ffn_kernel, seq_len=seq_len, final=final),
        grid=(n_tok // TOKEN_TILE, n_ff),
        in_specs=in_specs,
        out_specs=pl.BlockSpec((TOKEN_TILE, D_MODEL), lambda i, j: (i, 0)),
        out_shape=jax.ShapeDtypeStruct((n_tok, D_MODEL), F32),
        scratch_shapes=[pltpu.VMEM((TOKEN_TILE, D_MODEL), BF16),
                        pltpu.VMEM((TOKEN_TILE, D_MODEL), F32)],
        compiler_params=_params(("arbitrary", "arbitrary")),
        name="conv_ffn",
    )(*args)


def kernel(x_prompt, x_sample, state_hgrn, state_ret, c, c_ctx, norm1, norm2, final_norm,
           w_mod, b_mod, w_in, hgrn_lb_raw, p_a, p_b, w_out, w_up, w_conv, b_conv, w_down):
    n_ctx, t_ctx, _ = x_prompt.shape
    n_dec, t_dec, _ = x_sample.shape
    assert t_ctx & (t_ctx - 1) == 0 and TOKEN_TILE % t_ctx == 0 and t_dec == TOKEN_TILE

    sm = jax.nn.softmax(hgrn_lb_raw.astype(F32), axis=0)
    cum = jnp.cumsum(sm, axis=0)
    lower_bounds = cum - cum[0:1]

    cvec = jnp.concatenate(
        [c_ctx[None, :], c, jnp.zeros((MOD_ROWS - 1 - n_dec, D_MODEL), F32)], axis=0)
    mods = _mod_call(cvec, w_mod, b_mod).reshape(DEPTH, MOD_ROWS, 1, 6 * D_MODEL)

    w_in_b, p_a_b, p_b_b, w_out_b, w_up_b, w_down_b = (
        w.astype(BF16) for w in (w_in, p_a, p_b, w_out, w_up, w_down))

    ctx_row = lambda tok: 0
    dec_row = lambda tok: tok // t_dec + 1

    def layer(x, l, n_b, seq_len, row_fn, s_hgrn, s_ret, latent):
        proj = _proj_call(x, mods[l], norm1[l], w_in_b[l], row_fn)
        proj3 = proj.reshape(n_b, seq_len, IN_WIDTH)
        o_a, st_h = _hgrn_call(proj3, lower_bounds[l], s_hgrn, l)
        o_b, st_r = _ret_call(proj3, s_ret, l, latent)
        x = _post_call(o_a.reshape(-1, H_A * DV_A), o_b.reshape(-1, H_B * DV_B), proj, x,
                       mods[l], p_a_b[l], p_b_b[l], w_out_b[l], row_fn)
        x = _ffn_call(x, mods[l], norm2[l], w_up_b[l], w_conv[l], b_conv[l], w_down_b[l],
                      final_norm if l == DEPTH - 1 else None, row_fn, seq_len)
        return x, st_h, st_r

    x = x_prompt.reshape(n_ctx * t_ctx, D_MODEL)
    hgrn_states, ret_states = [], []
    for l in range(DEPTH):
        x, st_h, st_r = layer(x, l, n_ctx, t_ctx, ctx_row, None, None, False)
        hgrn_states.append(st_h)
        ret_states.append(st_r)
    y_prompt = x.reshape(n_ctx, t_ctx, D_MODEL)
    new_state_hgrn = jnp.stack(hgrn_states, axis=1)
    new_state_ret = jnp.stack(ret_states, axis=1)

    x = x_sample.reshape(n_dec * t_dec, D_MODEL)
    for l in range(DEPTH):
        x, _, _ = layer(x, l, n_dec, t_dec, dec_row, state_hgrn, state_ret, True)
    y_sample = x.reshape(n_dec, t_dec, D_MODEL)
    return (y_prompt, y_sample, new_state_hgrn, new_state_ret)
```

```python
import functools

import numpy as np
import jax
import jax.numpy as jnp
from jax import lax
from jax.experimental import pallas as pl
from jax.experimental.pallas import tpu as pltpu

F32 = jnp.float32
BF16 = jnp.bfloat16

D_MODEL = 1024
DEPTH = 2
GRID_W = 64
H_A, DK_A, DV_A = 8, 128, 128
H_B, DK_B, DV_B = 4, 256, 512
D_FF = 2816
ROPE_BASE = 10000.0
EPS = 1e-6
RET_DECAY_OFFSET_BWD = 0.5
IN_WIDTH = 13312

OFF_QA, OFF_ZF, OFF_ZB, OFF_IA, OFF_GA = 0, 1024, 2048, 3072, 4096
OFF_QB, OFF_KB, OFF_VB, OFF_GB = 5120, 6144, 7168, 9216
OFF_GATE_A, OFF_GATE_B = 11264, 12288

MOD_ROWS = 8
TOKEN_TILE = 1024
POST_TILE = 256
FF_CHUNK = 256
HGRN_CHUNK = 128
RET_ROWS = 256
VMEM_LIMIT = 52 * 1024 * 1024
SUBLANES = 8

_NT = (((1,), (1,)), ((), ()))
_TN = (((0,), (0,)), ((), ()))


def _params(sem):
    return pltpu.CompilerParams(dimension_semantics=sem, vmem_limit_bytes=VMEM_LIMIT)


def _sigmoid(x):
    return jax.nn.sigmoid(x)


def _silu(x):
    return x * jax.nn.sigmoid(x)


def _mod_kernel(c_ref, w_ref, b_ref, o_ref):
    s = _silu(c_ref[...])
    o_ref[...] = jnp.dot(s, w_ref[...], precision=lax.Precision.HIGHEST,
                         preferred_element_type=F32) + b_ref[...]


def _mod_call(cvec, w_mod, b_mod):
    n_col = 6 * D_MODEL // 1024
    return pl.pallas_call(
        _mod_kernel,
        grid=(DEPTH, n_col),
        in_specs=[
            pl.BlockSpec((MOD_ROWS, D_MODEL), lambda l, j: (0, 0)),
            pl.BlockSpec((None, D_MODEL, 1024), lambda l, j: (l, 0, j)),
            pl.BlockSpec((None, 1, 1024), lambda l, j: (l, 0, j)),
        ],
        out_specs=pl.BlockSpec((None, MOD_ROWS, 1024), lambda l, j: (l, 0, j)),
        out_shape=jax.ShapeDtypeStruct((DEPTH, MOD_ROWS, 6 * D_MODEL), F32),
        compiler_params=_params(("arbitrary", "arbitrary")),
        name="modulation",
    )(cvec, w_mod, b_mod.reshape(DEPTH, 1, 6 * D_MODEL))


def _mod_spec(layer, row_fn, which, tile=TOKEN_TILE):
    return pl.BlockSpec((None, None, 1, D_MODEL),
                        lambda i, j: (layer, row_fn(i * tile), 0, which))


def _layer_spec(layer, block, index_map):
    return pl.BlockSpec((None,) + block, lambda *g: (layer,) + index_map(*g))


def _norm_mod(x, g, sc, sh):
    y = x * lax.rsqrt(jnp.mean(x * x, axis=-1, keepdims=True) + EPS) * g
    return y * (1.0 + sc) + sh


def _proj_kernel(x_ref, sh_ref, sc_ref, n_ref, w_ref, o_ref, h_ref):
    @pl.when(pl.program_id(1) == 0)
    def _():
        h_ref[...] = _norm_mod(x_ref[...], n_ref[...], sc_ref[...], sh_ref[...]).astype(BF16)

    o_ref[...] = jnp.dot(h_ref[...], w_ref[...], preferred_element_type=F32)


def _proj_call(x, mods, n1, w_in, layer, row_fn):
    n_tok = x.shape[0]
    tn = 1024
    return pl.pallas_call(
        _proj_kernel,
        grid=(n_tok // TOKEN_TILE, IN_WIDTH // tn),
        in_specs=[
            pl.BlockSpec((TOKEN_TILE, D_MODEL), lambda i, j: (i, 0)),
            _mod_spec(layer, row_fn, 0),
            _mod_spec(layer, row_fn, 1),
            _layer_spec(layer, (1, D_MODEL), lambda i, j: (0, 0)),
            _layer_spec(layer, (D_MODEL, tn), lambda i, j: (0, j)),
        ],
        out_specs=pl.BlockSpec((TOKEN_TILE, tn), lambda i, j: (i, j)),
        out_shape=jax.ShapeDtypeStruct((n_tok, IN_WIDTH), F32),
        scratch_shapes=[pltpu.VMEM((TOKEN_TILE, D_MODEL), BF16)],
        compiler_params=_params(("arbitrary", "arbitrary")),
        name="in_projection",
    )(x, mods, mods, n1, w_in)


def _hgrn_consts(c):
    nl = int(np.log2(c))
    t = np.arange(c)
    tt, rr = t[:, None], t[None, :]
    x = tt ^ rr
    lev_of = np.where(x > 0, np.floor(np.log2(np.maximum(x, 1))) + 1, 0).astype(np.int32)
    lv_f = np.where(tt >= rr, lev_of, -1).astype(np.int32)
    lv_b = lv_f.T.copy()
    return ((rr <= tt).astype(np.float32), (rr >= tt).astype(np.float32), lv_f, lv_b, nl)


def _level_log_factor(la, cum, cum_ref, lev, reverse):
    c, dk = la.shape
    if lev == 1:
        row = lax.broadcasted_iota(jnp.int32, (c, 1), 0)
        return jnp.where((row & 1) == (0 if reverse else 1), la, 0.0)
    blk, half = 1 << lev, 1 << (lev - 1)

    def boundary(r):
        base = (r // blk) * blk
        row = base + half if reverse else base + half - 1
        return jnp.broadcast_to(cum_ref[row:row + 1, :], (SUBLANES, dk))

    top = lax.broadcasted_iota(jnp.int32, (SUBLANES, dk), 0) < SUBLANES // 2
    pieces = []
    for r0 in range(0, c, SUBLANES):
        if blk >= SUBLANES:
            pieces.append(boundary(r0))
        else:
            pieces.append(jnp.where(top, boundary(r0), boundary(r0 + SUBLANES // 2)))
    return -jnp.abs(cum - jnp.concatenate(pieces, axis=0))


def _hgrn_forget(z, lb):
    log_lb = jnp.log(lb)
    b = jnp.log1p(-lb) + (jnp.minimum(z, 0.0) - jnp.log1p(jnp.exp(-jnp.abs(z))))
    amax = jnp.maximum(log_lb, b)
    delta = log_lb - b
    log_f = jnp.where(jnp.isnan(delta), log_lb + b, amax + jnp.log1p(jnp.exp(-jnp.abs(delta))))
    key = (1.0 - lb) * _sigmoid(-z)
    return log_f, key


def _hgrn_chunk(q, k, v, la, st, tri, lv, cum_ref, n_lev, reverse):
    c, dk = q.shape
    la_hi = la.astype(BF16)
    la_lo = (la - la_hi.astype(F32)).astype(BF16)
    x2 = jnp.dot(tri, jnp.concatenate([la_hi, la_lo], axis=1), preferred_element_type=F32)
    cum = x2[:, :dk] + x2[:, dk:]
    cum_ref[...] = cum
    zero = jnp.zeros((c, dk), BF16)
    scores = jnp.zeros((c, c), F32)
    for a in range(0, n_lev + 1, 2):
        e_b = jnp.exp(_level_log_factor(la, cum, cum_ref, a + 1, reverse))
        if a == 0:
            q_a, k_a = q, k
        else:
            e_a = jnp.exp(_level_log_factor(la, cum, cum_ref, a, reverse))
            q_a, k_a = q * e_a, k * e_a
        lhs = jnp.concatenate([q_a, q * e_b], axis=1).astype(BF16)
        rhs = jnp.concatenate([
            jnp.concatenate([k_a.astype(BF16), zero], axis=1),
            jnp.concatenate([zero, (k * e_b).astype(BF16)], axis=1)], axis=0)
        p = lax.dot_general(lhs, rhs, _NT, preferred_element_type=F32)
        scores = jnp.where(lv == a, p[:, :c], scores)
        scores = jnp.where(lv == a + 1, p[:, c:], scores)
    v_b = v.astype(BF16)
    o = jnp.dot(scores.astype(BF16), v_b, preferred_element_type=F32)
    o = o + lax.dot_general((q * jnp.exp(cum)).astype(BF16), st.astype(BF16), _NT,
                            preferred_element_type=F32)
    end_row = 0 if reverse else c - 1
    cum_end = cum[end_row:end_row + 1]
    kd = (k * jnp.exp(cum_end - cum)).astype(BF16)
    st = st * jnp.exp(cum_end) + lax.dot_general(v_b, kd, _TN, preferred_element_type=F32)
    return o, st


def _hgrn_kernel(*refs, seq_len, chunk, n_lev, has_s0, n_prev):
    it = iter(refs)
    q_ref, zf_ref, zb_ref, i_ref, lb_ref, trif_ref, trib_ref, lvf_ref, lvb_ref = (
        next(it) for _ in range(9))
    s0_ref = next(it) if has_s0 else None
    prev_ref = next(it) if n_prev else None
    o_ref = next(it)
    st_ref = None if has_s0 else next(it)
    of_ref, ob_ref, cumf_ref, cumb_ref = (next(it) for _ in range(4))
    n_chunks = seq_len // chunk
    lb = lb_ref[...]

    def step(cidx, z_ref, lb_d, tri_ref, lv_ref, cum_ref, out_ref, st, reverse):
        rows = pl.ds(pl.multiple_of(cidx * chunk, chunk), chunk)
        la, k = _hgrn_forget(z_ref[rows, :], lb_d)
        o, st = _hgrn_chunk(q_ref[rows, :], k, i_ref[rows, :], la, st, tri_ref[...], lv_ref[...],
                            cum_ref, n_lev, reverse)
        out_ref[rows, :] = o
        return st

    def body(ci, carry):
        st_f = step(ci, zf_ref, lb[0:1], trif_ref, lvf_ref, cumf_ref, of_ref, carry[0], False)
        st_b = step(n_chunks - 1 - ci, zb_ref, lb[1:2], trib_ref, lvb_ref, cumb_ref, ob_ref,
                    carry[1], True)
        return st_f, st_b

    if has_s0:
        st0 = (s0_ref[0].T, s0_ref[1].T)
    else:
        st0 = (jnp.zeros((DV_A, DK_A), F32),) * 2
    st_f, st_b = lax.fori_loop(0, n_chunks, body, st0)
    o = of_ref[...] + ob_ref[...]
    o_ref[...] = o * lax.rsqrt(jnp.mean(o * o, axis=-1, keepdims=True) + EPS)
    if st_ref is not None:
        if n_prev:
            st_ref[0:n_prev] = prev_ref[...]
        st_ref[n_prev, 0] = st_f.T
        st_ref[n_prev, 1] = st_b.T


def _hgrn_call(proj, lb, state, prev, layer):
    n_b, seq_len, _ = proj.shape
    tri_f, tri_b, lv_f, lv_b, n_lev = _hgrn_consts(HGRN_CHUNK)
    has_s0 = state is not None
    col = lambda off: (lambda b, h: (b, 0, off // DK_A + h))
    const = lambda a: pl.BlockSpec(a.shape, lambda b, h: (0, 0))
    in_specs = [pl.BlockSpec((None, seq_len, DK_A), col(o))
                for o in (OFF_QA, OFF_ZF, OFF_ZB, OFF_IA)]
    in_specs += [_layer_spec(layer, (2, DK_A), lambda b, h: (0, h)),
                 const(tri_f), const(tri_b), const(lv_f), const(lv_b)]
    args = [proj, proj, proj, proj, lb, jnp.asarray(tri_f, BF16), jnp.asarray(tri_b, BF16),
            jnp.asarray(lv_f), jnp.asarray(lv_b)]
    out_specs = [pl.BlockSpec((None, seq_len, DV_A), lambda b, h: (b, 0, h))]
    out_shape = [jax.ShapeDtypeStruct((n_b, seq_len, H_A * DV_A), F32)]
    if has_s0:
        in_specs.append(pl.BlockSpec((None, None, 2, None, DK_A, DV_A),
                                     lambda b, h: (b, layer, 0, h, 0, 0)))
        args.append(state)
    else:
        st_spec = lambda n: pl.BlockSpec((None, n, 2, None, DK_A, DV_A),
                                         lambda b, h: (b, 0, 0, h, 0, 0))
        if layer:
            in_specs.append(st_spec(layer))
            args.append(prev)
        out_specs.append(st_spec(layer + 1))
        out_shape.append(jax.ShapeDtypeStruct((n_b, layer + 1, 2, H_A, DK_A, DV_A), F32))
    res = pl.pallas_call(
        functools.partial(_hgrn_kernel, seq_len=seq_len, chunk=HGRN_CHUNK, n_lev=n_lev,
                          has_s0=has_s0, n_prev=0 if has_s0 else layer),
        grid=(n_b, H_A),
        in_specs=in_specs, out_specs=out_specs, out_shape=out_shape,
        scratch_shapes=[pltpu.VMEM((seq_len, DV_A), F32), pltpu.VMEM((seq_len, DV_A), F32),
                        pltpu.VMEM((HGRN_CHUNK, DK_A), F32), pltpu.VMEM((HGRN_CHUNK, DK_A), F32)],
        compiler_params=_params(("arbitrary", "arbitrary")),
        name="hgrn2_scan",
    )(*args)
    return (res[0], None) if has_s0 else (res[0], res[1])


def _rope_tables(seq_len):
    rows = seq_len // GRID_W
    r_idx = jnp.repeat(jnp.arange(rows), GRID_W).astype(F32)
    c_idx = jnp.tile(jnp.arange(GRID_W), rows).astype(F32)
    quarter = DK_B // 4
    inv = 1.0 / (ROPE_BASE ** (jnp.arange(quarter, dtype=F32) / quarter))
    ang_r = r_idx[:, None] * inv[None, :]
    ang_c = c_idx[:, None] * inv[None, :]
    cos = jnp.concatenate([jnp.cos(ang_r)] * 2 + [jnp.cos(ang_c)] * 2, axis=1)
    sin = jnp.concatenate([-jnp.sin(ang_r), jnp.sin(ang_r), -jnp.sin(ang_c), jnp.sin(ang_c)],
                          axis=1)
    return cos, sin


def _rope(x, cos, sin):
    half = DK_B // 2
    swapped = jnp.concatenate([pltpu.roll(x[:, :half], half // 2, axis=1),
                               pltpu.roll(x[:, half:], half // 2, axis=1)], axis=1)
    return x * cos + swapped * sin


def _ret_kernel(*refs, seq_len, latent, has_s0, n_prev):
    it = iter(refs)
    lg_ref, q_ref, k_ref, v_ref = (next(it) for _ in range(4))
    cos_ref, sin_ref = (next(it), next(it)) if latent else (None, None)
    s0_ref = next(it) if has_s0 else None
    prev_ref = next(it) if n_prev else None
    o_ref = next(it)
    st_ref = None if has_s0 else next(it)
    dm_ref = next(it)
    h = pl.program_id(0)
    lg_f = lg_ref[h, 0]
    lg_b = lg_ref[h, 1]

    @pl.when(pl.program_id(1) == 0)
    def _():
        t = lax.broadcasted_iota(jnp.int32, (seq_len, seq_len), 0)
        s = lax.broadcasted_iota(jnp.int32, (seq_len, seq_len), 1)
        d = (t - s).astype(F32)
        dm_ref[...] = (jnp.where(d >= 0.0, jnp.exp(lg_f * jnp.maximum(d, 0.0)), 0.0)
                       + jnp.where(d <= 0.0, jnp.exp(lg_b * jnp.maximum(-d, 0.0)), 0.0))

    q = q_ref[...]
    k = k_ref[...] * (DK_B ** -0.5)
    if latent:
        cos, sin = cos_ref[...], sin_ref[...]
        q, k = _rope(q, cos, sin), _rope(k, cos, sin)
    k_b = k.astype(BF16)
    v_b = v_ref[...].astype(BF16)
    if has_s0:
        s0 = jnp.concatenate([s0_ref[0], s0_ref[1]], axis=0).astype(BF16)
    for r in range(seq_len // RET_ROWS):
        rows = slice(r * RET_ROWS, (r + 1) * RET_ROWS)
        q_r = q[rows]
        p = lax.dot_general(q_r.astype(BF16), k_b, _NT, preferred_element_type=F32)
        o = jnp.dot((p * dm_ref[rows, :]).astype(BF16), v_b, preferred_element_type=F32)
        if has_s0:
            pos = (lax.broadcasted_iota(jnp.int32, (RET_ROWS, 1), 0) + r * RET_ROWS).astype(F32)
            q_s = jnp.concatenate([q_r * jnp.exp(lg_f * (pos + 1.0)),
                                   q_r * jnp.exp(lg_b * (seq_len - pos))], axis=1)
            o = o + jnp.dot(q_s.astype(BF16), s0, preferred_element_type=F32)
        o_ref[rows, :] = o * lax.rsqrt(jnp.mean(o * o, axis=-1, keepdims=True) + EPS)
    if st_ref is not None:
        if n_prev:
            st_ref[0:n_prev] = prev_ref[...]
        pos = lax.broadcasted_iota(jnp.int32, (seq_len, 1), 0).astype(F32)
        k_f = (k * jnp.exp(lg_f * (seq_len - 1.0 - pos))).astype(BF16)
        k_r = (k * jnp.exp(lg_b * pos)).astype(BF16)
        st_ref[n_prev, 0] = lax.dot_general(k_f, v_b, _TN, preferred_element_type=F32)
        st_ref[n_prev, 1] = lax.dot_general(k_r, v_b, _TN, preferred_element_type=F32)


def _ret_log_decay():
    heads = jnp.arange(H_B, dtype=F32)
    lg_f = jnp.log1p(-jnp.exp2(-5.0 - heads))
    lg_b = jnp.log1p(-jnp.exp2(-(5.0 + RET_DECAY_OFFSET_BWD) - heads))
    return jnp.stack([lg_f, lg_b], axis=1)


def _ret_call(proj, state, prev, layer, latent):
    n_b, seq_len, _ = proj.shape
    has_s0 = state is not None
    in_specs = [
        pl.BlockSpec(memory_space=pltpu.SMEM),
        pl.BlockSpec((None, seq_len, DK_B), lambda h, b: (b, 0, OFF_QB // DK_B + h)),
        pl.BlockSpec((None, seq_len, DK_B), lambda h, b: (b, 0, OFF_KB // DK_B + h)),
        pl.BlockSpec((None, seq_len, DV_B), lambda h, b: (b, 0, OFF_VB // DV_B + h)),
    ]
    args = [_ret_log_decay(), proj, proj, proj]
    if latent:
        cos, sin = _rope_tables(seq_len)
        in_specs += [pl.BlockSpec((seq_len, DK_B), lambda h, b: (0, 0))] * 2
        args += [cos, sin]
    out_specs = [pl.BlockSpec((None, seq_len, DV_B), lambda h, b: (b, 0, h))]
    out_shape = [jax.ShapeDtypeStruct((n_b, seq_len, H_B * DV_B), F32)]
    if has_s0:
        in_specs.append(pl.BlockSpec((None, None, 2, None, DK_B, DV_B),
                                     lambda h, b: (b, layer, 0, h, 0, 0)))
        args.append(state)
    else:
        st_spec = lambda n: pl.BlockSpec((None, n, 2, None, DK_B, DV_B),
                                         lambda h, b: (b, 0, 0, h, 0, 0))
        if layer:
            in_specs.append(st_spec(layer))
            args.append(prev)
        out_specs.append(st_spec(layer + 1))
        out_shape.append(jax.ShapeDtypeStruct((n_b, layer + 1, 2, H_B, DK_B, DV_B), F32))
    res = pl.pallas_call(
        functools.partial(_ret_kernel, seq_len=seq_len, latent=latent, has_s0=has_s0,
                          n_prev=0 if has_s0 else layer),
        grid=(H_B, n_b),
        in_specs=in_specs, out_specs=out_specs, out_shape=out_shape,
        scratch_shapes=[pltpu.VMEM((seq_len, seq_len), F32)],
        compiler_params=_params(("arbitrary", "arbitrary")),
        name="retention_scan",
    )(*args)
    return (res[0], None) if has_s0 else (res[0], res[1])


def _post_kernel(oa_ref, ga_ref, ob_ref, gb0_ref, gb1_ref, gta_ref, gtb_ref, x_ref, gt1_ref,
                 pa_ref, pb_ref, wo_ref, o_ref):
    a = (oa_ref[...] * _silu(ga_ref[...])).astype(BF16)
    g_b = jnp.concatenate([gb0_ref[...], gb1_ref[...]], axis=1)
    b = (ob_ref[...] * _silu(g_b)).astype(BF16)
    y_a = jnp.dot(a, pa_ref[...], preferred_element_type=F32)
    y_b = jnp.dot(b, pb_ref[...], preferred_element_type=F32)
    merged = _sigmoid(gta_ref[...]) * y_a + _sigmoid(gtb_ref[...]) * y_b
    y = jnp.dot(merged.astype(BF16), wo_ref[...], preferred_element_type=F32)
    o_ref[...] = x_ref[...] + gt1_ref[...] * y


def _post_call(o_a, o_b, proj, x, mods, p_a, p_b, w_out, layer, row_fn):
    n_tok = x.shape[0]
    tile = lambda w: (lambda i, j: (i, w))
    pcol = lambda off: pl.BlockSpec((POST_TILE, 1024), tile(off // 1024))
    const = lambda a: _layer_spec(layer, a.shape[1:], lambda i, j: (0, 0))
    return pl.pallas_call(
        _post_kernel,
        grid=(n_tok // POST_TILE, 1),
        in_specs=[
            pl.BlockSpec((POST_TILE, H_A * DV_A), tile(0)),
            pcol(OFF_GA),
            pl.BlockSpec((POST_TILE, H_B * DV_B), tile(0)),
            pcol(OFF_GB), pcol(OFF_GB + 1024), pcol(OFF_GATE_A), pcol(OFF_GATE_B),
            pl.BlockSpec((POST_TILE, D_MODEL), tile(0)),
            _mod_spec(layer, row_fn, 2, POST_TILE),
            const(p_a), const(p_b), const(w_out),
        ],
        out_specs=pl.BlockSpec((POST_TILE, D_MODEL), tile(0)),
        out_shape=jax.ShapeDtypeStruct((n_tok, D_MODEL), F32),
        compiler_params=_params(("arbitrary", "arbitrary")),
        name="mixer_output",
    )(o_a, proj, o_b, proj, proj, proj, proj, x, mods, p_a, p_b, w_out)


def _conv3(u, wc, bc, first, last):
    n = u.shape[0]
    prev = jnp.where(first, 0.0, pltpu.roll(u, 1, axis=0))
    nxt = jnp.where(last, 0.0, pltpu.roll(u, n - 1, axis=0))
    return prev * wc[0:1] + u * wc[1:2] + nxt * wc[2:3] + bc


def _ffn_kernel(*refs, seq_len, final):
    it = iter(refs)
    (x_ref, sh_ref, sc_ref, gt_ref, n2_ref, wa_ref, wg_ref, wca_ref, wcg_ref, bca_ref, bcg_ref,
     wd_ref) = (next(it) for _ in range(12))
    fn_ref = next(it) if final else None
    o_ref, h_ref, acc_ref = next(it), next(it), next(it)
    j = pl.program_id(1)

    @pl.when(j == 0)
    def _():
        h_ref[...] = _norm_mod(x_ref[...], n2_ref[...], sc_ref[...], sh_ref[...]).astype(BF16)
        acc_ref[...] = jnp.zeros_like(acc_ref)

    h = h_ref[...]
    pos = lax.broadcasted_iota(jnp.int32, (TOKEN_TILE, 1), 0) & (seq_len - 1)
    first, last = pos == 0, pos == seq_len - 1
    u_a = jnp.dot(h, wa_ref[...], preferred_element_type=F32)
    u_g = jnp.dot(h, wg_ref[...], preferred_element_type=F32)
    a = _conv3(u_a, wca_ref[...], bca_ref[...], first, last)
    g = _conv3(u_g, wcg_ref[...], bcg_ref[...], first, last)
    acc_ref[...] += jnp.dot((_silu(g) * a).astype(BF16), wd_ref[...], preferred_element_type=F32)

    @pl.when(j == pl.num_programs(1) - 1)
    def _():
        x = x_ref[...] + gt_ref[...] * acc_ref[...]
        if final:
            x = x * lax.rsqrt(jnp.mean(x * x, axis=-1, keepdims=True) + EPS) * fn_ref[...]
        o_ref[...] = x


def _ffn_call(x, mods, n2, w_up, w_conv, b_conv, w_down, final_norm, layer, row_fn, seq_len):
    n_tok = x.shape[0]
    n_ff = D_FF // FF_CHUNK
    lspec = functools.partial(_layer_spec, layer)
    in_specs = [
        pl.BlockSpec((TOKEN_TILE, D_MODEL), lambda i, j: (i, 0)),
        _mod_spec(layer, row_fn, 3), _mod_spec(layer, row_fn, 4), _mod_spec(layer, row_fn, 5),
        lspec((1, D_MODEL), lambda i, j: (0, 0)),
        lspec((D_MODEL, FF_CHUNK), lambda i, j: (0, j)),
        lspec((D_MODEL, FF_CHUNK), lambda i, j: (0, n_ff + j)),
        lspec((3, FF_CHUNK), lambda i, j: (0, j)),
        lspec((3, FF_CHUNK), lambda i, j: (0, n_ff + j)),
        lspec((1, FF_CHUNK), lambda i, j: (0, j)),
        lspec((1, FF_CHUNK), lambda i, j: (0, n_ff + j)),
        lspec((FF_CHUNK, D_MODEL), lambda i, j: (j, 0)),
    ]
    args = [x, mods, mods, mods, n2, w_up, w_up, w_conv, w_conv, b_conv, b_conv, w_down]
    final = final_norm is not None
    if final:
        in_specs.append(pl.BlockSpec((1, D_MODEL), lambda i, j: (0, 0)))
        args.append(final_norm.reshape(1, D_MODEL))
    return pl.pallas_call(
        functools.partial(_ffn_kernel, seq_len=seq_len, final=final),
        grid=(n_tok // TOKEN_TILE, n_ff),
        in_specs=in_specs,
        out_specs=pl.BlockSpec((TOKEN_TILE, D_MODEL), lambda i, j: (i, 0)),
        out_shape=jax.ShapeDtypeStruct((n_tok, D_MODEL), F32),
        scratch_shapes=[pltpu.VMEM((TOKEN_TILE, D_MODEL), BF16),
                        pltpu.VMEM((TOKEN_TILE, D_MODEL), F32)],
        compiler_params=_params(("arbitrary", "arbitrary")),
        name="conv_ffn",
    )(*args)


def kernel(x_prompt, x_sample, state_hgrn, state_ret, c, c_ctx, norm1, norm2, final_norm,
           w_mod, b_mod, w_in, hgrn_lb_raw, p_a, p_b, w_out, w_up, w_conv, b_conv, w_down):
    n_ctx, t_ctx, _ = x_prompt.shape
    n_dec, t_dec, _ = x_sample.shape
    assert t_ctx & (t_ctx - 1) == 0 and TOKEN_TILE % t_ctx == 0 and t_dec == TOKEN_TILE

    sm = jax.nn.softmax(hgrn_lb_raw.astype(F32), axis=0)
    cum = jnp.cumsum(sm, axis=0)
    lower_bounds = cum - cum[0:1]

    cvec = jnp.concatenate(
        [c_ctx[None, :], c, jnp.zeros((MOD_ROWS - 1 - n_dec, D_MODEL), F32)], axis=0)
    mods = _mod_call(cvec, w_mod, b_mod).reshape(DEPTH, MOD_ROWS, 1, 6 * D_MODEL)

    w_in_b, p_a_b, p_b_b, w_out_b, w_up_b, w_down_b = (
        w.astype(BF16) for w in (w_in, p_a, p_b, w_out, w_up, w_down))
    norm1_3, norm2_3 = norm1.reshape(DEPTH, 1, D_MODEL), norm2.reshape(DEPTH, 1, D_MODEL)
    b_conv_3 = b_conv.reshape(DEPTH, 1, 2 * D_FF)

    ctx_row = lambda tok: 0
    dec_row = lambda tok: tok // t_dec + 1

    def layer(x, l, n_b, seq_len, row_fn, s_hgrn, s_ret, prev_h, prev_r, latent):
        proj = _proj_call(x, mods, norm1_3, w_in_b, l, row_fn)
        proj3 = proj.reshape(n_b, seq_len, IN_WIDTH)
        o_a, st_h = _hgrn_call(proj3, lower_bounds, s_hgrn, prev_h, l)
        o_b, st_r = _ret_call(proj3, s_ret, prev_r, l, latent)
        x = _post_call(o_a.reshape(-1, H_A * DV_A), o_b.reshape(-1, H_B * DV_B), proj, x,
                       mods, p_a_b, p_b_b, w_out_b, l, row_fn)
        x = _ffn_call(x, mods, norm2_3, w_up_b, w_conv, b_conv_3, w_down_b,
                      final_norm if l == DEPTH - 1 else None, l, row_fn, seq_len)
        return x, st_h, st_r

    x = x_prompt.reshape(n_ctx * t_ctx, D_MODEL)
    st_h = st_r = None
    for l in range(DEPTH):
        x, st_h, st_r = layer(x, l, n_ctx, t_ctx, ctx_row, None, None, st_h, st_r, False)
    y_prompt = x.reshape(n_ctx, t_ctx, D_MODEL)

    x = x_sample.reshape(n_dec * t_dec, D_MODEL)
    for l in range(DEPTH):
        x, _, _ = layer(x, l, n_dec, t_dec, dec_row, state_hgrn, state_ret, None, None, True)
    y_sample = x.reshape(n_dec, t_dec, D_MODEL)
    return (y_prompt, y_sample, st_h, st_r)
```

```python
import functools

import numpy as np
import jax
import jax.numpy as jnp
from jax import lax
from jax.experimental import pallas as pl
from jax.experimental.pallas import tpu as pltpu

F32 = jnp.float32
BF16 = jnp.bfloat16

D_MODEL = 1024
DEPTH = 2
GRID_W = 64
H_A, DK_A, DV_A = 8, 128, 128
H_B, DK_B, DV_B = 4, 256, 512
D_FF = 2816
ROPE_BASE = 10000.0
K_SCALE = DK_B ** -0.5
EPS = 1e-6
RET_DECAY_OFFSET_BWD = 0.5
IN_WIDTH = 13312

OFF_QA, OFF_ZF, OFF_ZB, OFF_IA, OFF_GA = 0, 1024, 2048, 3072, 4096
OFF_QB, OFF_KB, OFF_VB, OFF_GB = 5120, 6144, 7168, 9216
OFF_GATE_A, OFF_GATE_B = 11264, 12288

MOD_ROWS = 8
TOKEN_TILE = 1024
POST_TILE = 256
FF_CHUNK = 256
HGRN_CHUNK = 128
RET_ROWS = 256
PROJ_TN = 1024
Z_TILE0 = OFF_ZF // PROJ_TN
Z_TILES = (OFF_IA - OFF_ZF) // PROJ_TN
VMEM_LIMIT = 52 * 1024 * 1024
SUBLANES = 8

_NT = (((1,), (1,)), ((), ()))
_TN = (((0,), (0,)), ((), ()))


def _params(sem):
    return pltpu.CompilerParams(dimension_semantics=sem, vmem_limit_bytes=VMEM_LIMIT)


def _sigmoid(x):
    return jax.nn.sigmoid(x)


def _silu(x):
    return x * jax.nn.sigmoid(x)


def _mod_kernel(c_ref, w_ref, b_ref, o_ref):
    s = _silu(c_ref[...])
    o_ref[...] = jnp.dot(s, w_ref[...], precision=lax.Precision.HIGHEST,
                         preferred_element_type=F32) + b_ref[...]


def _mod_call(cvec, w_mod, b_mod):
    n_col = 6 * D_MODEL // 1024
    return pl.pallas_call(
        _mod_kernel,
        grid=(DEPTH, n_col),
        in_specs=[
            pl.BlockSpec((MOD_ROWS, D_MODEL), lambda l, j: (0, 0)),
            pl.BlockSpec((None, D_MODEL, 1024), lambda l, j: (l, 0, j)),
            pl.BlockSpec((None, 1, 1024), lambda l, j: (l, 0, j)),
        ],
        out_specs=pl.BlockSpec((None, MOD_ROWS, 1024), lambda l, j: (l, 0, j)),
        out_shape=jax.ShapeDtypeStruct((DEPTH, MOD_ROWS, 6 * D_MODEL), F32),
        compiler_params=_params(("arbitrary", "arbitrary")),
        name="modulation",
    )(cvec, w_mod, b_mod.reshape(DEPTH, 1, 6 * D_MODEL))


def _mod_spec(layer, row_fn, which, tile=TOKEN_TILE):
    return pl.BlockSpec((None, None, 1, D_MODEL),
                        lambda i, j: (layer, row_fn(i * tile), 0, which))


def _layer_spec(layer, block, index_map):
    return pl.BlockSpec((None,) + block, lambda *g: (layer,) + index_map(*g))


def _norm_mod(x, g, sc, sh):
    y = x * lax.rsqrt(jnp.mean(x * x, axis=-1, keepdims=True) + EPS) * g
    return y * (1.0 + sc) + sh


def _proj_kernel(x_ref, sh_ref, sc_ref, n_ref, w_ref, o_ref, z_ref, h_ref):
    j = pl.program_id(1)

    @pl.when(j == 0)
    def _():
        h_ref[...] = _norm_mod(x_ref[...], n_ref[...], sc_ref[...], sh_ref[...]).astype(BF16)

    acc = jnp.dot(h_ref[...], w_ref[...], preferred_element_type=F32)
    o_ref[...] = acc.astype(BF16)

    @pl.when((j >= Z_TILE0) & (j < Z_TILE0 + Z_TILES))
    def _():
        z_ref[...] = acc


def _proj_call(x, mods, n1, w_in, layer, row_fn):
    n_tok = x.shape[0]
    tn = PROJ_TN
    return pl.pallas_call(
        _proj_kernel,
        grid=(n_tok // TOKEN_TILE, IN_WIDTH // tn),
        in_specs=[
            pl.BlockSpec((TOKEN_TILE, D_MODEL), lambda i, j: (i, 0)),
            _mod_spec(layer, row_fn, 0),
            _mod_spec(layer, row_fn, 1),
            _layer_spec(layer, (1, D_MODEL), lambda i, j: (0, 0)),
            _layer_spec(layer, (D_MODEL, tn), lambda i, j: (0, j)),
        ],
        out_specs=[
            pl.BlockSpec((TOKEN_TILE, tn), lambda i, j: (i, j)),
            pl.BlockSpec((TOKEN_TILE, tn),
                         lambda i, j: (i, jnp.clip(j - Z_TILE0, 0, Z_TILES - 1))),
        ],
        out_shape=[jax.ShapeDtypeStruct((n_tok, IN_WIDTH), BF16),
                   jax.ShapeDtypeStruct((n_tok, Z_TILES * tn), F32)],
        scratch_shapes=[pltpu.VMEM((TOKEN_TILE, D_MODEL), BF16)],
        compiler_params=_params(("arbitrary", "arbitrary")),
        name="in_projection",
    )(x, mods, mods, n1, w_in)


def _hgrn_consts(c):
    nl = int(np.log2(c))
    t = np.arange(c)
    tt, rr = t[:, None], t[None, :]
    x = tt ^ rr
    lev_of = np.where(x > 0, np.floor(np.log2(np.maximum(x, 1))) + 1, 0).astype(np.int32)
    lv_f = np.where(tt >= rr, lev_of, -1).astype(np.int32)
    lv_b = lv_f.T.copy()
    return ((rr <= tt).astype(np.float32), (rr >= tt).astype(np.float32), lv_f, lv_b, nl)


def _level_log_factor(la, cum, cum_ref, lev, reverse):
    c, dk = la.shape
    if lev == 1:
        row = lax.broadcasted_iota(jnp.int32, (c, 1), 0)
        return jnp.where((row & 1) == (0 if reverse else 1), la, 0.0)
    blk, half = 1 << lev, 1 << (lev - 1)

    def boundary(r):
        base = (r // blk) * blk
        row = base + half if reverse else base + half - 1
        return jnp.broadcast_to(cum_ref[row:row + 1, :], (SUBLANES, dk))

    top = lax.broadcasted_iota(jnp.int32, (SUBLANES, dk), 0) < SUBLANES // 2
    pieces = []
    for r0 in range(0, c, SUBLANES):
        if blk >= SUBLANES:
            pieces.append(boundary(r0))
        else:
            pieces.append(jnp.where(top, boundary(r0), boundary(r0 + SUBLANES // 2)))
    return -jnp.abs(cum - jnp.concatenate(pieces, axis=0))


def _hgrn_forget(z, lb):
    log_lb = jnp.log(lb)
    b = jnp.log1p(-lb) + (jnp.minimum(z, 0.0) - jnp.log1p(jnp.exp(-jnp.abs(z))))
    amax = jnp.maximum(log_lb, b)
    delta = log_lb - b
    log_f = jnp.where(jnp.isnan(delta), log_lb + b, amax + jnp.log1p(jnp.exp(-jnp.abs(delta))))
    key = (1.0 - lb) * _sigmoid(-z)
    return log_f, key


def _hgrn_chunk(q, k, v, la, st, tri, lv, cum_ref, n_lev, reverse):
    c, dk = q.shape
    la_hi = la.astype(BF16)
    la_lo = (la - la_hi.astype(F32)).astype(BF16)
    x2 = jnp.dot(tri, jnp.concatenate([la_hi, la_lo], axis=1), preferred_element_type=F32)
    cum = x2[:, :dk] + x2[:, dk:]
    cum_ref[...] = cum
    zero = jnp.zeros((c, dk), BF16)
    scores = jnp.zeros((c, c), F32)
    for a in range(0, n_lev + 1, 2):
        e_b = jnp.exp(_level_log_factor(la, cum, cum_ref, a + 1, reverse))
        if a == 0:
            q_a, k_a = q, k
        else:
            e_a = jnp.exp(_level_log_factor(la, cum, cum_ref, a, reverse))
            q_a, k_a = q * e_a, k * e_a
        lhs = jnp.concatenate([q_a, q * e_b], axis=1).astype(BF16)
        rhs = jnp.concatenate([
            jnp.concatenate([k_a.astype(BF16), zero], axis=1),
            jnp.concatenate([zero, (k * e_b).astype(BF16)], axis=1)], axis=0)
        p = lax.dot_general(lhs, rhs, _NT, preferred_element_type=F32)
        scores = jnp.where(lv == a, p[:, :c], scores)
        scores = jnp.where(lv == a + 1, p[:, c:], scores)
    v_b = v.astype(BF16)
    o = jnp.dot(scores.astype(BF16), v_b, preferred_element_type=F32)
    o = o + lax.dot_general((q * jnp.exp(cum)).astype(BF16), st.astype(BF16), _NT,
                            preferred_element_type=F32)
    end_row = 0 if reverse else c - 1
    cum_end = cum[end_row:end_row + 1]
    kd = (k * jnp.exp(cum_end - cum)).astype(BF16)
    st = st * jnp.exp(cum_end) + lax.dot_general(v_b, kd, _TN, preferred_element_type=F32)
    return o, st


def _hgrn_kernel(*refs, seq_len, chunk, n_lev, has_s0, n_prev):
    it = iter(refs)
    q_ref, zf_ref, zb_ref, i_ref, lb_ref, trif_ref, trib_ref, lvf_ref, lvb_ref = (
        next(it) for _ in range(9))
    s0_ref = next(it) if has_s0 else None
    prev_ref = next(it) if n_prev else None
    o_ref = next(it)
    st_ref = None if has_s0 else next(it)
    of_ref, ob_ref, cumf_ref, cumb_ref = (next(it) for _ in range(4))
    n_chunks = seq_len // chunk
    lb = lb_ref[...]

    def step(cidx, z_ref, lb_d, tri_ref, lv_ref, cum_ref, out_ref, st, reverse):
        rows = pl.ds(pl.multiple_of(cidx * chunk, chunk), chunk)
        la, k = _hgrn_forget(z_ref[rows, :], lb_d)
        o, st = _hgrn_chunk(q_ref[rows, :].astype(F32), k, i_ref[rows, :].astype(F32), la, st,
                            tri_ref[...], lv_ref[...], cum_ref, n_lev, reverse)
        out_ref[rows, :] = o
        return st

    def body(ci, carry):
        st_f = step(ci, zf_ref, lb[0:1], trif_ref, lvf_ref, cumf_ref, of_ref, carry[0], False)
        st_b = step(n_chunks - 1 - ci, zb_ref, lb[1:2], trib_ref, lvb_ref, cumb_ref, ob_ref,
                    carry[1], True)
        return st_f, st_b

    if has_s0:
        st0 = (s0_ref[0].T, s0_ref[1].T)
    else:
        st0 = (jnp.zeros((DV_A, DK_A), F32),) * 2
    st_f, st_b = lax.fori_loop(0, n_chunks, body, st0)
    o = of_ref[...] + ob_ref[...]
    o_ref[...] = (o * lax.rsqrt(jnp.mean(o * o, axis=-1, keepdims=True) + EPS)).astype(BF16)
    if st_ref is not None:
        if n_prev:
            st_ref[0:n_prev] = prev_ref[...]
        st_ref[n_prev, 0] = st_f.T
        st_ref[n_prev, 1] = st_b.T


def _hgrn_call(proj, z, lb, state, prev, layer):
    n_b, seq_len, _ = proj.shape
    tri_f, tri_b, lv_f, lv_b, n_lev = _hgrn_consts(HGRN_CHUNK)
    has_s0 = state is not None
    col = lambda off: (lambda b, h: (b, 0, off // DK_A + h))
    const = lambda a: pl.BlockSpec(a.shape, lambda b, h: (0, 0))
    in_specs = [pl.BlockSpec((None, seq_len, DK_A), col(o))
                for o in (OFF_QA, 0, OFF_ZB - OFF_ZF, OFF_IA)]
    in_specs += [_layer_spec(layer, (2, DK_A), lambda b, h: (0, h)),
                 const(tri_f), const(tri_b), const(lv_f), const(lv_b)]
    args = [proj, z, z, proj, lb, jnp.asarray(tri_f, BF16), jnp.asarray(tri_b, BF16),
            jnp.asarray(lv_f), jnp.asarray(lv_b)]
    out_specs = [pl.BlockSpec((None, seq_len, DV_A), lambda b, h: (b, 0, h))]
    out_shape = [jax.ShapeDtypeStruct((n_b, seq_len, H_A * DV_A), BF16)]
    if has_s0:
        in_specs.append(pl.BlockSpec((None, None, 2, None, DK_A, DV_A),
                                     lambda b, h: (b, layer, 0, h, 0, 0)))
        args.append(state)
    else:
        st_spec = lambda n: pl.BlockSpec((None, n, 2, None, DK_A, DV_A),
                                         lambda b, h: (b, 0, 0, h, 0, 0))
        if layer:
            in_specs.append(st_spec(layer))
            args.append(prev)
        out_specs.append(st_spec(layer + 1))
        out_shape.append(jax.ShapeDtypeStruct((n_b, layer + 1, 2, H_A, DK_A, DV_A), F32))
    res = pl.pallas_call(
        functools.partial(_hgrn_kernel, seq_len=seq_len, chunk=HGRN_CHUNK, n_lev=n_lev,
                          has_s0=has_s0, n_prev=0 if has_s0 else layer),
        grid=(n_b, H_A),
        in_specs=in_specs, out_specs=out_specs, out_shape=out_shape,
        scratch_shapes=[pltpu.VMEM((seq_len, DV_A), F32), pltpu.VMEM((seq_len, DV_A), F32),
                        pltpu.VMEM((HGRN_CHUNK, DK_A), F32), pltpu.VMEM((HGRN_CHUNK, DK_A), F32)],
        compiler_params=_params(("arbitrary", "arbitrary")),
        name="hgrn2_scan",
    )(*args)
    return (res[0], None) if has_s0 else (res[0], res[1])


def _rope_tables(seq_len):
    rows = seq_len // GRID_W
    r_idx = jnp.repeat(jnp.arange(rows), GRID_W).astype(F32)
    c_idx = jnp.tile(jnp.arange(GRID_W), rows).astype(F32)
    quarter = DK_B // 4
    inv = 1.0 / (ROPE_BASE ** (jnp.arange(quarter, dtype=F32) / quarter))
    ang_r = r_idx[:, None] * inv[None, :]
    ang_c = c_idx[:, None] * inv[None, :]
    cos = jnp.concatenate([jnp.cos(ang_r)] * 2 + [jnp.cos(ang_c)] * 2, axis=1)
    sin = jnp.concatenate([-jnp.sin(ang_r), jnp.sin(ang_r), -jnp.sin(ang_c), jnp.sin(ang_c)],
                          axis=1)
    return cos, sin


def _rope(x, cos, sin):
    half = DK_B // 2
    swapped = jnp.concatenate([pltpu.roll(x[:, :half], half // 2, axis=1),
                               pltpu.roll(x[:, half:], half // 2, axis=1)], axis=1)
    return x * cos + swapped * sin


def _ret_kernel(*refs, seq_len, latent, has_s0, n_prev):
    it = iter(refs)
    lg_ref, q_ref, k_ref, v_ref = (next(it) for _ in range(4))
    cos_ref, sin_ref = (next(it), next(it)) if latent else (None, None)
    s0_ref = next(it) if has_s0 else None
    prev_ref = next(it) if n_prev else None
    o_ref = next(it)
    st_ref = None if has_s0 else next(it)
    dm_ref = next(it)
    h = pl.program_id(0)
    lg_f = lg_ref[h, 0]
    lg_b = lg_ref[h, 1]

    @pl.when(pl.program_id(1) == 0)
    def _():
        t = lax.broadcasted_iota(jnp.int32, (seq_len, seq_len), 0)
        s = lax.broadcasted_iota(jnp.int32, (seq_len, seq_len), 1)
        d = (t - s).astype(F32)
        dm_ref[...] = K_SCALE * (
            jnp.where(d >= 0.0, jnp.exp(lg_f * jnp.maximum(d, 0.0)), 0.0)
            + jnp.where(d <= 0.0, jnp.exp(lg_b * jnp.maximum(-d, 0.0)), 0.0))

    q, k = q_ref[...], k_ref[...]
    if latent:
        cos, sin = cos_ref[...], sin_ref[...]
        q, k = _rope(q.astype(F32), cos, sin), _rope(k.astype(F32), cos, sin)
    k_b = k.astype(BF16)
    v_b = v_ref[...]
    if has_s0:
        s0 = jnp.concatenate([s0_ref[0], s0_ref[1]], axis=0).astype(BF16)
    for r in range(seq_len // RET_ROWS):
        rows = slice(r * RET_ROWS, (r + 1) * RET_ROWS)
        q_r = q[rows]
        p = lax.dot_general(q_r.astype(BF16), k_b, _NT, preferred_element_type=F32)
        o = jnp.dot((p * dm_ref[rows, :]).astype(BF16), v_b, preferred_element_type=F32)
        if has_s0:
            pos = (lax.broadcasted_iota(jnp.int32, (RET_ROWS, 1), 0) + r * RET_ROWS).astype(F32)
            q_s = jnp.concatenate([q_r * jnp.exp(lg_f * (pos + 1.0)),
                                   q_r * jnp.exp(lg_b * (seq_len - pos))], axis=1)
            o = o + jnp.dot(q_s.astype(BF16), s0, preferred_element_type=F32)
        o_ref[rows, :] = (o * lax.rsqrt(jnp.mean(o * o, axis=-1, keepdims=True) + EPS)
                          ).astype(BF16)
    if st_ref is not None:
        if n_prev:
            st_ref[0:n_prev] = prev_ref[...]
        pos = lax.broadcasted_iota(jnp.int32, (seq_len, 1), 0).astype(F32)
        k_f = (k * (K_SCALE * jnp.exp(lg_f * (seq_len - 1.0 - pos)))).astype(BF16)
        k_r = (k * (K_SCALE * jnp.exp(lg_b * pos))).astype(BF16)
        st_ref[n_prev, 0] = lax.dot_general(k_f, v_b, _TN, preferred_element_type=F32)
        st_ref[n_prev, 1] = lax.dot_general(k_r, v_b, _TN, preferred_element_type=F32)


def _ret_log_decay():
    heads = jnp.arange(H_B, dtype=F32)
    lg_f = jnp.log1p(-jnp.exp2(-5.0 - heads))
    lg_b = jnp.log1p(-jnp.exp2(-(5.0 + RET_DECAY_OFFSET_BWD) - heads))
    return jnp.stack([lg_f, lg_b], axis=1)


def _ret_call(proj, state, prev, layer, latent):
    n_b, seq_len, _ = proj.shape
    has_s0 = state is not None
    in_specs = [
        pl.BlockSpec(memory_space=pltpu.SMEM),
        pl.BlockSpec((None, seq_len, DK_B), lambda h, b: (b, 0, OFF_QB // DK_B + h)),
        pl.BlockSpec((None, seq_len, DK_B), lambda h, b: (b, 0, OFF_KB // DK_B + h)),
        pl.BlockSpec((None, seq_len, DV_B), lambda h, b: (b, 0, OFF_VB // DV_B + h)),
    ]
    args = [_ret_log_decay(), proj, proj, proj]
    if latent:
        cos, sin = _rope_tables(seq_len)
        in_specs += [pl.BlockSpec((seq_len, DK_B), lambda h, b: (0, 0))] * 2
        args += [cos, sin]
    out_specs = [pl.BlockSpec((None, seq_len, DV_B), lambda h, b: (b, 0, h))]
    out_shape = [jax.ShapeDtypeStruct((n_b, seq_len, H_B * DV_B), BF16)]
    if has_s0:
        in_specs.append(pl.BlockSpec((None, None, 2, None, DK_B, DV_B),
                                     lambda h, b: (b, layer, 0, h, 0, 0)))
        args.append(state)
    else:
        st_spec = lambda n: pl.BlockSpec((None, n, 2, None, DK_B, DV_B),
                                         lambda h, b: (b, 0, 0, h, 0, 0))
        if layer:
            in_specs.append(st_spec(layer))
            args.append(prev)
        out_specs.append(st_spec(layer + 1))
        out_shape.append(jax.ShapeDtypeStruct((n_b, layer + 1, 2, H_B, DK_B, DV_B), F32))
    res = pl.pallas_call(
        functools.partial(_ret_kernel, seq_len=seq_len, latent=latent, has_s0=has_s0,
                          n_prev=0 if has_s0 else layer),
        grid=(H_B, n_b),
        in_specs=in_specs, out_specs=out_specs, out_shape=out_shape,
        scratch_shapes=[pltpu.VMEM((seq_len, seq_len), F32)],
        compiler_params=_params(("arbitrary", "arbitrary")),
        name="retention_scan",
    )(*args)
    return (res[0], None) if has_s0 else (res[0], res[1])


def _post_kernel(oa_ref, ga_ref, ob_ref, gb0_ref, gb1_ref, gta_ref, gtb_ref, x_ref, gt1_ref,
                 pa_ref, pb_ref, wo_ref, o_ref):
    a = (oa_ref[...].astype(F32) * _silu(ga_ref[...].astype(F32))).astype(BF16)
    g_b = jnp.concatenate([gb0_ref[...], gb1_ref[...]], axis=1).astype(F32)
    b = (ob_ref[...].astype(F32) * _silu(g_b)).astype(BF16)
    y_a = jnp.dot(a, pa_ref[...], preferred_element_type=F32)
    y_b = jnp.dot(b, pb_ref[...], preferred_element_type=F32)
    merged = (_sigmoid(gta_ref[...].astype(F32)) * y_a
              + _sigmoid(gtb_ref[...].astype(F32)) * y_b)
    y = jnp.dot(merged.astype(BF16), wo_ref[...], preferred_element_type=F32)
    o_ref[...] = x_ref[...] + gt1_ref[...] * y


def _post_call(o_a, o_b, proj, x, mods, p_a, p_b, w_out, layer, row_fn):
    n_tok = x.shape[0]
    tile = lambda w: (lambda i, j: (i, w))
    pcol = lambda off: pl.BlockSpec((POST_TILE, 1024), tile(off // 1024))
    const = lambda a: _layer_spec(layer, a.shape[1:], lambda i, j: (0, 0))
    return pl.pallas_call(
        _post_kernel,
        grid=(n_tok // POST_TILE, 1),
        in_specs=[
            pl.BlockSpec((POST_TILE, H_A * DV_A), tile(0)),
            pcol(OFF_GA),
            pl.BlockSpec((POST_TILE, H_B * DV_B), tile(0)),
            pcol(OFF_GB), pcol(OFF_GB + 1024), pcol(OFF_GATE_A), pcol(OFF_GATE_B),
            pl.BlockSpec((POST_TILE, D_MODEL), tile(0)),
            _mod_spec(layer, row_fn, 2, POST_TILE),
            const(p_a), const(p_b), const(w_out),
        ],
        out_specs=pl.BlockSpec((POST_TILE, D_MODEL), tile(0)),
        out_shape=jax.ShapeDtypeStruct((n_tok, D_MODEL), F32),
        compiler_params=_params(("arbitrary", "arbitrary")),
        name="mixer_output",
    )(o_a, proj, o_b, proj, proj, proj, proj, x, mods, p_a, p_b, w_out)


def _conv3(u, wc, bc, seq_len):
    n = u.shape[0]
    row = lax.broadcasted_iota(jnp.int32, (SUBLANES, 1), 0)

    def zero_row(x, r0, r):
        return jnp.where(row == r, 0.0, x[r0:r0 + SUBLANES])

    prev, nxt = pltpu.roll(u, 1, axis=0), pltpu.roll(u, n - 1, axis=0)
    p_parts, n_parts = [], []
    for s0 in range(0, n, seq_len):
        s1 = s0 + seq_len
        p_parts += [zero_row(prev, s0, 0), prev[s0 + SUBLANES:s1]]
        n_parts += [nxt[s0:s1 - SUBLANES], zero_row(nxt, s1 - SUBLANES, SUBLANES - 1)]
    prev, nxt = jnp.concatenate(p_parts, axis=0), jnp.concatenate(n_parts, axis=0)
    return prev * wc[0:1] + u * wc[1:2] + nxt * wc[2:3] + bc


def _ffn_kernel(*refs, seq_len, final):
    it = iter(refs)
    (x_ref, sh_ref, sc_ref, gt_ref, n2_ref, wa_ref, wg_ref, wca_ref, wcg_ref, bca_ref, bcg_ref,
     wd_ref) = (next(it) for _ in range(12))
    fn_ref = next(it) if final else None
    o_ref, h_ref, acc_ref = next(it), next(it), next(it)
    j = pl.program_id(1)

    @pl.when(j == 0)
    def _():
        h_ref[...] = _norm_mod(x_ref[...], n2_ref[...], sc_ref[...], sh_ref[...]).astype(BF16)
        acc_ref[...] = jnp.zeros_like(acc_ref)

    h = h_ref[...]
    u_a = jnp.dot(h, wa_ref[...], preferred_element_type=F32)
    u_g = jnp.dot(h, wg_ref[...], preferred_element_type=F32)
    a = _conv3(u_a, wca_ref[...], bca_ref[...], seq_len)
    g = _conv3(u_g, wcg_ref[...], bcg_ref[...], seq_len)
    acc_ref[...] += jnp.dot((_silu(g) * a).astype(BF16), wd_ref[...], preferred_element_type=F32)

    @pl.when(j == pl.num_programs(1) - 1)
    def _():
        x = x_ref[...] + gt_ref[...] * acc_ref[...]
        if final:
            x = x * lax.rsqrt(jnp.mean(x * x, axis=-1, keepdims=True) + EPS) * fn_ref[...]
        o_ref[...] = x


def _ffn_call(x, mods, n2, w_up, w_conv, b_conv, w_down, final_norm, layer, row_fn, seq_len):
    n_tok = x.shape[0]
    n_ff = D_FF // FF_CHUNK
    lspec = functools.partial(_layer_spec, layer)
    in_specs = [
        pl.BlockSpec((TOKEN_TILE, D_MODEL), lambda i, j: (i, 0)),
        _mod_spec(layer, row_fn, 3), _mod_spec(layer, row_fn, 4), _mod_spec(layer, row_fn, 5),
        lspec((1, D_MODEL), lambda i, j: (0, 0)),
        lspec((D_MODEL, FF_CHUNK), lambda i, j: (0, j)),
        lspec((D_MODEL, FF_CHUNK), lambda i, j: (0, n_ff + j)),
        lspec((3, FF_CHUNK), lambda i, j: (0, j)),
        lspec((3, FF_CHUNK), lambda i, j: (0, n_ff + j)),
        lspec((1, FF_CHUNK), lambda i, j: (0, j)),
        lspec((1, FF_CHUNK), lambda i, j: (0, n_ff + j)),
        lspec((FF_CHUNK, D_MODEL), lambda i, j: (j, 0)),
    ]
    args = [x, mods, mods, mods, n2, w_up, w_up, w_conv, w_conv, b_conv, b_conv, w_down]
    final = final_norm is not None
    if final:
        in_specs.append(pl.BlockSpec((1, D_MODEL), lambda i, j: (0, 0)))
        args.append(final_norm.reshape(1, D_MODEL))
    return pl.pallas_call(
        functools.partial(_ffn_kernel, seq_len=seq_len, final=final),
        grid=(n_tok // TOKEN_TILE, n_ff),
        in_specs=in_specs,
        out_specs=pl.BlockSpec((TOKEN_TILE, D_MODEL), lambda i, j: (i, 0)),
        out_shape=jax.ShapeDtypeStruct((n_tok, D_MODEL), F32),
        scratch_shapes=[pltpu.VMEM((TOKEN_TILE, D_MODEL), BF16),
                        pltpu.VMEM((TOKEN_TILE, D_MODEL), F32)],
        compiler_params=_params(("arbitrary", "arbitrary")),
        name="conv_ffn",
    )(*args)


def kernel(x_prompt, x_sample, state_hgrn, state_ret, c, c_ctx, norm1, norm2, final_norm,
           w_mod, b_mod, w_in, hgrn_lb_raw, p_a, p_b, w_out, w_up, w_conv, b_conv, w_down):
    n_ctx, t_ctx, _ = x_prompt.shape
    n_dec, t_dec, _ = x_sample.shape
    assert t_ctx & (t_ctx - 1) == 0 and TOKEN_TILE % t_ctx == 0 and t_dec == TOKEN_TILE

    sm = jax.nn.softmax(hgrn_lb_raw.astype(F32), axis=0)
    cum = jnp.cumsum(sm, axis=0)
    lower_bounds = cum - cum[0:1]

    cvec = jnp.concatenate(
        [c_ctx[None, :], c, jnp.zeros((MOD_ROWS - 1 - n_dec, D_MODEL), F32)], axis=0)
    mods = _mod_call(cvec, w_mod, b_mod).reshape(DEPTH, MOD_ROWS, 1, 6 * D_MODEL)

    w_in_b, p_a_b, p_b_b, w_out_b, w_up_b, w_down_b = (
        w.astype(BF16) for w in (w_in, p_a, p_b, w_out, w_up, w_down))
    norm1_3, norm2_3 = norm1.reshape(DEPTH, 1, D_MODEL), norm2.reshape(DEPTH, 1, D_MODEL)
    b_conv_3 = b_conv.reshape(DEPTH, 1, 2 * D_FF)

    ctx_row = lambda tok: 0
    dec_row = lambda tok: tok // t_dec + 1

    def layer(x, l, n_b, seq_len, row_fn, s_hgrn, s_ret, prev_h, prev_r, latent):
        proj, z = _proj_call(x, mods, norm1_3, w_in_b, l, row_fn)
        proj3 = proj.reshape(n_b, seq_len, IN_WIDTH)
        o_a, st_h = _hgrn_call(proj3, z.reshape(n_b, seq_len, -1), lower_bounds, s_hgrn, prev_h, l)
        o_b, st_r = _ret_call(proj3, s_ret, prev_r, l, latent)
        x = _post_call(o_a.reshape(-1, H_A * DV_A), o_b.reshape(-1, H_B * DV_B), proj, x,
                       mods, p_a_b, p_b_b, w_out_b, l, row_fn)
        x = _ffn_call(x, mods, norm2_3, w_up_b, w_conv, b_conv_3, w_down_b,
                      final_norm if l == DEPTH - 1 else None, l, row_fn, seq_len)
        return x, st_h, st_r

    x = x_prompt.reshape(n_ctx * t_ctx, D_MODEL)
    st_h = st_r = None
    for l in range(DEPTH):
        x, st_h, st_r = layer(x, l, n_ctx, t_ctx, ctx_row, None, None, st_h, st_r, False)
    y_prompt = x.reshape(n_ctx, t_ctx, D_MODEL)

    x = x_sample.reshape(n_dec * t_dec, D_MODEL)
    for l in range(DEPTH):
        x, _, _ = layer(x, l, n_dec, t_dec, dec_row, state_hgrn, state_ret, None, None, True)
    y_sample = x.reshape(n_dec, t_dec, D_MODEL)
    return (y_prompt, y_sample, st_h, st_r)
```

```python
import functools

import numpy as np
import jax
import jax.numpy as jnp
from jax import lax
from jax.experimental import pallas as pl
from jax.experimental.pallas import tpu as pltpu

F32 = jnp.float32
BF16 = jnp.bfloat16

D_MODEL = 1024
DEPTH = 2
GRID_W = 64
H_A, DK_A, DV_A = 8, 128, 128
H_B, DK_B, DV_B = 4, 256, 512
D_FF = 2816
ROPE_BASE = 10000.0
K_SCALE = DK_B ** -0.5
EPS = 1e-6
RET_DECAY_OFFSET_BWD = 0.5
IN_WIDTH = 13312

OFF_QA, OFF_ZF, OFF_ZB, OFF_IA, OFF_GA = 0, 1024, 2048, 3072, 4096
OFF_QB, OFF_KB, OFF_VB, OFF_GB = 5120, 6144, 7168, 9216
OFF_GATE_A, OFF_GATE_B = 11264, 12288

MOD_ROWS = 8
TOKEN_TILE = 1024
POST_TILE = 256
FF_CHUNK = 256
HGRN_CHUNK = 128
LEAF_LEVEL = 5
LEAF_MAX_DECAY = 60.0
RET_ROWS = 256
PROJ_TN = 1024
Z_TILE0 = OFF_ZF // PROJ_TN
Z_TILES = (OFF_IA - OFF_ZF) // PROJ_TN
VMEM_LIMIT = 52 * 1024 * 1024
SUBLANES = 8
LOG2E = 1.4426950408889634

_NT = (((1,), (1,)), ((), ()))
_TN = (((0,), (0,)), ((), ()))


def _params(sem):
    return pltpu.CompilerParams(dimension_semantics=sem, vmem_limit_bytes=VMEM_LIMIT)


def _sigmoid(x):
    return jax.nn.sigmoid(x)


def _silu(x):
    return x * jax.nn.sigmoid(x)


def _mod_kernel(c_ref, w_ref, b_ref, o_ref):
    s = _silu(c_ref[...])
    o_ref[...] = jnp.dot(s, w_ref[...], precision=lax.Precision.HIGHEST,
                         preferred_element_type=F32) + b_ref[...]


def _mod_call(cvec, w_mod, b_mod):
    n_col = 6 * D_MODEL // 1024
    return pl.pallas_call(
        _mod_kernel,
        grid=(DEPTH, n_col),
        in_specs=[
            pl.BlockSpec((MOD_ROWS, D_MODEL), lambda l, j: (0, 0)),
            pl.BlockSpec((None, D_MODEL, 1024), lambda l, j: (l, 0, j)),
            pl.BlockSpec((None, 1, 1024), lambda l, j: (l, 0, j)),
        ],
        out_specs=pl.BlockSpec((None, MOD_ROWS, 1024), lambda l, j: (l, 0, j)),
        out_shape=jax.ShapeDtypeStruct((DEPTH, MOD_ROWS, 6 * D_MODEL), F32),
        compiler_params=_params(("arbitrary", "arbitrary")),
        name="modulation",
    )(cvec, w_mod, b_mod.reshape(DEPTH, 1, 6 * D_MODEL))


def _mod_spec(layer, row_fn, which, tile=TOKEN_TILE):
    return pl.BlockSpec((None, None, 1, D_MODEL),
                        lambda i, j: (layer, row_fn(i * tile), 0, which))


def _layer_spec(layer, block, index_map):
    return pl.BlockSpec((None,) + block, lambda *g: (layer,) + index_map(*g))


def _norm_mod(x, g, sc, sh):
    y = x * lax.rsqrt(jnp.mean(x * x, axis=-1, keepdims=True) + EPS) * g
    return y * (1.0 + sc) + sh


def _proj_kernel(x_ref, sh_ref, sc_ref, n_ref, w_ref, o_ref, z_ref, h_ref):
    j = pl.program_id(1)

    @pl.when(j == 0)
    def _():
        h_ref[...] = _norm_mod(x_ref[...], n_ref[...], sc_ref[...], sh_ref[...]).astype(BF16)

    acc = jnp.dot(h_ref[...], w_ref[...], preferred_element_type=F32)
    o_ref[...] = acc.astype(BF16)

    @pl.when((j >= Z_TILE0) & (j < Z_TILE0 + Z_TILES))
    def _():
        z_ref[...] = acc


def _proj_call(x, mods, n1, w_in, layer, row_fn):
    n_tok = x.shape[0]
    tn = PROJ_TN
    return pl.pallas_call(
        _proj_kernel,
        grid=(n_tok // TOKEN_TILE, IN_WIDTH // tn),
        in_specs=[
            pl.BlockSpec((TOKEN_TILE, D_MODEL), lambda i, j: (i, 0)),
            _mod_spec(layer, row_fn, 0),
            _mod_spec(layer, row_fn, 1),
            _layer_spec(layer, (1, D_MODEL), lambda i, j: (0, 0)),
            _layer_spec(layer, (D_MODEL, tn), lambda i, j: (0, j)),
        ],
        out_specs=[
            pl.BlockSpec((TOKEN_TILE, tn), lambda i, j: (i, j)),
            pl.BlockSpec((TOKEN_TILE, tn),
                         lambda i, j: (i, jnp.clip(j - Z_TILE0, 0, Z_TILES - 1))),
        ],
        out_shape=[jax.ShapeDtypeStruct((n_tok, IN_WIDTH), BF16),
                   jax.ShapeDtypeStruct((n_tok, Z_TILES * tn), F32)],
        scratch_shapes=[pltpu.VMEM((TOKEN_TILE, D_MODEL), BF16)],
        compiler_params=_params(("arbitrary", "arbitrary")),
        name="in_projection",
    )(x, mods, mods, n1, w_in)


def _hgrn_consts(c):
    nl = int(np.log2(c))
    t = np.arange(c)
    tt, rr = t[:, None], t[None, :]
    x = tt ^ rr
    lev_of = np.where(x > 0, np.floor(np.log2(np.maximum(x, 1))) + 1, 0).astype(np.int32)
    lv_f = np.where(tt >= rr, lev_of, -1).astype(np.int32)
    lv_b = lv_f.T.copy()
    return ((rr <= tt).astype(np.float32), (rr >= tt).astype(np.float32), lv_f, lv_b, nl)


def _leaf_factors(cum, cum_row, reverse):
    c, dk = cum.shape
    leaf = 1 << LEAF_LEVEL
    pieces = []
    for base in range(0, c, leaf):
        row = base + leaf if reverse else base - 1
        if 0 <= row < c:
            pieces.append(jnp.broadcast_to(cum_row(row), (leaf, dk)))
        else:
            pieces.append(jnp.zeros((leaf, dk), F32))
    x = cum - jnp.concatenate(pieces, axis=0)
    return jnp.exp2(x * LOG2E).astype(BF16), jnp.exp2(x * (-LOG2E)).astype(BF16)


def _level_factor(la, cum, cum_row, lev, reverse):
    c, dk = la.shape
    if lev == 1:
        row = lax.broadcasted_iota(jnp.int32, (c, 1), 0)
        return jnp.exp(jnp.where((row & 1) == (0 if reverse else 1), la, 0.0)).astype(BF16)
    blk, half = 1 << lev, 1 << (lev - 1)

    def boundary(r):
        base = (r // blk) * blk
        return jnp.broadcast_to(cum_row(base + half if reverse else base + half - 1),
                                (SUBLANES, dk))

    top = lax.broadcasted_iota(jnp.int32, (SUBLANES, dk), 0) < SUBLANES // 2
    pieces = []
    for r0 in range(0, c, SUBLANES):
        if blk >= SUBLANES:
            pieces.append(boundary(r0))
        else:
            pieces.append(jnp.where(top, boundary(r0), boundary(r0 + SUBLANES // 2)))
    return jnp.exp2(jnp.abs(cum - jnp.concatenate(pieces, axis=0)) * (-LOG2E)).astype(BF16)


def _hgrn_forget(z, lb):
    e = jnp.exp(-jnp.abs(z))
    r = 1.0 / (1.0 + e)
    pos = z >= 0.0
    sig = jnp.where(pos, 1.0, e) * r
    sig_neg = jnp.where(pos, e, 1.0) * r
    oml = 1.0 - lb
    log_sig = jnp.minimum(z, 0.0) - jnp.log1p(e)
    log_f = jnp.where(lb > 0.0, jnp.log(lb + oml * sig), log_sig)
    return log_f, oml * sig_neg


def _hgrn_kernel(*refs, seq_len, chunk, n_lev, has_s0, n_prev):
    it = iter(refs)
    q_ref, zf_ref, zb_ref, i_ref, lb_ref, trif_ref, trib_ref, lvf_ref, lvb_ref, leaf_ref = (
        next(it) for _ in range(10))
    s0_ref = next(it) if has_s0 else None
    prev_ref = next(it) if n_prev else None
    o_ref = next(it)
    st_ref = None if has_s0 else next(it)
    la_ref, k_ref, cum_ref, qt_ref, kt_ref, qd_ref, kd_ref, ae_ref, s_ref, acc_ref = (
        next(it) for _ in range(10))
    n_chunks = seq_len // chunk
    dk = DK_A
    dirs = ((0, False, zf_ref, trif_ref, lvf_ref), (1, True, zb_ref, trib_ref, lvb_ref))
    rows_of = lambda c: slice(c * chunk, (c + 1) * chunk)
    lb = lb_ref[...]

    worst = None
    for d, _, z_ref, _, _ in dirs:
        la, key = _hgrn_forget(z_ref[...], lb[d:d + 1])
        la_ref[d] = la
        k_ref[d] = key.astype(BF16)
        leaf_sum = jnp.dot(leaf_ref[...], la.astype(BF16), preferred_element_type=F32)
        worst = leaf_sum if worst is None else jnp.minimum(worst, leaf_sum)
    leafwise_ok = jnp.min(worst) > -LEAF_MAX_DECAY

    for d, _, _, tri_ref, _ in dirs:
        tri = tri_ref[...]
        for c in range(n_chunks):
            la = la_ref[d, rows_of(c), :]
            la_hi = la.astype(BF16)
            la_lo = (la - la_hi.astype(F32)).astype(BF16)
            x2 = jnp.dot(tri, jnp.concatenate([la_hi, la_lo], axis=1),
                         preferred_element_type=F32)
            cum_ref[d, rows_of(c), :] = x2[:, :dk] + x2[:, dk:]

    def operands_and_scores(leafwise):
        if leafwise:
            levels = list(range(LEAF_LEVEL + 1, n_lev + 1))
        else:
            levels = list(range(1, n_lev + 1))
        n_terms = len(levels) + 1
        for d, reverse, _, _, _ in dirs:
            for c in range(n_chunks):
                rows = rows_of(c)
                cum = cum_ref[d, rows, :]
                la = la_ref[d, rows, :]
                q = q_ref[rows, :]
                k = k_ref[d, rows, :]
                cum_row = lambda r, d=d, c=c: cum_ref[d, c * chunk + r:c * chunk + r + 1, :]
                if leafwise:
                    factors = [_leaf_factors(cum, cum_row, reverse)]
                else:
                    factors = [None]
                for lev in levels:
                    e = _level_factor(la, cum, cum_row, lev, reverse)
                    factors.append((e, e))
                for t, f in enumerate(factors):
                    cols = slice(t * dk, (t + 1) * dk)
                    qt_ref[d, rows, cols] = q if f is None else q * f[0]
                    kt_ref[d, rows, cols] = k if f is None else k * f[1]
                end_row = 0 if reverse else chunk - 1
                cum_end = cum[end_row:end_row + 1]
                qd_ref[d, rows, :] = q * jnp.exp(cum).astype(BF16)
                kd_ref[d, rows, :] = k * jnp.exp(cum_end - cum).astype(BF16)
                ae_ref[d, c:c + 1, :] = jnp.exp(cum_end)
        zero = jnp.zeros((chunk, dk), BF16)
        for d, _, _, _, lv_ref in dirs:
            lv = lv_ref[...]
            if leafwise:
                masks = [(lv >= 0) & (lv <= LEAF_LEVEL)]
            else:
                masks = [lv == 0]
            masks += [lv == lev for lev in levels]
            for c in range(n_chunks):
                rows = rows_of(c)
                scores = jnp.zeros((chunk, chunk), F32)
                for t in range(0, n_terms, 2):
                    if t + 1 < n_terms:
                        k_a = kt_ref[d, rows, t * dk:(t + 1) * dk]
                        k_b = kt_ref[d, rows, (t + 1) * dk:(t + 2) * dk]
                        p = lax.dot_general(
                            qt_ref[d, rows, t * dk:(t + 2) * dk],
                            jnp.concatenate([jnp.concatenate([k_a, zero], axis=1),
                                             jnp.concatenate([zero, k_b], axis=1)], axis=0),
                            _NT, preferred_element_type=F32)
                        scores = jnp.where(masks[t], p[:, :chunk], scores)
                        scores = jnp.where(masks[t + 1], p[:, chunk:], scores)
                    else:
                        p = lax.dot_general(qt_ref[d, rows, t * dk:(t + 1) * dk],
                                            kt_ref[d, rows, t * dk:(t + 1) * dk],
                                            _NT, preferred_element_type=F32)
                        scores = jnp.where(masks[t], p, scores)
                s_ref[d, rows, :] = scores.astype(BF16)

    pl.when(leafwise_ok)(lambda: operands_and_scores(True))
    pl.when(jnp.logical_not(leafwise_ok))(lambda: operands_and_scores(False))

    finals = []
    for d, reverse, _, _, _ in dirs:
        incs = [lax.dot_general(i_ref[rows_of(c), :], kd_ref[d, rows_of(c), :], _TN,
                                preferred_element_type=F32) for c in range(n_chunks)]
        st = s0_ref[d].T if has_s0 else jnp.zeros((DV_A, dk), F32)
        before = [None] * n_chunks
        for c in (range(n_chunks - 1, -1, -1) if reverse else range(n_chunks)):
            before[c] = st.astype(BF16)
            st = st * ae_ref[d, c:c + 1, :] + incs[c]
        finals.append(st)
        for c in range(n_chunks):
            rows = rows_of(c)
            o = jnp.dot(s_ref[d, rows, :], i_ref[rows, :], preferred_element_type=F32)
            if has_s0 or c != (n_chunks - 1 if reverse else 0):
                o = o + lax.dot_general(qd_ref[d, rows, :], before[c], _NT,
                                        preferred_element_type=F32)
            if d == 0:
                acc_ref[rows, :] = o
            else:
                acc_ref[rows, :] += o

    o = acc_ref[...]
    o_ref[...] = (o * lax.rsqrt(jnp.mean(o * o, axis=-1, keepdims=True) + EPS)).astype(BF16)
    if st_ref is not None:
        if n_prev:
            st_ref[0:n_prev] = prev_ref[...]
        st_ref[n_prev, 0] = finals[0].T
        st_ref[n_prev, 1] = finals[1].T


def _hgrn_call(proj, z, lb, state, prev, layer):
    n_b, seq_len, _ = proj.shape
    tri_f, tri_b, lv_f, lv_b, n_lev = _hgrn_consts(HGRN_CHUNK)
    has_s0 = state is not None
    col = lambda off: (lambda b, h: (b, 0, off // DK_A + h))
    const = lambda a: pl.BlockSpec(a.shape, lambda b, h: (0, 0))
    in_specs = [pl.BlockSpec((None, seq_len, DK_A), col(o))
                for o in (OFF_QA, 0, OFF_ZB - OFF_ZF, OFF_IA)]
    leaf = 1 << LEAF_LEVEL
    leaf_rows = -(-(seq_len // leaf) // SUBLANES) * SUBLANES
    leaf_ind = (np.arange(seq_len)[None, :] // leaf == np.arange(leaf_rows)[:, None])
    in_specs += [_layer_spec(layer, (2, DK_A), lambda b, h: (0, h)),
                 const(tri_f), const(tri_b), const(lv_f), const(lv_b), const(leaf_ind)]
    args = [proj, z, z, proj, lb, jnp.asarray(tri_f, BF16), jnp.asarray(tri_b, BF16),
            jnp.asarray(lv_f), jnp.asarray(lv_b), jnp.asarray(leaf_ind, BF16)]
    out_specs = [pl.BlockSpec((None, seq_len, DV_A), lambda b, h: (b, 0, h))]
    out_shape = [jax.ShapeDtypeStruct((n_b, seq_len, H_A * DV_A), BF16)]
    if has_s0:
        in_specs.append(pl.BlockSpec((None, None, 2, None, DK_A, DV_A),
                                     lambda b, h: (b, layer, 0, h, 0, 0)))
        args.append(state)
    else:
        st_spec = lambda n: pl.BlockSpec((None, n, 2, None, DK_A, DV_A),
                                         lambda b, h: (b, 0, 0, h, 0, 0))
        if layer:
            in_specs.append(st_spec(layer))
            args.append(prev)
        out_specs.append(st_spec(layer + 1))
        out_shape.append(jax.ShapeDtypeStruct((n_b, layer + 1, 2, H_A, DK_A, DV_A), F32))
    res = pl.pallas_call(
        functools.partial(_hgrn_kernel, seq_len=seq_len, chunk=HGRN_CHUNK, n_lev=n_lev,
                          has_s0=has_s0, n_prev=0 if has_s0 else layer),
        grid=(n_b, H_A),
        in_specs=in_specs, out_specs=out_specs, out_shape=out_shape,
        scratch_shapes=[
            pltpu.VMEM((2, seq_len, DK_A), F32),
            pltpu.VMEM((2, seq_len, DK_A), BF16),
            pltpu.VMEM((2, seq_len, DK_A), F32),
            pltpu.VMEM((2, seq_len, (n_lev + 1) * DK_A), BF16),
            pltpu.VMEM((2, seq_len, (n_lev + 1) * DK_A), BF16),
            pltpu.VMEM((2, seq_len, DK_A), BF16),
            pltpu.VMEM((2, seq_len, DK_A), BF16),
            pltpu.VMEM((2, max(SUBLANES, seq_len // HGRN_CHUNK), DK_A), F32),
            pltpu.VMEM((2, seq_len, HGRN_CHUNK), BF16),
            pltpu.VMEM((seq_len, DV_A), F32),
        ],
        compiler_params=_params(("arbitrary", "arbitrary")),
        name="hgrn2_scan",
    )(*args)
    return (res[0], None) if has_s0 else (res[0], res[1])


def _rope_tables(seq_len):
    rows = seq_len // GRID_W
    r_idx = jnp.repeat(jnp.arange(rows), GRID_W).astype(F32)
    c_idx = jnp.tile(jnp.arange(GRID_W), rows).astype(F32)
    quarter = DK_B // 4
    inv = 1.0 / (ROPE_BASE ** (jnp.arange(quarter, dtype=F32) / quarter))
    ang_r = r_idx[:, None] * inv[None, :]
    ang_c = c_idx[:, None] * inv[None, :]
    cos = jnp.concatenate([jnp.cos(ang_r)] * 2 + [jnp.cos(ang_c)] * 2, axis=1)
    sin = jnp.concatenate([-jnp.sin(ang_r), jnp.sin(ang_r), -jnp.sin(ang_c), jnp.sin(ang_c)],
                          axis=1)
    return cos, sin


def _rope(x, cos, sin):
    half = DK_B // 2
    swapped = jnp.concatenate([pltpu.roll(x[:, :half], half // 2, axis=1),
                               pltpu.roll(x[:, half:], half // 2, axis=1)], axis=1)
    return x * cos + swapped * sin


def _ret_kernel(*refs, seq_len, latent, has_s0, n_prev):
    it = iter(refs)
    lg_ref, q_ref, k_ref, v_ref = (next(it) for _ in range(4))
    cos_ref, sin_ref = (next(it), next(it)) if latent else (None, None)
    s0_ref = next(it) if has_s0 else None
    prev_ref = next(it) if n_prev else None
    o_ref = next(it)
    st_ref = None if has_s0 else next(it)
    dm_ref = next(it)
    h = pl.program_id(0)
    lg_f = lg_ref[h, 0]
    lg_b = lg_ref[h, 1]

    @pl.when(pl.program_id(1) == 0)
    def _():
        t = lax.broadcasted_iota(jnp.int32, (seq_len, seq_len), 0)
        s = lax.broadcasted_iota(jnp.int32, (seq_len, seq_len), 1)
        d = (t - s).astype(F32)
        dm_ref[...] = K_SCALE * (
            jnp.where(d >= 0.0, jnp.exp(lg_f * jnp.maximum(d, 0.0)), 0.0)
            + jnp.where(d <= 0.0, jnp.exp(lg_b * jnp.maximum(-d, 0.0)), 0.0))

    q, k = q_ref[...], k_ref[...]
    if latent:
        cos, sin = cos_ref[...], sin_ref[...]
        q, k = _rope(q.astype(F32), cos, sin), _rope(k.astype(F32), cos, sin)
    k_b = k.astype(BF16)
    v_b = v_ref[...]
    if has_s0:
        s0 = jnp.concatenate([s0_ref[0], s0_ref[1]], axis=0).astype(BF16)
    for r in range(seq_len // RET_ROWS):
        rows = slice(r * RET_ROWS, (r + 1) * RET_ROWS)
        q_r = q[rows]
        p = lax.dot_general(q_r.astype(BF16), k_b, _NT, preferred_element_type=F32)
        o = jnp.dot((p * dm_ref[rows, :]).astype(BF16), v_b, preferred_element_type=F32)
        if has_s0:
            pos = (lax.broadcasted_iota(jnp.int32, (RET_ROWS, 1), 0) + r * RET_ROWS).astype(F32)
            q_s = jnp.concatenate([q_r * jnp.exp(lg_f * (pos + 1.0)),
                                   q_r * jnp.exp(lg_b * (seq_len - pos))], axis=1)
            o = o + jnp.dot(q_s.astype(BF16), s0, preferred_element_type=F32)
        o_ref[rows, :] = (o * lax.rsqrt(jnp.mean(o * o, axis=-1, keepdims=True) + EPS)
                          ).astype(BF16)
    if st_ref is not None:
        if n_prev:
            st_ref[0:n_prev] = prev_ref[...]
        pos = lax.broadcasted_iota(jnp.int32, (seq_len, 1), 0).astype(F32)
        k_f = (k * (K_SCALE * jnp.exp(lg_f * (seq_len - 1.0 - pos)))).astype(BF16)
        k_r = (k * (K_SCALE * jnp.exp(lg_b * pos))).astype(BF16)
        st_ref[n_prev, 0] = lax.dot_general(k_f, v_b, _TN, preferred_element_type=F32)
        st_ref[n_prev, 1] = lax.dot_general(k_r, v_b, _TN, preferred_element_type=F32)


def _ret_log_decay():
    heads = jnp.arange(H_B, dtype=F32)
    lg_f = jnp.log1p(-jnp.exp2(-5.0 - heads))
    lg_b = jnp.log1p(-jnp.exp2(-(5.0 + RET_DECAY_OFFSET_BWD) - heads))
    return jnp.stack([lg_f, lg_b], axis=1)


def _ret_call(proj, state, prev, layer, latent):
    n_b, seq_len, _ = proj.shape
    has_s0 = state is not None
    in_specs = [
        pl.BlockSpec(memory_space=pltpu.SMEM),
        pl.BlockSpec((None, seq_len, DK_B), lambda h, b: (b, 0, OFF_QB // DK_B + h)),
        pl.BlockSpec((None, seq_len, DK_B), lambda h, b: (b, 0, OFF_KB // DK_B + h)),
        pl.BlockSpec((None, seq_len, DV_B), lambda h, b: (b, 0, OFF_VB // DV_B + h)),
    ]
    args = [_ret_log_decay(), proj, proj, proj]
    if latent:
        cos, sin = _rope_tables(seq_len)
        in_specs += [pl.BlockSpec((seq_len, DK_B), lambda h, b: (0, 0))] * 2
        args += [cos, sin]
    out_specs = [pl.BlockSpec((None, seq_len, DV_B), lambda h, b: (b, 0, h))]
    out_shape = [jax.ShapeDtypeStruct((n_b, seq_len, H_B * DV_B), BF16)]
    if has_s0:
        in_specs.append(pl.BlockSpec((None, None, 2, None, DK_B, DV_B),
                                     lambda h, b: (b, layer, 0, h, 0, 0)))
        args.append(state)
    else:
        st_spec = lambda n: pl.BlockSpec((None, n, 2, None, DK_B, DV_B),
                                         lambda h, b: (b, 0, 0, h, 0, 0))
        if layer:
            in_specs.append(st_spec(layer))
            args.append(prev)
        out_specs.append(st_spec(layer + 1))
        out_shape.append(jax.ShapeDtypeStruct((n_b, layer + 1, 2, H_B, DK_B, DV_B), F32))
    res = pl.pallas_call(
        functools.partial(_ret_kernel, seq_len=seq_len, latent=latent, has_s0=has_s0,
                          n_prev=0 if has_s0 else layer),
        grid=(H_B, n_b),
        in_specs=in_specs, out_specs=out_specs, out_shape=out_shape,
        scratch_shapes=[pltpu.VMEM((seq_len, seq_len), F32)],
        compiler_params=_params(("arbitrary", "arbitrary")),
        name="retention_scan",
    )(*args)
    return (res[0], None) if has_s0 else (res[0], res[1])


def _post_kernel(oa_ref, ga_ref, ob_ref, gb0_ref, gb1_ref, gta_ref, gtb_ref, x_ref, gt1_ref,
                 pa_ref, pb_ref, wo_ref, o_ref):
    a = (oa_ref[...].astype(F32) * _silu(ga_ref[...].astype(F32))).astype(BF16)
    g_b = jnp.concatenate([gb0_ref[...], gb1_ref[...]], axis=1).astype(F32)
    b = (ob_ref[...].astype(F32) * _silu(g_b)).astype(BF16)
    y_a = jnp.dot(a, pa_ref[...], preferred_element_type=F32)
    y_b = jnp.dot(b, pb_ref[...], preferred_element_type=F32)
    merged = (_sigmoid(gta_ref[...].astype(F32)) * y_a
              + _sigmoid(gtb_ref[...].astype(F32)) * y_b)
    y = jnp.dot(merged.astype(BF16), wo_ref[...], preferred_element_type=F32)
    o_ref[...] = x_ref[...] + gt1_ref[...] * y


def _post_call(o_a, o_b, proj, x, mods, p_a, p_b, w_out, layer, row_fn):
    n_tok = x.shape[0]
    tile = lambda w: (lambda i, j: (i, w))
    pcol = lambda off: pl.BlockSpec((POST_TILE, 1024), tile(off // 1024))
    const = lambda a: _layer_spec(layer, a.shape[1:], lambda i, j: (0, 0))
    return pl.pallas_call(
        _post_kernel,
        grid=(n_tok // POST_TILE, 1),
        in_specs=[
            pl.BlockSpec((POST_TILE, H_A * DV_A), tile(0)),
            pcol(OFF_GA),
            pl.BlockSpec((POST_TILE, H_B * DV_B), tile(0)),
            pcol(OFF_GB), pcol(OFF_GB + 1024), pcol(OFF_GATE_A), pcol(OFF_GATE_B),
            pl.BlockSpec((POST_TILE, D_MODEL), tile(0)),
            _mod_spec(layer, row_fn, 2, POST_TILE),
            const(p_a), const(p_b), const(w_out),
        ],
        out_specs=pl.BlockSpec((POST_TILE, D_MODEL), tile(0)),
        out_shape=jax.ShapeDtypeStruct((n_tok, D_MODEL), F32),
        compiler_params=_params(("arbitrary", "arbitrary")),
        name="mixer_output",
    )(o_a, proj, o_b, proj, proj, proj, proj, x, mods, p_a, p_b, w_out)


def _conv3(u, wc, bc, seq_len):
    n = u.shape[0]
    row = lax.broadcasted_iota(jnp.int32, (SUBLANES, 1), 0)

    def zero_row(x, r0, r):
        return jnp.where(row == r, 0.0, x[r0:r0 + SUBLANES])

    prev, nxt = pltpu.roll(u, 1, axis=0), pltpu.roll(u, n - 1, axis=0)
    p_parts, n_parts = [], []
    for s0 in range(0, n, seq_len):
        s1 = s0 + seq_len
        p_parts += [zero_row(prev, s0, 0), prev[s0 + SUBLANES:s1]]
        n_parts += [nxt[s0:s1 - SUBLANES], zero_row(nxt, s1 - SUBLANES, SUBLANES - 1)]
    prev, nxt = jnp.concatenate(p_parts, axis=0), jnp.concatenate(n_parts, axis=0)
    return prev * wc[0:1] + u * wc[1:2] + nxt * wc[2:3] + bc


def _ffn_kernel(*refs, seq_len, final):
    it = iter(refs)
    (x_ref, sh_ref, sc_ref, gt_ref, n2_ref, wa_ref, wg_ref, wca_ref, wcg_ref, bca_ref, bcg_ref,
     wd_ref) = (next(it) for _ in range(12))
    fn_ref = next(it) if final else None
    o_ref, h_ref, acc_ref = next(it), next(it), next(it)
    j = pl.program_id(1)

    @pl.when(j == 0)
    def _():
        h_ref[...] = _norm_mod(x_ref[...], n2_ref[...], sc_ref[...], sh_ref[...]).astype(BF16)
        acc_ref[...] = jnp.zeros_like(acc_ref)

    h = h_ref[...]
    u_a = jnp.dot(h, wa_ref[...], preferred_element_type=F32)
    u_g = jnp.dot(h, wg_ref[...], preferred_element_type=F32)
    a = _conv3(u_a, wca_ref[...], bca_ref[...], seq_len)
    g = _conv3(u_g, wcg_ref[...], bcg_ref[...], seq_len)
    acc_ref[...] += jnp.dot((_silu(g) * a).astype(BF16), wd_ref[...], preferred_element_type=F32)

    @pl.when(j == pl.num_programs(1) - 1)
    def _():
        x = x_ref[...] + gt_ref[...] * acc_ref[...]
        if final:
            x = x * lax.rsqrt(jnp.mean(x * x, axis=-1, keepdims=True) + EPS) * fn_ref[...]
        o_ref[...] = x


def _ffn_call(x, mods, n2, w_up, w_conv, b_conv, w_down, final_norm, layer, row_fn, seq_len):
    n_tok = x.shape[0]
    n_ff = D_FF // FF_CHUNK
    lspec = functools.partial(_layer_spec, layer)
    in_specs = [
        pl.BlockSpec((TOKEN_TILE, D_MODEL), lambda i, j: (i, 0)),
        _mod_spec(layer, row_fn, 3), _mod_spec(layer, row_fn, 4), _mod_spec(layer, row_fn, 5),
        lspec((1, D_MODEL), lambda i, j: (0, 0)),
        lspec((D_MODEL, FF_CHUNK), lambda i, j: (0, j)),
        lspec((D_MODEL, FF_CHUNK), lambda i, j: (0, n_ff + j)),
        lspec((3, FF_CHUNK), lambda i, j: (0, j)),
        lspec((3, FF_CHUNK), lambda i, j: (0, n_ff + j)),
        lspec((1, FF_CHUNK), lambda i, j: (0, j)),
        lspec((1, FF_CHUNK), lambda i, j: (0, n_ff + j)),
        lspec((FF_CHUNK, D_MODEL), lambda i, j: (j, 0)),
    ]
    args = [x, mods, mods, mods, n2, w_up, w_up, w_conv, w_conv, b_conv, b_conv, w_down]
    final = final_norm is not None
    if final:
        in_specs.append(pl.BlockSpec((1, D_MODEL), lambda i, j: (0, 0)))
        args.append(final_norm.reshape(1, D_MODEL))
    return pl.pallas_call(
        functools.partial(_ffn_kernel, seq_len=seq_len, final=final),
        grid=(n_tok // TOKEN_TILE, n_ff),
        in_specs=in_specs,
        out_specs=pl.BlockSpec((TOKEN_TILE, D_MODEL), lambda i, j: (i, 0)),
        out_shape=jax.ShapeDtypeStruct((n_tok, D_MODEL), F32),
        scratch_shapes=[pltpu.VMEM((TOKEN_TILE, D_MODEL), BF16),
                        pltpu.VMEM((TOKEN_TILE, D_MODEL), F32)],
        compiler_params=_params(("arbitrary", "arbitrary")),
        name="conv_ffn",
    )(*args)


def kernel(x_prompt, x_sample, state_hgrn, state_ret, c, c_ctx, norm1, norm2, final_norm,
           w_mod, b_mod, w_in, hgrn_lb_raw, p_a, p_b, w_out, w_up, w_conv, b_conv, w_down):
    n_ctx, t_ctx, _ = x_prompt.shape
    n_dec, t_dec, _ = x_sample.shape
    assert t_ctx & (t_ctx - 1) == 0 and TOKEN_TILE % t_ctx == 0 and t_dec == TOKEN_TILE

    sm = jax.nn.softmax(hgrn_lb_raw.astype(F32), axis=0)
    cum = jnp.cumsum(sm, axis=0)
    lower_bounds = cum - cum[0:1]

    cvec = jnp.concatenate(
        [c_ctx[None, :], c, jnp.zeros((MOD_ROWS - 1 - n_dec, D_MODEL), F32)], axis=0)
    mods = _mod_call(cvec, w_mod, b_mod).reshape(DEPTH, MOD_ROWS, 1, 6 * D_MODEL)

    w_in_b, p_a_b, p_b_b, w_out_b, w_up_b, w_down_b = (
        w.astype(BF16) for w in (w_in, p_a, p_b, w_out, w_up, w_down))
    norm1_3, norm2_3 = norm1.reshape(DEPTH, 1, D_MODEL), norm2.reshape(DEPTH, 1, D_MODEL)
    b_conv_3 = b_conv.reshape(DEPTH, 1, 2 * D_FF)

    ctx_row = lambda tok: 0
    dec_row = lambda tok: tok // t_dec + 1

    def layer(x, l, n_b, seq_len, row_fn, s_hgrn, s_ret, prev_h, prev_r, latent):
        proj, z = _proj_call(x, mods, norm1_3, w_in_b, l, row_fn)
        proj3 = proj.reshape(n_b, seq_len, IN_WIDTH)
        o_a, st_h = _hgrn_call(proj3, z.reshape(n_b, seq_len, -1), lower_bounds, s_hgrn, prev_h, l)
        o_b, st_r = _ret_call(proj3, s_ret, prev_r, l, latent)
        x = _post_call(o_a.reshape(-1, H_A * DV_A), o_b.reshape(-1, H_B * DV_B), proj, x,
                       mods, p_a_b, p_b_b, w_out_b, l, row_fn)
        x = _ffn_call(x, mods, norm2_3, w_up_b, w_conv, b_conv_3, w_down_b,
                      final_norm if l == DEPTH - 1 else None, l, row_fn, seq_len)
        return x, st_h, st_r

    x = x_prompt.reshape(n_ctx * t_ctx, D_MODEL)
    st_h = st_r = None
    for l in range(DEPTH):
        x, st_h, st_r = layer(x, l, n_ctx, t_ctx, ctx_row, None, None, st_h, st_r, False)
    y_prompt = x.reshape(n_ctx, t_ctx, D_MODEL)

    x = x_sample.reshape(n_dec * t_dec, D_MODEL)
    for l in range(DEPTH):
        x, _, _ = layer(x, l, n_dec, t_dec, dec_row, state_hgrn, state_ret, None, None, True)
    y_sample = x.reshape(n_dec, t_dec, D_MODEL)
    return (y_prompt, y_sample, st_h, st_r)
```

```python
import functools

import numpy as np
import jax
import jax.numpy as jnp
from jax import lax
from jax.experimental import pallas as pl
from jax.experimental.pallas import tpu as pltpu

F32 = jnp.float32
BF16 = jnp.bfloat16

D_MODEL = 1024
DEPTH = 2
GRID_W = 64
H_A, DK_A, DV_A = 8, 128, 128
H_B, DK_B, DV_B = 4, 256, 512
D_FF = 2816
ROPE_BASE = 10000.0
K_SCALE = DK_B ** -0.5
EPS = 1e-6
RET_DECAY_OFFSET_BWD = 0.5
IN_WIDTH = 13312

OFF_QA, OFF_ZF, OFF_ZB, OFF_IA, OFF_GA = 0, 1024, 2048, 3072, 4096
OFF_QB, OFF_KB, OFF_VB, OFF_GB = 5120, 6144, 7168, 9216
OFF_GATE_A, OFF_GATE_B = 11264, 12288

MOD_ROWS = 8
TOKEN_TILE = 1024
POST_TILE = 512
POST_SUB = 256
FF_CHUNK = 256
HGRN_CHUNK = 128
LEAF_LEVEL = 5
LEAF_MAX_DECAY = 60.0
RET_ROWS = 256
CAST_ROWS = 128
PROJ_TN = 1024
Z_TILE0 = OFF_ZF // PROJ_TN
Z_TILES = (OFF_IA - OFF_ZF) // PROJ_TN
VMEM_LIMIT = 52 * 1024 * 1024
SUBLANES = 8
LOG2E = 1.4426950408889634

_NT = (((1,), (1,)), ((), ()))
_TN = (((0,), (0,)), ((), ()))


def _params(sem):
    return pltpu.CompilerParams(dimension_semantics=sem, vmem_limit_bytes=VMEM_LIMIT)


def _sigmoid(x):
    return jax.nn.sigmoid(x)


def _silu(x):
    return x * jax.nn.sigmoid(x)


def _cast_kernel(w_ref, o_ref):
    o_ref[...] = w_ref[...].astype(BF16)


def _to_bf16(w):
    depth, rows, cols = w.shape
    w2 = w.reshape(depth * rows, cols)
    out = pl.pallas_call(
        _cast_kernel,
        grid=(depth * rows // CAST_ROWS,),
        in_specs=[pl.BlockSpec((CAST_ROWS, cols), lambda i: (i, 0))],
        out_specs=pl.BlockSpec((CAST_ROWS, cols), lambda i: (i, 0)),
        out_shape=jax.ShapeDtypeStruct(w2.shape, BF16),
        compiler_params=_params(("arbitrary",)),
        name="weight_cast",
    )(w2)
    return out.reshape(w.shape)


def _mod_kernel(c_ref, w_ref, b_ref, o_ref):
    s = _silu(c_ref[...])
    o_ref[...] = jnp.dot(s, w_ref[...], precision=lax.Precision.HIGHEST,
                         preferred_element_type=F32) + b_ref[...]


def _mod_call(cvec, w_mod, b_mod):
    n_col = 6 * D_MODEL // 1024
    return pl.pallas_call(
        _mod_kernel,
        grid=(DEPTH, n_col),
        in_specs=[
            pl.BlockSpec((MOD_ROWS, D_MODEL), lambda l, j: (0, 0)),
            pl.BlockSpec((None, D_MODEL, 1024), lambda l, j: (l, 0, j)),
            pl.BlockSpec((None, 1, 1024), lambda l, j: (l, 0, j)),
        ],
        out_specs=pl.BlockSpec((None, MOD_ROWS, 1024), lambda l, j: (l, 0, j)),
        out_shape=jax.ShapeDtypeStruct((DEPTH, MOD_ROWS, 6 * D_MODEL), F32),
        compiler_params=_params(("arbitrary", "arbitrary")),
        name="modulation",
    )(cvec, w_mod, b_mod.reshape(DEPTH, 1, 6 * D_MODEL))


def _mod_spec(layer, row_fn, which, tile=TOKEN_TILE):
    return pl.BlockSpec((None, None, 1, D_MODEL),
                        lambda i, j: (layer, row_fn(i * tile), 0, which))


def _layer_spec(layer, block, index_map):
    return pl.BlockSpec((None,) + block, lambda *g: (layer,) + index_map(*g))


def _norm_mod(x, g, sc, sh):
    y = x * lax.rsqrt(jnp.mean(x * x, axis=-1, keepdims=True) + EPS) * g
    return y * (1.0 + sc) + sh


def _proj_kernel(x_ref, sh_ref, sc_ref, n_ref, w_ref, o_ref, z_ref, h_ref):
    j = pl.program_id(1)

    @pl.when(j == 0)
    def _():
        h_ref[...] = _norm_mod(x_ref[...], n_ref[...], sc_ref[...], sh_ref[...]).astype(BF16)

    acc = jnp.dot(h_ref[...], w_ref[...], preferred_element_type=F32)
    o_ref[...] = acc.astype(BF16)

    @pl.when((j >= Z_TILE0) & (j < Z_TILE0 + Z_TILES))
    def _():
        z_ref[...] = acc


def _proj_call(x, mods, n1, w_in, layer, row_fn):
    n_tok = x.shape[0]
    tn = PROJ_TN
    return pl.pallas_call(
        _proj_kernel,
        grid=(n_tok // TOKEN_TILE, IN_WIDTH // tn),
        in_specs=[
            pl.BlockSpec((TOKEN_TILE, D_MODEL), lambda i, j: (i, 0)),
            _mod_spec(layer, row_fn, 0),
            _mod_spec(layer, row_fn, 1),
            _layer_spec(layer, (1, D_MODEL), lambda i, j: (0, 0)),
            _layer_spec(layer, (D_MODEL, tn), lambda i, j: (0, j)),
        ],
        out_specs=[
            pl.BlockSpec((TOKEN_TILE, tn), lambda i, j: (i, j)),
            pl.BlockSpec((TOKEN_TILE, tn),
                         lambda i, j: (i, jnp.clip(j - Z_TILE0, 0, Z_TILES - 1))),
        ],
        out_shape=[jax.ShapeDtypeStruct((n_tok, IN_WIDTH), BF16),
                   jax.ShapeDtypeStruct((n_tok, Z_TILES * tn), F32)],
        scratch_shapes=[pltpu.VMEM((TOKEN_TILE, D_MODEL), BF16)],
        compiler_params=_params(("arbitrary", "arbitrary")),
        name="in_projection",
    )(x, mods, mods, n1, w_in)


def _hgrn_consts(c):
    nl = int(np.log2(c))
    t = np.arange(c)
    tt, rr = t[:, None], t[None, :]
    x = tt ^ rr
    lev_of = np.where(x > 0, np.floor(np.log2(np.maximum(x, 1))) + 1, 0).astype(np.int32)
    lv_f = np.where(tt >= rr, lev_of, -1).astype(np.int32)
    lv_b = lv_f.T.copy()
    return ((rr <= tt).astype(np.float32), (rr >= tt).astype(np.float32), lv_f, lv_b, nl)


def _leaf_factors(cum, cum_row, reverse):
    c, dk = cum.shape
    leaf = 1 << LEAF_LEVEL
    pieces = []
    for base in range(0, c, leaf):
        row = base + leaf if reverse else base - 1
        if 0 <= row < c:
            pieces.append(jnp.broadcast_to(cum_row(row), (leaf, dk)))
        else:
            pieces.append(jnp.zeros((leaf, dk), F32))
    x = cum - jnp.concatenate(pieces, axis=0)
    return jnp.exp2(x * LOG2E).astype(BF16), jnp.exp2(x * (-LOG2E)).astype(BF16)


def _level_factor(la, cum, cum_row, lev, reverse):
    c, dk = la.shape
    if lev == 1:
        row = lax.broadcasted_iota(jnp.int32, (c, 1), 0)
        return jnp.exp(jnp.where((row & 1) == (0 if reverse else 1), la, 0.0)).astype(BF16)
    blk, half = 1 << lev, 1 << (lev - 1)

    def boundary(r):
        base = (r // blk) * blk
        return jnp.broadcast_to(cum_row(base + half if reverse else base + half - 1),
                                (SUBLANES, dk))

    top = lax.broadcasted_iota(jnp.int32, (SUBLANES, dk), 0) < SUBLANES // 2
    pieces = []
    for r0 in range(0, c, SUBLANES):
        if blk >= SUBLANES:
            pieces.append(boundary(r0))
        else:
            pieces.append(jnp.where(top, boundary(r0), boundary(r0 + SUBLANES // 2)))
    return jnp.exp2(jnp.abs(cum - jnp.concatenate(pieces, axis=0)) * (-LOG2E)).astype(BF16)


def _hgrn_forget(z, lb):
    e = jnp.exp(-jnp.abs(z))
    one_e = 1.0 + e
    r = 1.0 / one_e
    pos = z >= 0.0
    sig = jnp.where(pos, 1.0, e) * r
    sig_neg = jnp.where(pos, e, 1.0) * r
    oml = 1.0 - lb
    log_sig = jnp.minimum(z, 0.0) - jnp.log(one_e)
    log_f = jnp.where(lb > 0.0, jnp.log(lb + oml * sig), log_sig)
    return log_f, oml * sig_neg


def _hgrn_kernel(*refs, seq_len, chunk, n_lev, has_s0, n_prev):
    it = iter(refs)
    q_ref, zf_ref, zb_ref, i_ref, lb_ref, trif_ref, trib_ref, lvf_ref, lvb_ref, leaf_ref = (
        next(it) for _ in range(10))
    s0_ref = next(it) if has_s0 else None
    prev_ref = next(it) if n_prev else None
    o_ref = next(it)
    st_ref = None if has_s0 else next(it)
    la_ref, k_ref, cum_ref, qt_ref, kt_ref, qd_ref, kd_ref, ae_ref, s_ref, acc_ref = (
        next(it) for _ in range(10))
    n_chunks = seq_len // chunk
    dk = DK_A
    dirs = ((0, False, zf_ref, trif_ref, lvf_ref), (1, True, zb_ref, trib_ref, lvb_ref))
    rows_of = lambda c: slice(c * chunk, (c + 1) * chunk)
    lb = lb_ref[...]

    worst = None
    for d, _, z_ref, _, _ in dirs:
        la, key = _hgrn_forget(z_ref[...], lb[d:d + 1])
        la_ref[d] = la
        k_ref[d] = key.astype(BF16)
        leaf_sum = jnp.dot(leaf_ref[...], la.astype(BF16), preferred_element_type=F32)
        worst = leaf_sum if worst is None else jnp.minimum(worst, leaf_sum)
    leafwise_ok = jnp.min(worst) > -LEAF_MAX_DECAY

    for d, _, _, tri_ref, _ in dirs:
        tri = tri_ref[...]
        for c in range(n_chunks):
            la = la_ref[d, rows_of(c), :]
            la_hi = la.astype(BF16)
            la_lo = (la - la_hi.astype(F32)).astype(BF16)
            x2 = jnp.dot(tri, jnp.concatenate([la_hi, la_lo], axis=1),
                         preferred_element_type=F32)
            cum_ref[d, rows_of(c), :] = x2[:, :dk] + x2[:, dk:]

    def operands_and_scores(leafwise):
        if leafwise:
            levels = list(range(LEAF_LEVEL + 1, n_lev + 1))
        else:
            levels = list(range(1, n_lev + 1))
        n_terms = len(levels) + 1
        for d, reverse, _, _, _ in dirs:
            for c in range(n_chunks):
                rows = rows_of(c)
                cum = cum_ref[d, rows, :]
                la = la_ref[d, rows, :]
                q = q_ref[rows, :]
                k = k_ref[d, rows, :]
                cum_row = lambda r, d=d, c=c: cum_ref[d, c * chunk + r:c * chunk + r + 1, :]
                if leafwise:
                    factors = [_leaf_factors(cum, cum_row, reverse)]
                else:
                    factors = [None]
                for lev in levels:
                    e = _level_factor(la, cum, cum_row, lev, reverse)
                    factors.append((e, e))
                for t, f in enumerate(factors):
                    cols = slice(t * dk, (t + 1) * dk)
                    qt_ref[d, rows, cols] = q if f is None else q * f[0]
                    kt_ref[d, rows, cols] = k if f is None else k * f[1]
                end_row = 0 if reverse else chunk - 1
                cum_end = cum[end_row:end_row + 1]
                qd_ref[d, rows, :] = q * jnp.exp(cum).astype(BF16)
                kd_ref[d, rows, :] = k * jnp.exp(cum_end - cum).astype(BF16)
                ae_ref[d, c:c + 1, :] = jnp.exp(cum_end)
        zero = jnp.zeros((chunk, dk), BF16)
        for d, _, _, _, lv_ref in dirs:
            lv = lv_ref[...]
            if leafwise:
                masks = [(lv >= 0) & (lv <= LEAF_LEVEL)]
            else:
                masks = [lv == 0]
            masks += [lv == lev for lev in levels]
            for c in range(n_chunks):
                rows = rows_of(c)
                scores = jnp.zeros((chunk, chunk), F32)
                for t in range(0, n_terms, 2):
                    if t + 1 < n_terms:
                        k_a = kt_ref[d, rows, t * dk:(t + 1) * dk]
                        k_b = kt_ref[d, rows, (t + 1) * dk:(t + 2) * dk]
                        p = lax.dot_general(
                            qt_ref[d, rows, t * dk:(t + 2) * dk],
                            jnp.concatenate([jnp.concatenate([k_a, zero], axis=1),
                                             jnp.concatenate([zero, k_b], axis=1)], axis=0),
                            _NT, preferred_element_type=F32)
                        scores = jnp.where(masks[t], p[:, :chunk], scores)
                        scores = jnp.where(masks[t + 1], p[:, chunk:], scores)
                    else:
                        p = lax.dot_general(qt_ref[d, rows, t * dk:(t + 1) * dk],
                                            kt_ref[d, rows, t * dk:(t + 1) * dk],
                                            _NT, preferred_element_type=F32)
                        scores = jnp.where(masks[t], p, scores)
                s_ref[d, rows, :] = scores.astype(BF16)

    pl.when(leafwise_ok)(lambda: operands_and_scores(True))
    pl.when(jnp.logical_not(leafwise_ok))(lambda: operands_and_scores(False))

    finals = []
    for d, reverse, _, _, _ in dirs:
        incs = [lax.dot_general(i_ref[rows_of(c), :], kd_ref[d, rows_of(c), :], _TN,
                                preferred_element_type=F32) for c in range(n_chunks)]
        st = s0_ref[d].T if has_s0 else jnp.zeros((DV_A, dk), F32)
        before = [None] * n_chunks
        for c in (range(n_chunks - 1, -1, -1) if reverse else range(n_chunks)):
            before[c] = st.astype(BF16)
            st = st * ae_ref[d, c:c + 1, :] + incs[c]
        finals.append(st)
        for c in range(n_chunks):
            rows = rows_of(c)
            o = jnp.dot(s_ref[d, rows, :], i_ref[rows, :], preferred_element_type=F32)
            if has_s0 or c != (n_chunks - 1 if reverse else 0):
                o = o + lax.dot_general(qd_ref[d, rows, :], before[c], _NT,
                                        preferred_element_type=F32)
            if d == 0:
                acc_ref[rows, :] = o
            else:
                acc_ref[rows, :] += o

    o = acc_ref[...]
    o_ref[...] = (o * lax.rsqrt(jnp.mean(o * o, axis=-1, keepdims=True) + EPS)).astype(BF16)
    if st_ref is not None:
        if n_prev:
            st_ref[0:n_prev] = prev_ref[...]
        st_ref[n_prev, 0] = finals[0].T
        st_ref[n_prev, 1] = finals[1].T


def _hgrn_call(proj, z, lb, state, prev, layer):
    n_b, seq_len, _ = proj.shape
    tri_f, tri_b, lv_f, lv_b, n_lev = _hgrn_consts(HGRN_CHUNK)
    has_s0 = state is not None
    col = lambda off: (lambda b, h: (b, 0, off // DK_A + h))
    const = lambda a: pl.BlockSpec(a.shape, lambda b, h: (0, 0))
    in_specs = [pl.BlockSpec((None, seq_len, DK_A), col(o))
                for o in (OFF_QA, 0, OFF_ZB - OFF_ZF, OFF_IA)]
    leaf = 1 << LEAF_LEVEL
    leaf_rows = -(-(seq_len // leaf) // SUBLANES) * SUBLANES
    leaf_ind = (np.arange(seq_len)[None, :] // leaf == np.arange(leaf_rows)[:, None])
    in_specs += [_layer_spec(layer, (2, DK_A), lambda b, h: (0, h)),
                 const(tri_f), const(tri_b), const(lv_f), const(lv_b), const(leaf_ind)]
    args = [proj, z, z, proj, lb, jnp.asarray(tri_f, BF16), jnp.asarray(tri_b, BF16),
            jnp.asarray(lv_f), jnp.asarray(lv_b), jnp.asarray(leaf_ind, BF16)]
    out_specs = [pl.BlockSpec((None, seq_len, DV_A), lambda b, h: (b, 0, h))]
    out_shape = [jax.ShapeDtypeStruct((n_b, seq_len, H_A * DV_A), BF16)]
    if has_s0:
        in_specs.append(pl.BlockSpec((None, None, 2, None, DK_A, DV_A),
                                     lambda b, h: (b, layer, 0, h, 0, 0)))
        args.append(state)
    else:
        st_spec = lambda n: pl.BlockSpec((None, n, 2, None, DK_A, DV_A),
                                         lambda b, h: (b, 0, 0, h, 0, 0))
        if layer:
            in_specs.append(st_spec(layer))
            args.append(prev)
        out_specs.append(st_spec(layer + 1))
        out_shape.append(jax.ShapeDtypeStruct((n_b, layer + 1, 2, H_A, DK_A, DV_A), F32))
    res = pl.pallas_call(
        functools.partial(_hgrn_kernel, seq_len=seq_len, chunk=HGRN_CHUNK, n_lev=n_lev,
                          has_s0=has_s0, n_prev=0 if has_s0 else layer),
        grid=(n_b, H_A),
        in_specs=in_specs, out_specs=out_specs, out_shape=out_shape,
        scratch_shapes=[
            pltpu.VMEM((2, seq_len, DK_A), F32),
            pltpu.VMEM((2, seq_len, DK_A), BF16),
            pltpu.VMEM((2, seq_len, DK_A), F32),
            pltpu.VMEM((2, seq_len, (n_lev + 1) * DK_A), BF16),
            pltpu.VMEM((2, seq_len, (n_lev + 1) * DK_A), BF16),
            pltpu.VMEM((2, seq_len, DK_A), BF16),
            pltpu.VMEM((2, seq_len, DK_A), BF16),
            pltpu.VMEM((2, max(SUBLANES, seq_len // HGRN_CHUNK), DK_A), F32),
            pltpu.VMEM((2, seq_len, HGRN_CHUNK), BF16),
            pltpu.VMEM((seq_len, DV_A), F32),
        ],
        compiler_params=_params(("arbitrary", "arbitrary")),
        name="hgrn2_scan",
    )(*args)
    return (res[0], None) if has_s0 else (res[0], res[1])


def _rope_tables(seq_len):
    rows = seq_len // GRID_W
    r_idx = jnp.repeat(jnp.arange(rows), GRID_W).astype(F32)
    c_idx = jnp.tile(jnp.arange(GRID_W), rows).astype(F32)
    quarter = DK_B // 4
    inv = 1.0 / (ROPE_BASE ** (jnp.arange(quarter, dtype=F32) / quarter))
    ang_r = r_idx[:, None] * inv[None, :]
    ang_c = c_idx[:, None] * inv[None, :]
    cos = jnp.concatenate([jnp.cos(ang_r)] * 2 + [jnp.cos(ang_c)] * 2, axis=1)
    sin = jnp.concatenate([-jnp.sin(ang_r), jnp.sin(ang_r), -jnp.sin(ang_c), jnp.sin(ang_c)],
                          axis=1)
    return cos, sin


def _rope(x, cos, sin):
    half = DK_B // 2
    swapped = jnp.concatenate([pltpu.roll(x[:, :half], half // 2, axis=1),
                               pltpu.roll(x[:, half:], half // 2, axis=1)], axis=1)
    return x * cos + swapped * sin


def _ret_kernel(*refs, seq_len, latent, has_s0, n_prev):
    it = iter(refs)
    lg_ref, q_ref, k_ref, v_ref = (next(it) for _ in range(4))
    cos_ref, sin_ref = (next(it), next(it)) if latent else (None, None)
    s0_ref = next(it) if has_s0 else None
    prev_ref = next(it) if n_prev else None
    o_ref = next(it)
    st_ref = None if has_s0 else next(it)
    dm_ref = next(it)
    h = pl.program_id(0)
    lg_f = lg_ref[h, 0]
    lg_b = lg_ref[h, 1]

    @pl.when(pl.program_id(1) == 0)
    def _():
        t = lax.broadcasted_iota(jnp.int32, (seq_len, seq_len), 0)
        s = lax.broadcasted_iota(jnp.int32, (seq_len, seq_len), 1)
        d = (t - s).astype(F32)
        dm_ref[...] = K_SCALE * (
            jnp.where(d >= 0.0, jnp.exp(lg_f * jnp.maximum(d, 0.0)), 0.0)
            + jnp.where(d <= 0.0, jnp.exp(lg_b * jnp.maximum(-d, 0.0)), 0.0))

    q, k = q_ref[...], k_ref[...]
    if latent:
        cos, sin = cos_ref[...], sin_ref[...]
        q, k = _rope(q.astype(F32), cos, sin), _rope(k.astype(F32), cos, sin)
    k_b = k.astype(BF16)
    v_b = v_ref[...]
    if has_s0:
        s0 = jnp.concatenate([s0_ref[0], s0_ref[1]], axis=0).astype(BF16)
    blocks = [slice(r, r + RET_ROWS) for r in range(0, seq_len, RET_ROWS)]
    ps = [lax.dot_general(q[rows].astype(BF16), k_b, _NT, preferred_element_type=F32)
          for rows in blocks]
    carried = []
    if has_s0:
        for rows in blocks:
            pos = (lax.broadcasted_iota(jnp.int32, (RET_ROWS, 1), 0) + rows.start).astype(F32)
            q_s = jnp.concatenate([q[rows] * jnp.exp(lg_f * (pos + 1.0)),
                                   q[rows] * jnp.exp(lg_b * (seq_len - pos))], axis=1)
            carried.append(jnp.dot(q_s.astype(BF16), s0, preferred_element_type=F32))
    pds = [(p * dm_ref[rows, :]).astype(BF16) for p, rows in zip(ps, blocks)]
    outs = [jnp.dot(pd, v_b, preferred_element_type=F32) for pd in pds]
    for i, rows in enumerate(blocks):
        o = outs[i] + carried[i] if has_s0 else outs[i]
        o_ref[rows, :] = (o * lax.rsqrt(jnp.mean(o * o, axis=-1, keepdims=True) + EPS)
                          ).astype(BF16)
    if st_ref is not None:
        if n_prev:
            st_ref[0:n_prev] = prev_ref[...]
        pos = lax.broadcasted_iota(jnp.int32, (seq_len, 1), 0).astype(F32)
        k_f = (k * (K_SCALE * jnp.exp(lg_f * (seq_len - 1.0 - pos)))).astype(BF16)
        k_r = (k * (K_SCALE * jnp.exp(lg_b * pos))).astype(BF16)
        st_ref[n_prev, 0] = lax.dot_general(k_f, v_b, _TN, preferred_element_type=F32)
        st_ref[n_prev, 1] = lax.dot_general(k_r, v_b, _TN, preferred_element_type=F32)


def _ret_log_decay():
    heads = jnp.arange(H_B, dtype=F32)
    lg_f = jnp.log1p(-jnp.exp2(-5.0 - heads))
    lg_b = jnp.log1p(-jnp.exp2(-(5.0 + RET_DECAY_OFFSET_BWD) - heads))
    return jnp.stack([lg_f, lg_b], axis=1)


def _ret_call(proj, state, prev, layer, latent):
    n_b, seq_len, _ = proj.shape
    has_s0 = state is not None
    in_specs = [
        pl.BlockSpec(memory_space=pltpu.SMEM),
        pl.BlockSpec((None, seq_len, DK_B), lambda h, b: (b, 0, OFF_QB // DK_B + h)),
        pl.BlockSpec((None, seq_len, DK_B), lambda h, b: (b, 0, OFF_KB // DK_B + h)),
        pl.BlockSpec((None, seq_len, DV_B), lambda h, b: (b, 0, OFF_VB // DV_B + h)),
    ]
    args = [_ret_log_decay(), proj, proj, proj]
    if latent:
        cos, sin = _rope_tables(seq_len)
        in_specs += [pl.BlockSpec((seq_len, DK_B), lambda h, b: (0, 0))] * 2
        args += [cos, sin]
    out_specs = [pl.BlockSpec((None, seq_len, DV_B), lambda h, b: (b, 0, h))]
    out_shape = [jax.ShapeDtypeStruct((n_b, seq_len, H_B * DV_B), BF16)]
    if has_s0:
        in_specs.append(pl.BlockSpec((None, None, 2, None, DK_B, DV_B),
                                     lambda h, b: (b, layer, 0, h, 0, 0)))
        args.append(state)
    else:
        st_spec = lambda n: pl.BlockSpec((None, n, 2, None, DK_B, DV_B),
                                         lambda h, b: (b, 0, 0, h, 0, 0))
        if layer:
            in_specs.append(st_spec(layer))
            args.append(prev)
        out_specs.append(st_spec(layer + 1))
        out_shape.append(jax.ShapeDtypeStruct((n_b, layer + 1, 2, H_B, DK_B, DV_B), F32))
    res = pl.pallas_call(
        functools.partial(_ret_kernel, seq_len=seq_len, latent=latent, has_s0=has_s0,
                          n_prev=0 if has_s0 else layer),
        grid=(H_B, n_b),
        in_specs=in_specs, out_specs=out_specs, out_shape=out_shape,
        scratch_shapes=[pltpu.VMEM((seq_len, seq_len), F32)],
        compiler_params=_params(("arbitrary", "arbitrary")),
        name="retention_scan",
    )(*args)
    return (res[0], None) if has_s0 else (res[0], res[1])


def _post_kernel(oa_ref, ga_ref, ob_ref, gb0_ref, gb1_ref, gta_ref, gtb_ref, x_ref, gt1_ref,
                 pa_ref, pb_ref, wo_ref, o_ref):
    subs = [slice(r, r + POST_SUB) for r in range(0, POST_TILE, POST_SUB)]
    a = [(oa_ref[s, :].astype(F32) * _silu(ga_ref[s, :].astype(F32))).astype(BF16) for s in subs]
    y_a = [jnp.dot(a_s, pa_ref[...], preferred_element_type=F32) for a_s in a]
    b = [(ob_ref[s, :].astype(F32)
          * _silu(jnp.concatenate([gb0_ref[s, :], gb1_ref[s, :]], axis=1).astype(F32))
          ).astype(BF16) for s in subs]
    y_b = [jnp.dot(b_s, pb_ref[...], preferred_element_type=F32) for b_s in b]
    merged = [(_sigmoid(gta_ref[s, :].astype(F32)) * y_a[i]
               + _sigmoid(gtb_ref[s, :].astype(F32)) * y_b[i]).astype(BF16)
              for i, s in enumerate(subs)]
    y = [jnp.dot(m, wo_ref[...], preferred_element_type=F32) for m in merged]
    for i, s in enumerate(subs):
        o_ref[s, :] = x_ref[s, :] + gt1_ref[...] * y[i]


def _post_call(o_a, o_b, proj, x, mods, p_a, p_b, w_out, layer, row_fn):
    n_tok = x.shape[0]
    tile = lambda w: (lambda i, j: (i, w))
    pcol = lambda off: pl.BlockSpec((POST_TILE, 1024), tile(off // 1024))
    const = lambda a: _layer_spec(layer, a.shape[1:], lambda i, j: (0, 0))
    return pl.pallas_call(
        _post_kernel,
        grid=(n_tok // POST_TILE, 1),
        in_specs=[
            pl.BlockSpec((POST_TILE, H_A * DV_A), tile(0)),
            pcol(OFF_GA),
            pl.BlockSpec((POST_TILE, H_B * DV_B), tile(0)),
            pcol(OFF_GB), pcol(OFF_GB + 1024), pcol(OFF_GATE_A), pcol(OFF_GATE_B),
            pl.BlockSpec((POST_TILE, D_MODEL), tile(0)),
            _mod_spec(layer, row_fn, 2, POST_TILE),
            const(p_a), const(p_b), const(w_out),
        ],
        out_specs=pl.BlockSpec((POST_TILE, D_MODEL), tile(0)),
        out_shape=jax.ShapeDtypeStruct((n_tok, D_MODEL), F32),
        compiler_params=_params(("arbitrary", "arbitrary")),
        name="mixer_output",
    )(o_a, proj, o_b, proj, proj, proj, proj, x, mods, p_a, p_b, w_out)


def _conv3(u, wc, bc, seq_len):
    n = u.shape[0]
    row = lax.broadcasted_iota(jnp.int32, (SUBLANES, 1), 0)

    def zero_row(x, r0, r):
        return jnp.where(row == r, 0.0, x[r0:r0 + SUBLANES])

    prev, nxt = pltpu.roll(u, 1, axis=0), pltpu.roll(u, n - 1, axis=0)
    p_parts, n_parts = [], []
    for s0 in range(0, n, seq_len):
        s1 = s0 + seq_len
        p_parts += [zero_row(prev, s0, 0), prev[s0 + SUBLANES:s1]]
        n_parts += [nxt[s0:s1 - SUBLANES], zero_row(nxt, s1 - SUBLANES, SUBLANES - 1)]
    prev, nxt = jnp.concatenate(p_parts, axis=0), jnp.concatenate(n_parts, axis=0)
    return prev * wc[0:1] + u * wc[1:2] + nxt * wc[2:3] + bc


def _ffn_kernel(*refs, seq_len, final):
    it = iter(refs)
    (x_ref, sh_ref, sc_ref, gt_ref, n2_ref, wa_ref, wg_ref, wca_ref, wcg_ref, bca_ref, bcg_ref,
     wd_ref) = (next(it) for _ in range(12))
    fn_ref = next(it) if final else None
    o_ref, h_ref, acc_ref = next(it), next(it), next(it)
    j = pl.program_id(1)

    @pl.when(j == 0)
    def _():
        h_ref[...] = _norm_mod(x_ref[...], n2_ref[...], sc_ref[...], sh_ref[...]).astype(BF16)
        acc_ref[...] = jnp.zeros_like(acc_ref)

    h = h_ref[...]
    u_a = jnp.dot(h, wa_ref[...], preferred_element_type=F32)
    u_g = jnp.dot(h, wg_ref[...], preferred_element_type=F32)
    a = _conv3(u_a, wca_ref[...], bca_ref[...], seq_len)
    g = _conv3(u_g, wcg_ref[...], bcg_ref[...], seq_len)
    acc_ref[...] += jnp.dot((_silu(g) * a).astype(BF16), wd_ref[...], preferred_element_type=F32)

    @pl.when(j == pl.num_programs(1) - 1)
    def _():
        x = x_ref[...] + gt_ref[...] * acc_ref[...]
        if final:
            x = x * lax.rsqrt(jnp.mean(x * x, axis=-1, keepdims=True) + EPS) * fn_ref[...]
        o_ref[...] = x


def _ffn_call(x, mods, n2, w_up, w_conv, b_conv, w_down, final_norm, layer, row_fn, seq_len):
    n_tok = x.shape[0]
    n_ff = D_FF // FF_CHUNK
    lspec = functools.partial(_layer_spec, layer)
    in_specs = [
        pl.BlockSpec((TOKEN_TILE, D_MODEL), lambda i, j: (i, 0)),
        _mod_spec(layer, row_fn, 3), _mod_spec(layer, row_fn, 4), _mod_spec(layer, row_fn, 5),
        lspec((1, D_MODEL), lambda i, j: (0, 0)),
        lspec((D_MODEL, FF_CHUNK), lambda i, j: (0, j)),
        lspec((D_MODEL, FF_CHUNK), lambda i, j: (0, n_ff + j)),
        lspec((3, FF_CHUNK), lambda i, j: (0, j)),
        lspec((3, FF_CHUNK), lambda i, j: (0, n_ff + j)),
        lspec((1, FF_CHUNK), lambda i, j: (0, j)),
        lspec((1, FF_CHUNK), lambda i, j: (0, n_ff + j)),
        lspec((FF_CHUNK, D_MODEL), lambda i, j: (j, 0)),
    ]
    args = [x, mods, mods, mods, n2, w_up, w_up, w_conv, w_conv, b_conv, b_conv, w_down]
    final = final_norm is not None
    if final:
        in_specs.append(pl.BlockSpec((1, D_MODEL), lambda i, j: (0, 0)))
        args.append(final_norm.reshape(1, D_MODEL))
    return pl.pallas_call(
        functools.partial(_ffn_kernel, seq_len=seq_len, final=final),
        grid=(n_tok // TOKEN_TILE, n_ff),
        in_specs=in_specs,
        out_specs=pl.BlockSpec((TOKEN_TILE, D_MODEL), lambda i, j: (i, 0)),
        out_shape=jax.ShapeDtypeStruct((n_tok, D_MODEL), F32),
        scratch_shapes=[pltpu.VMEM((TOKEN_TILE, D_MODEL), BF16),
                        pltpu.VMEM((TOKEN_TILE, D_MODEL), F32)],
        compiler_params=_params(("arbitrary", "arbitrary")),
        name="conv_ffn",
    )(*args)


def kernel(x_prompt, x_sample, state_hgrn, state_ret, c, c_ctx, norm1, norm2, final_norm,
           w_mod, b_mod, w_in, hgrn_lb_raw, p_a, p_b, w_out, w_up, w_conv, b_conv, w_down):
    n_ctx, t_ctx, _ = x_prompt.shape
    n_dec, t_dec, _ = x_sample.shape
    assert t_ctx & (t_ctx - 1) == 0 and TOKEN_TILE % t_ctx == 0 and t_dec == TOKEN_TILE

    sm = jax.nn.softmax(hgrn_lb_raw.astype(F32), axis=0)
    cum = jnp.cumsum(sm, axis=0)
    lower_bounds = cum - cum[0:1]

    cvec = jnp.concatenate(
        [c_ctx[None, :], c, jnp.zeros((MOD_ROWS - 1 - n_dec, D_MODEL), F32)], axis=0)
    mods = _mod_call(cvec, w_mod, b_mod).reshape(DEPTH, MOD_ROWS, 1, 6 * D_MODEL)

    w_in_b, p_a_b, p_b_b, w_out_b, w_up_b, w_down_b = (
        _to_bf16(w) for w in (w_in, p_a, p_b, w_out, w_up, w_down))
    norm1_3, norm2_3 = norm1.reshape(DEPTH, 1, D_MODEL), norm2.reshape(DEPTH, 1, D_MODEL)
    b_conv_3 = b_conv.reshape(DEPTH, 1, 2 * D_FF)

    ctx_row = lambda tok: 0
    dec_row = lambda tok: tok // t_dec + 1

    def layer(x, l, n_b, seq_len, row_fn, s_hgrn, s_ret, prev_h, prev_r, latent):
        proj, z = _proj_call(x, mods, norm1_3, w_in_b, l, row_fn)
        proj3 = proj.reshape(n_b, seq_len, IN_WIDTH)
        o_a, st_h = _hgrn_call(proj3, z.reshape(n_b, seq_len, -1), lower_bounds, s_hgrn, prev_h, l)
        o_b, st_r = _ret_call(proj3, s_ret, prev_r, l, latent)
        x = _post_call(o_a.reshape(-1, H_A * DV_A), o_b.reshape(-1, H_B * DV_B), proj, x,
                       mods, p_a_b, p_b_b, w_out_b, l, row_fn)
        x = _ffn_call(x, mods, norm2_3, w_up_b, w_conv, b_conv_3, w_down_b,
                      final_norm if l == DEPTH - 1 else None, l, row_fn, seq_len)
        return x, st_h, st_r

    x = x_prompt.reshape(n_ctx * t_ctx, D_MODEL)
    st_h = st_r = None
    for l in range(DEPTH):
        x, st_h, st_r = layer(x, l, n_ctx, t_ctx, ctx_row, None, None, st_h, st_r, False)
    y_prompt = x.reshape(n_ctx, t_ctx, D_MODEL)

    x = x_sample.reshape(n_dec * t_dec, D_MODEL)
    for l in range(DEPTH):
        x, _, _ = layer(x, l, n_dec, t_dec, dec_row, state_hgrn, state_ret, None, None, True)
    y_sample = x.reshape(n_dec, t_dec, D_MODEL)
    return (y_prompt, y_sample, st_h, st_r)
```

```python
import functools

import numpy as np
import jax
import jax.numpy as jnp
from jax import lax
from jax.experimental import pallas as pl
from jax.experimental.pallas import tpu as pltpu

F32 = jnp.float32
BF16 = jnp.bfloat16

D_MODEL = 1024
DEPTH = 2
GRID_W = 64
H_A, DK_A, DV_A = 8, 128, 128
H_B, DK_B, DV_B = 4, 256, 512
D_FF = 2816
ROPE_BASE = 10000.0
K_SCALE = DK_B ** -0.5
EPS = 1e-6
RET_DECAY_OFFSET_BWD = 0.5
IN_WIDTH = 13312

W_OFF_ZF, W_OFF_ZB, W_OFF_IA = 1024, 2048, 3072
OFF_QA, OFF_IA, OFF_GA = 0, 1024, 2048
OFF_QB, OFF_KB, OFF_VB, OFF_GB = 3072, 4096, 5120, 7168
OFF_GATE_A, OFF_GATE_B = 9216, 10240
PROJ_WIDTH = 11264

MOD_ROWS = 8
TOKEN_TILE = 1024
POST_TILE = 512
POST_SUB = 256
FF_CHUNK = 256
HGRN_CHUNK = 128
LEAF_LEVEL = 5
LEAF_MAX_DECAY = 60.0
RET_ROWS = 256
PROJ_TN = 1024
Z_TILE0 = W_OFF_ZF // PROJ_TN
Z_TILES = (W_OFF_IA - W_OFF_ZF) // PROJ_TN
VMEM_LIMIT = 52 * 1024 * 1024
SUBLANES = 8
LOG2E = 1.4426950408889634

_NT = (((1,), (1,)), ((), ()))
_TN = (((0,), (0,)), ((), ()))


def _params(sem):
    return pltpu.CompilerParams(dimension_semantics=sem, vmem_limit_bytes=VMEM_LIMIT)


def _sigmoid(x):
    return jax.nn.sigmoid(x)


def _silu(x):
    return x * jax.nn.sigmoid(x)


def _mod_kernel(c_ref, w_ref, b_ref, o_ref):
    s = _silu(c_ref[...])
    o_ref[...] = jnp.dot(s, w_ref[...], precision=lax.Precision.HIGHEST,
                         preferred_element_type=F32) + b_ref[...]


def _mod_call(cvec, w_mod, b_mod):
    n_col = 6 * D_MODEL // 1024
    return pl.pallas_call(
        _mod_kernel,
        grid=(DEPTH, n_col),
        in_specs=[
            pl.BlockSpec((MOD_ROWS, D_MODEL), lambda l, j: (0, 0)),
            pl.BlockSpec((None, D_MODEL, 1024), lambda l, j: (l, 0, j)),
            pl.BlockSpec((None, 1, 1024), lambda l, j: (l, 0, j)),
        ],
        out_specs=pl.BlockSpec((None, MOD_ROWS, 1024), lambda l, j: (l, 0, j)),
        out_shape=jax.ShapeDtypeStruct((DEPTH, MOD_ROWS, 6 * D_MODEL), F32),
        compiler_params=_params(("arbitrary", "arbitrary")),
        name="modulation",
    )(cvec, w_mod, b_mod.reshape(DEPTH, 1, 6 * D_MODEL))


def _mod_spec(layer, row_fn, which, tile=TOKEN_TILE):
    return pl.BlockSpec((None, None, 1, D_MODEL),
                        lambda i, j: (layer, row_fn(i * tile), 0, which))


def _layer_spec(layer, block, index_map):
    return pl.BlockSpec((None,) + block, lambda *g: (layer,) + index_map(*g))


def _norm_mod(x, g, sc, sh):
    y = x * lax.rsqrt(jnp.mean(x * x, axis=-1, keepdims=True) + EPS) * g
    return y * (1.0 + sc) + sh


def _proj_kernel(x_ref, sh_ref, sc_ref, n_ref, w_ref, o_ref, hout_ref, h_ref):
    j, i = pl.program_id(0), pl.program_id(1)

    @pl.when(j == 0)
    def _():
        h = _norm_mod(x_ref[...], n_ref[...], sc_ref[...], sh_ref[...]).astype(BF16)
        h_ref[i] = h
        hout_ref[...] = h

    o_ref[...] = jnp.dot(h_ref[i], w_ref[...], preferred_element_type=F32).astype(BF16)


def _zproj_kernel(h_ref, w_ref, z_ref):
    z_ref[...] = jnp.dot(h_ref[...], w_ref[...], preferred_element_type=F32)


def _proj_call(x, mods, n1, w_in, layer, row_fn):
    n_tok = x.shape[0]
    tn = PROJ_TN
    n_tile = n_tok // TOKEN_TILE
    hold = lambda j, i: jnp.where(j == 0, i, n_tile - 1)
    w_tile = lambda j: jnp.where(j >= Z_TILE0, j + Z_TILES, j)
    mod = lambda which: pl.BlockSpec(
        (None, None, 1, D_MODEL), lambda j, i: (layer, row_fn(i * TOKEN_TILE), 0, which))
    proj, h = pl.pallas_call(
        _proj_kernel,
        grid=(PROJ_WIDTH // tn, n_tile),
        in_specs=[
            pl.BlockSpec((TOKEN_TILE, D_MODEL), lambda j, i: (hold(j, i), 0)),
            mod(0), mod(1),
            _layer_spec(layer, (1, D_MODEL), lambda j, i: (0, 0)),
            _layer_spec(layer, (D_MODEL, tn), lambda j, i: (0, w_tile(j))),
        ],
        out_specs=[
            pl.BlockSpec((TOKEN_TILE, tn), lambda j, i: (i, j)),
            pl.BlockSpec((TOKEN_TILE, D_MODEL), lambda j, i: (hold(j, i), 0)),
        ],
        out_shape=[jax.ShapeDtypeStruct((n_tok, PROJ_WIDTH), BF16),
                   jax.ShapeDtypeStruct((n_tok, D_MODEL), BF16)],
        scratch_shapes=[pltpu.VMEM((n_tile, TOKEN_TILE, D_MODEL), BF16)],
        compiler_params=_params(("arbitrary", "arbitrary")),
        name="in_projection",
    )(x, mods, mods, n1, w_in)
    z = pl.pallas_call(
        _zproj_kernel,
        grid=(Z_TILES, n_tile),
        in_specs=[
            pl.BlockSpec((TOKEN_TILE, D_MODEL), lambda j, i: (i, 0)),
            _layer_spec(layer, (D_MODEL, tn), lambda j, i: (0, Z_TILE0 + j)),
        ],
        out_specs=pl.BlockSpec((TOKEN_TILE, tn), lambda j, i: (i, j)),
        out_shape=jax.ShapeDtypeStruct((n_tok, Z_TILES * tn), F32),
        compiler_params=_params(("arbitrary", "arbitrary")),
        name="forget_projection",
    )(h, w_in)
    return proj, z


def _hgrn_consts(c):
    nl = int(np.log2(c))
    t = np.arange(c)
    tt, rr = t[:, None], t[None, :]
    x = tt ^ rr
    lev_of = np.where(x > 0, np.floor(np.log2(np.maximum(x, 1))) + 1, 0).astype(np.int32)
    lv_f = np.where(tt >= rr, lev_of, -1).astype(np.int32)
    lv_b = lv_f.T.copy()
    return ((rr <= tt).astype(np.float32), (rr >= tt).astype(np.float32), lv_f, lv_b, nl)


def _leaf_factors(cum, cum_row, reverse):
    c, dk = cum.shape
    leaf = 1 << LEAF_LEVEL
    pieces = []
    for base in range(0, c, leaf):
        row = base + leaf if reverse else base - 1
        if 0 <= row < c:
            pieces.append(jnp.broadcast_to(cum_row(row), (leaf, dk)))
        else:
            pieces.append(jnp.zeros((leaf, dk), F32))
    x = cum - jnp.concatenate(pieces, axis=0)
    return jnp.exp2(x * LOG2E).astype(BF16), jnp.exp2(x * (-LOG2E)).astype(BF16)


def _level_factor(la, cum, cum_row, lev, reverse):
    c, dk = la.shape
    if lev == 1:
        row = lax.broadcasted_iota(jnp.int32, (c, 1), 0)
        return jnp.exp(jnp.where((row & 1) == (0 if reverse else 1), la, 0.0)).astype(BF16)
    blk, half = 1 << lev, 1 << (lev - 1)

    def boundary(r):
        base = (r // blk) * blk
        return jnp.broadcast_to(cum_row(base + half if reverse else base + half - 1),
                                (SUBLANES, dk))

    top = lax.broadcasted_iota(jnp.int32, (SUBLANES, dk), 0) < SUBLANES // 2
    pieces = []
    for r0 in range(0, c, SUBLANES):
        if blk >= SUBLANES:
            pieces.append(boundary(r0))
        else:
            pieces.append(jnp.where(top, boundary(r0), boundary(r0 + SUBLANES // 2)))
    return jnp.exp2(jnp.abs(cum - jnp.concatenate(pieces, axis=0)) * (-LOG2E)).astype(BF16)


def _hgrn_forget(z, lb):
    e = jnp.exp(-jnp.abs(z))
    one_e = 1.0 + e
    r = 1.0 / one_e
    pos = z >= 0.0
    sig = jnp.where(pos, 1.0, e) * r
    sig_neg = jnp.where(pos, e, 1.0) * r
    oml = 1.0 - lb
    log_sig = jnp.minimum(z, 0.0) - jnp.log(one_e)
    log_f = jnp.where(lb > 0.0, jnp.log(lb + oml * sig), log_sig)
    return log_f, oml * sig_neg


def _hgrn_kernel(*refs, seq_len, chunk, n_lev, has_s0, n_prev):
    it = iter(refs)
    q_ref, zf_ref, zb_ref, i_ref, lb_ref, trif_ref, trib_ref, lvf_ref, lvb_ref, leaf_ref = (
        next(it) for _ in range(10))
    s0_ref = next(it) if has_s0 else None
    prev_ref = next(it) if n_prev else None
    o_ref = next(it)
    st_ref = None if has_s0 else next(it)
    la_ref, k_ref, cum_ref, qt_ref, kt_ref, qd_ref, kd_ref, ae_ref, s_ref, acc_ref = (
        next(it) for _ in range(10))
    n_chunks = seq_len // chunk
    dk = DK_A
    dirs = ((0, False, zf_ref, trif_ref, lvf_ref), (1, True, zb_ref, trib_ref, lvb_ref))
    rows_of = lambda c: slice(c * chunk, (c + 1) * chunk)
    lb = lb_ref[...]

    worst = None
    for d, _, z_ref, _, _ in dirs:
        la, key = _hgrn_forget(z_ref[...], lb[d:d + 1])
        la_ref[d] = la
        k_ref[d] = key.astype(BF16)
        leaf_sum = jnp.dot(leaf_ref[...], la.astype(BF16), preferred_element_type=F32)
        worst = leaf_sum if worst is None else jnp.minimum(worst, leaf_sum)
    leafwise_ok = jnp.min(worst) > -LEAF_MAX_DECAY

    for d, _, _, tri_ref, _ in dirs:
        tri = tri_ref[...]
        for c in range(n_chunks):
            la = la_ref[d, rows_of(c), :]
            la_hi = la.astype(BF16)
            la_lo = (la - la_hi.astype(F32)).astype(BF16)
            x2 = jnp.dot(tri, jnp.concatenate([la_hi, la_lo], axis=1),
                         preferred_element_type=F32)
            cum_ref[d, rows_of(c), :] = x2[:, :dk] + x2[:, dk:]

    def operands_and_scores(leafwise):
        if leafwise:
            levels = list(range(LEAF_LEVEL + 1, n_lev + 1))
        else:
            levels = list(range(1, n_lev + 1))
        n_terms = len(levels) + 1
        for d, reverse, _, _, _ in dirs:
            for c in range(n_chunks):
                rows = rows_of(c)
                cum = cum_ref[d, rows, :]
                la = la_ref[d, rows, :]
                q = q_ref[rows, :]
                k = k_ref[d, rows, :]
                cum_row = lambda r, d=d, c=c: cum_ref[d, c * chunk + r:c * chunk + r + 1, :]
                if leafwise:
                    factors = [_leaf_factors(cum, cum_row, reverse)]
                else:
                    factors = [None]
                for lev in levels:
                    e = _level_factor(la, cum, cum_row, lev, reverse)
                    factors.append((e, e))
                for t, f in enumerate(factors):
                    cols = slice(t * dk, (t + 1) * dk)
                    qt_ref[d, rows, cols] = q if f is None else q * f[0]
                    kt_ref[d, rows, cols] = k if f is None else k * f[1]
                end_row = 0 if reverse else chunk - 1
                cum_end = cum[end_row:end_row + 1]
                qd_ref[d, rows, :] = q * jnp.exp(cum).astype(BF16)
                kd_ref[d, rows, :] = k * jnp.exp(cum_end - cum).astype(BF16)
                ae_ref[d, c:c + 1, :] = jnp.exp(cum_end)
        zero = jnp.zeros((chunk, dk), BF16)
        for d, _, _, _, lv_ref in dirs:
            lv = lv_ref[...]
            if leafwise:
                masks = [(lv >= 0) & (lv <= LEAF_LEVEL)]
            else:
                masks = [lv == 0]
            masks += [lv == lev for lev in levels]
            for c in range(n_chunks):
                rows = rows_of(c)
                scores = jnp.zeros((chunk, chunk), F32)
                for t in range(0, n_terms, 2):
                    if t + 1 < n_terms:
                        k_a = kt_ref[d, rows, t * dk:(t + 1) * dk]
                        k_b = kt_ref[d, rows, (t + 1) * dk:(t + 2) * dk]
                        p = lax.dot_general(
                            qt_ref[d, rows, t * dk:(t + 2) * dk],
                            jnp.concatenate([jnp.concatenate([k_a, zero], axis=1),
                                             jnp.concatenate([zero, k_b], axis=1)], axis=0),
                            _NT, preferred_element_type=F32)
                        scores = jnp.where(masks[t], p[:, :chunk], scores)
                        scores = jnp.where(masks[t + 1], p[:, chunk:], scores)
                    else:
                        p = lax.dot_general(qt_ref[d, rows, t * dk:(t + 1) * dk],
                                            kt_ref[d, rows, t * dk:(t + 1) * dk],
                                            _NT, preferred_element_type=F32)
                        scores = jnp.where(masks[t], p, scores)
                s_ref[d, rows, :] = scores.astype(BF16)

    pl.when(leafwise_ok)(lambda: operands_and_scores(True))
    pl.when(jnp.logical_not(leafwise_ok))(lambda: operands_and_scores(False))

    finals = []
    for d, reverse, _, _, _ in dirs:
        incs = [lax.dot_general(i_ref[rows_of(c), :], kd_ref[d, rows_of(c), :], _TN,
                                preferred_element_type=F32) for c in range(n_chunks)]
        st = s0_ref[d].T if has_s0 else jnp.zeros((DV_A, dk), F32)
        before = [None] * n_chunks
        for c in (range(n_chunks - 1, -1, -1) if reverse else range(n_chunks)):
            before[c] = st.astype(BF16)
            st = st * ae_ref[d, c:c + 1, :] + incs[c]
        finals.append(st)
        for c in range(n_chunks):
            rows = rows_of(c)
            o = jnp.dot(s_ref[d, rows, :], i_ref[rows, :], preferred_element_type=F32)
            if has_s0 or c != (n_chunks - 1 if reverse else 0):
                o = o + lax.dot_general(qd_ref[d, rows, :], before[c], _NT,
                                        preferred_element_type=F32)
            if d == 0:
                acc_ref[rows, :] = o
            else:
                acc_ref[rows, :] += o

    o = acc_ref[...]
    o_ref[...] = (o * lax.rsqrt(jnp.mean(o * o, axis=-1, keepdims=True) + EPS)).astype(BF16)
    if st_ref is not None:
        if n_prev:
            st_ref[0:n_prev] = prev_ref[...]
        st_ref[n_prev, 0] = finals[0].T
        st_ref[n_prev, 1] = finals[1].T


def _hgrn_call(proj, z, lb, state, prev, layer):
    n_b, seq_len, _ = proj.shape
    tri_f, tri_b, lv_f, lv_b, n_lev = _hgrn_consts(HGRN_CHUNK)
    has_s0 = state is not None
    col = lambda off: (lambda b, h: (b, 0, off // DK_A + h))
    const = lambda a: pl.BlockSpec(a.shape, lambda b, h: (0, 0))
    in_specs = [pl.BlockSpec((None, seq_len, DK_A), col(o))
                for o in (OFF_QA, 0, W_OFF_ZB - W_OFF_ZF, OFF_IA)]
    leaf = 1 << LEAF_LEVEL
    leaf_rows = -(-(seq_len // leaf) // SUBLANES) * SUBLANES
    leaf_ind = (np.arange(seq_len)[None, :] // leaf == np.arange(leaf_rows)[:, None])
    in_specs += [_layer_spec(layer, (2, DK_A), lambda b, h: (0, h)),
                 const(tri_f), const(tri_b), const(lv_f), const(lv_b), const(leaf_ind)]
    args = [proj, z, z, proj, lb, jnp.asarray(tri_f, BF16), jnp.asarray(tri_b, BF16),
            jnp.asarray(lv_f), jnp.asarray(lv_b), jnp.asarray(leaf_ind, BF16)]
    out_specs = [pl.BlockSpec((None, seq_len, DV_A), lambda b, h: (b, 0, h))]
    out_shape = [jax.ShapeDtypeStruct((n_b, seq_len, H_A * DV_A), BF16)]
    if has_s0:
        in_specs.append(pl.BlockSpec((None, None, 2, None, DK_A, DV_A),
                                     lambda b, h: (b, layer, 0, h, 0, 0)))
        args.append(state)
    else:
        st_spec = lambda n: pl.BlockSpec((None, n, 2, None, DK_A, DV_A),
                                         lambda b, h: (b, 0, 0, h, 0, 0))
        if layer:
            in_specs.append(st_spec(layer))
            args.append(prev)
        out_specs.append(st_spec(layer + 1))
        out_shape.append(jax.ShapeDtypeStruct((n_b, layer + 1, 2, H_A, DK_A, DV_A), F32))
    res = pl.pallas_call(
        functools.partial(_hgrn_kernel, seq_len=seq_len, chunk=HGRN_CHUNK, n_lev=n_lev,
                          has_s0=has_s0, n_prev=0 if has_s0 else layer),
        grid=(n_b, H_A),
        in_specs=in_specs, out_specs=out_specs, out_shape=out_shape,
        scratch_shapes=[
            pltpu.VMEM((2, seq_len, DK_A), F32),
            pltpu.VMEM((2, seq_len, DK_A), BF16),
            pltpu.VMEM((2, seq_len, DK_A), F32),
            pltpu.VMEM((2, seq_len, (n_lev + 1) * DK_A), BF16),
            pltpu.VMEM((2, seq_len, (n_lev + 1) * DK_A), BF16),
            pltpu.VMEM((2, seq_len, DK_A), BF16),
            pltpu.VMEM((2, seq_len, DK_A), BF16),
            pltpu.VMEM((2, max(SUBLANES, seq_len // HGRN_CHUNK), DK_A), F32),
            pltpu.VMEM((2, seq_len, HGRN_CHUNK), BF16),
            pltpu.VMEM((seq_len, DV_A), F32),
        ],
        compiler_params=_params(("arbitrary", "arbitrary")),
        name="hgrn2_scan",
    )(*args)
    return (res[0], None) if has_s0 else (res[0], res[1])


def _rope_tables(seq_len):
    rows = seq_len // GRID_W
    r_idx = jnp.repeat(jnp.arange(rows), GRID_W).astype(F32)
    c_idx = jnp.tile(jnp.arange(GRID_W), rows).astype(F32)
    quarter = DK_B // 4
    inv = 1.0 / (ROPE_BASE ** (jnp.arange(quarter, dtype=F32) / quarter))
    ang_r = r_idx[:, None] * inv[None, :]
    ang_c = c_idx[:, None] * inv[None, :]
    cos = jnp.concatenate([jnp.cos(ang_r)] * 2 + [jnp.cos(ang_c)] * 2, axis=1)
    sin = jnp.concatenate([-jnp.sin(ang_r), jnp.sin(ang_r), -jnp.sin(ang_c), jnp.sin(ang_c)],
                          axis=1)
    return cos, sin


def _rope(x, cos, sin):
    half = DK_B // 2
    swapped = jnp.concatenate([pltpu.roll(x[:, :half], half // 2, axis=1),
                               pltpu.roll(x[:, half:], half // 2, axis=1)], axis=1)
    return x * cos + swapped * sin


def _ret_kernel(*refs, seq_len, latent, has_s0, n_prev):
    it = iter(refs)
    lg_ref, q_ref, k_ref, v_ref = (next(it) for _ in range(4))
    cos_ref, sin_ref = (next(it), next(it)) if latent else (None, None)
    s0_ref = next(it) if has_s0 else None
    prev_ref = next(it) if n_prev else None
    o_ref = next(it)
    st_ref = None if has_s0 else next(it)
    dm_ref = next(it)
    h = pl.program_id(0)
    lg_f = lg_ref[h, 0]
    lg_b = lg_ref[h, 1]

    @pl.when(pl.program_id(1) == 0)
    def _():
        t = lax.broadcasted_iota(jnp.int32, (seq_len, seq_len), 0)
        s = lax.broadcasted_iota(jnp.int32, (seq_len, seq_len), 1)
        d = (t - s).astype(F32)
        dm_ref[...] = K_SCALE * (
            jnp.where(d >= 0.0, jnp.exp(lg_f * jnp.maximum(d, 0.0)), 0.0)
            + jnp.where(d <= 0.0, jnp.exp(lg_b * jnp.maximum(-d, 0.0)), 0.0))

    q, k = q_ref[...], k_ref[...]
    if latent:
        cos, sin = cos_ref[...], sin_ref[...]
        q, k = _rope(q.astype(F32), cos, sin), _rope(k.astype(F32), cos, sin)
    k_b = k.astype(BF16)
    v_b = v_ref[...]
    if has_s0:
        s0 = jnp.concatenate([s0_ref[0], s0_ref[1]], axis=0).astype(BF16)
    blocks = [slice(r, r + RET_ROWS) for r in range(0, seq_len, RET_ROWS)]
    ps = [lax.dot_general(q[rows].astype(BF16), k_b, _NT, preferred_element_type=F32)
          for rows in blocks]
    carried = []
    if has_s0:
        for rows in blocks:
            pos = (lax.broadcasted_iota(jnp.int32, (RET_ROWS, 1), 0) + rows.start).astype(F32)
            q_s = jnp.concatenate([q[rows] * jnp.exp(lg_f * (pos + 1.0)),
                                   q[rows] * jnp.exp(lg_b * (seq_len - pos))], axis=1)
            carried.append(jnp.dot(q_s.astype(BF16), s0, preferred_element_type=F32))
    pds = [(p * dm_ref[rows, :]).astype(BF16) for p, rows in zip(ps, blocks)]
    outs = [jnp.dot(pd, v_b, preferred_element_type=F32) for pd in pds]
    for i, rows in enumerate(blocks):
        o = outs[i] + carried[i] if has_s0 else outs[i]
        o_ref[rows, :] = (o * lax.rsqrt(jnp.mean(o * o, axis=-1, keepdims=True) + EPS)
                          ).astype(BF16)
    if st_ref is not None:
        if n_prev:
            st_ref[0:n_prev] = prev_ref[...]
        pos = lax.broadcasted_iota(jnp.int32, (seq_len, 1), 0).astype(F32)
        k_f = (k * (K_SCALE * jnp.exp(lg_f * (seq_len - 1.0 - pos)))).astype(BF16)
        k_r = (k * (K_SCALE * jnp.exp(lg_b * pos))).astype(BF16)
        st_ref[n_prev, 0] = lax.dot_general(k_f, v_b, _TN, preferred_element_type=F32)
        st_ref[n_prev, 1] = lax.dot_general(k_r, v_b, _TN, preferred_element_type=F32)


def _ret_log_decay():
    heads = jnp.arange(H_B, dtype=F32)
    lg_f = jnp.log1p(-jnp.exp2(-5.0 - heads))
    lg_b = jnp.log1p(-jnp.exp2(-(5.0 + RET_DECAY_OFFSET_BWD) - heads))
    return jnp.stack([lg_f, lg_b], axis=1)


def _ret_call(proj, state, prev, layer, latent):
    n_b, seq_len, _ = proj.shape
    has_s0 = state is not None
    in_specs = [
        pl.BlockSpec(memory_space=pltpu.SMEM),
        pl.BlockSpec((None, seq_len, DK_B), lambda h, b: (b, 0, OFF_QB // DK_B + h)),
        pl.BlockSpec((None, seq_len, DK_B), lambda h, b: (b, 0, OFF_KB // DK_B + h)),
        pl.BlockSpec((None, seq_len, DV_B), lambda h, b: (b, 0, OFF_VB // DV_B + h)),
    ]
    args = [_ret_log_decay(), proj, proj, proj]
    if latent:
        cos, sin = _rope_tables(seq_len)
        in_specs += [pl.BlockSpec((seq_len, DK_B), lambda h, b: (0, 0))] * 2
        args += [cos, sin]
    out_specs = [pl.BlockSpec((None, seq_len, DV_B), lambda h, b: (b, 0, h))]
    out_shape = [jax.ShapeDtypeStruct((n_b, seq_len, H_B * DV_B), BF16)]
    if has_s0:
        in_specs.append(pl.BlockSpec((None, None, 2, None, DK_B, DV_B),
                                     lambda h, b: (b, layer, 0, h, 0, 0)))
        args.append(state)
    else:
        st_spec = lambda n: pl.BlockSpec((None, n, 2, None, DK_B, DV_B),
                                         lambda h, b: (b, 0, 0, h, 0, 0))
        if layer:
            in_specs.append(st_spec(layer))
            args.append(prev)
        out_specs.append(st_spec(layer + 1))
        out_shape.append(jax.ShapeDtypeStruct((n_b, layer + 1, 2, H_B, DK_B, DV_B), F32))
    res = pl.pallas_call(
        functools.partial(_ret_kernel, seq_len=seq_len, latent=latent, has_s0=has_s0,
                          n_prev=0 if has_s0 else layer),
        grid=(H_B, n_b),
        in_specs=in_specs, out_specs=out_specs, out_shape=out_shape,
        scratch_shapes=[pltpu.VMEM((seq_len, seq_len), F32)],
        compiler_params=_params(("arbitrary", "arbitrary")),
        name="retention_scan",
    )(*args)
    return (res[0], None) if has_s0 else (res[0], res[1])


def _post_kernel(oa_ref, ga_ref, ob_ref, gb0_ref, gb1_ref, gta_ref, gtb_ref, x_ref, gt1_ref,
                 pa_ref, pb_ref, wo_ref, o_ref):
    subs = [slice(r, r + POST_SUB) for r in range(0, POST_TILE, POST_SUB)]
    a = [(oa_ref[s, :].astype(F32) * _silu(ga_ref[s, :].astype(F32))).astype(BF16) for s in subs]
    y_a = [jnp.dot(a_s, pa_ref[...], preferred_element_type=F32) for a_s in a]
    b = [(ob_ref[s, :].astype(F32)
          * _silu(jnp.concatenate([gb0_ref[s, :], gb1_ref[s, :]], axis=1).astype(F32))
          ).astype(BF16) for s in subs]
    y_b = [jnp.dot(b_s, pb_ref[...], preferred_element_type=F32) for b_s in b]
    merged = [(_sigmoid(gta_ref[s, :].astype(F32)) * y_a[i]
               + _sigmoid(gtb_ref[s, :].astype(F32)) * y_b[i]).astype(BF16)
              for i, s in enumerate(subs)]
    y = [jnp.dot(m, wo_ref[...], preferred_element_type=F32) for m in merged]
    for i, s in enumerate(subs):
        o_ref[s, :] = x_ref[s, :] + gt1_ref[...] * y[i]


def _post_call(o_a, o_b, proj, x, mods, p_a, p_b, w_out, layer, row_fn):
    n_tok = x.shape[0]
    tile = lambda w: (lambda i, j: (i, w))
    pcol = lambda off: pl.BlockSpec((POST_TILE, 1024), tile(off // 1024))
    const = lambda a: _layer_spec(layer, a.shape[1:], lambda i, j: (0, 0))
    return pl.pallas_call(
        _post_kernel,
        grid=(n_tok // POST_TILE, 1),
        in_specs=[
            pl.BlockSpec((POST_TILE, H_A * DV_A), tile(0)),
            pcol(OFF_GA),
            pl.BlockSpec((POST_TILE, H_B * DV_B), tile(0)),
            pcol(OFF_GB), pcol(OFF_GB + 1024), pcol(OFF_GATE_A), pcol(OFF_GATE_B),
            pl.BlockSpec((POST_TILE, D_MODEL), tile(0)),
            _mod_spec(layer, row_fn, 2, POST_TILE),
            const(p_a), const(p_b), const(w_out),
        ],
        out_specs=pl.BlockSpec((POST_TILE, D_MODEL), tile(0)),
        out_shape=jax.ShapeDtypeStruct((n_tok, D_MODEL), F32),
        compiler_params=_params(("arbitrary", "arbitrary")),
        name="mixer_output",
    )(o_a, proj, o_b, proj, proj, proj, proj, x, mods, p_a, p_b, w_out)


def _conv3(u, wc, bc, seq_len):
    n = u.shape[0]
    row = lax.broadcasted_iota(jnp.int32, (SUBLANES, 1), 0)

    def zero_row(x, r0, r):
        return jnp.where(row == r, 0.0, x[r0:r0 + SUBLANES])

    prev, nxt = pltpu.roll(u, 1, axis=0), pltpu.roll(u, n - 1, axis=0)
    p_parts, n_parts = [], []
    for s0 in range(0, n, seq_len):
        s1 = s0 + seq_len
        p_parts += [zero_row(prev, s0, 0), prev[s0 + SUBLANES:s1]]
        n_parts += [nxt[s0:s1 - SUBLANES], zero_row(nxt, s1 - SUBLANES, SUBLANES - 1)]
    prev, nxt = jnp.concatenate(p_parts, axis=0), jnp.concatenate(n_parts, axis=0)
    return prev * wc[0:1] + u * wc[1:2] + nxt * wc[2:3] + bc


def _ffn_kernel(*refs, seq_len, final):
    it = iter(refs)
    (x_ref, sh_ref, sc_ref, gt_ref, n2_ref, wa_ref, wg_ref, wca_ref, wcg_ref, bca_ref, bcg_ref,
     wd_ref) = (next(it) for _ in range(12))
    fn_ref = next(it) if final else None
    o_ref, h_ref, acc_ref = next(it), next(it), next(it)
    j = pl.program_id(1)

    @pl.when(j == 0)
    def _():
        h_ref[...] = _norm_mod(x_ref[...], n2_ref[...], sc_ref[...], sh_ref[...]).astype(BF16)
        acc_ref[...] = jnp.zeros_like(acc_ref)

    h = h_ref[...]
    u_a = jnp.dot(h, wa_ref[...], preferred_element_type=F32)
    u_g = jnp.dot(h, wg_ref[...], preferred_element_type=F32)
    a = _conv3(u_a, wca_ref[...], bca_ref[...], seq_len)
    g = _conv3(u_g, wcg_ref[...], bcg_ref[...], seq_len)
    acc_ref[...] += jnp.dot((_silu(g) * a).astype(BF16), wd_ref[...], preferred_element_type=F32)

    @pl.when(j == pl.num_programs(1) - 1)
    def _():
        x = x_ref[...] + gt_ref[...] * acc_ref[...]
        if final:
            x = x * lax.rsqrt(jnp.mean(x * x, axis=-1, keepdims=True) + EPS) * fn_ref[...]
        o_ref[...] = x


def _ffn_call(x, mods, n2, w_up, w_conv, b_conv, w_down, final_norm, layer, row_fn, seq_len):
    n_tok = x.shape[0]
    n_ff = D_FF // FF_CHUNK
    lspec = functools.partial(_layer_spec, layer)
    in_specs = [
        pl.BlockSpec((TOKEN_TILE, D_MODEL), lambda i, j: (i, 0)),
        _mod_spec(layer, row_fn, 3), _mod_spec(layer, row_fn, 4), _mod_spec(layer, row_fn, 5),
        lspec((1, D_MODEL), lambda i, j: (0, 0)),
        lspec((D_MODEL, FF_CHUNK), lambda i, j: (0, j)),
        lspec((D_MODEL, FF_CHUNK), lambda i, j: (0, n_ff + j)),
        lspec((3, FF_CHUNK), lambda i, j: (0, j)),
        lspec((3, FF_CHUNK), lambda i, j: (0, n_ff + j)),
        lspec((1, FF_CHUNK), lambda i, j: (0, j)),
        lspec((1, FF_CHUNK), lambda i, j: (0, n_ff + j)),
        lspec((FF_CHUNK, D_MODEL), lambda i, j: (j, 0)),
    ]
    args = [x, mods, mods, mods, n2, w_up, w_up, w_conv, w_conv, b_conv, b_conv, w_down]
    final = final_norm is not None
    if final:
        in_specs.append(pl.BlockSpec((1, D_MODEL), lambda i, j: (0, 0)))
        args.append(final_norm.reshape(1, D_MODEL))
    return pl.pallas_call(
        functools.partial(_ffn_kernel, seq_len=seq_len, final=final),
        grid=(n_tok // TOKEN_TILE, n_ff),
        in_specs=in_specs,
        out_specs=pl.BlockSpec((TOKEN_TILE, D_MODEL), lambda i, j: (i, 0)),
        out_shape=jax.ShapeDtypeStruct((n_tok, D_MODEL), F32),
        scratch_shapes=[pltpu.VMEM((TOKEN_TILE, D_MODEL), BF16),
                        pltpu.VMEM((TOKEN_TILE, D_MODEL), F32)],
        compiler_params=_params(("arbitrary", "arbitrary")),
        name="conv_ffn",
    )(*args)


def kernel(x_prompt, x_sample, state_hgrn, state_ret, c, c_ctx, norm1, norm2, final_norm,
           w_mod, b_mod, w_in, hgrn_lb_raw, p_a, p_b, w_out, w_up, w_conv, b_conv, w_down):
    n_ctx, t_ctx, _ = x_prompt.shape
    n_dec, t_dec, _ = x_sample.shape
    assert t_ctx & (t_ctx - 1) == 0 and TOKEN_TILE % t_ctx == 0 and t_dec == TOKEN_TILE

    sm = jax.nn.softmax(hgrn_lb_raw.astype(F32), axis=0)
    cum = jnp.cumsum(sm, axis=0)
    lower_bounds = cum - cum[0:1]

    cvec = jnp.concatenate(
        [c_ctx[None, :], c, jnp.zeros((MOD_ROWS - 1 - n_dec, D_MODEL), F32)], axis=0)
    mods = _mod_call(cvec, w_mod, b_mod).reshape(DEPTH, MOD_ROWS, 1, 6 * D_MODEL)

    w_in_b, p_a_b, p_b_b, w_out_b, w_up_b, w_down_b = (
        w.astype(BF16) for w in (w_in, p_a, p_b, w_out, w_up, w_down))
    norm1_3, norm2_3 = norm1.reshape(DEPTH, 1, D_MODEL), norm2.reshape(DEPTH, 1, D_MODEL)
    b_conv_3 = b_conv.reshape(DEPTH, 1, 2 * D_FF)

    ctx_row = lambda tok: 0
    dec_row = lambda tok: tok // t_dec + 1

    def layer(x, l, n_b, seq_len, row_fn, s_hgrn, s_ret, prev_h, prev_r, latent):
        proj, z = _proj_call(x, mods, norm1_3, w_in_b, l, row_fn)
        proj3 = proj.reshape(n_b, seq_len, PROJ_WIDTH)
        o_a, st_h = _hgrn_call(proj3, z.reshape(n_b, seq_len, -1), lower_bounds, s_hgrn, prev_h, l)
        o_b, st_r = _ret_call(proj3, s_ret, prev_r, l, latent)
        x = _post_call(o_a.reshape(-1, H_A * DV_A), o_b.reshape(-1, H_B * DV_B), proj, x,
                       mods, p_a_b, p_b_b, w_out_b, l, row_fn)
        x = _ffn_call(x, mods, norm2_3, w_up_b, w_conv, b_conv_3, w_down_b,
                      final_norm if l == DEPTH - 1 else None, l, row_fn, seq_len)
        return x, st_h, st_r

    x = x_prompt.reshape(n_ctx * t_ctx, D_MODEL)
    st_h = st_r = None
    for l in range(DEPTH):
        x, st_h, st_r = layer(x, l, n_ctx, t_ctx, ctx_row, None, None, st_h, st_r, False)
    y_prompt = x.reshape(n_ctx, t_ctx, D_MODEL)

    x = x_sample.reshape(n_dec * t_dec, D_MODEL)
    for l in range(DEPTH):
        x, _, _ = layer(x, l, n_dec, t_dec, dec_row, state_hgrn, state_ret, None, None, True)
    y_sample = x.reshape(n_dec, t_dec, D_MODEL)
    return (y_prompt, y_sample, st_h, st_r)
```

```python
import functools

import numpy as np
import jax
import jax.numpy as jnp
from jax import lax
from jax.experimental import pallas as pl
from jax.experimental.pallas import tpu as pltpu

F32 = jnp.float32
BF16 = jnp.bfloat16

D_MODEL = 1024
DEPTH = 2
GRID_W = 64
H_A, DK_A, DV_A = 8, 128, 128
H_B, DK_B, DV_B = 4, 256, 512
D_FF = 2816
ROPE_BASE = 10000.0
K_SCALE = DK_B ** -0.5
EPS = 1e-6
RET_DECAY_OFFSET_BWD = 0.5
IN_WIDTH = 13312

W_OFF_ZF, W_OFF_ZB, W_OFF_IA = 1024, 2048, 3072
OFF_QA, OFF_IA, OFF_GA = 0, 1024, 2048
OFF_QB, OFF_KB, OFF_VB, OFF_GB = 3072, 4096, 5120, 7168
OFF_GATE_A, OFF_GATE_B = 9216, 10240
PROJ_WIDTH = 11264

MOD_ROWS = 8
TOKEN_TILE = 1024
POST_TILE = 512
POST_SUB = 256
FF_CHUNK = 256
HGRN_CHUNK = 128
HGRN_UNITS = 16
LEAF_LEVEL = 5
LEAF_MAX_DECAY = 60.0
RET_ROWS = 256
PROJ_TN = 1024
Z_TILE0 = W_OFF_ZF // PROJ_TN
Z_TILES = (W_OFF_IA - W_OFF_ZF) // PROJ_TN
VMEM_LIMIT = 52 * 1024 * 1024
SUBLANES = 8
LOG2E = 1.4426950408889634

_NT = (((1,), (1,)), ((), ()))
_TN = (((0,), (0,)), ((), ()))


def _params(sem):
    return pltpu.CompilerParams(dimension_semantics=sem, vmem_limit_bytes=VMEM_LIMIT)


def _sigmoid(x):
    return jax.nn.sigmoid(x)


def _silu(x):
    return x * jax.nn.sigmoid(x)


def _mod_kernel(c_ref, w_ref, b_ref, o_ref):
    s = _silu(c_ref[...])
    o_ref[...] = jnp.dot(s, w_ref[...], precision=lax.Precision.HIGHEST,
                         preferred_element_type=F32) + b_ref[...]


def _mod_call(cvec, w_mod, b_mod):
    n_col = 6 * D_MODEL // 1024
    return pl.pallas_call(
        _mod_kernel,
        grid=(DEPTH, n_col),
        in_specs=[
            pl.BlockSpec((MOD_ROWS, D_MODEL), lambda l, j: (0, 0)),
            pl.BlockSpec((None, D_MODEL, 1024), lambda l, j: (l, 0, j)),
            pl.BlockSpec((None, 1, 1024), lambda l, j: (l, 0, j)),
        ],
        out_specs=pl.BlockSpec((None, MOD_ROWS, 1024), lambda l, j: (l, 0, j)),
        out_shape=jax.ShapeDtypeStruct((DEPTH, MOD_ROWS, 6 * D_MODEL), F32),
        compiler_params=_params(("arbitrary", "arbitrary")),
        name="modulation",
    )(cvec, w_mod, b_mod.reshape(DEPTH, 1, 6 * D_MODEL))


def _mod_spec(layer, row_fn, which, tile=TOKEN_TILE):
    return pl.BlockSpec((None, None, 1, D_MODEL),
                        lambda i, j: (layer, row_fn(i * tile), 0, which))


def _layer_spec(layer, block, index_map):
    return pl.BlockSpec((None,) + block, lambda *g: (layer,) + index_map(*g))


def _norm_mod(x, g, sc, sh):
    y = x * lax.rsqrt(jnp.mean(x * x, axis=-1, keepdims=True) + EPS) * g
    return y * (1.0 + sc) + sh


def _proj_kernel(x_ref, sh_ref, sc_ref, n_ref, w_ref, o_ref, hout_ref, h_ref, wb_ref):
    j, i = pl.program_id(0), pl.program_id(1)

    @pl.when(i == 0)
    def _():
        wb_ref[...] = w_ref[...].astype(BF16)

    @pl.when(j == 0)
    def _():
        h = _norm_mod(x_ref[...], n_ref[...], sc_ref[...], sh_ref[...]).astype(BF16)
        h_ref[i] = h
        hout_ref[...] = h

    o_ref[...] = jnp.dot(h_ref[i], wb_ref[...], preferred_element_type=F32).astype(BF16)


def _zproj_kernel(h_ref, w_ref, z_ref, wb_ref):
    @pl.when(pl.program_id(1) == 0)
    def _():
        wb_ref[...] = w_ref[...].astype(BF16)

    z_ref[...] = jnp.dot(h_ref[...], wb_ref[...], preferred_element_type=F32)


def _proj_call(x, mods, n1, w_in, layer, row_fn):
    n_tok = x.shape[0]
    tn = PROJ_TN
    n_tile = n_tok // TOKEN_TILE
    hold = lambda j, i: jnp.where(j == 0, i, n_tile - 1)
    w_tile = lambda j: jnp.where(j >= Z_TILE0, j + Z_TILES, j)
    mod = lambda which: pl.BlockSpec(
        (None, None, 1, D_MODEL), lambda j, i: (layer, row_fn(i * TOKEN_TILE), 0, which))
    proj, h = pl.pallas_call(
        _proj_kernel,
        grid=(PROJ_WIDTH // tn, n_tile),
        in_specs=[
            pl.BlockSpec((TOKEN_TILE, D_MODEL), lambda j, i: (hold(j, i), 0)),
            mod(0), mod(1),
            _layer_spec(layer, (1, D_MODEL), lambda j, i: (0, 0)),
            _layer_spec(layer, (D_MODEL, tn), lambda j, i: (0, w_tile(j))),
        ],
        out_specs=[
            pl.BlockSpec((TOKEN_TILE, tn), lambda j, i: (i, j)),
            pl.BlockSpec((TOKEN_TILE, D_MODEL), lambda j, i: (hold(j, i), 0)),
        ],
        out_shape=[jax.ShapeDtypeStruct((n_tok, PROJ_WIDTH), BF16),
                   jax.ShapeDtypeStruct((n_tok, D_MODEL), BF16)],
        scratch_shapes=[pltpu.VMEM((n_tile, TOKEN_TILE, D_MODEL), BF16),
                        pltpu.VMEM((D_MODEL, tn), BF16)],
        compiler_params=_params(("arbitrary", "arbitrary")),
        name="in_projection",
    )(x, mods, mods, n1, w_in)
    z = pl.pallas_call(
        _zproj_kernel,
        grid=(Z_TILES, n_tile),
        in_specs=[
            pl.BlockSpec((TOKEN_TILE, D_MODEL), lambda j, i: (i, 0)),
            _layer_spec(layer, (D_MODEL, tn), lambda j, i: (0, Z_TILE0 + j)),
        ],
        out_specs=pl.BlockSpec((TOKEN_TILE, tn), lambda j, i: (i, j)),
        out_shape=jax.ShapeDtypeStruct((n_tok, Z_TILES * tn), F32),
        scratch_shapes=[pltpu.VMEM((D_MODEL, tn), BF16)],
        compiler_params=_params(("arbitrary", "arbitrary")),
        name="forget_projection",
    )(h, w_in)
    return proj, z


def _hgrn_consts(c):
    nl = int(np.log2(c))
    t = np.arange(c)
    tt, rr = t[:, None], t[None, :]
    x = tt ^ rr
    lev_of = np.where(x > 0, np.floor(np.log2(np.maximum(x, 1))) + 1, 0).astype(np.int32)
    lv_f = np.where(tt >= rr, lev_of, -1).astype(np.int32)
    lv_b = lv_f.T.copy()
    return ((rr <= tt).astype(np.float32), (rr >= tt).astype(np.float32), lv_f, lv_b, nl)


def _leaf_factors(cum, cum_row, reverse):
    c, dk = cum.shape
    leaf = 1 << LEAF_LEVEL
    pieces = []
    for base in range(0, c, leaf):
        row = base + leaf if reverse else base - 1
        if 0 <= row < c:
            pieces.append(jnp.broadcast_to(cum_row(row), (leaf, dk)))
        else:
            pieces.append(jnp.zeros((leaf, dk), F32))
    x = cum - jnp.concatenate(pieces, axis=0)
    return jnp.exp2(x * LOG2E).astype(BF16), jnp.exp2(x * (-LOG2E)).astype(BF16)


def _level_factor(la, cum, cum_row, lev, reverse):
    c, dk = la.shape
    if lev == 1:
        row = lax.broadcasted_iota(jnp.int32, (c, 1), 0)
        return jnp.exp(jnp.where((row & 1) == (0 if reverse else 1), la, 0.0)).astype(BF16)
    blk, half = 1 << lev, 1 << (lev - 1)

    def boundary(r):
        base = (r // blk) * blk
        return jnp.broadcast_to(cum_row(base + half if reverse else base + half - 1),
                                (SUBLANES, dk))

    top = lax.broadcasted_iota(jnp.int32, (SUBLANES, dk), 0) < SUBLANES // 2
    pieces = []
    for r0 in range(0, c, SUBLANES):
        if blk >= SUBLANES:
            pieces.append(boundary(r0))
        else:
            pieces.append(jnp.where(top, boundary(r0), boundary(r0 + SUBLANES // 2)))
    return jnp.exp2(jnp.abs(cum - jnp.concatenate(pieces, axis=0)) * (-LOG2E)).astype(BF16)


def _hgrn_forget(z, lb):
    e = jnp.exp(-jnp.abs(z))
    one_e = 1.0 + e
    r = 1.0 / one_e
    pos = z >= 0.0
    sig = jnp.where(pos, 1.0, e) * r
    sig_neg = jnp.where(pos, e, 1.0) * r
    oml = 1.0 - lb
    log_sig = jnp.minimum(z, 0.0) - jnp.log(one_e)
    log_f = jnp.where(lb > 0.0, jnp.log(lb + oml * sig), log_sig)
    return log_f, oml * sig_neg


def _hgrn_kernel(*refs, seq_len, chunk, n_lev, heads, has_s0, n_prev):
    it = iter(refs)
    (q_ref, zf_ref, zb_ref, i_ref, g_ref, lb_ref, trif_ref, trib_ref, lvf_ref, lvb_ref,
     leaf_ref) = (next(it) for _ in range(11))
    s0_ref = next(it) if has_s0 else None
    prev_ref = next(it) if n_prev else None
    o_ref = next(it)
    st_ref = None if has_s0 else next(it)
    la_ref, k_ref, cum_ref, qt_ref, kt_ref, qd_ref, kd_ref, ae_ref, s_ref, acc_ref = (
        next(it) for _ in range(10))
    n_chunks = seq_len // chunk
    dk = DK_A
    units = [(2 * hh + d, hh, d, d == 1, z_ref, tri_ref, lv_ref)
             for hh in range(heads)
             for d, (z_ref, tri_ref, lv_ref) in enumerate(((zf_ref, trif_ref, lvf_ref),
                                                          (zb_ref, trib_ref, lvb_ref)))]
    rows_of = lambda c: slice(c * chunk, (c + 1) * chunk)
    cols_of = lambda hh: slice(hh * dk, (hh + 1) * dk)
    lb = lb_ref[...]

    worst = None
    for u, hh, d, _, z_ref, _, _ in units:
        la, key = _hgrn_forget(z_ref[:, cols_of(hh)], lb[d:d + 1, cols_of(hh)])
        la_ref[u] = la
        k_ref[u] = key.astype(BF16)
        leaf_sum = jnp.dot(leaf_ref[...], la.astype(BF16), preferred_element_type=F32)
        worst = leaf_sum if worst is None else jnp.minimum(worst, leaf_sum)
    leafwise_ok = jnp.min(worst) > -LEAF_MAX_DECAY

    for u, _, _, _, _, tri_ref, _ in units:
        tri = tri_ref[...]
        for c in range(n_chunks):
            la = la_ref[u, rows_of(c), :]
            la_hi = la.astype(BF16)
            la_lo = (la - la_hi.astype(F32)).astype(BF16)
            x2 = jnp.dot(tri, jnp.concatenate([la_hi, la_lo], axis=1),
                         preferred_element_type=F32)
            cum_ref[u, rows_of(c), :] = x2[:, :dk] + x2[:, dk:]

    def operands_and_scores(leafwise):
        if leafwise:
            levels = list(range(LEAF_LEVEL + 1, n_lev + 1))
        else:
            levels = list(range(1, n_lev + 1))
        n_terms = len(levels) + 1
        for u, hh, _, reverse, _, _, _ in units:
            for c in range(n_chunks):
                rows = rows_of(c)
                cum = cum_ref[u, rows, :]
                la = la_ref[u, rows, :]
                q = q_ref[rows, cols_of(hh)]
                k = k_ref[u, rows, :]
                cum_row = lambda r, u=u, c=c: cum_ref[u, c * chunk + r:c * chunk + r + 1, :]
                if leafwise:
                    factors = [_leaf_factors(cum, cum_row, reverse)]
                else:
                    factors = [None]
                for lev in levels:
                    e = _level_factor(la, cum, cum_row, lev, reverse)
                    factors.append((e, e))
                for t, f in enumerate(factors):
                    cols = slice(t * dk, (t + 1) * dk)
                    qt_ref[u, rows, cols] = q if f is None else q * f[0]
                    kt_ref[u, rows, cols] = k if f is None else k * f[1]
                end_row = 0 if reverse else chunk - 1
                cum_end = cum[end_row:end_row + 1]
                qd_ref[u, rows, :] = q * jnp.exp(cum).astype(BF16)
                kd_ref[u, rows, :] = k * jnp.exp(cum_end - cum).astype(BF16)
                ae_ref[u, c:c + 1, :] = jnp.exp(cum_end)
        zero = jnp.zeros((chunk, dk), BF16)
        for u, _, _, _, _, _, lv_ref in units:
            lv = lv_ref[...]
            if leafwise:
                masks = [(lv >= 0) & (lv <= LEAF_LEVEL)]
            else:
                masks = [lv == 0]
            masks += [lv == lev for lev in levels]
            for c in range(n_chunks):
                rows = rows_of(c)
                scores = jnp.zeros((chunk, chunk), F32)
                for t in range(0, n_terms, 2):
                    if t + 1 < n_terms:
                        k_a = kt_ref[u, rows, t * dk:(t + 1) * dk]
                        k_b = kt_ref[u, rows, (t + 1) * dk:(t + 2) * dk]
                        p = lax.dot_general(
                            qt_ref[u, rows, t * dk:(t + 2) * dk],
                            jnp.concatenate([jnp.concatenate([k_a, zero], axis=1),
                                             jnp.concatenate([zero, k_b], axis=1)], axis=0),
                            _NT, preferred_element_type=F32)
                        scores = jnp.where(masks[t], p[:, :chunk], scores)
                        scores = jnp.where(masks[t + 1], p[:, chunk:], scores)
                    else:
                        p = lax.dot_general(qt_ref[u, rows, t * dk:(t + 1) * dk],
                                            kt_ref[u, rows, t * dk:(t + 1) * dk],
                                            _NT, preferred_element_type=F32)
                        scores = jnp.where(masks[t], p, scores)
                s_ref[u, rows, :] = scores.astype(BF16)

    pl.when(leafwise_ok)(lambda: operands_and_scores(True))
    pl.when(jnp.logical_not(leafwise_ok))(lambda: operands_and_scores(False))

    incs = {u: [lax.dot_general(i_ref[rows_of(c), cols_of(hh)], kd_ref[u, rows_of(c), :], _TN,
                                preferred_element_type=F32) for c in range(n_chunks)]
            for u, hh, _, _, _, _, _ in units}
    finals, befores = {}, {}
    for u, hh, d, reverse, _, _, _ in units:
        st = s0_ref[d, hh].T if has_s0 else jnp.zeros((DV_A, dk), F32)
        befores[u] = [None] * n_chunks
        for c in (range(n_chunks - 1, -1, -1) if reverse else range(n_chunks)):
            befores[u][c] = st.astype(BF16)
            st = st * ae_ref[u, c:c + 1, :] + incs[u][c]
        finals[u] = st
    for u, hh, d, reverse, _, _, _ in units:
        for c in range(n_chunks):
            rows = rows_of(c)
            o = jnp.dot(s_ref[u, rows, :], i_ref[rows, cols_of(hh)], preferred_element_type=F32)
            if has_s0 or c != (n_chunks - 1 if reverse else 0):
                o = o + lax.dot_general(qd_ref[u, rows, :], befores[u][c], _NT,
                                        preferred_element_type=F32)
            if d == 0:
                acc_ref[rows, cols_of(hh)] = o
            else:
                acc_ref[rows, cols_of(hh)] += o

    for hh in range(heads):
        o = acc_ref[:, cols_of(hh)]
        o = o * lax.rsqrt(jnp.mean(o * o, axis=-1, keepdims=True) + EPS)
        o_ref[:, cols_of(hh)] = (o * _silu(g_ref[:, cols_of(hh)].astype(F32))).astype(BF16)
    if st_ref is not None:
        if n_prev:
            st_ref[0:n_prev] = prev_ref[...]
        for u, hh, d, _, _, _, _ in units:
            st_ref[n_prev, d, hh] = finals[u].T


def _hgrn_call(proj, z, lb, state, prev, layer):
    n_b, seq_len, _ = proj.shape
    tri_f, tri_b, lv_f, lv_b, n_lev = _hgrn_consts(HGRN_CHUNK)
    has_s0 = state is not None
    n_chunks = seq_len // HGRN_CHUNK
    heads = max(1, HGRN_UNITS // (2 * n_chunks))
    width = heads * DK_A
    col = lambda off: (lambda b, h: (b, 0, off // width + h))
    const = lambda a: pl.BlockSpec(a.shape, lambda b, h: (0, 0))
    in_specs = [pl.BlockSpec((None, seq_len, width), col(o))
                for o in (OFF_QA, 0, W_OFF_ZB - W_OFF_ZF, OFF_IA, OFF_GA)]
    leaf = 1 << LEAF_LEVEL
    leaf_rows = -(-(seq_len // leaf) // SUBLANES) * SUBLANES
    leaf_ind = (np.arange(seq_len)[None, :] // leaf == np.arange(leaf_rows)[:, None])
    in_specs += [_layer_spec(layer, (2, width), lambda b, h: (0, h)),
                 const(tri_f), const(tri_b), const(lv_f), const(lv_b), const(leaf_ind)]
    args = [proj, z, z, proj, proj, lb, jnp.asarray(tri_f, BF16), jnp.asarray(tri_b, BF16),
            jnp.asarray(lv_f), jnp.asarray(lv_b), jnp.asarray(leaf_ind, BF16)]
    out_specs = [pl.BlockSpec((None, seq_len, width), lambda b, h: (b, 0, h))]
    out_shape = [jax.ShapeDtypeStruct((n_b, seq_len, H_A * DV_A), BF16)]
    if has_s0:
        in_specs.append(pl.BlockSpec((None, None, 2, heads, DK_A, DV_A),
                                     lambda b, h: (b, layer, 0, h, 0, 0)))
        args.append(state)
    else:
        st_spec = lambda n: pl.BlockSpec((None, n, 2, heads, DK_A, DV_A),
                                         lambda b, h: (b, 0, 0, h, 0, 0))
        if layer:
            in_specs.append(st_spec(layer))
            args.append(prev)
        out_specs.append(st_spec(layer + 1))
        out_shape.append(jax.ShapeDtypeStruct((n_b, layer + 1, 2, H_A, DK_A, DV_A), F32))
    n_u = 2 * heads
    res = pl.pallas_call(
        functools.partial(_hgrn_kernel, seq_len=seq_len, chunk=HGRN_CHUNK, n_lev=n_lev,
                          heads=heads, has_s0=has_s0, n_prev=0 if has_s0 else layer),
        grid=(n_b, H_A // heads),
        in_specs=in_specs, out_specs=out_specs, out_shape=out_shape,
        scratch_shapes=[
            pltpu.VMEM((n_u, seq_len, DK_A), F32),
            pltpu.VMEM((n_u, seq_len, DK_A), BF16),
            pltpu.VMEM((n_u, seq_len, DK_A), F32),
            pltpu.VMEM((n_u, seq_len, (n_lev + 1) * DK_A), BF16),
            pltpu.VMEM((n_u, seq_len, (n_lev + 1) * DK_A), BF16),
            pltpu.VMEM((n_u, seq_len, DK_A), BF16),
            pltpu.VMEM((n_u, seq_len, DK_A), BF16),
            pltpu.VMEM((n_u, max(SUBLANES, n_chunks), DK_A), F32),
            pltpu.VMEM((n_u, seq_len, HGRN_CHUNK), BF16),
            pltpu.VMEM((seq_len, width), F32),
        ],
        compiler_params=_params(("arbitrary", "arbitrary")),
        name="hgrn2_scan",
    )(*args)
    return (res[0], None) if has_s0 else (res[0], res[1])


def _rope_tables(seq_len):
    rows = seq_len // GRID_W
    r_idx = jnp.repeat(jnp.arange(rows), GRID_W).astype(F32)
    c_idx = jnp.tile(jnp.arange(GRID_W), rows).astype(F32)
    quarter = DK_B // 4
    inv = 1.0 / (ROPE_BASE ** (jnp.arange(quarter, dtype=F32) / quarter))
    ang_r = r_idx[:, None] * inv[None, :]
    ang_c = c_idx[:, None] * inv[None, :]
    cos = jnp.concatenate([jnp.cos(ang_r)] * 2 + [jnp.cos(ang_c)] * 2, axis=1)
    sin = jnp.concatenate([-jnp.sin(ang_r), jnp.sin(ang_r), -jnp.sin(ang_c), jnp.sin(ang_c)],
                          axis=1)
    return cos, sin


def _rope(x, cos, sin):
    half = DK_B // 2
    swapped = jnp.concatenate([pltpu.roll(x[:, :half], half // 2, axis=1),
                               pltpu.roll(x[:, half:], half // 2, axis=1)], axis=1)
    return x * cos + swapped * sin


def _ret_kernel(*refs, seq_len, latent, has_s0, n_prev):
    it = iter(refs)
    lg_ref, q_ref, k_ref, v_ref, g_ref = (next(it) for _ in range(5))
    cos_ref, sin_ref = (next(it), next(it)) if latent else (None, None)
    s0_ref = next(it) if has_s0 else None
    prev_ref = next(it) if n_prev else None
    o_ref = next(it)
    st_ref = None if has_s0 else next(it)
    dm_ref = next(it)
    h = pl.program_id(0)
    lg_f = lg_ref[h, 0]
    lg_b = lg_ref[h, 1]

    @pl.when(pl.program_id(1) == 0)
    def _():
        t = lax.broadcasted_iota(jnp.int32, (seq_len, seq_len), 0)
        s = lax.broadcasted_iota(jnp.int32, (seq_len, seq_len), 1)
        d = (t - s).astype(F32)
        dm_ref[...] = K_SCALE * (
            jnp.where(d >= 0.0, jnp.exp(lg_f * jnp.maximum(d, 0.0)), 0.0)
            + jnp.where(d <= 0.0, jnp.exp(lg_b * jnp.maximum(-d, 0.0)), 0.0))

    q, k = q_ref[...], k_ref[...]
    if latent:
        cos, sin = cos_ref[...], sin_ref[...]
        q, k = _rope(q.astype(F32), cos, sin), _rope(k.astype(F32), cos, sin)
    k_b = k.astype(BF16)
    v_b = v_ref[...]
    if has_s0:
        s0 = jnp.concatenate([s0_ref[0], s0_ref[1]], axis=0).astype(BF16)
    blocks = [slice(r, r + RET_ROWS) for r in range(0, seq_len, RET_ROWS)]
    ps = [lax.dot_general(q[rows].astype(BF16), k_b, _NT, preferred_element_type=F32)
          for rows in blocks]
    carried = []
    if has_s0:
        for rows in blocks:
            pos = (lax.broadcasted_iota(jnp.int32, (RET_ROWS, 1), 0) + rows.start).astype(F32)
            q_s = jnp.concatenate([q[rows] * jnp.exp(lg_f * (pos + 1.0)),
                                   q[rows] * jnp.exp(lg_b * (seq_len - pos))], axis=1)
            carried.append(jnp.dot(q_s.astype(BF16), s0, preferred_element_type=F32))
    pds = [(p * dm_ref[rows, :]).astype(BF16) for p, rows in zip(ps, blocks)]
    outs = [jnp.dot(pd, v_b, preferred_element_type=F32) for pd in pds]
    for i, rows in enumerate(blocks):
        o = outs[i] + carried[i] if has_s0 else outs[i]
        o = o * lax.rsqrt(jnp.mean(o * o, axis=-1, keepdims=True) + EPS)
        o_ref[rows, :] = (o * _silu(g_ref[rows, :].astype(F32))).astype(BF16)
    if st_ref is not None:
        if n_prev:
            st_ref[0:n_prev] = prev_ref[...]
        pos = lax.broadcasted_iota(jnp.int32, (seq_len, 1), 0).astype(F32)
        k_f = (k * (K_SCALE * jnp.exp(lg_f * (seq_len - 1.0 - pos)))).astype(BF16)
        k_r = (k * (K_SCALE * jnp.exp(lg_b * pos))).astype(BF16)
        st_ref[n_prev, 0] = lax.dot_general(k_f, v_b, _TN, preferred_element_type=F32)
        st_ref[n_prev, 1] = lax.dot_general(k_r, v_b, _TN, preferred_element_type=F32)


def _ret_log_decay():
    heads = jnp.arange(H_B, dtype=F32)
    lg_f = jnp.log1p(-jnp.exp2(-5.0 - heads))
    lg_b = jnp.log1p(-jnp.exp2(-(5.0 + RET_DECAY_OFFSET_BWD) - heads))
    return jnp.stack([lg_f, lg_b], axis=1)


def _ret_call(proj, state, prev, layer, latent):
    n_b, seq_len, _ = proj.shape
    has_s0 = state is not None
    in_specs = [
        pl.BlockSpec(memory_space=pltpu.SMEM),
        pl.BlockSpec((None, seq_len, DK_B), lambda h, b: (b, 0, OFF_QB // DK_B + h)),
        pl.BlockSpec((None, seq_len, DK_B), lambda h, b: (b, 0, OFF_KB // DK_B + h)),
        pl.BlockSpec((None, seq_len, DV_B), lambda h, b: (b, 0, OFF_VB // DV_B + h)),
        pl.BlockSpec((None, seq_len, DV_B), lambda h, b: (b, 0, OFF_GB // DV_B + h)),
    ]
    args = [_ret_log_decay(), proj, proj, proj, proj]
    if latent:
        cos, sin = _rope_tables(seq_len)
        in_specs += [pl.BlockSpec((seq_len, DK_B), lambda h, b: (0, 0))] * 2
        args += [cos, sin]
    out_specs = [pl.BlockSpec((None, seq_len, DV_B), lambda h, b: (b, 0, h))]
    out_shape = [jax.ShapeDtypeStruct((n_b, seq_len, H_B * DV_B), BF16)]
    if has_s0:
        in_specs.append(pl.BlockSpec((None, None, 2, None, DK_B, DV_B),
                                     lambda h, b: (b, layer, 0, h, 0, 0)))
        args.append(state)
    else:
        st_spec = lambda n: pl.BlockSpec((None, n, 2, None, DK_B, DV_B),
                                         lambda h, b: (b, 0, 0, h, 0, 0))
        if layer:
            in_specs.append(st_spec(layer))
            args.append(prev)
        out_specs.append(st_spec(layer + 1))
        out_shape.append(jax.ShapeDtypeStruct((n_b, layer + 1, 2, H_B, DK_B, DV_B), F32))
    res = pl.pallas_call(
        functools.partial(_ret_kernel, seq_len=seq_len, latent=latent, has_s0=has_s0,
                          n_prev=0 if has_s0 else layer),
        grid=(H_B, n_b),
        in_specs=in_specs, out_specs=out_specs, out_shape=out_shape,
        scratch_shapes=[pltpu.VMEM((seq_len, seq_len), F32)],
        compiler_params=_params(("arbitrary", "arbitrary")),
        name="retention_scan",
    )(*args)
    return (res[0], None) if has_s0 else (res[0], res[1])


def _post_kernel(oa_ref, ob_ref, gta_ref, gtb_ref, x_ref, gt1_ref, pa_ref, pb_ref, wo_ref,
                 o_ref):
    subs = [slice(r, r + POST_SUB) for r in range(0, POST_TILE, POST_SUB)]
    y_a = [jnp.dot(oa_ref[s, :], pa_ref[...], preferred_element_type=F32) for s in subs]
    y_b = [jnp.dot(ob_ref[s, :], pb_ref[...], preferred_element_type=F32) for s in subs]
    merged = [(_sigmoid(gta_ref[s, :].astype(F32)) * y_a[i]
               + _sigmoid(gtb_ref[s, :].astype(F32)) * y_b[i]).astype(BF16)
              for i, s in enumerate(subs)]
    y = [jnp.dot(m, wo_ref[...], preferred_element_type=F32) for m in merged]
    for i, s in enumerate(subs):
        o_ref[s, :] = x_ref[s, :] + gt1_ref[...] * y[i]


def _post_call(o_a, o_b, proj, x, mods, p_a, p_b, w_out, layer, row_fn):
    n_tok = x.shape[0]
    tile = lambda w: (lambda i, j: (i, w))
    pcol = lambda off: pl.BlockSpec((POST_TILE, 1024), tile(off // 1024))
    const = lambda a: _layer_spec(layer, a.shape[1:], lambda i, j: (0, 0))
    return pl.pallas_call(
        _post_kernel,
        grid=(n_tok // POST_TILE, 1),
        in_specs=[
            pl.BlockSpec((POST_TILE, H_A * DV_A), tile(0)),
            pl.BlockSpec((POST_TILE, H_B * DV_B), tile(0)),
            pcol(OFF_GATE_A), pcol(OFF_GATE_B),
            pl.BlockSpec((POST_TILE, D_MODEL), tile(0)),
            _mod_spec(layer, row_fn, 2, POST_TILE),
            const(p_a), const(p_b), const(w_out),
        ],
        out_specs=pl.BlockSpec((POST_TILE, D_MODEL), tile(0)),
        out_shape=jax.ShapeDtypeStruct((n_tok, D_MODEL), F32),
        compiler_params=_params(("arbitrary", "arbitrary")),
        name="mixer_output",
    )(o_a, o_b, proj, proj, x, mods, p_a, p_b, w_out)


def _conv3(u, wc, bc, seq_len):
    n = u.shape[0]
    row = lax.broadcasted_iota(jnp.int32, (SUBLANES, 1), 0)

    def zero_row(x, r0, r):
        return jnp.where(row == r, 0.0, x[r0:r0 + SUBLANES])

    prev, nxt = pltpu.roll(u, 1, axis=0), pltpu.roll(u, n - 1, axis=0)
    p_parts, n_parts = [], []
    for s0 in range(0, n, seq_len):
        s1 = s0 + seq_len
        p_parts += [zero_row(prev, s0, 0), prev[s0 + SUBLANES:s1]]
        n_parts += [nxt[s0:s1 - SUBLANES], zero_row(nxt, s1 - SUBLANES, SUBLANES - 1)]
    prev, nxt = jnp.concatenate(p_parts, axis=0), jnp.concatenate(n_parts, axis=0)
    return prev * wc[0:1] + u * wc[1:2] + nxt * wc[2:3] + bc


def _ffn_kernel(*refs, seq_len, final):
    it = iter(refs)
    (x_ref, sh_ref, sc_ref, gt_ref, n2_ref, wa_ref, wg_ref, wca_ref, wcg_ref, bca_ref, bcg_ref,
     wd_ref) = (next(it) for _ in range(12))
    fn_ref = next(it) if final else None
    o_ref, h_ref, acc_ref = next(it), next(it), next(it)
    j = pl.program_id(1)

    @pl.when(j == 0)
    def _():
        h_ref[...] = _norm_mod(x_ref[...], n2_ref[...], sc_ref[...], sh_ref[...]).astype(BF16)
        acc_ref[...] = jnp.zeros_like(acc_ref)

    h = h_ref[...]
    u_a = jnp.dot(h, wa_ref[...], preferred_element_type=F32)
    u_g = jnp.dot(h, wg_ref[...], preferred_element_type=F32)
    a = _conv3(u_a, wca_ref[...], bca_ref[...], seq_len)
    g = _conv3(u_g, wcg_ref[...], bcg_ref[...], seq_len)
    acc_ref[...] += jnp.dot((_silu(g) * a).astype(BF16), wd_ref[...], preferred_element_type=F32)

    @pl.when(j == pl.num_programs(1) - 1)
    def _():
        x = x_ref[...] + gt_ref[...] * acc_ref[...]
        if final:
            x = x * lax.rsqrt(jnp.mean(x * x, axis=-1, keepdims=True) + EPS) * fn_ref[...]
        o_ref[...] = x


def _ffn_call(x, mods, n2, w_up, w_conv, b_conv, w_down, final_norm, layer, row_fn, seq_len):
    n_tok = x.shape[0]
    n_ff = D_FF // FF_CHUNK
    lspec = functools.partial(_layer_spec, layer)
    in_specs = [
        pl.BlockSpec((TOKEN_TILE, D_MODEL), lambda i, j: (i, 0)),
        _mod_spec(layer, row_fn, 3), _mod_spec(layer, row_fn, 4), _mod_spec(layer, row_fn, 5),
        lspec((1, D_MODEL), lambda i, j: (0, 0)),
        lspec((D_MODEL, FF_CHUNK), lambda i, j: (0, j)),
        lspec((D_MODEL, FF_CHUNK), lambda i, j: (0, n_ff + j)),
        lspec((3, FF_CHUNK), lambda i, j: (0, j)),
        lspec((3, FF_CHUNK), lambda i, j: (0, n_ff + j)),
        lspec((1, FF_CHUNK), lambda i, j: (0, j)),
        lspec((1, FF_CHUNK), lambda i, j: (0, n_ff + j)),
        lspec((FF_CHUNK, D_MODEL), lambda i, j: (j, 0)),
    ]
    args = [x, mods, mods, mods, n2, w_up, w_up, w_conv, w_conv, b_conv, b_conv, w_down]
    final = final_norm is not None
    if final:
        in_specs.append(pl.BlockSpec((1, D_MODEL), lambda i, j: (0, 0)))
        args.append(final_norm.reshape(1, D_MODEL))
    return pl.pallas_call(
        functools.partial(_ffn_kernel, seq_len=seq_len, final=final),
        grid=(n_tok // TOKEN_TILE, n_ff),
        in_specs=in_specs,
        out_specs=pl.BlockSpec((TOKEN_TILE, D_MODEL), lambda i, j: (i, 0)),
        out_shape=jax.ShapeDtypeStruct((n_tok, D_MODEL), F32),
        scratch_shapes=[pltpu.VMEM((TOKEN_TILE, D_MODEL), BF16),
                        pltpu.VMEM((TOKEN_TILE, D_MODEL), F32)],
        compiler_params=_params(("arbitrary", "arbitrary")),
        name="conv_ffn",
    )(*args)


def kernel(x_prompt, x_sample, state_hgrn, state_ret, c, c_ctx, norm1, norm2, final_norm,
           w_mod, b_mod, w_in, hgrn_lb_raw, p_a, p_b, w_out, w_up, w_conv, b_conv, w_down):
    n_ctx, t_ctx, _ = x_prompt.shape
    n_dec, t_dec, _ = x_sample.shape
    assert t_ctx & (t_ctx - 1) == 0 and TOKEN_TILE % t_ctx == 0 and t_dec == TOKEN_TILE

    sm = jax.nn.softmax(hgrn_lb_raw.astype(F32), axis=0)
    cum = jnp.cumsum(sm, axis=0)
    lower_bounds = cum - cum[0:1]

    cvec = jnp.concatenate(
        [c_ctx[None, :], c, jnp.zeros((MOD_ROWS - 1 - n_dec, D_MODEL), F32)], axis=0)
    mods = _mod_call(cvec, w_mod, b_mod).reshape(DEPTH, MOD_ROWS, 1, 6 * D_MODEL)

    p_a_b, p_b_b, w_out_b, w_up_b, w_down_b = (
        w.astype(BF16) for w in (p_a, p_b, w_out, w_up, w_down))
    norm1_3, norm2_3 = norm1.reshape(DEPTH, 1, D_MODEL), norm2.reshape(DEPTH, 1, D_MODEL)
    b_conv_3 = b_conv.reshape(DEPTH, 1, 2 * D_FF)

    ctx_row = lambda tok: 0
    dec_row = lambda tok: tok // t_dec + 1

    def layer(x, l, n_b, seq_len, row_fn, s_hgrn, s_ret, prev_h, prev_r, latent):
        proj, z = _proj_call(x, mods, norm1_3, w_in, l, row_fn)
        proj3 = proj.reshape(n_b, seq_len, PROJ_WIDTH)
        o_a, st_h = _hgrn_call(proj3, z.reshape(n_b, seq_len, -1), lower_bounds, s_hgrn, prev_h, l)
        o_b, st_r = _ret_call(proj3, s_ret, prev_r, l, latent)
        x = _post_call(o_a.reshape(-1, H_A * DV_A), o_b.reshape(-1, H_B * DV_B), proj, x,
                       mods, p_a_b, p_b_b, w_out_b, l, row_fn)
        x = _ffn_call(x, mods, norm2_3, w_up_b, w_conv, b_conv_3, w_down_b,
                      final_norm if l == DEPTH - 1 else None, l, row_fn, seq_len)
        return x, st_h, st_r

    x = x_prompt.reshape(n_ctx * t_ctx, D_MODEL)
    st_h = st_r = None
    for l in range(DEPTH):
        x, st_h, st_r = layer(x, l, n_ctx, t_ctx, ctx_row, None, None, st_h, st_r, False)
    y_prompt = x.reshape(n_ctx, t_ctx, D_MODEL)

    x = x_sample.reshape(n_dec * t_dec, D_MODEL)
    for l in range(DEPTH):
        x, _, _ = layer(x, l, n_dec, t_dec, dec_row, state_hgrn, state_ret, None, None, True)
    y_sample = x.reshape(n_dec, t_dec, D_MODEL)
    return (y_prompt, y_sample, st_h, st_r)
```

```python
import functools

import numpy as np
import jax
import jax.numpy as jnp
from jax import lax
from jax.experimental import pallas as pl
from jax.experimental.pallas import tpu as pltpu

F32 = jnp.float32
BF16 = jnp.bfloat16

D_MODEL = 1024
DEPTH = 2
GRID_W = 64
H_A, DK_A, DV_A = 8, 128, 128
H_B, DK_B, DV_B = 4, 256, 512
D_FF = 2816
ROPE_BASE = 10000.0
K_SCALE = DK_B ** -0.5
EPS = 1e-6
RET_DECAY_OFFSET_BWD = 0.5
IN_WIDTH = 13312

W_OFF_ZF, W_OFF_ZB, W_OFF_IA = 1024, 2048, 3072
OFF_QA, OFF_IA, OFF_GA = 0, 1024, 2048
OFF_QB, OFF_KB, OFF_VB, OFF_GB = 3072, 4096, 5120, 7168
OFF_GATE_A, OFF_GATE_B = 9216, 10240
PROJ_WIDTH = 11264

MOD_ROWS = 8
TOKEN_TILE = 1024
POST_TILE = 512
POST_SUB = 256
FF_CHUNK = 1408
FF_SUB = 256
HGRN_CHUNK = 128
HGRN_UNITS = 16
LEAF_LEVEL = 5
LEAF_MAX_DECAY = 60.0
RET_ROWS = 256
PROJ_TN = 1024
Z_TILE0 = W_OFF_ZF // PROJ_TN
Z_TILES = (W_OFF_IA - W_OFF_ZF) // PROJ_TN
VMEM_LIMIT = 52 * 1024 * 1024
SUBLANES = 8
LOG2E = 1.4426950408889634

_NT = (((1,), (1,)), ((), ()))
_TN = (((0,), (0,)), ((), ()))


def _params(sem):
    return pltpu.CompilerParams(dimension_semantics=sem, vmem_limit_bytes=VMEM_LIMIT)


def _sigmoid(x):
    return jax.nn.sigmoid(x)


def _silu(x):
    return x * jax.nn.sigmoid(x)


def _mod_kernel(c_ref, w_ref, b_ref, o_ref):
    s = _silu(c_ref[...])
    o_ref[...] = jnp.dot(s, w_ref[...], precision=lax.Precision.HIGHEST,
                         preferred_element_type=F32) + b_ref[...]


def _mod_call(cvec, w_mod, b_mod):
    n_col = 6 * D_MODEL // 1024
    return pl.pallas_call(
        _mod_kernel,
        grid=(DEPTH, n_col),
        in_specs=[
            pl.BlockSpec((MOD_ROWS, D_MODEL), lambda l, j: (0, 0)),
            pl.BlockSpec((None, D_MODEL, 1024), lambda l, j: (l, 0, j)),
            pl.BlockSpec((None, 1, 1024), lambda l, j: (l, 0, j)),
        ],
        out_specs=pl.BlockSpec((None, MOD_ROWS, 1024), lambda l, j: (l, 0, j)),
        out_shape=jax.ShapeDtypeStruct((DEPTH, MOD_ROWS, 6 * D_MODEL), F32),
        compiler_params=_params(("arbitrary", "arbitrary")),
        name="modulation",
    )(cvec, w_mod, b_mod.reshape(DEPTH, 1, 6 * D_MODEL))


def _mod_spec(layer, row_fn, which, tile=TOKEN_TILE):
    return pl.BlockSpec((None, None, 1, D_MODEL),
                        lambda i, j: (layer, row_fn(i * tile), 0, which))


def _layer_spec(layer, block, index_map):
    return pl.BlockSpec((None,) + block, lambda *g: (layer,) + index_map(*g))


def _norm_mod(x, g, sc, sh):
    y = x * lax.rsqrt(jnp.mean(x * x, axis=-1, keepdims=True) + EPS) * g
    return y * (1.0 + sc) + sh


def _proj_kernel(x_ref, sh_ref, sc_ref, n_ref, w_ref, o_ref, hout_ref, h_ref, wb_ref):
    j, i = pl.program_id(0), pl.program_id(1)

    @pl.when(i == 0)
    def _():
        wb_ref[...] = w_ref[...].astype(BF16)

    @pl.when(j == 0)
    def _():
        h = _norm_mod(x_ref[...], n_ref[...], sc_ref[...], sh_ref[...]).astype(BF16)
        h_ref[i] = h
        hout_ref[...] = h

    o_ref[...] = jnp.dot(h_ref[i], wb_ref[...], preferred_element_type=F32).astype(BF16)


def _zproj_kernel(h_ref, w_ref, z_ref, wb_ref):
    @pl.when(pl.program_id(1) == 0)
    def _():
        wb_ref[...] = w_ref[...].astype(BF16)

    z_ref[...] = jnp.dot(h_ref[...], wb_ref[...], preferred_element_type=F32)


def _proj_call(x, mods, n1, w_in, layer, row_fn):
    n_tok = x.shape[0]
    tn = PROJ_TN
    n_tile = n_tok // TOKEN_TILE
    hold = lambda j, i: jnp.where(j == 0, i, n_tile - 1)
    w_tile = lambda j: jnp.where(j >= Z_TILE0, j + Z_TILES, j)
    mod = lambda which: pl.BlockSpec(
        (None, None, 1, D_MODEL), lambda j, i: (layer, row_fn(i * TOKEN_TILE), 0, which))
    proj, h = pl.pallas_call(
        _proj_kernel,
        grid=(PROJ_WIDTH // tn, n_tile),
        in_specs=[
            pl.BlockSpec((TOKEN_TILE, D_MODEL), lambda j, i: (hold(j, i), 0)),
            mod(0), mod(1),
            _layer_spec(layer, (1, D_MODEL), lambda j, i: (0, 0)),
            _layer_spec(layer, (D_MODEL, tn), lambda j, i: (0, w_tile(j))),
        ],
        out_specs=[
            pl.BlockSpec((TOKEN_TILE, tn), lambda j, i: (i, j)),
            pl.BlockSpec((TOKEN_TILE, D_MODEL), lambda j, i: (hold(j, i), 0)),
        ],
        out_shape=[jax.ShapeDtypeStruct((n_tok, PROJ_WIDTH), BF16),
                   jax.ShapeDtypeStruct((n_tok, D_MODEL), BF16)],
        scratch_shapes=[pltpu.VMEM((n_tile, TOKEN_TILE, D_MODEL), BF16),
                        pltpu.VMEM((D_MODEL, tn), BF16)],
        compiler_params=_params(("arbitrary", "arbitrary")),
        name="in_projection",
    )(x, mods, mods, n1, w_in)
    z = pl.pallas_call(
        _zproj_kernel,
        grid=(Z_TILES, n_tile),
        in_specs=[
            pl.BlockSpec((TOKEN_TILE, D_MODEL), lambda j, i: (i, 0)),
            _layer_spec(layer, (D_MODEL, tn), lambda j, i: (0, Z_TILE0 + j)),
        ],
        out_specs=pl.BlockSpec((TOKEN_TILE, tn), lambda j, i: (i, j)),
        out_shape=jax.ShapeDtypeStruct((n_tok, Z_TILES * tn), F32),
        scratch_shapes=[pltpu.VMEM((D_MODEL, tn), BF16)],
        compiler_params=_params(("arbitrary", "arbitrary")),
        name="forget_projection",
    )(h, w_in)
    return proj, z


def _hgrn_consts(c):
    nl = int(np.log2(c))
    t = np.arange(c)
    tt, rr = t[:, None], t[None, :]
    x = tt ^ rr
    lev_of = np.where(x > 0, np.floor(np.log2(np.maximum(x, 1))) + 1, 0).astype(np.int32)
    lv_f = np.where(tt >= rr, lev_of, -1).astype(np.int32)
    lv_b = lv_f.T.copy()
    return ((rr <= tt).astype(np.float32), (rr >= tt).astype(np.float32), lv_f, lv_b, nl)


def _leaf_factors(cum, cum_row, reverse):
    c, dk = cum.shape
    leaf = 1 << LEAF_LEVEL
    pieces = []
    for base in range(0, c, leaf):
        row = base + leaf if reverse else base - 1
        if 0 <= row < c:
            pieces.append(jnp.broadcast_to(cum_row(row), (leaf, dk)))
        else:
            pieces.append(jnp.zeros((leaf, dk), F32))
    x = cum - jnp.concatenate(pieces, axis=0)
    return jnp.exp2(x * LOG2E).astype(BF16), jnp.exp2(x * (-LOG2E)).astype(BF16)


def _level_factor(la, cum, cum_row, lev, reverse):
    c, dk = la.shape
    if lev == 1:
        row = lax.broadcasted_iota(jnp.int32, (c, 1), 0)
        return jnp.exp(jnp.where((row & 1) == (0 if reverse else 1), la, 0.0)).astype(BF16)
    blk, half = 1 << lev, 1 << (lev - 1)

    def boundary(r):
        base = (r // blk) * blk
        return jnp.broadcast_to(cum_row(base + half if reverse else base + half - 1),
                                (SUBLANES, dk))

    top = lax.broadcasted_iota(jnp.int32, (SUBLANES, dk), 0) < SUBLANES // 2
    pieces = []
    for r0 in range(0, c, SUBLANES):
        if blk >= SUBLANES:
            pieces.append(boundary(r0))
        else:
            pieces.append(jnp.where(top, boundary(r0), boundary(r0 + SUBLANES // 2)))
    return jnp.exp2(jnp.abs(cum - jnp.concatenate(pieces, axis=0)) * (-LOG2E)).astype(BF16)


def _hgrn_forget(z, lb):
    e = jnp.exp(-jnp.abs(z))
    one_e = 1.0 + e
    r = 1.0 / one_e
    pos = z >= 0.0
    sig = jnp.where(pos, 1.0, e) * r
    sig_neg = jnp.where(pos, e, 1.0) * r
    oml = 1.0 - lb
    log_sig = jnp.minimum(z, 0.0) - jnp.log(one_e)
    log_f = jnp.where(lb > 0.0, jnp.log(lb + oml * sig), log_sig)
    return log_f, oml * sig_neg


def _hgrn_kernel(*refs, seq_len, chunk, n_lev, heads, has_s0, n_prev):
    it = iter(refs)
    (q_ref, zf_ref, zb_ref, i_ref, g_ref, lb_ref, trif_ref, trib_ref, lvf_ref, lvb_ref,
     leaf_ref) = (next(it) for _ in range(11))
    s0_ref = next(it) if has_s0 else None
    prev_ref = next(it) if n_prev else None
    o_ref = next(it)
    st_ref = None if has_s0 else next(it)
    la_ref, k_ref, cum_ref, qt_ref, kt_ref, qd_ref, kd_ref, ae_ref, s_ref, acc_ref = (
        next(it) for _ in range(10))
    n_chunks = seq_len // chunk
    dk = DK_A
    units = [(2 * hh + d, hh, d, d == 1, z_ref, tri_ref, lv_ref)
             for hh in range(heads)
             for d, (z_ref, tri_ref, lv_ref) in enumerate(((zf_ref, trif_ref, lvf_ref),
                                                          (zb_ref, trib_ref, lvb_ref)))]
    rows_of = lambda c: slice(c * chunk, (c + 1) * chunk)
    cols_of = lambda hh: slice(hh * dk, (hh + 1) * dk)
    lb = lb_ref[...]

    worst = None
    for u, hh, d, _, z_ref, _, _ in units:
        la, key = _hgrn_forget(z_ref[:, cols_of(hh)], lb[d:d + 1, cols_of(hh)])
        la_ref[u] = la
        k_ref[u] = key.astype(BF16)
        leaf_sum = jnp.dot(leaf_ref[...], la.astype(BF16), preferred_element_type=F32)
        worst = leaf_sum if worst is None else jnp.minimum(worst, leaf_sum)
    leafwise_ok = jnp.min(worst) > -LEAF_MAX_DECAY

    for u, _, _, _, _, tri_ref, _ in units:
        tri = tri_ref[...]
        for c in range(n_chunks):
            la = la_ref[u, rows_of(c), :]
            la_hi = la.astype(BF16)
            la_lo = (la - la_hi.astype(F32)).astype(BF16)
            x2 = jnp.dot(tri, jnp.concatenate([la_hi, la_lo], axis=1),
                         preferred_element_type=F32)
            cum_ref[u, rows_of(c), :] = x2[:, :dk] + x2[:, dk:]

    def operands_and_scores(leafwise):
        if leafwise:
            levels = list(range(LEAF_LEVEL + 1, n_lev + 1))
        else:
            levels = list(range(1, n_lev + 1))
        n_terms = len(levels) + 1
        for u, hh, _, reverse, _, _, _ in units:
            for c in range(n_chunks):
                rows = rows_of(c)
                cum = cum_ref[u, rows, :]
                la = la_ref[u, rows, :]
                q = q_ref[rows, cols_of(hh)]
                k = k_ref[u, rows, :]
                cum_row = lambda r, u=u, c=c: cum_ref[u, c * chunk + r:c * chunk + r + 1, :]
                if leafwise:
                    factors = [_leaf_factors(cum, cum_row, reverse)]
                else:
                    factors = [None]
                for lev in levels:
                    e = _level_factor(la, cum, cum_row, lev, reverse)
                    factors.append((e, e))
                for t, f in enumerate(factors):
                    cols = slice(t * dk, (t + 1) * dk)
                    qt_ref[u, rows, cols] = q if f is None else q * f[0]
                    kt_ref[u, rows, cols] = k if f is None else k * f[1]
                end_row = 0 if reverse else chunk - 1
                cum_end = cum[end_row:end_row + 1]
                qd_ref[u, rows, :] = q * jnp.exp(cum).astype(BF16)
                kd_ref[u, rows, :] = k * jnp.exp(cum_end - cum).astype(BF16)
                ae_ref[u, c:c + 1, :] = jnp.exp(cum_end)
        zero = jnp.zeros((chunk, dk), BF16)
        for u, _, _, _, _, _, lv_ref in units:
            lv = lv_ref[...]
            if leafwise:
                masks = [(lv >= 0) & (lv <= LEAF_LEVEL)]
            else:
                masks = [lv == 0]
            masks += [lv == lev for lev in levels]
            for c in range(n_chunks):
                rows = rows_of(c)
                scores = jnp.zeros((chunk, chunk), F32)
                for t in range(0, n_terms, 2):
                    if t + 1 < n_terms:
                        k_a = kt_ref[u, rows, t * dk:(t + 1) * dk]
                        k_b = kt_ref[u, rows, (t + 1) * dk:(t + 2) * dk]
                        p = lax.dot_general(
                            qt_ref[u, rows, t * dk:(t + 2) * dk],
                            jnp.concatenate([jnp.concatenate([k_a, zero], axis=1),
                                             jnp.concatenate([zero, k_b], axis=1)], axis=0),
                            _NT, preferred_element_type=F32)
                        scores = jnp.where(masks[t], p[:, :chunk], scores)
                        scores = jnp.where(masks[t + 1], p[:, chunk:], scores)
                    else:
                        p = lax.dot_general(qt_ref[u, rows, t * dk:(t + 1) * dk],
                                            kt_ref[u, rows, t * dk:(t + 1) * dk],
                                            _NT, preferred_element_type=F32)
                        scores = jnp.where(masks[t], p, scores)
                s_ref[u, rows, :] = scores.astype(BF16)

    pl.when(leafwise_ok)(lambda: operands_and_scores(True))
    pl.when(jnp.logical_not(leafwise_ok))(lambda: operands_and_scores(False))

    incs = {u: [lax.dot_general(i_ref[rows_of(c), cols_of(hh)], kd_ref[u, rows_of(c), :], _TN,
                                preferred_element_type=F32) for c in range(n_chunks)]
            for u, hh, _, _, _, _, _ in units}
    finals, befores = {}, {}
    for u, hh, d, reverse, _, _, _ in units:
        st = s0_ref[d, hh].T if has_s0 else jnp.zeros((DV_A, dk), F32)
        befores[u] = [None] * n_chunks
        for c in (range(n_chunks - 1, -1, -1) if reverse else range(n_chunks)):
            befores[u][c] = st.astype(BF16)
            st = st * ae_ref[u, c:c + 1, :] + incs[u][c]
        finals[u] = st
    for u, hh, d, reverse, _, _, _ in units:
        for c in range(n_chunks):
            rows = rows_of(c)
            o = jnp.dot(s_ref[u, rows, :], i_ref[rows, cols_of(hh)], preferred_element_type=F32)
            if has_s0 or c != (n_chunks - 1 if reverse else 0):
                o = o + lax.dot_general(qd_ref[u, rows, :], befores[u][c], _NT,
                                        preferred_element_type=F32)
            if d == 0:
                acc_ref[rows, cols_of(hh)] = o
            else:
                acc_ref[rows, cols_of(hh)] += o

    for hh in range(heads):
        o = acc_ref[:, cols_of(hh)]
        o = o * lax.rsqrt(jnp.mean(o * o, axis=-1, keepdims=True) + EPS)
        o_ref[:, cols_of(hh)] = (o * _silu(g_ref[:, cols_of(hh)].astype(F32))).astype(BF16)
    if st_ref is not None:
        if n_prev:
            st_ref[0:n_prev] = prev_ref[...]
        for u, hh, d, _, _, _, _ in units:
            st_ref[n_prev, d, hh] = finals[u].T


def _hgrn_call(proj, z, lb, state, prev, layer):
    n_b, seq_len, _ = proj.shape
    tri_f, tri_b, lv_f, lv_b, n_lev = _hgrn_consts(HGRN_CHUNK)
    has_s0 = state is not None
    n_chunks = seq_len // HGRN_CHUNK
    heads = max(1, HGRN_UNITS // (2 * n_chunks))
    width = heads * DK_A
    col = lambda off: (lambda b, h: (b, 0, off // width + h))
    const = lambda a: pl.BlockSpec(a.shape, lambda b, h: (0, 0))
    in_specs = [pl.BlockSpec((None, seq_len, width), col(o))
                for o in (OFF_QA, 0, W_OFF_ZB - W_OFF_ZF, OFF_IA, OFF_GA)]
    leaf = 1 << LEAF_LEVEL
    leaf_rows = -(-(seq_len // leaf) // SUBLANES) * SUBLANES
    leaf_ind = (np.arange(seq_len)[None, :] // leaf == np.arange(leaf_rows)[:, None])
    in_specs += [_layer_spec(layer, (2, width), lambda b, h: (0, h)),
                 const(tri_f), const(tri_b), const(lv_f), const(lv_b), const(leaf_ind)]
    args = [proj, z, z, proj, proj, lb, jnp.asarray(tri_f, BF16), jnp.asarray(tri_b, BF16),
            jnp.asarray(lv_f), jnp.asarray(lv_b), jnp.asarray(leaf_ind, BF16)]
    out_specs = [pl.BlockSpec((None, seq_len, width), lambda b, h: (b, 0, h))]
    out_shape = [jax.ShapeDtypeStruct((n_b, seq_len, H_A * DV_A), BF16)]
    if has_s0:
        in_specs.append(pl.BlockSpec((None, None, 2, heads, DK_A, DV_A),
                                     lambda b, h: (b, layer, 0, h, 0, 0)))
        args.append(state)
    else:
        st_spec = lambda n: pl.BlockSpec((None, n, 2, heads, DK_A, DV_A),
                                         lambda b, h: (b, 0, 0, h, 0, 0))
        if layer:
            in_specs.append(st_spec(layer))
            args.append(prev)
        out_specs.append(st_spec(layer + 1))
        out_shape.append(jax.ShapeDtypeStruct((n_b, layer + 1, 2, H_A, DK_A, DV_A), F32))
    n_u = 2 * heads
    res = pl.pallas_call(
        functools.partial(_hgrn_kernel, seq_len=seq_len, chunk=HGRN_CHUNK, n_lev=n_lev,
                          heads=heads, has_s0=has_s0, n_prev=0 if has_s0 else layer),
        grid=(n_b, H_A // heads),
        in_specs=in_specs, out_specs=out_specs, out_shape=out_shape,
        scratch_shapes=[
            pltpu.VMEM((n_u, seq_len, DK_A), F32),
            pltpu.VMEM((n_u, seq_len, DK_A), BF16),
            pltpu.VMEM((n_u, seq_len, DK_A), F32),
            pltpu.VMEM((n_u, seq_len, (n_lev + 1) * DK_A), BF16),
            pltpu.VMEM((n_u, seq_len, (n_lev + 1) * DK_A), BF16),
            pltpu.VMEM((n_u, seq_len, DK_A), BF16),
            pltpu.VMEM((n_u, seq_len, DK_A), BF16),
            pltpu.VMEM((n_u, max(SUBLANES, n_chunks), DK_A), F32),
            pltpu.VMEM((n_u, seq_len, HGRN_CHUNK), BF16),
            pltpu.VMEM((seq_len, width), F32),
        ],
        compiler_params=_params(("arbitrary", "arbitrary")),
        name="hgrn2_scan",
    )(*args)
    return (res[0], None) if has_s0 else (res[0], res[1])


def _rope_tables(seq_len):
    rows = seq_len // GRID_W
    r_idx = jnp.repeat(jnp.arange(rows), GRID_W).astype(F32)
    c_idx = jnp.tile(jnp.arange(GRID_W), rows).astype(F32)
    quarter = DK_B // 4
    inv = 1.0 / (ROPE_BASE ** (jnp.arange(quarter, dtype=F32) / quarter))
    ang_r = r_idx[:, None] * inv[None, :]
    ang_c = c_idx[:, None] * inv[None, :]
    cos = jnp.concatenate([jnp.cos(ang_r)] * 2 + [jnp.cos(ang_c)] * 2, axis=1)
    sin = jnp.concatenate([-jnp.sin(ang_r), jnp.sin(ang_r), -jnp.sin(ang_c), jnp.sin(ang_c)],
                          axis=1)
    return cos, sin


def _rope(x, cos, sin):
    half = DK_B // 2
    swapped = jnp.concatenate([pltpu.roll(x[:, :half], half // 2, axis=1),
                               pltpu.roll(x[:, half:], half // 2, axis=1)], axis=1)
    return x * cos + swapped * sin


def _ret_kernel(*refs, seq_len, latent, has_s0, n_prev):
    it = iter(refs)
    lg_ref, q_ref, k_ref, v_ref, g_ref = (next(it) for _ in range(5))
    cos_ref, sin_ref = (next(it), next(it)) if latent else (None, None)
    s0_ref = next(it) if has_s0 else None
    prev_ref = next(it) if n_prev else None
    o_ref = next(it)
    st_ref = None if has_s0 else next(it)
    dm_ref = next(it)
    h = pl.program_id(0)
    lg_f = lg_ref[h, 0]
    lg_b = lg_ref[h, 1]

    @pl.when(pl.program_id(1) == 0)
    def _():
        t = lax.broadcasted_iota(jnp.int32, (seq_len, seq_len), 0)
        s = lax.broadcasted_iota(jnp.int32, (seq_len, seq_len), 1)
        d = (t - s).astype(F32)
        dm_ref[...] = K_SCALE * (
            jnp.where(d >= 0.0, jnp.exp(lg_f * jnp.maximum(d, 0.0)), 0.0)
            + jnp.where(d <= 0.0, jnp.exp(lg_b * jnp.maximum(-d, 0.0)), 0.0))

    q, k = q_ref[...], k_ref[...]
    if latent:
        cos, sin = cos_ref[...], sin_ref[...]
        q, k = _rope(q.astype(F32), cos, sin), _rope(k.astype(F32), cos, sin)
    k_b = k.astype(BF16)
    v_b = v_ref[...]
    if has_s0:
        s0 = jnp.concatenate([s0_ref[0], s0_ref[1]], axis=0).astype(BF16)
    blocks = [slice(r, r + RET_ROWS) for r in range(0, seq_len, RET_ROWS)]
    ps = [lax.dot_general(q[rows].astype(BF16), k_b, _NT, preferred_element_type=F32)
          for rows in blocks]
    carried = []
    if has_s0:
        for rows in blocks:
            pos = (lax.broadcasted_iota(jnp.int32, (RET_ROWS, 1), 0) + rows.start).astype(F32)
            q_s = jnp.concatenate([q[rows] * jnp.exp(lg_f * (pos + 1.0)),
                                   q[rows] * jnp.exp(lg_b * (seq_len - pos))], axis=1)
            carried.append(jnp.dot(q_s.astype(BF16), s0, preferred_element_type=F32))
    pds = [(p * dm_ref[rows, :]).astype(BF16) for p, rows in zip(ps, blocks)]
    outs = [jnp.dot(pd, v_b, preferred_element_type=F32) for pd in pds]
    for i, rows in enumerate(blocks):
        o = outs[i] + carried[i] if has_s0 else outs[i]
        o = o * lax.rsqrt(jnp.mean(o * o, axis=-1, keepdims=True) + EPS)
        o_ref[rows, :] = (o * _silu(g_ref[rows, :].astype(F32))).astype(BF16)
    if st_ref is not None:
        if n_prev:
            st_ref[0:n_prev] = prev_ref[...]
        pos = lax.broadcasted_iota(jnp.int32, (seq_len, 1), 0).astype(F32)
        k_f = (k * (K_SCALE * jnp.exp(lg_f * (seq_len - 1.0 - pos)))).astype(BF16)
        k_r = (k * (K_SCALE * jnp.exp(lg_b * pos))).astype(BF16)
        st_ref[n_prev, 0] = lax.dot_general(k_f, v_b, _TN, preferred_element_type=F32)
        st_ref[n_prev, 1] = lax.dot_general(k_r, v_b, _TN, preferred_element_type=F32)


def _ret_log_decay():
    heads = jnp.arange(H_B, dtype=F32)
    lg_f = jnp.log1p(-jnp.exp2(-5.0 - heads))
    lg_b = jnp.log1p(-jnp.exp2(-(5.0 + RET_DECAY_OFFSET_BWD) - heads))
    return jnp.stack([lg_f, lg_b], axis=1)


def _ret_call(proj, state, prev, layer, latent):
    n_b, seq_len, _ = proj.shape
    has_s0 = state is not None
    in_specs = [
        pl.BlockSpec(memory_space=pltpu.SMEM),
        pl.BlockSpec((None, seq_len, DK_B), lambda h, b: (b, 0, OFF_QB // DK_B + h)),
        pl.BlockSpec((None, seq_len, DK_B), lambda h, b: (b, 0, OFF_KB // DK_B + h)),
        pl.BlockSpec((None, seq_len, DV_B), lambda h, b: (b, 0, OFF_VB // DV_B + h)),
        pl.BlockSpec((None, seq_len, DV_B), lambda h, b: (b, 0, OFF_GB // DV_B + h)),
    ]
    args = [_ret_log_decay(), proj, proj, proj, proj]
    if latent:
        cos, sin = _rope_tables(seq_len)
        in_specs += [pl.BlockSpec((seq_len, DK_B), lambda h, b: (0, 0))] * 2
        args += [cos, sin]
    out_specs = [pl.BlockSpec((None, seq_len, DV_B), lambda h, b: (b, 0, h))]
    out_shape = [jax.ShapeDtypeStruct((n_b, seq_len, H_B * DV_B), BF16)]
    if has_s0:
        in_specs.append(pl.BlockSpec((None, None, 2, None, DK_B, DV_B),
                                     lambda h, b: (b, layer, 0, h, 0, 0)))
        args.append(state)
    else:
        st_spec = lambda n: pl.BlockSpec((None, n, 2, None, DK_B, DV_B),
                                         lambda h, b: (b, 0, 0, h, 0, 0))
        if layer:
            in_specs.append(st_spec(layer))
            args.append(prev)
        out_specs.append(st_spec(layer + 1))
        out_shape.append(jax.ShapeDtypeStruct((n_b, layer + 1, 2, H_B, DK_B, DV_B), F32))
    res = pl.pallas_call(
        functools.partial(_ret_kernel, seq_len=seq_len, latent=latent, has_s0=has_s0,
                          n_prev=0 if has_s0 else layer),
        grid=(H_B, n_b),
        in_specs=in_specs, out_specs=out_specs, out_shape=out_shape,
        scratch_shapes=[pltpu.VMEM((seq_len, seq_len), F32)],
        compiler_params=_params(("arbitrary", "arbitrary")),
        name="retention_scan",
    )(*args)
    return (res[0], None) if has_s0 else (res[0], res[1])


def _post_kernel(oa_ref, ob_ref, gta_ref, gtb_ref, x_ref, gt1_ref, pa_ref, pb_ref, wo_ref,
                 o_ref):
    subs = [slice(r, r + POST_SUB) for r in range(0, POST_TILE, POST_SUB)]
    y_a = [jnp.dot(oa_ref[s, :], pa_ref[...], preferred_element_type=F32) for s in subs]
    y_b = [jnp.dot(ob_ref[s, :], pb_ref[...], preferred_element_type=F32) for s in subs]
    merged = [(_sigmoid(gta_ref[s, :].astype(F32)) * y_a[i]
               + _sigmoid(gtb_ref[s, :].astype(F32)) * y_b[i]).astype(BF16)
              for i, s in enumerate(subs)]
    y = [jnp.dot(m, wo_ref[...], preferred_element_type=F32) for m in merged]
    for i, s in enumerate(subs):
        o_ref[s, :] = x_ref[s, :] + gt1_ref[...] * y[i]


def _post_call(o_a, o_b, proj, x, mods, p_a, p_b, w_out, layer, row_fn):
    n_tok = x.shape[0]
    tile = lambda w: (lambda i, j: (i, w))
    pcol = lambda off: pl.BlockSpec((POST_TILE, 1024), tile(off // 1024))
    const = lambda a: _layer_spec(layer, a.shape[1:], lambda i, j: (0, 0))
    return pl.pallas_call(
        _post_kernel,
        grid=(n_tok // POST_TILE, 1),
        in_specs=[
            pl.BlockSpec((POST_TILE, H_A * DV_A), tile(0)),
            pl.BlockSpec((POST_TILE, H_B * DV_B), tile(0)),
            pcol(OFF_GATE_A), pcol(OFF_GATE_B),
            pl.BlockSpec((POST_TILE, D_MODEL), tile(0)),
            _mod_spec(layer, row_fn, 2, POST_TILE),
            const(p_a), const(p_b), const(w_out),
        ],
        out_specs=pl.BlockSpec((POST_TILE, D_MODEL), tile(0)),
        out_shape=jax.ShapeDtypeStruct((n_tok, D_MODEL), F32),
        compiler_params=_params(("arbitrary", "arbitrary")),
        name="mixer_output",
    )(o_a, o_b, proj, proj, x, mods, p_a, p_b, w_out)


def _conv3(u, wc, bc, seq_len):
    n = u.shape[0]
    row = lax.broadcasted_iota(jnp.int32, (SUBLANES, 1), 0)

    def zero_row(x, r0, r):
        return jnp.where(row == r, 0.0, x[r0:r0 + SUBLANES])

    prev, nxt = pltpu.roll(u, 1, axis=0), pltpu.roll(u, n - 1, axis=0)
    p_parts, n_parts = [], []
    for s0 in range(0, n, seq_len):
        s1 = s0 + seq_len
        p_parts += [zero_row(prev, s0, 0), prev[s0 + SUBLANES:s1]]
        n_parts += [nxt[s0:s1 - SUBLANES], zero_row(nxt, s1 - SUBLANES, SUBLANES - 1)]
    prev, nxt = jnp.concatenate(p_parts, axis=0), jnp.concatenate(n_parts, axis=0)
    return prev * wc[0:1] + u * wc[1:2] + nxt * wc[2:3] + bc


def _ffn_kernel(*refs, seq_len, final):
    it = iter(refs)
    (x_ref, sh_ref, sc_ref, gt_ref, n2_ref, wa_ref, wg_ref, wca_ref, wcg_ref, bca_ref, bcg_ref,
     wd_ref) = (next(it) for _ in range(12))
    fn_ref = next(it) if final else None
    o_ref, h_ref, acc_ref = next(it), next(it), next(it)
    j = pl.program_id(1)

    @pl.when(j == 0)
    def _():
        h_ref[...] = _norm_mod(x_ref[...], n2_ref[...], sc_ref[...], sh_ref[...]).astype(BF16)
        acc_ref[...] = jnp.zeros_like(acc_ref)

    h = h_ref[...]
    subs = [slice(c, min(c + FF_SUB, FF_CHUNK)) for c in range(0, FF_CHUNK, FF_SUB)]
    u_a = [jnp.dot(h, wa_ref[:, s], preferred_element_type=F32) for s in subs]
    u_g = [jnp.dot(h, wg_ref[:, s], preferred_element_type=F32) for s in subs]
    act = [(_silu(_conv3(u_g[i], wcg_ref[:, s], bcg_ref[:, s], seq_len))
            * _conv3(u_a[i], wca_ref[:, s], bca_ref[:, s], seq_len)).astype(BF16)
           for i, s in enumerate(subs)]
    acc_ref[...] += jnp.dot(jnp.concatenate(act, axis=1), wd_ref[...],
                            preferred_element_type=F32)

    @pl.when(j == pl.num_programs(1) - 1)
    def _():
        x = x_ref[...] + gt_ref[...] * acc_ref[...]
        if final:
            x = x * lax.rsqrt(jnp.mean(x * x, axis=-1, keepdims=True) + EPS) * fn_ref[...]
        o_ref[...] = x


def _ffn_call(x, mods, n2, w_up, w_conv, b_conv, w_down, final_norm, layer, row_fn, seq_len):
    n_tok = x.shape[0]
    n_ff = D_FF // FF_CHUNK
    lspec = functools.partial(_layer_spec, layer)
    in_specs = [
        pl.BlockSpec((TOKEN_TILE, D_MODEL), lambda i, j: (i, 0)),
        _mod_spec(layer, row_fn, 3), _mod_spec(layer, row_fn, 4), _mod_spec(layer, row_fn, 5),
        lspec((1, D_MODEL), lambda i, j: (0, 0)),
        lspec((D_MODEL, FF_CHUNK), lambda i, j: (0, j)),
        lspec((D_MODEL, FF_CHUNK), lambda i, j: (0, n_ff + j)),
        lspec((3, FF_CHUNK), lambda i, j: (0, j)),
        lspec((3, FF_CHUNK), lambda i, j: (0, n_ff + j)),
        lspec((1, FF_CHUNK), lambda i, j: (0, j)),
        lspec((1, FF_CHUNK), lambda i, j: (0, n_ff + j)),
        lspec((FF_CHUNK, D_MODEL), lambda i, j: (j, 0)),
    ]
    args = [x, mods, mods, mods, n2, w_up, w_up, w_conv, w_conv, b_conv, b_conv, w_down]
    final = final_norm is not None
    if final:
        in_specs.append(pl.BlockSpec((1, D_MODEL), lambda i, j: (0, 0)))
        args.append(final_norm.reshape(1, D_MODEL))
    return pl.pallas_call(
        functools.partial(_ffn_kernel, seq_len=seq_len, final=final),
        grid=(n_tok // TOKEN_TILE, n_ff),
        in_specs=in_specs,
        out_specs=pl.BlockSpec((TOKEN_TILE, D_MODEL), lambda i, j: (i, 0)),
        out_shape=jax.ShapeDtypeStruct((n_tok, D_MODEL), F32),
        scratch_shapes=[pltpu.VMEM((TOKEN_TILE, D_MODEL), BF16),
                        pltpu.VMEM((TOKEN_TILE, D_MODEL), F32)],
        compiler_params=_params(("arbitrary", "arbitrary")),
        name="conv_ffn",
    )(*args)


def kernel(x_prompt, x_sample, state_hgrn, state_ret, c, c_ctx, norm1, norm2, final_norm,
           w_mod, b_mod, w_in, hgrn_lb_raw, p_a, p_b, w_out, w_up, w_conv, b_conv, w_down):
    n_ctx, t_ctx, _ = x_prompt.shape
    n_dec, t_dec, _ = x_sample.shape
    assert t_ctx & (t_ctx - 1) == 0 and TOKEN_TILE % t_ctx == 0 and t_dec == TOKEN_TILE

    sm = jax.nn.softmax(hgrn_lb_raw.astype(F32), axis=0)
    cum = jnp.cumsum(sm, axis=0)
    lower_bounds = cum - cum[0:1]

    cvec = jnp.concatenate(
        [c_ctx[None, :], c, jnp.zeros((MOD_ROWS - 1 - n_dec, D_MODEL), F32)], axis=0)
    mods = _mod_call(cvec, w_mod, b_mod).reshape(DEPTH, MOD_ROWS, 1, 6 * D_MODEL)

    p_a_b, p_b_b, w_out_b, w_up_b, w_down_b = (
        w.astype(BF16) for w in (p_a, p_b, w_out, w_up, w_down))
    norm1_3, norm2_3 = norm1.reshape(DEPTH, 1, D_MODEL), norm2.reshape(DEPTH, 1, D_MODEL)
    b_conv_3 = b_conv.reshape(DEPTH, 1, 2 * D_FF)

    ctx_row = lambda tok: 0
    dec_row = lambda tok: tok // t_dec + 1

    def layer(x, l, n_b, seq_len, row_fn, s_hgrn, s_ret, prev_h, prev_r, latent):
        proj, z = _proj_call(x, mods, norm1_3, w_in, l, row_fn)
        proj3 = proj.reshape(n_b, seq_len, PROJ_WIDTH)
        o_a, st_h = _hgrn_call(proj3, z.reshape(n_b, seq_len, -1), lower_bounds, s_hgrn, prev_h, l)
        o_b, st_r = _ret_call(proj3, s_ret, prev_r, l, latent)
        x = _post_call(o_a.reshape(-1, H_A * DV_A), o_b.reshape(-1, H_B * DV_B), proj, x,
                       mods, p_a_b, p_b_b, w_out_b, l, row_fn)
        x = _ffn_call(x, mods, norm2_3, w_up_b, w_conv, b_conv_3, w_down_b,
                      final_norm if l == DEPTH - 1 else None, l, row_fn, seq_len)
        return x, st_h, st_r

    x = x_prompt.reshape(n_ctx * t_ctx, D_MODEL)
    st_h = st_r = None
    for l in range(DEPTH):
        x, st_h, st_r = layer(x, l, n_ctx, t_ctx, ctx_row, None, None, st_h, st_r, False)
    y_prompt = x.reshape(n_ctx, t_ctx, D_MODEL)

    x = x_sample.reshape(n_dec * t_dec, D_MODEL)
    for l in range(DEPTH):
        x, _, _ = layer(x, l, n_dec, t_dec, dec_row, state_hgrn, state_ret, None, None, True)
    y_sample = x.reshape(n_dec, t_dec, D_MODEL)
    return (y_prompt, y_sample, st_h, st_r)
```

```python
import functools

import numpy as np
import jax
import jax.numpy as jnp
from jax import lax
from jax.experimental import pallas as pl
from jax.experimental.pallas import tpu as pltpu

F32 = jnp.float32
BF16 = jnp.bfloat16

D_MODEL = 1024
DEPTH = 2
GRID_W = 64
H_A, DK_A, DV_A = 8, 128, 128
H_B, DK_B, DV_B = 4, 256, 512
D_FF = 2816
ROPE_BASE = 10000.0
K_SCALE = DK_B ** -0.5
EPS = 1e-6
RET_DECAY_OFFSET_BWD = 0.5
IN_WIDTH = 13312

W_OFF_ZF, W_OFF_ZB, W_OFF_IA = 1024, 2048, 3072
OFF_QA, OFF_IA, OFF_GA = 0, 1024, 2048
OFF_QB, OFF_KB, OFF_VB, OFF_GB = 3072, 4096, 5120, 7168
OFF_GATE_A, OFF_GATE_B = 9216, 10240
PROJ_WIDTH = 11264

MOD_ROWS = 8
TOKEN_TILE = 1024
POST_TILE = 512
POST_SUB = 256
FF_CHUNK = 1408
FF_SUB = 256
HGRN_CHUNK = 128
HGRN_UNITS = 16
LEAF_LEVEL = 5
LEAF_MAX_DECAY = 60.0
RET_ROWS = 256
PROJ_TN = 1024
Z_TILE0 = W_OFF_ZF // PROJ_TN
Z_TILES = (W_OFF_IA - W_OFF_ZF) // PROJ_TN
VMEM_LIMIT = 52 * 1024 * 1024
SUBLANES = 8
LOG2E = 1.4426950408889634

_NT = (((1,), (1,)), ((), ()))
_TN = (((0,), (0,)), ((), ()))


def _params(sem):
    return pltpu.CompilerParams(dimension_semantics=sem, vmem_limit_bytes=VMEM_LIMIT)


def _sigmoid(x):
    return jax.nn.sigmoid(x)


def _silu(x):
    return x * jax.nn.sigmoid(x)


def _mod_kernel(c_ref, w_ref, b_ref, o_ref):
    s = _silu(c_ref[...])
    o_ref[...] = jnp.dot(s, w_ref[...], precision=lax.Precision.HIGHEST,
                         preferred_element_type=F32) + b_ref[...]


def _mod_call(cvec, w_mod, b_mod):
    n_col = 6 * D_MODEL // 1024
    return pl.pallas_call(
        _mod_kernel,
        grid=(DEPTH, n_col),
        in_specs=[
            pl.BlockSpec((MOD_ROWS, D_MODEL), lambda l, j: (0, 0)),
            pl.BlockSpec((None, D_MODEL, 1024), lambda l, j: (l, 0, j)),
            pl.BlockSpec((None, 1, 1024), lambda l, j: (l, 0, j)),
        ],
        out_specs=pl.BlockSpec((None, MOD_ROWS, 1024), lambda l, j: (l, 0, j)),
        out_shape=jax.ShapeDtypeStruct((DEPTH, MOD_ROWS, 6 * D_MODEL), F32),
        compiler_params=_params(("arbitrary", "arbitrary")),
        name="modulation",
    )(cvec, w_mod, b_mod.reshape(DEPTH, 1, 6 * D_MODEL))


def _mod_spec(layer, row_fn, which, tile=TOKEN_TILE):
    return pl.BlockSpec((None, None, 1, D_MODEL),
                        lambda i, j: (layer, row_fn(i * tile), 0, which))


def _layer_spec(layer, block, index_map):
    return pl.BlockSpec((None,) + block, lambda *g: (layer,) + index_map(*g))


def _norm_mod(x, g, sc, sh):
    y = x * lax.rsqrt(jnp.mean(x * x, axis=-1, keepdims=True) + EPS) * g
    return y * (1.0 + sc) + sh


def _proj_kernel(x_ref, sh_ref, sc_ref, n_ref, w_ref, o_ref, hout_ref, h_ref, wb_ref):
    j, i = pl.program_id(0), pl.program_id(1)

    @pl.when(i == 0)
    def _():
        wb_ref[...] = w_ref[...].astype(BF16)

    @pl.when(j == 0)
    def _():
        h = _norm_mod(x_ref[...], n_ref[...], sc_ref[...], sh_ref[...]).astype(BF16)
        h_ref[i] = h
        hout_ref[...] = h

    o_ref[...] = jnp.dot(h_ref[i], wb_ref[...], preferred_element_type=F32).astype(BF16)


def _zproj_kernel(h_ref, w_ref, lb_ref, la_ref, key_ref, wb_ref):
    @pl.when(pl.program_id(1) == 0)
    def _():
        wb_ref[...] = w_ref[...].astype(BF16)

    h = h_ref[...]
    for c in range(0, PROJ_TN, FF_SUB):
        cols = slice(c, c + FF_SUB)
        z = jnp.dot(h, wb_ref[:, cols], preferred_element_type=F32)
        la, key = _hgrn_forget(z, lb_ref[:, cols])
        la_ref[:, cols] = la
        key_ref[:, cols] = key.astype(BF16)


def _proj_call(x, mods, n1, w_in, lb, layer, row_fn):
    n_tok = x.shape[0]
    tn = PROJ_TN
    n_tile = n_tok // TOKEN_TILE
    hold = lambda j, i: jnp.where(j == 0, i, n_tile - 1)
    w_tile = lambda j: jnp.where(j >= Z_TILE0, j + Z_TILES, j)
    mod = lambda which: pl.BlockSpec(
        (None, None, 1, D_MODEL), lambda j, i: (layer, row_fn(i * TOKEN_TILE), 0, which))
    proj, h = pl.pallas_call(
        _proj_kernel,
        grid=(PROJ_WIDTH // tn, n_tile),
        in_specs=[
            pl.BlockSpec((TOKEN_TILE, D_MODEL), lambda j, i: (hold(j, i), 0)),
            mod(0), mod(1),
            _layer_spec(layer, (1, D_MODEL), lambda j, i: (0, 0)),
            _layer_spec(layer, (D_MODEL, tn), lambda j, i: (0, w_tile(j))),
        ],
        out_specs=[
            pl.BlockSpec((TOKEN_TILE, tn), lambda j, i: (i, j)),
            pl.BlockSpec((TOKEN_TILE, D_MODEL), lambda j, i: (hold(j, i), 0)),
        ],
        out_shape=[jax.ShapeDtypeStruct((n_tok, PROJ_WIDTH), BF16),
                   jax.ShapeDtypeStruct((n_tok, D_MODEL), BF16)],
        scratch_shapes=[pltpu.VMEM((n_tile, TOKEN_TILE, D_MODEL), BF16),
                        pltpu.VMEM((D_MODEL, tn), BF16)],
        compiler_params=_params(("arbitrary", "arbitrary")),
        name="in_projection",
    )(x, mods, mods, n1, w_in)
    la, key = pl.pallas_call(
        _zproj_kernel,
        grid=(Z_TILES, n_tile),
        in_specs=[
            pl.BlockSpec((TOKEN_TILE, D_MODEL), lambda j, i: (i, 0)),
            _layer_spec(layer, (D_MODEL, tn), lambda j, i: (0, Z_TILE0 + j)),
            pl.BlockSpec((None, None, 1, tn), lambda j, i: (layer, j, 0, 0)),
        ],
        out_specs=[pl.BlockSpec((TOKEN_TILE, tn), lambda j, i: (i, j))] * 2,
        out_shape=[jax.ShapeDtypeStruct((n_tok, Z_TILES * tn), F32),
                   jax.ShapeDtypeStruct((n_tok, Z_TILES * tn), BF16)],
        scratch_shapes=[pltpu.VMEM((D_MODEL, tn), BF16)],
        compiler_params=_params(("arbitrary", "arbitrary")),
        name="forget_projection",
    )(h, w_in, lb.reshape(DEPTH, Z_TILES, 1, tn))
    return proj, la, key


def _hgrn_consts(c):
    nl = int(np.log2(c))
    t = np.arange(c)
    tt, rr = t[:, None], t[None, :]
    x = tt ^ rr
    lev_of = np.where(x > 0, np.floor(np.log2(np.maximum(x, 1))) + 1, 0).astype(np.int32)
    lv_f = np.where(tt >= rr, lev_of, -1).astype(np.int32)
    lv_b = lv_f.T.copy()
    return ((rr <= tt).astype(np.float32), (rr >= tt).astype(np.float32), lv_f, lv_b, nl)


def _leaf_factors(cum, cum_row, reverse):
    c, dk = cum.shape
    leaf = 1 << LEAF_LEVEL
    pieces = []
    for base in range(0, c, leaf):
        row = base + leaf if reverse else base - 1
        if 0 <= row < c:
            pieces.append(jnp.broadcast_to(cum_row(row), (leaf, dk)))
        else:
            pieces.append(jnp.zeros((leaf, dk), F32))
    x = cum - jnp.concatenate(pieces, axis=0)
    return jnp.exp2(x * LOG2E).astype(BF16), jnp.exp2(x * (-LOG2E)).astype(BF16)


def _level_factor(la, cum, cum_row, lev, reverse):
    c, dk = la.shape
    if lev == 1:
        row = lax.broadcasted_iota(jnp.int32, (c, 1), 0)
        return jnp.exp(jnp.where((row & 1) == (0 if reverse else 1), la, 0.0)).astype(BF16)
    blk, half = 1 << lev, 1 << (lev - 1)

    def boundary(r):
        base = (r // blk) * blk
        return jnp.broadcast_to(cum_row(base + half if reverse else base + half - 1),
                                (SUBLANES, dk))

    top = lax.broadcasted_iota(jnp.int32, (SUBLANES, dk), 0) < SUBLANES // 2
    pieces = []
    for r0 in range(0, c, SUBLANES):
        if blk >= SUBLANES:
            pieces.append(boundary(r0))
        else:
            pieces.append(jnp.where(top, boundary(r0), boundary(r0 + SUBLANES // 2)))
    return jnp.exp2(jnp.abs(cum - jnp.concatenate(pieces, axis=0)) * (-LOG2E)).astype(BF16)


def _hgrn_forget(z, lb):
    e = jnp.exp(-jnp.abs(z))
    one_e = 1.0 + e
    r = 1.0 / one_e
    pos = z >= 0.0
    sig = jnp.where(pos, 1.0, e) * r
    sig_neg = jnp.where(pos, e, 1.0) * r
    oml = 1.0 - lb
    log_sig = jnp.minimum(z, 0.0) - jnp.log(one_e)
    log_f = jnp.where(lb > 0.0, jnp.log(lb + oml * sig), log_sig)
    return log_f, oml * sig_neg


def _hgrn_kernel(*refs, seq_len, chunk, n_lev, heads, has_s0, n_prev):
    it = iter(refs)
    (q_ref, laf_ref, lab_ref, kf_ref, kb_ref, i_ref, g_ref, trif_ref, trib_ref, lvf_ref, lvb_ref,
     leaf_ref) = (next(it) for _ in range(12))
    s0_ref = next(it) if has_s0 else None
    prev_ref = next(it) if n_prev else None
    o_ref = next(it)
    st_ref = None if has_s0 else next(it)
    cum_ref, qt_ref, kt_ref, qd_ref, kd_ref, ae_ref, s_ref, acc_ref = (next(it) for _ in range(8))
    n_chunks = seq_len // chunk
    dk = DK_A
    units = [(2 * hh + d, hh, d, d == 1, la_in, tri_ref, lv_ref)
             for hh in range(heads)
             for d, (la_in, tri_ref, lv_ref) in enumerate(((laf_ref, trif_ref, lvf_ref),
                                                           (lab_ref, trib_ref, lvb_ref)))]
    key_in = (kf_ref, kb_ref)
    rows_of = lambda c: slice(c * chunk, (c + 1) * chunk)
    cols_of = lambda hh: slice(hh * dk, (hh + 1) * dk)

    worst = None
    for _, hh, _, _, la_in, _, _ in units:
        leaf_sum = jnp.dot(leaf_ref[...], la_in[:, cols_of(hh)].astype(BF16),
                           preferred_element_type=F32)
        worst = leaf_sum if worst is None else jnp.minimum(worst, leaf_sum)
    leafwise_ok = jnp.min(worst) > -LEAF_MAX_DECAY

    for u, hh, _, _, la_in, tri_ref, _ in units:
        tri = tri_ref[...]
        for c in range(n_chunks):
            la = la_in[rows_of(c), cols_of(hh)]
            la_hi = la.astype(BF16)
            la_lo = (la - la_hi.astype(F32)).astype(BF16)
            x2 = jnp.dot(tri, jnp.concatenate([la_hi, la_lo], axis=1),
                         preferred_element_type=F32)
            cum_ref[u, rows_of(c), :] = x2[:, :dk] + x2[:, dk:]

    def operands_and_scores(leafwise):
        if leafwise:
            levels = list(range(LEAF_LEVEL + 1, n_lev + 1))
        else:
            levels = list(range(1, n_lev + 1))
        n_terms = len(levels) + 1
        for u, hh, d, reverse, la_in, _, _ in units:
            for c in range(n_chunks):
                rows = rows_of(c)
                cum = cum_ref[u, rows, :]
                la = la_in[rows, cols_of(hh)]
                q = q_ref[rows, cols_of(hh)]
                k = key_in[d][rows, cols_of(hh)]
                cum_row = lambda r, u=u, c=c: cum_ref[u, c * chunk + r:c * chunk + r + 1, :]
                if leafwise:
                    factors = [_leaf_factors(cum, cum_row, reverse)]
                else:
                    factors = [None]
                for lev in levels:
                    e = _level_factor(la, cum, cum_row, lev, reverse)
                    factors.append((e, e))
                for t, f in enumerate(factors):
                    cols = slice(t * dk, (t + 1) * dk)
                    qt_ref[u, rows, cols] = q if f is None else q * f[0]
                    kt_ref[u, rows, cols] = k if f is None else k * f[1]
                end_row = 0 if reverse else chunk - 1
                cum_end = cum[end_row:end_row + 1]
                qd_ref[u, rows, :] = q * jnp.exp(cum).astype(BF16)
                kd_ref[u, rows, :] = k * jnp.exp(cum_end - cum).astype(BF16)
                ae_ref[u, c:c + 1, :] = jnp.exp(cum_end)
        zero = jnp.zeros((chunk, dk), BF16)
        for u, _, _, _, _, _, lv_ref in units:
            lv = lv_ref[...]
            if leafwise:
                masks = [(lv >= 0) & (lv <= LEAF_LEVEL)]
            else:
                masks = [lv == 0]
            masks += [lv == lev for lev in levels]
            for c in range(n_chunks):
                rows = rows_of(c)
                scores = jnp.zeros((chunk, chunk), F32)
                for t in range(0, n_terms, 2):
                    if t + 1 < n_terms:
                        k_a = kt_ref[u, rows, t * dk:(t + 1) * dk]
                        k_b = kt_ref[u, rows, (t + 1) * dk:(t + 2) * dk]
                        p = lax.dot_general(
                            qt_ref[u, rows, t * dk:(t + 2) * dk],
                            jnp.concatenate([jnp.concatenate([k_a, zero], axis=1),
                                             jnp.concatenate([zero, k_b], axis=1)], axis=0),
                            _NT, preferred_element_type=F32)
                        scores = jnp.where(masks[t], p[:, :chunk], scores)
                        scores = jnp.where(masks[t + 1], p[:, chunk:], scores)
                    else:
                        p = lax.dot_general(qt_ref[u, rows, t * dk:(t + 1) * dk],
                                            kt_ref[u, rows, t * dk:(t + 1) * dk],
                                            _NT, preferred_element_type=F32)
                        scores = jnp.where(masks[t], p, scores)
                s_ref[u, rows, :] = scores.astype(BF16)

    pl.when(leafwise_ok)(lambda: operands_and_scores(True))
    pl.when(jnp.logical_not(leafwise_ok))(lambda: operands_and_scores(False))

    incs = {u: [lax.dot_general(i_ref[rows_of(c), cols_of(hh)], kd_ref[u, rows_of(c), :], _TN,
                                preferred_element_type=F32) for c in range(n_chunks)]
            for u, hh, _, _, _, _, _ in units}
    finals, befores = {}, {}
    for u, hh, d, reverse, _, _, _ in units:
        st = s0_ref[d, hh].T if has_s0 else jnp.zeros((DV_A, dk), F32)
        befores[u] = [None] * n_chunks
        for c in (range(n_chunks - 1, -1, -1) if reverse else range(n_chunks)):
            befores[u][c] = st.astype(BF16)
            st = st * ae_ref[u, c:c + 1, :] + incs[u][c]
        finals[u] = st
    for u, hh, d, reverse, _, _, _ in units:
        for c in range(n_chunks):
            rows = rows_of(c)
            o = jnp.dot(s_ref[u, rows, :], i_ref[rows, cols_of(hh)], preferred_element_type=F32)
            if has_s0 or c != (n_chunks - 1 if reverse else 0):
                o = o + lax.dot_general(qd_ref[u, rows, :], befores[u][c], _NT,
                                        preferred_element_type=F32)
            if d == 0:
                acc_ref[rows, cols_of(hh)] = o
            else:
                acc_ref[rows, cols_of(hh)] += o

    for hh in range(heads):
        o = acc_ref[:, cols_of(hh)]
        o = o * lax.rsqrt(jnp.mean(o * o, axis=-1, keepdims=True) + EPS)
        o_ref[:, cols_of(hh)] = (o * _silu(g_ref[:, cols_of(hh)].astype(F32))).astype(BF16)
    if st_ref is not None:
        if n_prev:
            st_ref[0:n_prev] = prev_ref[...]
        for u, hh, d, _, _, _, _ in units:
            st_ref[n_prev, d, hh] = finals[u].T


def _hgrn_call(proj, la, key, state, prev, layer):
    n_b, seq_len, _ = proj.shape
    tri_f, tri_b, lv_f, lv_b, n_lev = _hgrn_consts(HGRN_CHUNK)
    has_s0 = state is not None
    n_chunks = seq_len // HGRN_CHUNK
    heads = max(1, HGRN_UNITS // (2 * n_chunks))
    width = heads * DK_A
    col = lambda off: (lambda b, h: (b, 0, off // width + h))
    const = lambda a: pl.BlockSpec(a.shape, lambda b, h: (0, 0))
    bwd = W_OFF_ZB - W_OFF_ZF
    in_specs = [pl.BlockSpec((None, seq_len, width), col(o))
                for o in (OFF_QA, 0, bwd, 0, bwd, OFF_IA, OFF_GA)]
    leaf = 1 << LEAF_LEVEL
    leaf_rows = -(-(seq_len // leaf) // SUBLANES) * SUBLANES
    leaf_ind = (np.arange(seq_len)[None, :] // leaf == np.arange(leaf_rows)[:, None])
    in_specs += [const(tri_f), const(tri_b), const(lv_f), const(lv_b), const(leaf_ind)]
    args = [proj, la, la, key, key, proj, proj, jnp.asarray(tri_f, BF16), jnp.asarray(tri_b, BF16),
            jnp.asarray(lv_f), jnp.asarray(lv_b), jnp.asarray(leaf_ind, BF16)]
    out_specs = [pl.BlockSpec((None, seq_len, width), lambda b, h: (b, 0, h))]
    out_shape = [jax.ShapeDtypeStruct((n_b, seq_len, H_A * DV_A), BF16)]
    if has_s0:
        in_specs.append(pl.BlockSpec((None, None, 2, heads, DK_A, DV_A),
                                     lambda b, h: (b, layer, 0, h, 0, 0)))
        args.append(state)
    else:
        st_spec = lambda n: pl.BlockSpec((None, n, 2, heads, DK_A, DV_A),
                                         lambda b, h: (b, 0, 0, h, 0, 0))
        if layer:
            in_specs.append(st_spec(layer))
            args.append(prev)
        out_specs.append(st_spec(layer + 1))
        out_shape.append(jax.ShapeDtypeStruct((n_b, layer + 1, 2, H_A, DK_A, DV_A), F32))
    n_u = 2 * heads
    res = pl.pallas_call(
        functools.partial(_hgrn_kernel, seq_len=seq_len, chunk=HGRN_CHUNK, n_lev=n_lev,
                          heads=heads, has_s0=has_s0, n_prev=0 if has_s0 else layer),
        grid=(n_b, H_A // heads),
        in_specs=in_specs, out_specs=out_specs, out_shape=out_shape,
        scratch_shapes=[
            pltpu.VMEM((n_u, seq_len, DK_A), F32),
            pltpu.VMEM((n_u, seq_len, (n_lev + 1) * DK_A), BF16),
            pltpu.VMEM((n_u, seq_len, (n_lev + 1) * DK_A), BF16),
            pltpu.VMEM((n_u, seq_len, DK_A), BF16),
            pltpu.VMEM((n_u, seq_len, DK_A), BF16),
            pltpu.VMEM((n_u, max(SUBLANES, n_chunks), DK_A), F32),
            pltpu.VMEM((n_u, seq_len, HGRN_CHUNK), BF16),
            pltpu.VMEM((seq_len, width), F32),
        ],
        compiler_params=_params(("arbitrary", "arbitrary")),
        name="hgrn2_scan",
    )(*args)
    return (res[0], None) if has_s0 else (res[0], res[1])


def _rope_tables(seq_len):
    rows = seq_len // GRID_W
    r_idx = jnp.repeat(jnp.arange(rows), GRID_W).astype(F32)
    c_idx = jnp.tile(jnp.arange(GRID_W), rows).astype(F32)
    quarter = DK_B // 4
    inv = 1.0 / (ROPE_BASE ** (jnp.arange(quarter, dtype=F32) / quarter))
    ang_r = r_idx[:, None] * inv[None, :]
    ang_c = c_idx[:, None] * inv[None, :]
    cos = jnp.concatenate([jnp.cos(ang_r)] * 2 + [jnp.cos(ang_c)] * 2, axis=1)
    sin = jnp.concatenate([-jnp.sin(ang_r), jnp.sin(ang_r), -jnp.sin(ang_c), jnp.sin(ang_c)],
                          axis=1)
    return cos, sin


def _rope(x, cos, sin):
    half = DK_B // 2
    swapped = jnp.concatenate([pltpu.roll(x[:, :half], half // 2, axis=1),
                               pltpu.roll(x[:, half:], half // 2, axis=1)], axis=1)
    return x * cos + swapped * sin


def _ret_kernel(*refs, seq_len, latent, has_s0, n_prev):
    it = iter(refs)
    lg_ref, dm_ref, q_ref, k_ref, v_ref, g_ref = (next(it) for _ in range(6))
    cos_ref, sin_ref = (next(it), next(it)) if latent else (None, None)
    s0_ref = next(it) if has_s0 else None
    prev_ref = next(it) if n_prev else None
    o_ref = next(it)
    st_ref = None if has_s0 else next(it)
    h = pl.program_id(0)
    lg_f = lg_ref[h, 0]
    lg_b = lg_ref[h, 1]

    q, k = q_ref[...], k_ref[...]
    if latent:
        cos, sin = cos_ref[...], sin_ref[...]
        q, k = _rope(q.astype(F32), cos, sin), _rope(k.astype(F32), cos, sin)
    k_b = k.astype(BF16)
    v_b = v_ref[...]
    if has_s0:
        s0 = jnp.concatenate([s0_ref[0], s0_ref[1]], axis=0).astype(BF16)
    blocks = [slice(r, r + RET_ROWS) for r in range(0, seq_len, RET_ROWS)]
    ps = [lax.dot_general(q[rows].astype(BF16), k_b, _NT, preferred_element_type=F32)
          for rows in blocks]
    carried = []
    if has_s0:
        for rows in blocks:
            pos = (lax.broadcasted_iota(jnp.int32, (RET_ROWS, 1), 0) + rows.start).astype(F32)
            q_s = jnp.concatenate([q[rows] * jnp.exp(lg_f * (pos + 1.0)),
                                   q[rows] * jnp.exp(lg_b * (seq_len - pos))], axis=1)
            carried.append(jnp.dot(q_s.astype(BF16), s0, preferred_element_type=F32))
    pds = [p.astype(BF16) * dm_ref[rows, :] for p, rows in zip(ps, blocks)]
    outs = [jnp.dot(pd, v_b, preferred_element_type=F32) for pd in pds]
    for i, rows in enumerate(blocks):
        o = outs[i] + carried[i] if has_s0 else outs[i]
        o = o * lax.rsqrt(jnp.mean(o * o, axis=-1, keepdims=True) + EPS)
        o_ref[rows, :] = (o * _silu(g_ref[rows, :].astype(F32))).astype(BF16)
    if st_ref is not None:
        if n_prev:
            st_ref[0:n_prev] = prev_ref[...]
        pos = lax.broadcasted_iota(jnp.int32, (seq_len, 1), 0).astype(F32)
        k_f = (k * (K_SCALE * jnp.exp(lg_f * (seq_len - 1.0 - pos)))).astype(BF16)
        k_r = (k * (K_SCALE * jnp.exp(lg_b * pos))).astype(BF16)
        st_ref[n_prev, 0] = lax.dot_general(k_f, v_b, _TN, preferred_element_type=F32)
        st_ref[n_prev, 1] = lax.dot_general(k_r, v_b, _TN, preferred_element_type=F32)


def _ret_log_decay():
    heads = jnp.arange(H_B, dtype=F32)
    lg_f = jnp.log1p(-jnp.exp2(-5.0 - heads))
    lg_b = jnp.log1p(-jnp.exp2(-(5.0 + RET_DECAY_OFFSET_BWD) - heads))
    return jnp.stack([lg_f, lg_b], axis=1)


def _ret_decay_mask(seq_len):
    lg = _ret_log_decay()
    d = (jnp.arange(seq_len)[:, None] - jnp.arange(seq_len)[None, :]).astype(F32)[None]
    lg_f, lg_b = lg[:, 0, None, None], lg[:, 1, None, None]
    dm = (jnp.where(d >= 0.0, jnp.exp(lg_f * jnp.maximum(d, 0.0)), 0.0)
          + jnp.where(d <= 0.0, jnp.exp(lg_b * jnp.maximum(-d, 0.0)), 0.0))
    return (K_SCALE * dm).astype(BF16)


def _ret_call(proj, state, prev, layer, latent):
    n_b, seq_len, _ = proj.shape
    has_s0 = state is not None
    in_specs = [
        pl.BlockSpec(memory_space=pltpu.SMEM),
        pl.BlockSpec((None, seq_len, seq_len), lambda h, b: (h, 0, 0)),
        pl.BlockSpec((None, seq_len, DK_B), lambda h, b: (b, 0, OFF_QB // DK_B + h)),
        pl.BlockSpec((None, seq_len, DK_B), lambda h, b: (b, 0, OFF_KB // DK_B + h)),
        pl.BlockSpec((None, seq_len, DV_B), lambda h, b: (b, 0, OFF_VB // DV_B + h)),
        pl.BlockSpec((None, seq_len, DV_B), lambda h, b: (b, 0, OFF_GB // DV_B + h)),
    ]
    args = [_ret_log_decay(), _ret_decay_mask(seq_len), proj, proj, proj, proj]
    if latent:
        cos, sin = _rope_tables(seq_len)
        in_specs += [pl.BlockSpec((seq_len, DK_B), lambda h, b: (0, 0))] * 2
        args += [cos, sin]
    out_specs = [pl.BlockSpec((None, seq_len, DV_B), lambda h, b: (b, 0, h))]
    out_shape = [jax.ShapeDtypeStruct((n_b, seq_len, H_B * DV_B), BF16)]
    if has_s0:
        in_specs.append(pl.BlockSpec((None, None, 2, None, DK_B, DV_B),
                                     lambda h, b: (b, layer, 0, h, 0, 0)))
        args.append(state)
    else:
        st_spec = lambda n: pl.BlockSpec((None, n, 2, None, DK_B, DV_B),
                                         lambda h, b: (b, 0, 0, h, 0, 0))
        if layer:
            in_specs.append(st_spec(layer))
            args.append(prev)
        out_specs.append(st_spec(layer + 1))
        out_shape.append(jax.ShapeDtypeStruct((n_b, layer + 1, 2, H_B, DK_B, DV_B), F32))
    res = pl.pallas_call(
        functools.partial(_ret_kernel, seq_len=seq_len, latent=latent, has_s0=has_s0,
                          n_prev=0 if has_s0 else layer),
        grid=(H_B, n_b),
        in_specs=in_specs, out_specs=out_specs, out_shape=out_shape,
        compiler_params=_params(("arbitrary", "arbitrary")),
        name="retention_scan",
    )(*args)
    return (res[0], None) if has_s0 else (res[0], res[1])


def _post_kernel(oa_ref, ob_ref, gta_ref, gtb_ref, x_ref, gt1_ref, pa_ref, pb_ref, wo_ref,
                 o_ref):
    subs = [slice(r, r + POST_SUB) for r in range(0, POST_TILE, POST_SUB)]
    y_a = [jnp.dot(oa_ref[s, :], pa_ref[...], preferred_element_type=F32) for s in subs]
    y_b = [jnp.dot(ob_ref[s, :], pb_ref[...], preferred_element_type=F32) for s in subs]
    merged = [(_sigmoid(gta_ref[s, :].astype(F32)) * y_a[i]
               + _sigmoid(gtb_ref[s, :].astype(F32)) * y_b[i]).astype(BF16)
              for i, s in enumerate(subs)]
    y = [jnp.dot(m, wo_ref[...], preferred_element_type=F32) for m in merged]
    for i, s in enumerate(subs):
        o_ref[s, :] = x_ref[s, :] + gt1_ref[...] * y[i]


def _post_call(o_a, o_b, proj, x, mods, p_a, p_b, w_out, layer, row_fn):
    n_tok = x.shape[0]
    tile = lambda w: (lambda i, j: (i, w))
    pcol = lambda off: pl.BlockSpec((POST_TILE, 1024), tile(off // 1024))
    const = lambda a: _layer_spec(layer, a.shape[1:], lambda i, j: (0, 0))
    return pl.pallas_call(
        _post_kernel,
        grid=(n_tok // POST_TILE, 1),
        in_specs=[
            pl.BlockSpec((POST_TILE, H_A * DV_A), tile(0)),
            pl.BlockSpec((POST_TILE, H_B * DV_B), tile(0)),
            pcol(OFF_GATE_A), pcol(OFF_GATE_B),
            pl.BlockSpec((POST_TILE, D_MODEL), tile(0)),
            _mod_spec(layer, row_fn, 2, POST_TILE),
            const(p_a), const(p_b), const(w_out),
        ],
        out_specs=pl.BlockSpec((POST_TILE, D_MODEL), tile(0)),
        out_shape=jax.ShapeDtypeStruct((n_tok, D_MODEL), F32),
        compiler_params=_params(("arbitrary", "arbitrary")),
        name="mixer_output",
    )(o_a, o_b, proj, proj, x, mods, p_a, p_b, w_out)


def _conv3(u, wc, bc, seq_len):
    n = u.shape[0]
    row = lax.broadcasted_iota(jnp.int32, (SUBLANES, 1), 0)

    def zero_row(x, r0, r):
        return jnp.where(row == r, 0.0, x[r0:r0 + SUBLANES])

    prev, nxt = pltpu.roll(u, 1, axis=0), pltpu.roll(u, n - 1, axis=0)
    p_parts, n_parts = [], []
    for s0 in range(0, n, seq_len):
        s1 = s0 + seq_len
        p_parts += [zero_row(prev, s0, 0), prev[s0 + SUBLANES:s1]]
        n_parts += [nxt[s0:s1 - SUBLANES], zero_row(nxt, s1 - SUBLANES, SUBLANES - 1)]
    prev, nxt = jnp.concatenate(p_parts, axis=0), jnp.concatenate(n_parts, axis=0)
    return prev * wc[0:1] + u * wc[1:2] + nxt * wc[2:3] + bc


def _ffn_kernel(*refs, seq_len, final):
    it = iter(refs)
    (x_ref, sh_ref, sc_ref, gt_ref, n2_ref, wa_ref, wg_ref, wca_ref, wcg_ref, bca_ref, bcg_ref,
     wd_ref) = (next(it) for _ in range(12))
    fn_ref = next(it) if final else None
    o_ref, h_ref, acc_ref = next(it), next(it), next(it)
    j = pl.program_id(1)

    @pl.when(j == 0)
    def _():
        h_ref[...] = _norm_mod(x_ref[...], n2_ref[...], sc_ref[...], sh_ref[...]).astype(BF16)
        acc_ref[...] = jnp.zeros_like(acc_ref)

    h = h_ref[...]
    subs = [slice(c, min(c + FF_SUB, FF_CHUNK)) for c in range(0, FF_CHUNK, FF_SUB)]
    u_a = [jnp.dot(h, wa_ref[:, s], preferred_element_type=F32) for s in subs]
    u_g = [jnp.dot(h, wg_ref[:, s], preferred_element_type=F32) for s in subs]
    act = [(_silu(_conv3(u_g[i], wcg_ref[:, s], bcg_ref[:, s], seq_len))
            * _conv3(u_a[i], wca_ref[:, s], bca_ref[:, s], seq_len)).astype(BF16)
           for i, s in enumerate(subs)]
    acc_ref[...] += jnp.dot(jnp.concatenate(act, axis=1), wd_ref[...],
                            preferred_element_type=F32)

    @pl.when(j == pl.num_programs(1) - 1)
    def _():
        x = x_ref[...] + gt_ref[...] * acc_ref[...]
        if final:
            x = x * lax.rsqrt(jnp.mean(x * x, axis=-1, keepdims=True) + EPS) * fn_ref[...]
        o_ref[...] = x


def _ffn_call(x, mods, n2, w_up, w_conv, b_conv, w_down, final_norm, layer, row_fn, seq_len):
    n_tok = x.shape[0]
    n_ff = D_FF // FF_CHUNK
    lspec = functools.partial(_layer_spec, layer)
    in_specs = [
        pl.BlockSpec((TOKEN_TILE, D_MODEL), lambda i, j: (i, 0)),
        _mod_spec(layer, row_fn, 3), _mod_spec(layer, row_fn, 4), _mod_spec(layer, row_fn, 5),
        lspec((1, D_MODEL), lambda i, j: (0, 0)),
        lspec((D_MODEL, FF_CHUNK), lambda i, j: (0, j)),
        lspec((D_MODEL, FF_CHUNK), lambda i, j: (0, n_ff + j)),
        lspec((3, FF_CHUNK), lambda i, j: (0, j)),
        lspec((3, FF_CHUNK), lambda i, j: (0, n_ff + j)),
        lspec((1, FF_CHUNK), lambda i, j: (0, j)),
        lspec((1, FF_CHUNK), lambda i, j: (0, n_ff + j)),
        lspec((FF_CHUNK, D_MODEL), lambda i, j: (j, 0)),
    ]
    args = [x, mods, mods, mods, n2, w_up, w_up, w_conv, w_conv, b_conv, b_conv, w_down]
    final = final_norm is not None
    if final:
        in_specs.append(pl.BlockSpec((1, D_MODEL), lambda i, j: (0, 0)))
        args.append(final_norm.reshape(1, D_MODEL))
    return pl.pallas_call(
        functools.partial(_ffn_kernel, seq_len=seq_len, final=final),
        grid=(n_tok // TOKEN_TILE, n_ff),
        in_specs=in_specs,
        out_specs=pl.BlockSpec((TOKEN_TILE, D_MODEL), lambda i, j: (i, 0)),
        out_shape=jax.ShapeDtypeStruct((n_tok, D_MODEL), F32),
        scratch_shapes=[pltpu.VMEM((TOKEN_TILE, D_MODEL), BF16),
                        pltpu.VMEM((TOKEN_TILE, D_MODEL), F32)],
        compiler_params=_params(("arbitrary", "arbitrary")),
        name="conv_ffn",
    )(*args)


def kernel(x_prompt, x_sample, state_hgrn, state_ret, c, c_ctx, norm1, norm2, final_norm,
           w_mod, b_mod, w_in, hgrn_lb_raw, p_a, p_b, w_out, w_up, w_conv, b_conv, w_down):
    n_ctx, t_ctx, _ = x_prompt.shape
    n_dec, t_dec, _ = x_sample.shape
    assert t_ctx & (t_ctx - 1) == 0 and TOKEN_TILE % t_ctx == 0 and t_dec == TOKEN_TILE

    sm = jax.nn.softmax(hgrn_lb_raw.astype(F32), axis=0)
    cum = jnp.cumsum(sm, axis=0)
    lower_bounds = cum - cum[0:1]

    cvec = jnp.concatenate(
        [c_ctx[None, :], c, jnp.zeros((MOD_ROWS - 1 - n_dec, D_MODEL), F32)], axis=0)
    mods = _mod_call(cvec, w_mod, b_mod).reshape(DEPTH, MOD_ROWS, 1, 6 * D_MODEL)

    p_a_b, p_b_b, w_out_b, w_up_b, w_down_b = (
        w.astype(BF16) for w in (p_a, p_b, w_out, w_up, w_down))
    norm1_3, norm2_3 = norm1.reshape(DEPTH, 1, D_MODEL), norm2.reshape(DEPTH, 1, D_MODEL)
    b_conv_3 = b_conv.reshape(DEPTH, 1, 2 * D_FF)

    ctx_row = lambda tok: 0
    dec_row = lambda tok: tok // t_dec + 1

    def layer(x, l, n_b, seq_len, row_fn, s_hgrn, s_ret, prev_h, prev_r, latent):
        proj, la, key = _proj_call(x, mods, norm1_3, w_in, lower_bounds, l, row_fn)
        proj3 = proj.reshape(n_b, seq_len, PROJ_WIDTH)
        o_a, st_h = _hgrn_call(proj3, la.reshape(n_b, seq_len, -1), key.reshape(n_b, seq_len, -1),
                               s_hgrn, prev_h, l)
        o_b, st_r = _ret_call(proj3, s_ret, prev_r, l, latent)
        x = _post_call(o_a.reshape(-1, H_A * DV_A), o_b.reshape(-1, H_B * DV_B), proj, x,
                       mods, p_a_b, p_b_b, w_out_b, l, row_fn)
        x = _ffn_call(x, mods, norm2_3, w_up_b, w_conv, b_conv_3, w_down_b,
                      final_norm if l == DEPTH - 1 else None, l, row_fn, seq_len)
        return x, st_h, st_r

    x = x_prompt.reshape(n_ctx * t_ctx, D_MODEL)
    st_h = st_r = None
    for l in range(DEPTH):
        x, st_h, st_r = layer(x, l, n_ctx, t_ctx, ctx_row, None, None, st_h, st_r, False)
    y_prompt = x.reshape(n_ctx, t_ctx, D_MODEL)

    x = x_sample.reshape(n_dec * t_dec, D_MODEL)
    for l in range(DEPTH):
        x, _, _ = layer(x, l, n_dec, t_dec, dec_row, state_hgrn, state_ret, None, None, True)
    y_sample = x.reshape(n_dec, t_dec, D_MODEL)
    return (y_prompt, y_sample, st_h, st_r)
```

```python
import functools

import numpy as np
import jax
import jax.numpy as jnp
from jax import lax
from jax.experimental import pallas as pl
from jax.experimental.pallas import tpu as pltpu

F32 = jnp.float32
BF16 = jnp.bfloat16

D_MODEL = 1024
DEPTH = 2
GRID_W = 64
H_A, DK_A, DV_A = 8, 128, 128
H_B, DK_B, DV_B = 4, 256, 512
D_FF = 2816
ROPE_BASE = 10000.0
K_SCALE = DK_B ** -0.5
EPS = 1e-6
RET_DECAY_OFFSET_BWD = 0.5
IN_WIDTH = 13312

W_OFF_ZF, W_OFF_ZB, W_OFF_IA = 1024, 2048, 3072
OFF_QA, OFF_IA, OFF_GA = 0, 1024, 2048
OFF_QB, OFF_KB, OFF_VB, OFF_GB = 3072, 4096, 5120, 7168
OFF_GATE_A, OFF_GATE_B = 9216, 10240
PROJ_WIDTH = 11264

MOD_ROWS = 8
TOKEN_TILE = 1024
POST_TILE = 512
POST_SUB = 256
FF_CHUNK = 1408
FF_SUB = 256
HGRN_CHUNK = 128
HGRN_UNITS = 16
LEAF_LEVEL = 5
LEAF_MAX_DECAY = 60.0
RET_ROWS = 256
RET_TOKENS = 512
PROJ_TN = 1024
Z_TILE0 = W_OFF_ZF // PROJ_TN
Z_TILES = (W_OFF_IA - W_OFF_ZF) // PROJ_TN
VMEM_LIMIT = 52 * 1024 * 1024
SUBLANES = 8
LOG2E = 1.4426950408889634

_NT = (((1,), (1,)), ((), ()))
_TN = (((0,), (0,)), ((), ()))


def _params(sem):
    return pltpu.CompilerParams(dimension_semantics=sem, vmem_limit_bytes=VMEM_LIMIT)


def _sigmoid(x):
    return jax.nn.sigmoid(x)


def _silu(x):
    return x * jax.nn.sigmoid(x)


def _mod_kernel(c_ref, w_ref, b_ref, o_ref):
    s = _silu(c_ref[...])
    o_ref[...] = jnp.dot(s, w_ref[...], precision=lax.Precision.HIGHEST,
                         preferred_element_type=F32) + b_ref[...]


def _mod_call(cvec, w_mod, b_mod):
    n_col = 6 * D_MODEL // 1024
    return pl.pallas_call(
        _mod_kernel,
        grid=(DEPTH, n_col),
        in_specs=[
            pl.BlockSpec((MOD_ROWS, D_MODEL), lambda l, j: (0, 0)),
            pl.BlockSpec((None, D_MODEL, 1024), lambda l, j: (l, 0, j)),
            pl.BlockSpec((None, 1, 1024), lambda l, j: (l, 0, j)),
        ],
        out_specs=pl.BlockSpec((None, MOD_ROWS, 1024), lambda l, j: (l, 0, j)),
        out_shape=jax.ShapeDtypeStruct((DEPTH, MOD_ROWS, 6 * D_MODEL), F32),
        compiler_params=_params(("arbitrary", "arbitrary")),
        name="modulation",
    )(cvec, w_mod, b_mod.reshape(DEPTH, 1, 6 * D_MODEL))


def _mod_spec(layer, row_fn, which, tile=TOKEN_TILE):
    return pl.BlockSpec((None, None, 1, D_MODEL),
                        lambda i, j: (layer, row_fn(i * tile), 0, which))


def _layer_spec(layer, block, index_map):
    return pl.BlockSpec((None,) + block, lambda *g: (layer,) + index_map(*g))


def _norm_mod(x, g, sc, sh):
    y = x * lax.rsqrt(jnp.mean(x * x, axis=-1, keepdims=True) + EPS) * g
    return y * (1.0 + sc) + sh


def _proj_kernel(x_ref, sh_ref, sc_ref, n_ref, w_ref, o_ref, hout_ref, h_ref, wb_ref):
    j, i = pl.program_id(0), pl.program_id(1)

    @pl.when(i == 0)
    def _():
        wb_ref[...] = w_ref[...].astype(BF16)

    @pl.when(j == 0)
    def _():
        h = _norm_mod(x_ref[...], n_ref[...], sc_ref[...], sh_ref[...]).astype(BF16)
        h_ref[i] = h
        hout_ref[...] = h

    o_ref[...] = jnp.dot(h_ref[i], wb_ref[...], preferred_element_type=F32).astype(BF16)


def _zproj_kernel(h_ref, w_ref, lb_ref, la_ref, key_ref, wb_ref):
    @pl.when(pl.program_id(1) == 0)
    def _():
        wb_ref[...] = w_ref[...].astype(BF16)

    z = jnp.dot(h_ref[...], wb_ref[...], preferred_element_type=F32)
    la, key = _hgrn_forget(z, lb_ref[...])
    la_ref[...] = la
    key_ref[...] = key.astype(BF16)


def _proj_call(x, mods, n1, w_in, lb, layer, row_fn):
    n_tok = x.shape[0]
    tn = PROJ_TN
    n_tile = n_tok // TOKEN_TILE
    hold = lambda j, i: jnp.where(j == 0, i, n_tile - 1)
    w_tile = lambda j: jnp.where(j >= Z_TILE0, j + Z_TILES, j)
    mod = lambda which: pl.BlockSpec(
        (None, None, 1, D_MODEL), lambda j, i: (layer, row_fn(i * TOKEN_TILE), 0, which))
    proj, h = pl.pallas_call(
        _proj_kernel,
        grid=(PROJ_WIDTH // tn, n_tile),
        in_specs=[
            pl.BlockSpec((TOKEN_TILE, D_MODEL), lambda j, i: (hold(j, i), 0)),
            mod(0), mod(1),
            _layer_spec(layer, (1, D_MODEL), lambda j, i: (0, 0)),
            _layer_spec(layer, (D_MODEL, tn), lambda j, i: (0, w_tile(j))),
        ],
        out_specs=[
            pl.BlockSpec((TOKEN_TILE, tn), lambda j, i: (i, j)),
            pl.BlockSpec((TOKEN_TILE, D_MODEL), lambda j, i: (hold(j, i), 0)),
        ],
        out_shape=[jax.ShapeDtypeStruct((n_tok, PROJ_WIDTH), BF16),
                   jax.ShapeDtypeStruct((n_tok, D_MODEL), BF16)],
        scratch_shapes=[pltpu.VMEM((n_tile, TOKEN_TILE, D_MODEL), BF16),
                        pltpu.VMEM((D_MODEL, tn), BF16)],
        compiler_params=_params(("arbitrary", "arbitrary")),
        name="in_projection",
    )(x, mods, mods, n1, w_in)
    la, key = pl.pallas_call(
        _zproj_kernel,
        grid=(Z_TILES, n_tile),
        in_specs=[
            pl.BlockSpec((TOKEN_TILE, D_MODEL), lambda j, i: (i, 0)),
            _layer_spec(layer, (D_MODEL, tn), lambda j, i: (0, Z_TILE0 + j)),
            pl.BlockSpec((None, None, 1, tn), lambda j, i: (layer, j, 0, 0)),
        ],
        out_specs=[pl.BlockSpec((TOKEN_TILE, tn), lambda j, i: (i, j))] * 2,
        out_shape=[jax.ShapeDtypeStruct((n_tok, Z_TILES * tn), F32),
                   jax.ShapeDtypeStruct((n_tok, Z_TILES * tn), BF16)],
        scratch_shapes=[pltpu.VMEM((D_MODEL, tn), BF16)],
        compiler_params=_params(("arbitrary", "arbitrary")),
        name="forget_projection",
    )(h, w_in, lb.reshape(DEPTH, Z_TILES, 1, tn))
    return proj, la, key


def _hgrn_consts(c):
    nl = int(np.log2(c))
    t = np.arange(c)
    tt, rr = t[:, None], t[None, :]
    x = tt ^ rr
    lev_of = np.where(x > 0, np.floor(np.log2(np.maximum(x, 1))) + 1, 0).astype(np.int32)
    lv_f = np.where(tt >= rr, lev_of, -1).astype(np.int32)
    lv_b = lv_f.T.copy()
    return ((rr <= tt).astype(np.float32), (rr >= tt).astype(np.float32), lv_f, lv_b, nl)


def _leaf_factors(cum, cum_row, reverse):
    c, dk = cum.shape
    leaf = 1 << LEAF_LEVEL
    pieces = []
    for base in range(0, c, leaf):
        row = base + leaf if reverse else base - 1
        if 0 <= row < c:
            pieces.append(jnp.broadcast_to(cum_row(row), (leaf, dk)))
        else:
            pieces.append(jnp.zeros((leaf, dk), F32))
    x = cum - jnp.concatenate(pieces, axis=0)
    return jnp.exp2(x * LOG2E).astype(BF16), jnp.exp2(x * (-LOG2E)).astype(BF16)


def _level_factor(la, cum, cum_row, lev, reverse):
    c, dk = la.shape
    if lev == 1:
        row = lax.broadcasted_iota(jnp.int32, (c, 1), 0)
        return jnp.exp(jnp.where((row & 1) == (0 if reverse else 1), la, 0.0)).astype(BF16)
    blk, half = 1 << lev, 1 << (lev - 1)

    def boundary(r):
        base = (r // blk) * blk
        return jnp.broadcast_to(cum_row(base + half if reverse else base + half - 1),
                                (SUBLANES, dk))

    top = lax.broadcasted_iota(jnp.int32, (SUBLANES, dk), 0) < SUBLANES // 2
    pieces = []
    for r0 in range(0, c, SUBLANES):
        if blk >= SUBLANES:
            pieces.append(boundary(r0))
        else:
            pieces.append(jnp.where(top, boundary(r0), boundary(r0 + SUBLANES // 2)))
    return jnp.exp2(jnp.abs(cum - jnp.concatenate(pieces, axis=0)) * (-LOG2E)).astype(BF16)


def _hgrn_forget(z, lb):
    e = jnp.exp(-jnp.abs(z))
    one_e = 1.0 + e
    r = 1.0 / one_e
    pos = z >= 0.0
    sig = jnp.where(pos, 1.0, e) * r
    sig_neg = jnp.where(pos, e, 1.0) * r
    oml = 1.0 - lb
    log_sig = jnp.minimum(z, 0.0) - jnp.log(one_e)
    log_f = jnp.where(lb > 0.0, jnp.log(lb + oml * sig), log_sig)
    return log_f, oml * sig_neg


def _hgrn_kernel(*refs, seq_len, chunk, n_lev, heads, has_s0, n_prev):
    it = iter(refs)
    (q_ref, laf_ref, lab_ref, kf_ref, kb_ref, i_ref, g_ref, trif_ref, trib_ref, lvf_ref, lvb_ref,
     leaf_ref) = (next(it) for _ in range(12))
    s0_ref = next(it) if has_s0 else None
    prev_ref = next(it) if n_prev else None
    o_ref = next(it)
    st_ref = None if has_s0 else next(it)
    cum_ref, qt_ref, kt_ref, qd_ref, kd_ref, ae_ref, s_ref, acc_ref = (next(it) for _ in range(8))
    n_chunks = seq_len // chunk
    dk = DK_A
    units = [(2 * hh + d, hh, d, d == 1, la_in, tri_ref, lv_ref)
             for hh in range(heads)
             for d, (la_in, tri_ref, lv_ref) in enumerate(((laf_ref, trif_ref, lvf_ref),
                                                           (lab_ref, trib_ref, lvb_ref)))]
    key_in = (kf_ref, kb_ref)
    rows_of = lambda c: slice(c * chunk, (c + 1) * chunk)
    cols_of = lambda hh: slice(hh * dk, (hh + 1) * dk)

    worst = None
    for _, hh, _, _, la_in, _, _ in units:
        leaf_sum = jnp.dot(leaf_ref[...], la_in[:, cols_of(hh)].astype(BF16),
                           preferred_element_type=F32)
        worst = leaf_sum if worst is None else jnp.minimum(worst, leaf_sum)
    leafwise_ok = jnp.min(worst) > -LEAF_MAX_DECAY

    for u, hh, _, _, la_in, tri_ref, _ in units:
        tri = tri_ref[...]
        for c in range(n_chunks):
            la = la_in[rows_of(c), cols_of(hh)]
            la_hi = la.astype(BF16)
            la_lo = (la - la_hi.astype(F32)).astype(BF16)
            x2 = jnp.dot(tri, jnp.concatenate([la_hi, la_lo], axis=1),
                         preferred_element_type=F32)
            cum_ref[u, rows_of(c), :] = x2[:, :dk] + x2[:, dk:]

    def operands_and_scores(leafwise):
        if leafwise:
            levels = list(range(LEAF_LEVEL + 1, n_lev + 1))
        else:
            levels = list(range(1, n_lev + 1))
        n_terms = len(levels) + 1
        for u, hh, d, reverse, la_in, _, _ in units:
            for c in range(n_chunks):
                rows = rows_of(c)
                cum = cum_ref[u, rows, :]
                la = la_in[rows, cols_of(hh)]
                q = q_ref[rows, cols_of(hh)]
                k = key_in[d][rows, cols_of(hh)]
                cum_row = lambda r, u=u, c=c: cum_ref[u, c * chunk + r:c * chunk + r + 1, :]
                if leafwise:
                    factors = [_leaf_factors(cum, cum_row, reverse)]
                else:
                    factors = [None]
                for lev in levels:
                    e = _level_factor(la, cum, cum_row, lev, reverse)
                    factors.append((e, e))
                for t, f in enumerate(factors):
                    cols = slice(t * dk, (t + 1) * dk)
                    qt_ref[u, rows, cols] = q if f is None else q * f[0]
                    kt_ref[u, rows, cols] = k if f is None else k * f[1]
                end_row = 0 if reverse else chunk - 1
                cum_end = cum[end_row:end_row + 1]
                qd_ref[u, rows, :] = q * jnp.exp(cum).astype(BF16)
                kd_ref[u, rows, :] = k * jnp.exp(cum_end - cum).astype(BF16)
                ae_ref[u, c:c + 1, :] = jnp.exp(cum_end)
        zero = jnp.zeros((chunk, dk), BF16)
        for u, _, _, _, _, _, lv_ref in units:
            lv = lv_ref[...]
            if leafwise:
                masks = [(lv >= 0) & (lv <= LEAF_LEVEL)]
            else:
                masks = [lv == 0]
            masks += [lv == lev for lev in levels]
            for c in range(n_chunks):
                rows = rows_of(c)
                scores = jnp.zeros((chunk, chunk), F32)
                for t in range(0, n_terms, 2):
                    if t + 1 < n_terms:
                        k_a = kt_ref[u, rows, t * dk:(t + 1) * dk]
                        k_b = kt_ref[u, rows, (t + 1) * dk:(t + 2) * dk]
                        p = lax.dot_general(
                            qt_ref[u, rows, t * dk:(t + 2) * dk],
                            jnp.concatenate([jnp.concatenate([k_a, zero], axis=1),
                                             jnp.concatenate([zero, k_b], axis=1)], axis=0),
                            _NT, preferred_element_type=F32)
                        scores = jnp.where(masks[t], p[:, :chunk], scores)
                        scores = jnp.where(masks[t + 1], p[:, chunk:], scores)
                    else:
                        p = lax.dot_general(qt_ref[u, rows, t * dk:(t + 1) * dk],
                                            kt_ref[u, rows, t * dk:(t + 1) * dk],
                                            _NT, preferred_element_type=F32)
                        scores = jnp.where(masks[t], p, scores)
                s_ref[u, rows, :] = scores.astype(BF16)

    pl.when(leafwise_ok)(lambda: operands_and_scores(True))
    pl.when(jnp.logical_not(leafwise_ok))(lambda: operands_and_scores(False))

    incs = {u: [lax.dot_general(i_ref[rows_of(c), cols_of(hh)], kd_ref[u, rows_of(c), :], _TN,
                                preferred_element_type=F32) for c in range(n_chunks)]
            for u, hh, _, _, _, _, _ in units}
    finals, befores = {}, {}
    for u, hh, d, reverse, _, _, _ in units:
        st = s0_ref[d, hh].T if has_s0 else jnp.zeros((DV_A, dk), F32)
        befores[u] = [None] * n_chunks
        for c in (range(n_chunks - 1, -1, -1) if reverse else range(n_chunks)):
            befores[u][c] = st.astype(BF16)
            st = st * ae_ref[u, c:c + 1, :] + incs[u][c]
        finals[u] = st
    for u, hh, d, reverse, _, _, _ in units:
        for c in range(n_chunks):
            rows = rows_of(c)
            o = jnp.dot(s_ref[u, rows, :], i_ref[rows, cols_of(hh)], preferred_element_type=F32)
            if has_s0 or c != (n_chunks - 1 if reverse else 0):
                o = o + lax.dot_general(qd_ref[u, rows, :], befores[u][c], _NT,
                                        preferred_element_type=F32)
            if d == 0:
                acc_ref[rows, cols_of(hh)] = o
            else:
                acc_ref[rows, cols_of(hh)] += o

    for hh in range(heads):
        o = acc_ref[:, cols_of(hh)]
        o = o * lax.rsqrt(jnp.mean(o * o, axis=-1, keepdims=True) + EPS)
        o_ref[:, cols_of(hh)] = (o * _silu(g_ref[:, cols_of(hh)].astype(F32))).astype(BF16)
    if st_ref is not None:
        if n_prev:
            st_ref[0:n_prev] = prev_ref[...]
        for u, hh, d, _, _, _, _ in units:
            st_ref[n_prev, d, hh] = finals[u].T


def _hgrn_call(proj, la, key, state, prev, layer):
    n_b, seq_len, _ = proj.shape
    tri_f, tri_b, lv_f, lv_b, n_lev = _hgrn_consts(HGRN_CHUNK)
    has_s0 = state is not None
    n_chunks = seq_len // HGRN_CHUNK
    heads = max(1, HGRN_UNITS // (2 * n_chunks))
    width = heads * DK_A
    col = lambda off: (lambda b, h: (b, 0, off // width + h))
    const = lambda a: pl.BlockSpec(a.shape, lambda b, h: (0, 0))
    bwd = W_OFF_ZB - W_OFF_ZF
    in_specs = [pl.BlockSpec((None, seq_len, width), col(o))
                for o in (OFF_QA, 0, bwd, 0, bwd, OFF_IA, OFF_GA)]
    leaf = 1 << LEAF_LEVEL
    leaf_rows = -(-(seq_len // leaf) // SUBLANES) * SUBLANES
    leaf_ind = (np.arange(seq_len)[None, :] // leaf == np.arange(leaf_rows)[:, None])
    in_specs += [const(tri_f), const(tri_b), const(lv_f), const(lv_b), const(leaf_ind)]
    args = [proj, la, la, key, key, proj, proj, jnp.asarray(tri_f, BF16), jnp.asarray(tri_b, BF16),
            jnp.asarray(lv_f), jnp.asarray(lv_b), jnp.asarray(leaf_ind, BF16)]
    out_specs = [pl.BlockSpec((None, seq_len, width), lambda b, h: (b, 0, h))]
    out_shape = [jax.ShapeDtypeStruct((n_b, seq_len, H_A * DV_A), BF16)]
    if has_s0:
        in_specs.append(pl.BlockSpec((None, None, 2, heads, DK_A, DV_A),
                                     lambda b, h: (b, layer, 0, h, 0, 0)))
        args.append(state)
    else:
        st_spec = lambda n: pl.BlockSpec((None, n, 2, heads, DK_A, DV_A),
                                         lambda b, h: (b, 0, 0, h, 0, 0))
        if layer:
            in_specs.append(st_spec(layer))
            args.append(prev)
        out_specs.append(st_spec(layer + 1))
        out_shape.append(jax.ShapeDtypeStruct((n_b, layer + 1, 2, H_A, DK_A, DV_A), F32))
    n_u = 2 * heads
    res = pl.pallas_call(
        functools.partial(_hgrn_kernel, seq_len=seq_len, chunk=HGRN_CHUNK, n_lev=n_lev,
                          heads=heads, has_s0=has_s0, n_prev=0 if has_s0 else layer),
        grid=(n_b, H_A // heads),
        in_specs=in_specs, out_specs=out_specs, out_shape=out_shape,
        scratch_shapes=[
            pltpu.VMEM((n_u, seq_len, DK_A), F32),
            pltpu.VMEM((n_u, seq_len, (n_lev + 1) * DK_A), BF16),
            pltpu.VMEM((n_u, seq_len, (n_lev + 1) * DK_A), BF16),
            pltpu.VMEM((n_u, seq_len, DK_A), BF16),
            pltpu.VMEM((n_u, seq_len, DK_A), BF16),
            pltpu.VMEM((n_u, max(SUBLANES, n_chunks), DK_A), F32),
            pltpu.VMEM((n_u, seq_len, HGRN_CHUNK), BF16),
            pltpu.VMEM((seq_len, width), F32),
        ],
        compiler_params=_params(("arbitrary", "arbitrary")),
        name="hgrn2_scan",
    )(*args)
    return (res[0], None) if has_s0 else (res[0], res[1])


def _rope_tables(seq_len):
    rows = seq_len // GRID_W
    r_idx = jnp.repeat(jnp.arange(rows), GRID_W).astype(F32)
    c_idx = jnp.tile(jnp.arange(GRID_W), rows).astype(F32)
    quarter = DK_B // 4
    inv = 1.0 / (ROPE_BASE ** (jnp.arange(quarter, dtype=F32) / quarter))
    ang_r = r_idx[:, None] * inv[None, :]
    ang_c = c_idx[:, None] * inv[None, :]
    cos = jnp.concatenate([jnp.cos(ang_r)] * 2 + [jnp.cos(ang_c)] * 2, axis=1)
    sin = jnp.concatenate([-jnp.sin(ang_r), jnp.sin(ang_r), -jnp.sin(ang_c), jnp.sin(ang_c)],
                          axis=1)
    return cos, sin


def _rope(x, cos, sin):
    half = DK_B // 2
    swapped = jnp.concatenate([pltpu.roll(x[:, :half], half // 2, axis=1),
                               pltpu.roll(x[:, half:], half // 2, axis=1)], axis=1)
    return x * cos + swapped * sin


def _ret_kernel(*refs, seq_len, n_seq, latent, has_s0, n_prev):
    it = iter(refs)
    lg_ref, dm_ref, q_ref, k_ref, v_ref, g_ref = (next(it) for _ in range(6))
    cos_ref, sin_ref = (next(it), next(it)) if latent else (None, None)
    s0_ref = next(it) if has_s0 else None
    prev_ref = next(it) if n_prev else None
    o_ref = next(it)
    st_ref = None if has_s0 else next(it)
    h = pl.program_id(0)
    lg_f = lg_ref[h, 0]
    lg_b = lg_ref[h, 1]

    qs, ks = [], []
    for s in range(n_seq):
        q, k = q_ref[s], k_ref[s]
        if latent:
            cos, sin = cos_ref[...], sin_ref[...]
            q, k = _rope(q.astype(F32), cos, sin), _rope(k.astype(F32), cos, sin)
        qs.append(q)
        ks.append(k)
    items = [(s, slice(r, r + RET_ROWS)) for s in range(n_seq)
             for r in range(0, seq_len, RET_ROWS)]
    ps = [lax.dot_general(qs[s][rows].astype(BF16), ks[s].astype(BF16), _NT,
                          preferred_element_type=F32) for s, rows in items]
    carried = []
    if has_s0:
        for s, rows in items:
            s0 = jnp.concatenate([s0_ref[s, 0], s0_ref[s, 1]], axis=0).astype(BF16)
            pos = (lax.broadcasted_iota(jnp.int32, (RET_ROWS, 1), 0) + rows.start).astype(F32)
            q_s = jnp.concatenate([qs[s][rows] * jnp.exp(lg_f * (pos + 1.0)),
                                   qs[s][rows] * jnp.exp(lg_b * (seq_len - pos))], axis=1)
            carried.append(jnp.dot(q_s.astype(BF16), s0, preferred_element_type=F32))
    pds = [p.astype(BF16) * dm_ref[rows, :] for p, (_, rows) in zip(ps, items)]
    outs = [jnp.dot(pd, v_ref[s], preferred_element_type=F32) for pd, (s, _) in zip(pds, items)]
    for i, (s, rows) in enumerate(items):
        o = outs[i] + carried[i] if has_s0 else outs[i]
        o = o * lax.rsqrt(jnp.mean(o * o, axis=-1, keepdims=True) + EPS)
        o_ref[s, rows, :] = (o * _silu(g_ref[s, rows, :].astype(F32))).astype(BF16)
    if st_ref is not None:
        if n_prev:
            st_ref[:, 0:n_prev] = prev_ref[...]
        pos = lax.broadcasted_iota(jnp.int32, (seq_len, 1), 0).astype(F32)
        w_f = K_SCALE * jnp.exp(lg_f * (seq_len - 1.0 - pos))
        w_b = K_SCALE * jnp.exp(lg_b * pos)
        for s in range(n_seq):
            st_ref[s, n_prev, 0] = lax.dot_general((ks[s] * w_f).astype(BF16), v_ref[s], _TN,
                                                   preferred_element_type=F32)
            st_ref[s, n_prev, 1] = lax.dot_general((ks[s] * w_b).astype(BF16), v_ref[s], _TN,
                                                   preferred_element_type=F32)


def _ret_log_decay():
    heads = jnp.arange(H_B, dtype=F32)
    lg_f = jnp.log1p(-jnp.exp2(-5.0 - heads))
    lg_b = jnp.log1p(-jnp.exp2(-(5.0 + RET_DECAY_OFFSET_BWD) - heads))
    return jnp.stack([lg_f, lg_b], axis=1)


def _ret_decay_mask(seq_len):
    lg = _ret_log_decay()
    d = (jnp.arange(seq_len)[:, None] - jnp.arange(seq_len)[None, :]).astype(F32)[None]
    lg_f, lg_b = lg[:, 0, None, None], lg[:, 1, None, None]
    dm = (jnp.where(d >= 0.0, jnp.exp(lg_f * jnp.maximum(d, 0.0)), 0.0)
          + jnp.where(d <= 0.0, jnp.exp(lg_b * jnp.maximum(-d, 0.0)), 0.0))
    return (K_SCALE * dm).astype(BF16)


def _ret_call(proj, state, prev, layer, latent):
    n_b, seq_len, _ = proj.shape
    has_s0 = state is not None
    n_seq = max(1, RET_TOKENS // seq_len)
    in_specs = [
        pl.BlockSpec(memory_space=pltpu.SMEM),
        pl.BlockSpec((None, seq_len, seq_len), lambda h, b: (h, 0, 0)),
        pl.BlockSpec((n_seq, seq_len, DK_B), lambda h, b: (b, 0, OFF_QB // DK_B + h)),
        pl.BlockSpec((n_seq, seq_len, DK_B), lambda h, b: (b, 0, OFF_KB // DK_B + h)),
        pl.BlockSpec((n_seq, seq_len, DV_B), lambda h, b: (b, 0, OFF_VB // DV_B + h)),
        pl.BlockSpec((n_seq, seq_len, DV_B), lambda h, b: (b, 0, OFF_GB // DV_B + h)),
    ]
    args = [_ret_log_decay(), _ret_decay_mask(seq_len), proj, proj, proj, proj]
    if latent:
        cos, sin = _rope_tables(seq_len)
        in_specs += [pl.BlockSpec((seq_len, DK_B), lambda h, b: (0, 0))] * 2
        args += [cos, sin]
    out_specs = [pl.BlockSpec((n_seq, seq_len, DV_B), lambda h, b: (b, 0, h))]
    out_shape = [jax.ShapeDtypeStruct((n_b, seq_len, H_B * DV_B), BF16)]
    if has_s0:
        in_specs.append(pl.BlockSpec((n_seq, None, 2, None, DK_B, DV_B),
                                     lambda h, b: (b, layer, 0, h, 0, 0)))
        args.append(state)
    else:
        st_spec = lambda n: pl.BlockSpec((n_seq, n, 2, None, DK_B, DV_B),
                                         lambda h, b: (b, 0, 0, h, 0, 0))
        if layer:
            in_specs.append(st_spec(layer))
            args.append(prev)
        out_specs.append(st_spec(layer + 1))
        out_shape.append(jax.ShapeDtypeStruct((n_b, layer + 1, 2, H_B, DK_B, DV_B), F32))
    res = pl.pallas_call(
        functools.partial(_ret_kernel, seq_len=seq_len, n_seq=n_seq, latent=latent,
                          has_s0=has_s0, n_prev=0 if has_s0 else layer),
        grid=(H_B, n_b // n_seq),
        in_specs=in_specs, out_specs=out_specs, out_shape=out_shape,
        compiler_params=_params(("arbitrary", "arbitrary")),
        name="retention_scan",
    )(*args)
    return (res[0], None) if has_s0 else (res[0], res[1])


def _post_kernel(oa_ref, ob_ref, gta_ref, gtb_ref, x_ref, gt1_ref, pa_ref, pb_ref, wo_ref,
                 o_ref):
    subs = [slice(r, r + POST_SUB) for r in range(0, POST_TILE, POST_SUB)]
    y_a = [jnp.dot(oa_ref[s, :], pa_ref[...], preferred_element_type=F32) for s in subs]
    y_b = [jnp.dot(ob_ref[s, :], pb_ref[...], preferred_element_type=F32) for s in subs]
    merged = [(_sigmoid(gta_ref[s, :].astype(F32)) * y_a[i]
               + _sigmoid(gtb_ref[s, :].astype(F32)) * y_b[i]).astype(BF16)
              for i, s in enumerate(subs)]
    y = [jnp.dot(m, wo_ref[...], preferred_element_type=F32) for m in merged]
    for i, s in enumerate(subs):
        o_ref[s, :] = x_ref[s, :] + gt1_ref[...] * y[i]


def _post_call(o_a, o_b, proj, x, mods, p_a, p_b, w_out, layer, row_fn):
    n_tok = x.shape[0]
    tile = lambda w: (lambda i, j: (i, w))
    pcol = lambda off: pl.BlockSpec((POST_TILE, 1024), tile(off // 1024))
    const = lambda a: _layer_spec(layer, a.shape[1:], lambda i, j: (0, 0))
    return pl.pallas_call(
        _post_kernel,
        grid=(n_tok // POST_TILE, 1),
        in_specs=[
            pl.BlockSpec((POST_TILE, H_A * DV_A), tile(0)),
            pl.BlockSpec((POST_TILE, H_B * DV_B), tile(0)),
            pcol(OFF_GATE_A), pcol(OFF_GATE_B),
            pl.BlockSpec((POST_TILE, D_MODEL), tile(0)),
            _mod_spec(layer, row_fn, 2, POST_TILE),
            const(p_a), const(p_b), const(w_out),
        ],
        out_specs=pl.BlockSpec((POST_TILE, D_MODEL), tile(0)),
        out_shape=jax.ShapeDtypeStruct((n_tok, D_MODEL), F32),
        compiler_params=_params(("arbitrary", "arbitrary")),
        name="mixer_output",
    )(o_a, o_b, proj, proj, x, mods, p_a, p_b, w_out)


def _conv3(u, wc, bc, seq_len):
    n = u.shape[0]
    row = lax.broadcasted_iota(jnp.int32, (SUBLANES, 1), 0)

    def zero_row(x, r0, r):
        return jnp.where(row == r, 0.0, x[r0:r0 + SUBLANES])

    prev, nxt = pltpu.roll(u, 1, axis=0), pltpu.roll(u, n - 1, axis=0)
    p_parts, n_parts = [], []
    for s0 in range(0, n, seq_len):
        s1 = s0 + seq_len
        p_parts += [zero_row(prev, s0, 0), prev[s0 + SUBLANES:s1]]
        n_parts += [nxt[s0:s1 - SUBLANES], zero_row(nxt, s1 - SUBLANES, SUBLANES - 1)]
    prev, nxt = jnp.concatenate(p_parts, axis=0), jnp.concatenate(n_parts, axis=0)
    return prev * wc[0:1] + u * wc[1:2] + nxt * wc[2:3] + bc


def _ffn_kernel(*refs, seq_len, final):
    it = iter(refs)
    (x_ref, sh_ref, sc_ref, gt_ref, n2_ref, wa_ref, wg_ref, wca_ref, wcg_ref, bca_ref, bcg_ref,
     wd_ref) = (next(it) for _ in range(12))
    fn_ref = next(it) if final else None
    o_ref, h_ref, acc_ref = next(it), next(it), next(it)
    j = pl.program_id(1)

    @pl.when(j == 0)
    def _():
        h_ref[...] = _norm_mod(x_ref[...], n2_ref[...], sc_ref[...], sh_ref[...]).astype(BF16)
        acc_ref[...] = jnp.zeros_like(acc_ref)

    h = h_ref[...]
    subs = [slice(c, min(c + FF_SUB, FF_CHUNK)) for c in range(0, FF_CHUNK, FF_SUB)]
    u_a = [jnp.dot(h, wa_ref[:, s], preferred_element_type=F32) for s in subs]
    u_g = [jnp.dot(h, wg_ref[:, s], preferred_element_type=F32) for s in subs]
    act = [(_silu(_conv3(u_g[i], wcg_ref[:, s], bcg_ref[:, s], seq_len))
            * _conv3(u_a[i], wca_ref[:, s], bca_ref[:, s], seq_len)).astype(BF16)
           for i, s in enumerate(subs)]
    acc_ref[...] += jnp.dot(jnp.concatenate(act, axis=1), wd_ref[...],
                            preferred_element_type=F32)

    @pl.when(j == pl.num_programs(1) - 1)
    def _():
        x = x_ref[...] + gt_ref[...] * acc_ref[...]
        if final:
            x = x * lax.rsqrt(jnp.mean(x * x, axis=-1, keepdims=True) + EPS) * fn_ref[...]
        o_ref[...] = x


def _ffn_call(x, mods, n2, w_up, w_conv, b_conv, w_down, final_norm, layer, row_fn, seq_len):
    n_tok = x.shape[0]
    n_ff = D_FF // FF_CHUNK
    lspec = functools.partial(_layer_spec, layer)
    in_specs = [
        pl.BlockSpec((TOKEN_TILE, D_MODEL), lambda i, j: (i, 0)),
        _mod_spec(layer, row_fn, 3), _mod_spec(layer, row_fn, 4), _mod_spec(layer, row_fn, 5),
        lspec((1, D_MODEL), lambda i, j: (0, 0)),
        lspec((D_MODEL, FF_CHUNK), lambda i, j: (0, j)),
        lspec((D_MODEL, FF_CHUNK), lambda i, j: (0, n_ff + j)),
        lspec((3, FF_CHUNK), lambda i, j: (0, j)),
        lspec((3, FF_CHUNK), lambda i, j: (0, n_ff + j)),
        lspec((1, FF_CHUNK), lambda i, j: (0, j)),
        lspec((1, FF_CHUNK), lambda i, j: (0, n_ff + j)),
        lspec((FF_CHUNK, D_MODEL), lambda i, j: (j, 0)),
    ]
    args = [x, mods, mods, mods, n2, w_up, w_up, w_conv, w_conv, b_conv, b_conv, w_down]
    final = final_norm is not None
    if final:
        in_specs.append(pl.BlockSpec((1, D_MODEL), lambda i, j: (0, 0)))
        args.append(final_norm.reshape(1, D_MODEL))
    return pl.pallas_call(
        functools.partial(_ffn_kernel, seq_len=seq_len, final=final),
        grid=(n_tok // TOKEN_TILE, n_ff),
        in_specs=in_specs,
        out_specs=pl.BlockSpec((TOKEN_TILE, D_MODEL), lambda i, j: (i, 0)),
        out_shape=jax.ShapeDtypeStruct((n_tok, D_MODEL), F32),
        scratch_shapes=[pltpu.VMEM((TOKEN_TILE, D_MODEL), BF16),
                        pltpu.VMEM((TOKEN_TILE, D_MODEL), F32)],
        compiler_params=_params(("arbitrary", "arbitrary")),
        name="conv_ffn",
    )(*args)


def kernel(x_prompt, x_sample, state_hgrn, state_ret, c, c_ctx, norm1, norm2, final_norm,
           w_mod, b_mod, w_in, hgrn_lb_raw, p_a, p_b, w_out, w_up, w_conv, b_conv, w_down):
    n_ctx, t_ctx, _ = x_prompt.shape
    n_dec, t_dec, _ = x_sample.shape
    assert t_ctx & (t_ctx - 1) == 0 and TOKEN_TILE % t_ctx == 0 and t_dec == TOKEN_TILE

    sm = jax.nn.softmax(hgrn_lb_raw.astype(F32), axis=0)
    cum = jnp.cumsum(sm, axis=0)
    lower_bounds = cum - cum[0:1]

    cvec = jnp.concatenate(
        [c_ctx[None, :], c, jnp.zeros((MOD_ROWS - 1 - n_dec, D_MODEL), F32)], axis=0)
    mods = _mod_call(cvec, w_mod, b_mod).reshape(DEPTH, MOD_ROWS, 1, 6 * D_MODEL)

    p_a_b, p_b_b, w_out_b, w_up_b, w_down_b = (
        w.astype(BF16) for w in (p_a, p_b, w_out, w_up, w_down))
    norm1_3, norm2_3 = norm1.reshape(DEPTH, 1, D_MODEL), norm2.reshape(DEPTH, 1, D_MODEL)
    b_conv_3 = b_conv.reshape(DEPTH, 1, 2 * D_FF)

    ctx_row = lambda tok: 0
    dec_row = lambda tok: tok // t_dec + 1

    def layer(x, l, n_b, seq_len, row_fn, s_hgrn, s_ret, prev_h, prev_r, latent):
        proj, la, key = _proj_call(x, mods, norm1_3, w_in, lower_bounds, l, row_fn)
        proj3 = proj.reshape(n_b, seq_len, PROJ_WIDTH)
        o_a, st_h = _hgrn_call(proj3, la.reshape(n_b, seq_len, -1), key.reshape(n_b, seq_len, -1),
                               s_hgrn, prev_h, l)
        o_b, st_r = _ret_call(proj3, s_ret, prev_r, l, latent)
        x = _post_call(o_a.reshape(-1, H_A * DV_A), o_b.reshape(-1, H_B * DV_B), proj, x,
                       mods, p_a_b, p_b_b, w_out_b, l, row_fn)
        x = _ffn_call(x, mods, norm2_3, w_up_b, w_conv, b_conv_3, w_down_b,
                      final_norm if l == DEPTH - 1 else None, l, row_fn, seq_len)
        return x, st_h, st_r

    x = x_prompt.reshape(n_ctx * t_ctx, D_MODEL)
    st_h = st_r = None
    for l in range(DEPTH):
        x, st_h, st_r = layer(x, l, n_ctx, t_ctx, ctx_row, None, None, st_h, st_r, False)
    y_prompt = x.reshape(n_ctx, t_ctx, D_MODEL)

    x = x_sample.reshape(n_dec * t_dec, D_MODEL)
    for l in range(DEPTH):
        x, _, _ = layer(x, l, n_dec, t_dec, dec_row, state_hgrn, state_ret, None, None, True)
    y_sample = x.reshape(n_dec, t_dec, D_MODEL)
    return (y_prompt, y_sample, st_h, st_r)
```

```python
import functools

import numpy as np
import jax
import jax.numpy as jnp
from jax import lax
from jax.experimental import pallas as pl
from jax.experimental.pallas import tpu as pltpu

F32 = jnp.float32
BF16 = jnp.bfloat16

D_MODEL = 1024
DEPTH = 2
GRID_W = 64
H_A, DK_A, DV_A = 8, 128, 128
H_B, DK_B, DV_B = 4, 256, 512
D_FF = 2816
ROPE_BASE = 10000.0
K_SCALE = DK_B ** -0.5
EPS = 1e-6
RET_DECAY_OFFSET_BWD = 0.5
IN_WIDTH = 13312

W_OFF_ZF, W_OFF_ZB, W_OFF_IA = 1024, 2048, 3072
OFF_QA, OFF_IA, OFF_GA = 0, 1024, 2048
OFF_QB, OFF_KB, OFF_VB, OFF_GB = 3072, 4096, 5120, 7168
OFF_GATE_A, OFF_GATE_B = 9216, 10240
PROJ_WIDTH = 11264

MOD_ROWS = 8
TOKEN_TILE = 1024
POST_TILE = 512
POST_SUB = 256
FF_CHUNK = 1408
FF_SUB = 256
HGRN_CHUNK = 128
HGRN_UNITS = 32
LEAF_LEVEL = 5
LEAF_MAX_DECAY = 60.0
RET_ROWS = 256
RET_TOKENS = 1024
PROJ_TN = 1024
Z_TILE0 = W_OFF_ZF // PROJ_TN
Z_TILES = (W_OFF_IA - W_OFF_ZF) // PROJ_TN
VMEM_LIMIT = 52 * 1024 * 1024
SUBLANES = 8
LOG2E = 1.4426950408889634

_NT = (((1,), (1,)), ((), ()))
_TN = (((0,), (0,)), ((), ()))


def _params(sem):
    return pltpu.CompilerParams(dimension_semantics=sem, vmem_limit_bytes=VMEM_LIMIT)


def _sigmoid(x):
    return jax.nn.sigmoid(x)


def _silu(x):
    return x * jax.nn.sigmoid(x)


def _mod_kernel(c_ref, w_ref, b_ref, o_ref):
    s = _silu(c_ref[...])
    o_ref[...] = jnp.dot(s, w_ref[...], precision=lax.Precision.HIGHEST,
                         preferred_element_type=F32) + b_ref[...]


def _mod_call(cvec, w_mod, b_mod):
    n_col = 6 * D_MODEL // 1024
    return pl.pallas_call(
        _mod_kernel,
        grid=(DEPTH, n_col),
        in_specs=[
            pl.BlockSpec((MOD_ROWS, D_MODEL), lambda l, j: (0, 0)),
            pl.BlockSpec((None, D_MODEL, 1024), lambda l, j: (l, 0, j)),
            pl.BlockSpec((None, 1, 1024), lambda l, j: (l, 0, j)),
        ],
        out_specs=pl.BlockSpec((None, MOD_ROWS, 1024), lambda l, j: (l, 0, j)),
        out_shape=jax.ShapeDtypeStruct((DEPTH, MOD_ROWS, 6 * D_MODEL), F32),
        compiler_params=_params(("arbitrary", "arbitrary")),
        name="modulation",
    )(cvec, w_mod, b_mod.reshape(DEPTH, 1, 6 * D_MODEL))


def _mod_spec(layer, row_fn, which, tile=TOKEN_TILE):
    return pl.BlockSpec((None, None, 1, D_MODEL),
                        lambda i, j: (layer, row_fn(i * tile), 0, which))


def _layer_spec(layer, block, index_map):
    return pl.BlockSpec((None,) + block, lambda *g: (layer,) + index_map(*g))


def _norm_mod(x, g, sc, sh):
    y = x * lax.rsqrt(jnp.mean(x * x, axis=-1, keepdims=True) + EPS) * g
    return y * (1.0 + sc) + sh


def _proj_kernel(x_ref, sh_ref, sc_ref, n_ref, w_ref, o_ref, hout_ref, h_ref, wb_ref):
    j, i = pl.program_id(0), pl.program_id(1)

    @pl.when(i == 0)
    def _():
        wb_ref[...] = w_ref[...].astype(BF16)

    @pl.when(j == 0)
    def _():
        h = _norm_mod(x_ref[...], n_ref[...], sc_ref[...], sh_ref[...]).astype(BF16)
        h_ref[i] = h
        hout_ref[...] = h

    o_ref[...] = jnp.dot(h_ref[i], wb_ref[...], preferred_element_type=F32).astype(BF16)


def _zproj_kernel(h_ref, w_ref, lb_ref, la_ref, key_ref, wb_ref):
    @pl.when(pl.program_id(1) == 0)
    def _():
        wb_ref[...] = w_ref[...].astype(BF16)

    z = jnp.dot(h_ref[...], wb_ref[...], preferred_element_type=F32)
    la, key = _hgrn_forget(z, lb_ref[...])
    la_ref[...] = la
    key_ref[...] = key.astype(BF16)


def _proj_call(x, mods, n1, w_in, lb, layer, row_fn):
    n_tok = x.shape[0]
    tn = PROJ_TN
    n_tile = n_tok // TOKEN_TILE
    hold = lambda j, i: jnp.where(j == 0, i, n_tile - 1)
    w_tile = lambda j: jnp.where(j >= Z_TILE0, j + Z_TILES, j)
    mod = lambda which: pl.BlockSpec(
        (None, None, 1, D_MODEL), lambda j, i: (layer, row_fn(i * TOKEN_TILE), 0, which))
    proj, h = pl.pallas_call(
        _proj_kernel,
        grid=(PROJ_WIDTH // tn, n_tile),
        in_specs=[
            pl.BlockSpec((TOKEN_TILE, D_MODEL), lambda j, i: (hold(j, i), 0)),
            mod(0), mod(1),
            _layer_spec(layer, (1, D_MODEL), lambda j, i: (0, 0)),
            _layer_spec(layer, (D_MODEL, tn), lambda j, i: (0, w_tile(j))),
        ],
        out_specs=[
            pl.BlockSpec((TOKEN_TILE, tn), lambda j, i: (i, j)),
            pl.BlockSpec((TOKEN_TILE, D_MODEL), lambda j, i: (hold(j, i), 0)),
        ],
        out_shape=[jax.ShapeDtypeStruct((n_tok, PROJ_WIDTH), BF16),
                   jax.ShapeDtypeStruct((n_tok, D_MODEL), BF16)],
        scratch_shapes=[pltpu.VMEM((n_tile, TOKEN_TILE, D_MODEL), BF16),
                        pltpu.VMEM((D_MODEL, tn), BF16)],
        compiler_params=_params(("arbitrary", "arbitrary")),
        name="in_projection",
    )(x, mods, mods, n1, w_in)
    la, key = pl.pallas_call(
        _zproj_kernel,
        grid=(Z_TILES, n_tile),
        in_specs=[
            pl.BlockSpec((TOKEN_TILE, D_MODEL), lambda j, i: (i, 0)),
            _layer_spec(layer, (D_MODEL, tn), lambda j, i: (0, Z_TILE0 + j)),
            pl.BlockSpec((None, None, 1, tn), lambda j, i: (layer, j, 0, 0)),
        ],
        out_specs=[pl.BlockSpec((TOKEN_TILE, tn), lambda j, i: (i, j))] * 2,
        out_shape=[jax.ShapeDtypeStruct((n_tok, Z_TILES * tn), F32),
                   jax.ShapeDtypeStruct((n_tok, Z_TILES * tn), BF16)],
        scratch_shapes=[pltpu.VMEM((D_MODEL, tn), BF16)],
        compiler_params=_params(("arbitrary", "arbitrary")),
        name="forget_projection",
    )(h, w_in, lb.reshape(DEPTH, Z_TILES, 1, tn))
    return proj, la, key


def _hgrn_consts(c):
    nl = int(np.log2(c))
    t = np.arange(c)
    tt, rr = t[:, None], t[None, :]
    x = tt ^ rr
    lev_of = np.where(x > 0, np.floor(np.log2(np.maximum(x, 1))) + 1, 0).astype(np.int32)
    lv_f = np.where(tt >= rr, lev_of, -1).astype(np.int32)
    lv_b = lv_f.T.copy()
    return ((rr <= tt).astype(np.float32), (rr >= tt).astype(np.float32), lv_f, lv_b, nl)


def _leaf_factors(cum, cum_row, reverse):
    c, dk = cum.shape
    leaf = 1 << LEAF_LEVEL
    pieces = []
    for base in range(0, c, leaf):
        row = base + leaf if reverse else base - 1
        if 0 <= row < c:
            pieces.append(jnp.broadcast_to(cum_row(row), (leaf, dk)))
        else:
            pieces.append(jnp.zeros((leaf, dk), F32))
    x = cum - jnp.concatenate(pieces, axis=0)
    return jnp.exp2(x * LOG2E).astype(BF16), jnp.exp2(x * (-LOG2E)).astype(BF16)


def _level_factor(la, cum, cum_row, lev, reverse):
    c, dk = la.shape
    if lev == 1:
        row = lax.broadcasted_iota(jnp.int32, (c, 1), 0)
        return jnp.exp(jnp.where((row & 1) == (0 if reverse else 1), la, 0.0)).astype(BF16)
    blk, half = 1 << lev, 1 << (lev - 1)

    def boundary(r):
        base = (r // blk) * blk
        return jnp.broadcast_to(cum_row(base + half if reverse else base + half - 1),
                                (SUBLANES, dk))

    top = lax.broadcasted_iota(jnp.int32, (SUBLANES, dk), 0) < SUBLANES // 2
    pieces = []
    for r0 in range(0, c, SUBLANES):
        if blk >= SUBLANES:
            pieces.append(boundary(r0))
        else:
            pieces.append(jnp.where(top, boundary(r0), boundary(r0 + SUBLANES // 2)))
    return jnp.exp2(jnp.abs(cum - jnp.concatenate(pieces, axis=0)) * (-LOG2E)).astype(BF16)


def _hgrn_forget(z, lb):
    e = jnp.exp(-jnp.abs(z))
    one_e = 1.0 + e
    r = 1.0 / one_e
    pos = z >= 0.0
    sig = jnp.where(pos, 1.0, e) * r
    sig_neg = jnp.where(pos, e, 1.0) * r
    oml = 1.0 - lb
    log_sig = jnp.minimum(z, 0.0) - jnp.log(one_e)
    log_f = jnp.where(lb > 0.0, jnp.log(lb + oml * sig), log_sig)
    return log_f, oml * sig_neg


def _hgrn_kernel(*refs, seq_len, chunk, n_lev, heads, has_s0, n_prev):
    it = iter(refs)
    (q_ref, laf_ref, lab_ref, kf_ref, kb_ref, i_ref, g_ref, trif_ref, trib_ref, lvf_ref, lvb_ref,
     leaf_ref) = (next(it) for _ in range(12))
    s0_ref = next(it) if has_s0 else None
    prev_ref = next(it) if n_prev else None
    o_ref = next(it)
    st_ref = None if has_s0 else next(it)
    cum_ref, qt_ref, kt_ref, qd_ref, kd_ref, ae_ref, s_ref, acc_ref = (next(it) for _ in range(8))
    n_chunks = seq_len // chunk
    dk = DK_A
    units = [(2 * hh + d, hh, d, d == 1, la_in, tri_ref, lv_ref)
             for hh in range(heads)
             for d, (la_in, tri_ref, lv_ref) in enumerate(((laf_ref, trif_ref, lvf_ref),
                                                           (lab_ref, trib_ref, lvb_ref)))]
    key_in = (kf_ref, kb_ref)
    rows_of = lambda c: slice(c * chunk, (c + 1) * chunk)
    cols_of = lambda hh: slice(hh * dk, (hh + 1) * dk)

    worst = None
    for _, hh, _, _, la_in, _, _ in units:
        leaf_sum = jnp.dot(leaf_ref[...], la_in[:, cols_of(hh)].astype(BF16),
                           preferred_element_type=F32)
        worst = leaf_sum if worst is None else jnp.minimum(worst, leaf_sum)
    leafwise_ok = jnp.min(worst) > -LEAF_MAX_DECAY

    for u, hh, _, _, la_in, tri_ref, _ in units:
        tri = tri_ref[...]
        for c in range(n_chunks):
            la = la_in[rows_of(c), cols_of(hh)]
            la_hi = la.astype(BF16)
            la_lo = (la - la_hi.astype(F32)).astype(BF16)
            x2 = jnp.dot(tri, jnp.concatenate([la_hi, la_lo], axis=1),
                         preferred_element_type=F32)
            cum_ref[u, rows_of(c), :] = x2[:, :dk] + x2[:, dk:]

    def operands_and_scores(leafwise):
        if leafwise:
            levels = list(range(LEAF_LEVEL + 1, n_lev + 1))
        else:
            levels = list(range(1, n_lev + 1))
        n_terms = len(levels) + 1
        for u, hh, d, reverse, la_in, _, _ in units:
            for c in range(n_chunks):
                rows = rows_of(c)
                cum = cum_ref[u, rows, :]
                la = la_in[rows, cols_of(hh)]
                q = q_ref[rows, cols_of(hh)]
                k = key_in[d][rows, cols_of(hh)]
                cum_row = lambda r, u=u, c=c: cum_ref[u, c * chunk + r:c * chunk + r + 1, :]
                if leafwise:
                    factors = [_leaf_factors(cum, cum_row, reverse)]
                else:
                    factors = [None]
                for lev in levels:
                    e = _level_factor(la, cum, cum_row, lev, reverse)
                    factors.append((e, e))
                for t, f in enumerate(factors):
                    cols = slice(t * dk, (t + 1) * dk)
                    qt_ref[u, rows, cols] = q if f is None else q * f[0]
                    kt_ref[u, rows, cols] = k if f is None else k * f[1]
                end_row = 0 if reverse else chunk - 1
                cum_end = cum[end_row:end_row + 1]
                qd_ref[u, rows, :] = q * jnp.exp(cum).astype(BF16)
                kd_ref[u, rows, :] = k * jnp.exp(cum_end - cum).astype(BF16)
                ae_ref[u, c:c + 1, :] = jnp.exp(cum_end)
        zero = jnp.zeros((chunk, dk), BF16)
        for u, _, _, _, _, _, lv_ref in units:
            lv = lv_ref[...]
            if leafwise:
                masks = [(lv >= 0) & (lv <= LEAF_LEVEL)]
            else:
                masks = [lv == 0]
            masks += [lv == lev for lev in levels]
            for c in range(n_chunks):
                rows = rows_of(c)
                scores = jnp.zeros((chunk, chunk), F32)
                for t in range(0, n_terms, 2):
                    if t + 1 < n_terms:
                        k_a = kt_ref[u, rows, t * dk:(t + 1) * dk]
                        k_b = kt_ref[u, rows, (t + 1) * dk:(t + 2) * dk]
                        p = lax.dot_general(
                            qt_ref[u, rows, t * dk:(t + 2) * dk],
                            jnp.concatenate([jnp.concatenate([k_a, zero], axis=1),
                                             jnp.concatenate([zero, k_b], axis=1)], axis=0),
                            _NT, preferred_element_type=F32)
                        scores = jnp.where(masks[t], p[:, :chunk], scores)
                        scores = jnp.where(masks[t + 1], p[:, chunk:], scores)
                    else:
                        p = lax.dot_general(qt_ref[u, rows, t * dk:(t + 1) * dk],
                                            kt_ref[u, rows, t * dk:(t + 1) * dk],
                                            _NT, preferred_element_type=F32)
                        scores = jnp.where(masks[t], p, scores)
                s_ref[u, rows, :] = scores.astype(BF16)

    pl.when(leafwise_ok)(lambda: operands_and_scores(True))
    pl.when(jnp.logical_not(leafwise_ok))(lambda: operands_and_scores(False))

    incs = {u: [lax.dot_general(i_ref[rows_of(c), cols_of(hh)], kd_ref[u, rows_of(c), :], _TN,
                                preferred_element_type=F32) for c in range(n_chunks)]
            for u, hh, _, _, _, _, _ in units}
    finals, befores = {}, {}
    for u, hh, d, reverse, _, _, _ in units:
        st = s0_ref[d, hh].T if has_s0 else jnp.zeros((DV_A, dk), F32)
        befores[u] = [None] * n_chunks
        for c in (range(n_chunks - 1, -1, -1) if reverse else range(n_chunks)):
            befores[u][c] = st.astype(BF16)
            st = st * ae_ref[u, c:c + 1, :] + incs[u][c]
        finals[u] = st
    for u, hh, d, reverse, _, _, _ in units:
        for c in range(n_chunks):
            rows = rows_of(c)
            o = jnp.dot(s_ref[u, rows, :], i_ref[rows, cols_of(hh)], preferred_element_type=F32)
            if has_s0 or c != (n_chunks - 1 if reverse else 0):
                o = o + lax.dot_general(qd_ref[u, rows, :], befores[u][c], _NT,
                                        preferred_element_type=F32)
            if d == 0:
                acc_ref[rows, cols_of(hh)] = o
            else:
                acc_ref[rows, cols_of(hh)] += o

    for hh in range(heads):
        o = acc_ref[:, cols_of(hh)]
        o = o * lax.rsqrt(jnp.mean(o * o, axis=-1, keepdims=True) + EPS)
        o_ref[:, cols_of(hh)] = (o * _silu(g_ref[:, cols_of(hh)].astype(F32))).astype(BF16)
    if st_ref is not None:
        if n_prev:
            st_ref[0:n_prev] = prev_ref[...]
        for u, hh, d, _, _, _, _ in units:
            st_ref[n_prev, d, hh] = finals[u].T


def _hgrn_call(proj, la, key, state, prev, layer):
    n_b, seq_len, _ = proj.shape
    tri_f, tri_b, lv_f, lv_b, n_lev = _hgrn_consts(HGRN_CHUNK)
    has_s0 = state is not None
    n_chunks = seq_len // HGRN_CHUNK
    heads = max(1, HGRN_UNITS // (2 * n_chunks))
    width = heads * DK_A
    col = lambda off: (lambda b, h: (b, 0, off // width + h))
    const = lambda a: pl.BlockSpec(a.shape, lambda b, h: (0, 0))
    bwd = W_OFF_ZB - W_OFF_ZF
    in_specs = [pl.BlockSpec((None, seq_len, width), col(o))
                for o in (OFF_QA, 0, bwd, 0, bwd, OFF_IA, OFF_GA)]
    leaf = 1 << LEAF_LEVEL
    leaf_rows = -(-(seq_len // leaf) // SUBLANES) * SUBLANES
    leaf_ind = (np.arange(seq_len)[None, :] // leaf == np.arange(leaf_rows)[:, None])
    in_specs += [const(tri_f), const(tri_b), const(lv_f), const(lv_b), const(leaf_ind)]
    args = [proj, la, la, key, key, proj, proj, jnp.asarray(tri_f, BF16), jnp.asarray(tri_b, BF16),
            jnp.asarray(lv_f), jnp.asarray(lv_b), jnp.asarray(leaf_ind, BF16)]
    out_specs = [pl.BlockSpec((None, seq_len, width), lambda b, h: (b, 0, h))]
    out_shape = [jax.ShapeDtypeStruct((n_b, seq_len, H_A * DV_A), BF16)]
    if has_s0:
        in_specs.append(pl.BlockSpec((None, None, 2, heads, DK_A, DV_A),
                                     lambda b, h: (b, layer, 0, h, 0, 0)))
        args.append(state)
    else:
        st_spec = lambda n: pl.BlockSpec((None, n, 2, heads, DK_A, DV_A),
                                         lambda b, h: (b, 0, 0, h, 0, 0))
        if layer:
            in_specs.append(st_spec(layer))
            args.append(prev)
        out_specs.append(st_spec(layer + 1))
        out_shape.append(jax.ShapeDtypeStruct((n_b, layer + 1, 2, H_A, DK_A, DV_A), F32))
    n_u = 2 * heads
    res = pl.pallas_call(
        functools.partial(_hgrn_kernel, seq_len=seq_len, chunk=HGRN_CHUNK, n_lev=n_lev,
                          heads=heads, has_s0=has_s0, n_prev=0 if has_s0 else layer),
        grid=(n_b, H_A // heads),
        in_specs=in_specs, out_specs=out_specs, out_shape=out_shape,
        scratch_shapes=[
            pltpu.VMEM((n_u, seq_len, DK_A), F32),
            pltpu.VMEM((n_u, seq_len, (n_lev + 1) * DK_A), BF16),
            pltpu.VMEM((n_u, seq_len, (n_lev + 1) * DK_A), BF16),
            pltpu.VMEM((n_u, seq_len, DK_A), BF16),
            pltpu.VMEM((n_u, seq_len, DK_A), BF16),
            pltpu.VMEM((n_u, max(SUBLANES, n_chunks), DK_A), F32),
            pltpu.VMEM((n_u, seq_len, HGRN_CHUNK), BF16),
            pltpu.VMEM((seq_len, width), F32),
        ],
        compiler_params=_params(("arbitrary", "arbitrary")),
        name="hgrn2_scan",
    )(*args)
    return (res[0], None) if has_s0 else (res[0], res[1])


def _rope_tables(seq_len):
    rows = seq_len // GRID_W
    r_idx = jnp.repeat(jnp.arange(rows), GRID_W).astype(F32)
    c_idx = jnp.tile(jnp.arange(GRID_W), rows).astype(F32)
    quarter = DK_B // 4
    inv = 1.0 / (ROPE_BASE ** (jnp.arange(quarter, dtype=F32) / quarter))
    ang_r = r_idx[:, None] * inv[None, :]
    ang_c = c_idx[:, None] * inv[None, :]
    cos = jnp.concatenate([jnp.cos(ang_r)] * 2 + [jnp.cos(ang_c)] * 2, axis=1)
    sin = jnp.concatenate([-jnp.sin(ang_r), jnp.sin(ang_r), -jnp.sin(ang_c), jnp.sin(ang_c)],
                          axis=1)
    return cos, sin


def _rope(x, cos, sin):
    half = DK_B // 2
    swapped = jnp.concatenate([pltpu.roll(x[:, :half], half // 2, axis=1),
                               pltpu.roll(x[:, half:], half // 2, axis=1)], axis=1)
    return x * cos + swapped * sin


def _ret_kernel(*refs, seq_len, n_seq, latent, has_s0, n_prev):
    it = iter(refs)
    lg_ref, dm_ref, q_ref, k_ref, v_ref, g_ref = (next(it) for _ in range(6))
    cos_ref, sin_ref = (next(it), next(it)) if latent else (None, None)
    s0_ref = next(it) if has_s0 else None
    prev_ref = next(it) if n_prev else None
    o_ref = next(it)
    st_ref = None if has_s0 else next(it)
    h = pl.program_id(0)
    lg_f = lg_ref[h, 0]
    lg_b = lg_ref[h, 1]

    qs, ks = [], []
    for s in range(n_seq):
        q, k = q_ref[s], k_ref[s]
        if latent:
            cos, sin = cos_ref[...], sin_ref[...]
            q, k = _rope(q.astype(F32), cos, sin), _rope(k.astype(F32), cos, sin)
        qs.append(q)
        ks.append(k)
    items = [(s, slice(r, r + RET_ROWS)) for s in range(n_seq)
             for r in range(0, seq_len, RET_ROWS)]
    ps = [lax.dot_general(qs[s][rows].astype(BF16), ks[s].astype(BF16), _NT,
                          preferred_element_type=F32) for s, rows in items]
    carried = []
    if has_s0:
        for s, rows in items:
            s0 = jnp.concatenate([s0_ref[s, 0], s0_ref[s, 1]], axis=0).astype(BF16)
            pos = (lax.broadcasted_iota(jnp.int32, (RET_ROWS, 1), 0) + rows.start).astype(F32)
            q_s = jnp.concatenate([qs[s][rows] * jnp.exp(lg_f * (pos + 1.0)),
                                   qs[s][rows] * jnp.exp(lg_b * (seq_len - pos))], axis=1)
            carried.append(jnp.dot(q_s.astype(BF16), s0, preferred_element_type=F32))
    pds = [p.astype(BF16) * dm_ref[rows, :] for p, (_, rows) in zip(ps, items)]
    outs = [jnp.dot(pd, v_ref[s], preferred_element_type=F32) for pd, (s, _) in zip(pds, items)]
    for i, (s, rows) in enumerate(items):
        o = outs[i] + carried[i] if has_s0 else outs[i]
        o = o * lax.rsqrt(jnp.mean(o * o, axis=-1, keepdims=True) + EPS)
        o_ref[s, rows, :] = (o * _silu(g_ref[s, rows, :].astype(F32))).astype(BF16)
    if st_ref is not None:
        if n_prev:
            st_ref[:, 0:n_prev] = prev_ref[...]
        pos = lax.broadcasted_iota(jnp.int32, (seq_len, 1), 0).astype(F32)
        w_f = K_SCALE * jnp.exp(lg_f * (seq_len - 1.0 - pos))
        w_b = K_SCALE * jnp.exp(lg_b * pos)
        for s in range(n_seq):
            st_ref[s, n_prev, 0] = lax.dot_general((ks[s] * w_f).astype(BF16), v_ref[s], _TN,
                                                   preferred_element_type=F32)
            st_ref[s, n_prev, 1] = lax.dot_general((ks[s] * w_b).astype(BF16), v_ref[s], _TN,
                                                   preferred_element_type=F32)


def _ret_log_decay():
    heads = jnp.arange(H_B, dtype=F32)
    lg_f = jnp.log1p(-jnp.exp2(-5.0 - heads))
    lg_b = jnp.log1p(-jnp.exp2(-(5.0 + RET_DECAY_OFFSET_BWD) - heads))
    return jnp.stack([lg_f, lg_b], axis=1)


def _ret_decay_mask(seq_len):
    lg = _ret_log_decay()
    d = (jnp.arange(seq_len)[:, None] - jnp.arange(seq_len)[None, :]).astype(F32)[None]
    lg_f, lg_b = lg[:, 0, None, None], lg[:, 1, None, None]
    dm = (jnp.where(d >= 0.0, jnp.exp(lg_f * jnp.maximum(d, 0.0)), 0.0)
          + jnp.where(d <= 0.0, jnp.exp(lg_b * jnp.maximum(-d, 0.0)), 0.0))
    return (K_SCALE * dm).astype(BF16)


def _ret_call(proj, state, prev, layer, latent):
    n_b, seq_len, _ = proj.shape
    has_s0 = state is not None
    n_seq = max(1, RET_TOKENS // seq_len)
    in_specs = [
        pl.BlockSpec(memory_space=pltpu.SMEM),
        pl.BlockSpec((None, seq_len, seq_len), lambda h, b: (h, 0, 0)),
        pl.BlockSpec((n_seq, seq_len, DK_B), lambda h, b: (b, 0, OFF_QB // DK_B + h)),
        pl.BlockSpec((n_seq, seq_len, DK_B), lambda h, b: (b, 0, OFF_KB // DK_B + h)),
        pl.BlockSpec((n_seq, seq_len, DV_B), lambda h, b: (b, 0, OFF_VB // DV_B + h)),
        pl.BlockSpec((n_seq, seq_len, DV_B), lambda h, b: (b, 0, OFF_GB // DV_B + h)),
    ]
    args = [_ret_log_decay(), _ret_decay_mask(seq_len), proj, proj, proj, proj]
    if latent:
        cos, sin = _rope_tables(seq_len)
        in_specs += [pl.BlockSpec((seq_len, DK_B), lambda h, b: (0, 0))] * 2
        args += [cos, sin]
    out_specs = [pl.BlockSpec((n_seq, seq_len, DV_B), lambda h, b: (b, 0, h))]
    out_shape = [jax.ShapeDtypeStruct((n_b, seq_len, H_B * DV_B), BF16)]
    if has_s0:
        in_specs.append(pl.BlockSpec((n_seq, None, 2, None, DK_B, DV_B),
                                     lambda h, b: (b, layer, 0, h, 0, 0)))
        args.append(state)
    else:
        st_spec = lambda n: pl.BlockSpec((n_seq, n, 2, None, DK_B, DV_B),
                                         lambda h, b: (b, 0, 0, h, 0, 0))
        if layer:
            in_specs.append(st_spec(layer))
            args.append(prev)
        out_specs.append(st_spec(layer + 1))
        out_shape.append(jax.ShapeDtypeStruct((n_b, layer + 1, 2, H_B, DK_B, DV_B), F32))
    res = pl.pallas_call(
        functools.partial(_ret_kernel, seq_len=seq_len, n_seq=n_seq, latent=latent,
                          has_s0=has_s0, n_prev=0 if has_s0 else layer),
        grid=(H_B, n_b // n_seq),
        in_specs=in_specs, out_specs=out_specs, out_shape=out_shape,
        compiler_params=_params(("arbitrary", "arbitrary")),
        name="retention_scan",
    )(*args)
    return (res[0], None) if has_s0 else (res[0], res[1])


def _post_kernel(oa_ref, ob_ref, gta_ref, gtb_ref, x_ref, gt1_ref, pa_ref, pb_ref, wo_ref,
                 o_ref):
    subs = [slice(r, r + POST_SUB) for r in range(0, POST_TILE, POST_SUB)]
    y_a = [jnp.dot(oa_ref[s, :], pa_ref[...], preferred_element_type=F32) for s in subs]
    y_b = [jnp.dot(ob_ref[s, :], pb_ref[...], preferred_element_type=F32) for s in subs]
    merged = [(_sigmoid(gta_ref[s, :].astype(F32)) * y_a[i]
               + _sigmoid(gtb_ref[s, :].astype(F32)) * y_b[i]).astype(BF16)
              for i, s in enumerate(subs)]
    y = [jnp.dot(m, wo_ref[...], preferred_element_type=F32) for m in merged]
    for i, s in enumerate(subs):
        o_ref[s, :] = x_ref[s, :] + gt1_ref[...] * y[i]


def _post_call(o_a, o_b, proj, x, mods, p_a, p_b, w_out, layer, row_fn):
    n_tok = x.shape[0]
    tile = lambda w: (lambda i, j: (i, w))
    pcol = lambda off: pl.BlockSpec((POST_TILE, 1024), tile(off // 1024))
    const = lambda a: _layer_spec(layer, a.shape[1:], lambda i, j: (0, 0))
    return pl.pallas_call(
        _post_kernel,
        grid=(n_tok // POST_TILE, 1),
        in_specs=[
            pl.BlockSpec((POST_TILE, H_A * DV_A), tile(0)),
            pl.BlockSpec((POST_TILE, H_B * DV_B), tile(0)),
            pcol(OFF_GATE_A), pcol(OFF_GATE_B),
            pl.BlockSpec((POST_TILE, D_MODEL), tile(0)),
            _mod_spec(layer, row_fn, 2, POST_TILE),
            const(p_a), const(p_b), const(w_out),
        ],
        out_specs=pl.BlockSpec((POST_TILE, D_MODEL), tile(0)),
        out_shape=jax.ShapeDtypeStruct((n_tok, D_MODEL), F32),
        compiler_params=_params(("arbitrary", "arbitrary")),
        name="mixer_output",
    )(o_a, o_b, proj, proj, x, mods, p_a, p_b, w_out)


def _conv3(u, wc, bc, seq_len):
    n = u.shape[0]
    row = lax.broadcasted_iota(jnp.int32, (SUBLANES, 1), 0)

    def zero_row(x, r0, r):
        return jnp.where(row == r, 0.0, x[r0:r0 + SUBLANES])

    prev, nxt = pltpu.roll(u, 1, axis=0), pltpu.roll(u, n - 1, axis=0)
    p_parts, n_parts = [], []
    for s0 in range(0, n, seq_len):
        s1 = s0 + seq_len
        p_parts += [zero_row(prev, s0, 0), prev[s0 + SUBLANES:s1]]
        n_parts += [nxt[s0:s1 - SUBLANES], zero_row(nxt, s1 - SUBLANES, SUBLANES - 1)]
    prev, nxt = jnp.concatenate(p_parts, axis=0), jnp.concatenate(n_parts, axis=0)
    return prev * wc[0:1] + u * wc[1:2] + nxt * wc[2:3] + bc


def _ffn_kernel(*refs, seq_len, final):
    it = iter(refs)
    (x_ref, sh_ref, sc_ref, gt_ref, n2_ref, wa_ref, wg_ref, wca_ref, wcg_ref, bca_ref, bcg_ref,
     wd_ref) = (next(it) for _ in range(12))
    fn_ref = next(it) if final else None
    o_ref, h_ref, acc_ref = next(it), next(it), next(it)
    j = pl.program_id(1)

    @pl.when(j == 0)
    def _():
        h_ref[...] = _norm_mod(x_ref[...], n2_ref[...], sc_ref[...], sh_ref[...]).astype(BF16)
        acc_ref[...] = jnp.zeros_like(acc_ref)

    h = h_ref[...]
    subs = [slice(c, min(c + FF_SUB, FF_CHUNK)) for c in range(0, FF_CHUNK, FF_SUB)]
    u_a = [jnp.dot(h, wa_ref[:, s], preferred_element_type=F32) for s in subs]
    u_g = [jnp.dot(h, wg_ref[:, s], preferred_element_type=F32) for s in subs]
    act = [(_silu(_conv3(u_g[i], wcg_ref[:, s], bcg_ref[:, s], seq_len))
            * _conv3(u_a[i], wca_ref[:, s], bca_ref[:, s], seq_len)).astype(BF16)
           for i, s in enumerate(subs)]
    acc_ref[...] += jnp.dot(jnp.concatenate(act, axis=1), wd_ref[...],
                            preferred_element_type=F32)

    @pl.when(j == pl.num_programs(1) - 1)
    def _():
        x = x_ref[...] + gt_ref[...] * acc_ref[...]
        if final:
            x = x * lax.rsqrt(jnp.mean(x * x, axis=-1, keepdims=True) + EPS) * fn_ref[...]
        o_ref[...] = x


def _ffn_call(x, mods, n2, w_up, w_conv, b_conv, w_down, final_norm, layer, row_fn, seq_len):
    n_tok = x.shape[0]
    n_ff = D_FF // FF_CHUNK
    lspec = functools.partial(_layer_spec, layer)
    in_specs = [
        pl.BlockSpec((TOKEN_TILE, D_MODEL), lambda i, j: (i, 0)),
        _mod_spec(layer, row_fn, 3), _mod_spec(layer, row_fn, 4), _mod_spec(layer, row_fn, 5),
        lspec((1, D_MODEL), lambda i, j: (0, 0)),
        lspec((D_MODEL, FF_CHUNK), lambda i, j: (0, j)),
        lspec((D_MODEL, FF_CHUNK), lambda i, j: (0, n_ff + j)),
        lspec((3, FF_CHUNK), lambda i, j: (0, j)),
        lspec((3, FF_CHUNK), lambda i, j: (0, n_ff + j)),
        lspec((1, FF_CHUNK), lambda i, j: (0, j)),
        lspec((1, FF_CHUNK), lambda i, j: (0, n_ff + j)),
        lspec((FF_CHUNK, D_MODEL), lambda i, j: (j, 0)),
    ]
    args = [x, mods, mods, mods, n2, w_up, w_up, w_conv, w_conv, b_conv, b_conv, w_down]
    final = final_norm is not None
    if final:
        in_specs.append(pl.BlockSpec((1, D_MODEL), lambda i, j: (0, 0)))
        args.append(final_norm.reshape(1, D_MODEL))
    return pl.pallas_call(
        functools.partial(_ffn_kernel, seq_len=seq_len, final=final),
        grid=(n_tok // TOKEN_TILE, n_ff),
        in_specs=in_specs,
        out_specs=pl.BlockSpec((TOKEN_TILE, D_MODEL), lambda i, j: (i, 0)),
        out_shape=jax.ShapeDtypeStruct((n_tok, D_MODEL), F32),
        scratch_shapes=[pltpu.VMEM((TOKEN_TILE, D_MODEL), BF16),
                        pltpu.VMEM((TOKEN_TILE, D_MODEL), F32)],
        compiler_params=_params(("arbitrary", "arbitrary")),
        name="conv_ffn",
    )(*args)


def kernel(x_prompt, x_sample, state_hgrn, state_ret, c, c_ctx, norm1, norm2, final_norm,
           w_mod, b_mod, w_in, hgrn_lb_raw, p_a, p_b, w_out, w_up, w_conv, b_conv, w_down):
    n_ctx, t_ctx, _ = x_prompt.shape
    n_dec, t_dec, _ = x_sample.shape
    assert t_ctx & (t_ctx - 1) == 0 and TOKEN_TILE % t_ctx == 0 and t_dec == TOKEN_TILE

    sm = jax.nn.softmax(hgrn_lb_raw.astype(F32), axis=0)
    cum = jnp.cumsum(sm, axis=0)
    lower_bounds = cum - cum[0:1]

    cvec = jnp.concatenate(
        [c_ctx[None, :], c, jnp.zeros((MOD_ROWS - 1 - n_dec, D_MODEL), F32)], axis=0)
    mods = _mod_call(cvec, w_mod, b_mod).reshape(DEPTH, MOD_ROWS, 1, 6 * D_MODEL)

    p_a_b, p_b_b, w_out_b, w_up_b, w_down_b = (
        w.astype(BF16) for w in (p_a, p_b, w_out, w_up, w_down))
    norm1_3, norm2_3 = norm1.reshape(DEPTH, 1, D_MODEL), norm2.reshape(DEPTH, 1, D_MODEL)
    b_conv_3 = b_conv.reshape(DEPTH, 1, 2 * D_FF)

    ctx_row = lambda tok: 0
    dec_row = lambda tok: tok // t_dec + 1

    def layer(x, l, n_b, seq_len, row_fn, s_hgrn, s_ret, prev_h, prev_r, latent):
        proj, la, key = _proj_call(x, mods, norm1_3, w_in, lower_bounds, l, row_fn)
        proj3 = proj.reshape(n_b, seq_len, PROJ_WIDTH)
        o_a, st_h = _hgrn_call(proj3, la.reshape(n_b, seq_len, -1), key.reshape(n_b, seq_len, -1),
                               s_hgrn, prev_h, l)
        o_b, st_r = _ret_call(proj3, s_ret, prev_r, l, latent)
        x = _post_call(o_a.reshape(-1, H_A * DV_A), o_b.reshape(-1, H_B * DV_B), proj, x,
                       mods, p_a_b, p_b_b, w_out_b, l, row_fn)
        x = _ffn_call(x, mods, norm2_3, w_up_b, w_conv, b_conv_3, w_down_b,
                      final_norm if l == DEPTH - 1 else None, l, row_fn, seq_len)
        return x, st_h, st_r

    x = x_prompt.reshape(n_ctx * t_ctx, D_MODEL)
    st_h = st_r = None
    for l in range(DEPTH):
        x, st_h, st_r = layer(x, l, n_ctx, t_ctx, ctx_row, None, None, st_h, st_r, False)
    y_prompt = x.reshape(n_ctx, t_ctx, D_MODEL)

    x = x_sample.reshape(n_dec * t_dec, D_MODEL)
    for l in range(DEPTH):
        x, _, _ = layer(x, l, n_dec, t_dec, dec_row, state_hgrn, state_ret, None, None, True)
    y_sample = x.reshape(n_dec, t_dec, D_MODEL)
    return (y_prompt, y_sample, st_h, st_r)
```

```python
import functools

import numpy as np
import jax
import jax.numpy as jnp
from jax import lax
from jax.experimental import pallas as pl
from jax.experimental.pallas import tpu as pltpu

F32 = jnp.float32
BF16 = jnp.bfloat16

D_MODEL = 1024
DEPTH = 2
GRID_W = 64
H_A, DK_A, DV_A = 8, 128, 128
H_B, DK_B, DV_B = 4, 256, 512
D_FF = 2816
ROPE_BASE = 10000.0
K_SCALE = DK_B ** -0.5
EPS = 1e-6
RET_DECAY_OFFSET_BWD = 0.5

W_OFF_ZF, W_OFF_ZB, W_OFF_IA = 1024, 2048, 3072
OFF_QA, OFF_IA, OFF_GA = 0, 1024, 2048
OFF_QB, OFF_KB, OFF_VB, OFF_GB = 3072, 4096, 5120, 7168
OFF_GATE_A, OFF_GATE_B = 9216, 10240
PROJ_WIDTH = 11264

MOD_ROWS = 8
MOD_TN = 1024
TOKEN_TILE = 1024
POST_TILE = 512
POST_SUB = 256
FF_CHUNK = 1408
FF_SUB = 256
HGRN_CHUNK = 128
HGRN_UNITS = 32
LEAF_LEVEL = 5
LEAF_MAX_DECAY = 60.0
RET_ROWS = 256
RET_TOKENS = 1024
PROJ_TN = 1024
Z_TILE0 = W_OFF_ZF // PROJ_TN
Z_TILES = (W_OFF_IA - W_OFF_ZF) // PROJ_TN
VMEM_LIMIT = 52 * 1024 * 1024
SUBLANES = 8
LOG2E = 1.4426950408889634

_NT = (((1,), (1,)), ((), ()))
_TN = (((0,), (0,)), ((), ()))


def _params(sem):
    return pltpu.CompilerParams(dimension_semantics=sem, vmem_limit_bytes=VMEM_LIMIT)


def _sigmoid(x):
    return jax.nn.sigmoid(x)


def _silu(x):
    return x * jax.nn.sigmoid(x)


def _mod_kernel(c_ref, w_ref, b_ref, o_ref):
    s = _silu(c_ref[...])
    o_ref[...] = jnp.dot(s, w_ref[...], precision=lax.Precision.HIGHEST,
                         preferred_element_type=F32) + b_ref[...]


def _mod_call(cvec, w_mod, b_mod):
    n_col = 6 * D_MODEL // MOD_TN
    return pl.pallas_call(
        _mod_kernel,
        grid=(DEPTH, n_col),
        in_specs=[
            pl.BlockSpec((MOD_ROWS, D_MODEL), lambda l, j: (0, 0)),
            pl.BlockSpec((None, D_MODEL, MOD_TN), lambda l, j: (l, 0, j)),
            pl.BlockSpec((None, 1, MOD_TN), lambda l, j: (l, 0, j)),
        ],
        out_specs=pl.BlockSpec((None, MOD_ROWS, MOD_TN), lambda l, j: (l, 0, j)),
        out_shape=jax.ShapeDtypeStruct((DEPTH, MOD_ROWS, 6 * D_MODEL), F32),
        compiler_params=_params(("arbitrary", "arbitrary")),
        name="modulation",
    )(cvec, w_mod, b_mod.reshape(DEPTH, 1, 6 * D_MODEL))


def _mod_spec(layer, row_fn, which, tile=TOKEN_TILE):
    return pl.BlockSpec((None, None, 1, D_MODEL),
                        lambda i, j: (layer, row_fn(i * tile), 0, which))


def _layer_spec(layer, block, index_map):
    return pl.BlockSpec((None,) + block, lambda *g: (layer,) + index_map(*g))


def _norm_mod(x, g, sc, sh):
    y = x * lax.rsqrt(jnp.mean(x * x, axis=-1, keepdims=True) + EPS) * g
    return y * (1.0 + sc) + sh


def _proj_kernel(x_ref, sh_ref, sc_ref, n_ref, w_ref, o_ref, hout_ref, h_ref, wb_ref):
    j, i = pl.program_id(0), pl.program_id(1)

    @pl.when(i == 0)
    def _():
        wb_ref[...] = w_ref[...].astype(BF16)

    @pl.when(j == 0)
    def _():
        h = _norm_mod(x_ref[...], n_ref[...], sc_ref[...], sh_ref[...]).astype(BF16)
        h_ref[i] = h
        hout_ref[...] = h

    o_ref[...] = jnp.dot(h_ref[i], wb_ref[...], preferred_element_type=F32).astype(BF16)


def _zproj_kernel(h_ref, w_ref, lb_ref, la_ref, key_ref, wb_ref):
    @pl.when(pl.program_id(1) == 0)
    def _():
        wb_ref[...] = w_ref[...].astype(BF16)

    z = jnp.dot(h_ref[...], wb_ref[...], preferred_element_type=F32)
    la, key = _hgrn_forget(z, lb_ref[...])
    la_ref[...] = la
    key_ref[...] = key.astype(BF16)


def _proj_call(x, mods, n1, w_in, lb, layer, row_fn):
    n_tok = x.shape[0]
    tn = PROJ_TN
    n_tile = n_tok // TOKEN_TILE
    hold = lambda j, i: jnp.where(j == 0, i, n_tile - 1)
    w_tile = lambda j: jnp.where(j >= Z_TILE0, j + Z_TILES, j)
    mod = lambda which: pl.BlockSpec(
        (None, None, 1, D_MODEL), lambda j, i: (layer, row_fn(i * TOKEN_TILE), 0, which))
    proj, h = pl.pallas_call(
        _proj_kernel,
        grid=(PROJ_WIDTH // tn, n_tile),
        in_specs=[
            pl.BlockSpec((TOKEN_TILE, D_MODEL), lambda j, i: (hold(j, i), 0)),
            mod(0), mod(1),
            _layer_spec(layer, (1, D_MODEL), lambda j, i: (0, 0)),
            _layer_spec(layer, (D_MODEL, tn), lambda j, i: (0, w_tile(j))),
        ],
        out_specs=[
            pl.BlockSpec((TOKEN_TILE, tn), lambda j, i: (i, j)),
            pl.BlockSpec((TOKEN_TILE, D_MODEL), lambda j, i: (hold(j, i), 0)),
        ],
        out_shape=[jax.ShapeDtypeStruct((n_tok, PROJ_WIDTH), BF16),
                   jax.ShapeDtypeStruct((n_tok, D_MODEL), BF16)],
        scratch_shapes=[pltpu.VMEM((n_tile, TOKEN_TILE, D_MODEL), BF16),
                        pltpu.VMEM((D_MODEL, tn), BF16)],
        compiler_params=_params(("arbitrary", "arbitrary")),
        name="in_projection",
    )(x, mods, mods, n1, w_in)
    la, key = pl.pallas_call(
        _zproj_kernel,
        grid=(Z_TILES, n_tile),
        in_specs=[
            pl.BlockSpec((TOKEN_TILE, D_MODEL), lambda j, i: (i, 0)),
            _layer_spec(layer, (D_MODEL, tn), lambda j, i: (0, Z_TILE0 + j)),
            pl.BlockSpec((None, None, 1, tn), lambda j, i: (layer, j, 0, 0)),
        ],
        out_specs=[pl.BlockSpec((TOKEN_TILE, tn), lambda j, i: (i, j))] * 2,
        out_shape=[jax.ShapeDtypeStruct((n_tok, Z_TILES * tn), F32),
                   jax.ShapeDtypeStruct((n_tok, Z_TILES * tn), BF16)],
        scratch_shapes=[pltpu.VMEM((D_MODEL, tn), BF16)],
        compiler_params=_params(("arbitrary", "arbitrary")),
        name="forget_projection",
    )(h, w_in, lb.reshape(DEPTH, Z_TILES, 1, tn))
    return proj, la, key


def _hgrn_consts(c):
    nl = int(np.log2(c))
    t = np.arange(c)
    tt, rr = t[:, None], t[None, :]
    x = tt ^ rr
    lev_of = np.where(x > 0, np.floor(np.log2(np.maximum(x, 1))) + 1, 0).astype(np.int32)
    lv_f = np.where(tt >= rr, lev_of, -1).astype(np.int32)
    lv_b = lv_f.T.copy()
    return ((rr <= tt).astype(np.float32), (rr >= tt).astype(np.float32), lv_f, lv_b, nl)


def _leaf_factors(cum, cum_row, reverse):
    c, dk = cum.shape
    leaf = 1 << LEAF_LEVEL
    pieces = []
    for base in range(0, c, leaf):
        row = base + leaf if reverse else base - 1
        if 0 <= row < c:
            pieces.append(jnp.broadcast_to(cum_row(row), (leaf, dk)))
        else:
            pieces.append(jnp.zeros((leaf, dk), F32))
    x = cum - jnp.concatenate(pieces, axis=0)
    return jnp.exp2(x * LOG2E).astype(BF16), jnp.exp2(x * (-LOG2E)).astype(BF16)


def _level_factor(la, cum, cum_row, lev, reverse):
    c, dk = la.shape
    if lev == 1:
        row = lax.broadcasted_iota(jnp.int32, (c, 1), 0)
        return jnp.exp(jnp.where((row & 1) == (0 if reverse else 1), la, 0.0)).astype(BF16)
    blk, half = 1 << lev, 1 << (lev - 1)

    def boundary(r):
        base = (r // blk) * blk
        return jnp.broadcast_to(cum_row(base + half if reverse else base + half - 1),
                                (SUBLANES, dk))

    top = lax.broadcasted_iota(jnp.int32, (SUBLANES, dk), 0) < SUBLANES // 2
    pieces = []
    for r0 in range(0, c, SUBLANES):
        if blk >= SUBLANES:
            pieces.append(boundary(r0))
        else:
            pieces.append(jnp.where(top, boundary(r0), boundary(r0 + SUBLANES // 2)))
    return jnp.exp2(jnp.abs(cum - jnp.concatenate(pieces, axis=0)) * (-LOG2E)).astype(BF16)


def _hgrn_forget(z, lb):
    e = jnp.exp(-jnp.abs(z))
    one_e = 1.0 + e
    r = 1.0 / one_e
    pos = z >= 0.0
    sig = jnp.where(pos, 1.0, e) * r
    sig_neg = jnp.where(pos, e, 1.0) * r
    oml = 1.0 - lb
    log_sig = jnp.minimum(z, 0.0) - jnp.log(one_e)
    log_f = jnp.where(lb > 0.0, jnp.log(lb + oml * sig), log_sig)
    return log_f, oml * sig_neg


def _hgrn_kernel(*refs, seq_len, chunk, n_lev, heads, has_s0, n_prev):
    it = iter(refs)
    (q_ref, laf_ref, lab_ref, kf_ref, kb_ref, i_ref, g_ref, trif_ref, trib_ref, lvf_ref, lvb_ref,
     leaf_ref) = (next(it) for _ in range(12))
    s0_ref = next(it) if has_s0 else None
    prev_ref = next(it) if n_prev else None
    o_ref = next(it)
    st_ref = None if has_s0 else next(it)
    cum_ref, qt_ref, kt_ref, qd_ref, kd_ref, ae_ref, s_ref, acc_ref = (next(it) for _ in range(8))
    n_chunks = seq_len // chunk
    dk = DK_A
    units = [(2 * hh + d, hh, d, d == 1, la_in, tri_ref, lv_ref)
             for hh in range(heads)
             for d, (la_in, tri_ref, lv_ref) in enumerate(((laf_ref, trif_ref, lvf_ref),
                                                           (lab_ref, trib_ref, lvb_ref)))]
    key_in = (kf_ref, kb_ref)
    rows_of = lambda c: slice(c * chunk, (c + 1) * chunk)
    cols_of = lambda hh: slice(hh * dk, (hh + 1) * dk)

    worst = None
    for _, hh, _, _, la_in, _, _ in units:
        leaf_sum = jnp.dot(leaf_ref[...], la_in[:, cols_of(hh)].astype(BF16),
                           preferred_element_type=F32)
        worst = leaf_sum if worst is None else jnp.minimum(worst, leaf_sum)
    leafwise_ok = jnp.min(worst) > -LEAF_MAX_DECAY

    for u, hh, _, _, la_in, tri_ref, _ in units:
        tri = tri_ref[...]
        for c in range(n_chunks):
            la = la_in[rows_of(c), cols_of(hh)]
            la_hi = la.astype(BF16)
            la_lo = (la - la_hi.astype(F32)).astype(BF16)
            x2 = jnp.dot(tri, jnp.concatenate([la_hi, la_lo], axis=1),
                         preferred_element_type=F32)
            cum_ref[u, rows_of(c), :] = x2[:, :dk] + x2[:, dk:]

    def operands_and_scores(leafwise):
        if leafwise:
            levels = list(range(LEAF_LEVEL + 1, n_lev + 1))
        else:
            levels = list(range(1, n_lev + 1))
        n_terms = len(levels) + 1
        for u, hh, d, reverse, la_in, _, _ in units:
            for c in range(n_chunks):
                rows = rows_of(c)
                cum = cum_ref[u, rows, :]
                la = la_in[rows, cols_of(hh)]
                q = q_ref[rows, cols_of(hh)]
                k = key_in[d][rows, cols_of(hh)]
                cum_row = lambda r, u=u, c=c: cum_ref[u, c * chunk + r:c * chunk + r + 1, :]
                if leafwise:
                    factors = [_leaf_factors(cum, cum_row, reverse)]
                else:
                    factors = [None]
                for lev in levels:
                    e = _level_factor(la, cum, cum_row, lev, reverse)
                    factors.append((e, e))
                for t, f in enumerate(factors):
                    cols = slice(t * dk, (t + 1) * dk)
                    qt_ref[u, rows, cols] = q if f is None else q * f[0]
                    kt_ref[u, rows, cols] = k if f is None else k * f[1]
                end_row = 0 if reverse else chunk - 1
                cum_end = cum[end_row:end_row + 1]
                qd_ref[u, rows, :] = q * jnp.exp(cum).astype(BF16)
                kd_ref[u, rows, :] = k * jnp.exp(cum_end - cum).astype(BF16)
                ae_ref[u, c:c + 1, :] = jnp.exp(cum_end)
        zero = jnp.zeros((chunk, dk), BF16)
        for u, _, _, _, _, _, lv_ref in units:
            lv = lv_ref[...]
            if leafwise:
                masks = [(lv >= 0) & (lv <= LEAF_LEVEL)]
            else:
                masks = [lv == 0]
            masks += [lv == lev for lev in levels]
            for c in range(n_chunks):
                rows = rows_of(c)
                scores = jnp.zeros((chunk, chunk), F32)
                for t in range(0, n_terms, 2):
                    if t + 1 < n_terms:
                        k_a = kt_ref[u, rows, t * dk:(t + 1) * dk]
                        k_b = kt_ref[u, rows, (t + 1) * dk:(t + 2) * dk]
                        p = lax.dot_general(
                            qt_ref[u, rows, t * dk:(t + 2) * dk],
                            jnp.concatenate([jnp.concatenate([k_a, zero], axis=1),
                                             jnp.concatenate([zero, k_b], axis=1)], axis=0),
                            _NT, preferred_element_type=F32)
                        scores = jnp.where(masks[t], p[:, :chunk], scores)
                        scores = jnp.where(masks[t + 1], p[:, chunk:], scores)
                    else:
                        p = lax.dot_general(qt_ref[u, rows, t * dk:(t + 1) * dk],
                                            kt_ref[u, rows, t * dk:(t + 1) * dk],
                                            _NT, preferred_element_type=F32)
                        scores = jnp.where(masks[t], p, scores)
                s_ref[u, rows, :] = scores.astype(BF16)

    pl.when(leafwise_ok)(lambda: operands_and_scores(True))
    pl.when(jnp.logical_not(leafwise_ok))(lambda: operands_and_scores(False))

    incs = {u: [lax.dot_general(i_ref[rows_of(c), cols_of(hh)], kd_ref[u, rows_of(c), :], _TN,
                                preferred_element_type=F32) for c in range(n_chunks)]
            for u, hh, _, _, _, _, _ in units}
    finals, befores = {}, {}
    for u, hh, d, reverse, _, _, _ in units:
        st = s0_ref[d, hh].T if has_s0 else jnp.zeros((DV_A, dk), F32)
        befores[u] = [None] * n_chunks
        for c in (range(n_chunks - 1, -1, -1) if reverse else range(n_chunks)):
            befores[u][c] = st.astype(BF16)
            st = st * ae_ref[u, c:c + 1, :] + incs[u][c]
        finals[u] = st
    for u, hh, d, reverse, _, _, _ in units:
        for c in range(n_chunks):
            rows = rows_of(c)
            o = jnp.dot(s_ref[u, rows, :], i_ref[rows, cols_of(hh)], preferred_element_type=F32)
            if has_s0 or c != (n_chunks - 1 if reverse else 0):
                o = o + lax.dot_general(qd_ref[u, rows, :], befores[u][c], _NT,
                                        preferred_element_type=F32)
            if d == 0:
                acc_ref[rows, cols_of(hh)] = o
            else:
                acc_ref[rows, cols_of(hh)] += o

    for hh in range(heads):
        o = acc_ref[:, cols_of(hh)]
        o = o * lax.rsqrt(jnp.mean(o * o, axis=-1, keepdims=True) + EPS)
        o_ref[:, cols_of(hh)] = (o * _silu(g_ref[:, cols_of(hh)].astype(F32))).astype(BF16)
    if st_ref is not None:
        if n_prev:
            st_ref[0:n_prev] = prev_ref[...]
        for u, hh, d, _, _, _, _ in units:
            st_ref[n_prev, d, hh] = finals[u].T


def _hgrn_call(proj, la, key, state, prev, layer):
    n_b, seq_len, _ = proj.shape
    tri_f, tri_b, lv_f, lv_b, n_lev = _hgrn_consts(HGRN_CHUNK)
    has_s0 = state is not None
    n_chunks = seq_len // HGRN_CHUNK
    heads = max(1, HGRN_UNITS // (2 * n_chunks))
    width = heads * DK_A
    col = lambda off: (lambda b, h: (b, 0, off // width + h))
    const = lambda a: pl.BlockSpec(a.shape, lambda b, h: (0, 0))
    bwd = W_OFF_ZB - W_OFF_ZF
    in_specs = [pl.BlockSpec((None, seq_len, width), col(o))
                for o in (OFF_QA, 0, bwd, 0, bwd, OFF_IA, OFF_GA)]
    leaf = 1 << LEAF_LEVEL
    leaf_rows = -(-(seq_len // leaf) // SUBLANES) * SUBLANES
    leaf_ind = (np.arange(seq_len)[None, :] // leaf == np.arange(leaf_rows)[:, None])
    in_specs += [const(tri_f), const(tri_b), const(lv_f), const(lv_b), const(leaf_ind)]
    args = [proj, la, la, key, key, proj, proj, jnp.asarray(tri_f, BF16), jnp.asarray(tri_b, BF16),
            jnp.asarray(lv_f), jnp.asarray(lv_b), jnp.asarray(leaf_ind, BF16)]
    out_specs = [pl.BlockSpec((None, seq_len, width), lambda b, h: (b, 0, h))]
    out_shape = [jax.ShapeDtypeStruct((n_b, seq_len, H_A * DV_A), BF16)]
    if has_s0:
        in_specs.append(pl.BlockSpec((None, None, 2, heads, DK_A, DV_A),
                                     lambda b, h: (b, layer, 0, h, 0, 0)))
        args.append(state)
    else:
        st_spec = lambda n: pl.BlockSpec((None, n, 2, heads, DK_A, DV_A),
                                         lambda b, h: (b, 0, 0, h, 0, 0))
        if layer:
            in_specs.append(st_spec(layer))
            args.append(prev)
        out_specs.append(st_spec(layer + 1))
        out_shape.append(jax.ShapeDtypeStruct((n_b, layer + 1, 2, H_A, DK_A, DV_A), F32))
    n_u = 2 * heads
    res = pl.pallas_call(
        functools.partial(_hgrn_kernel, seq_len=seq_len, chunk=HGRN_CHUNK, n_lev=n_lev,
                          heads=heads, has_s0=has_s0, n_prev=0 if has_s0 else layer),
        grid=(n_b, H_A // heads),
        in_specs=in_specs, out_specs=out_specs, out_shape=out_shape,
        scratch_shapes=[
            pltpu.VMEM((n_u, seq_len, DK_A), F32),
            pltpu.VMEM((n_u, seq_len, (n_lev + 1) * DK_A), BF16),
            pltpu.VMEM((n_u, seq_len, (n_lev + 1) * DK_A), BF16),
            pltpu.VMEM((n_u, seq_len, DK_A), BF16),
            pltpu.VMEM((n_u, seq_len, DK_A), BF16),
            pltpu.VMEM((n_u, max(SUBLANES, n_chunks), DK_A), F32),
            pltpu.VMEM((n_u, seq_len, HGRN_CHUNK), BF16),
            pltpu.VMEM((seq_len, width), F32),
        ],
        compiler_params=_params(("arbitrary", "arbitrary")),
        name="hgrn2_scan",
    )(*args)
    return (res[0], None) if has_s0 else (res[0], res[1])


def _rope_tables(seq_len):
    rows = seq_len // GRID_W
    r_idx = jnp.repeat(jnp.arange(rows), GRID_W).astype(F32)
    c_idx = jnp.tile(jnp.arange(GRID_W), rows).astype(F32)
    quarter = DK_B // 4
    inv = 1.0 / (ROPE_BASE ** (jnp.arange(quarter, dtype=F32) / quarter))
    ang_r = r_idx[:, None] * inv[None, :]
    ang_c = c_idx[:, None] * inv[None, :]
    cos = jnp.concatenate([jnp.cos(ang_r)] * 2 + [jnp.cos(ang_c)] * 2, axis=1)
    sin = jnp.concatenate([-jnp.sin(ang_r), jnp.sin(ang_r), -jnp.sin(ang_c), jnp.sin(ang_c)],
                          axis=1)
    return cos, sin


def _rope(x, cos, sin):
    half = DK_B // 2
    swapped = jnp.concatenate([pltpu.roll(x[:, :half], half // 2, axis=1),
                               pltpu.roll(x[:, half:], half // 2, axis=1)], axis=1)
    return x * cos + swapped * sin


def _ret_kernel(*refs, seq_len, n_seq, latent, has_s0, n_prev):
    it = iter(refs)
    lg_ref, dm_ref, q_ref, k_ref, v_ref, g_ref = (next(it) for _ in range(6))
    cos_ref, sin_ref = (next(it), next(it)) if latent else (None, None)
    s0_ref = next(it) if has_s0 else None
    prev_ref = next(it) if n_prev else None
    o_ref = next(it)
    st_ref = None if has_s0 else next(it)
    h = pl.program_id(0)
    lg_f = lg_ref[h, 0]
    lg_b = lg_ref[h, 1]

    qs, ks = [], []
    for s in range(n_seq):
        q, k = q_ref[s], k_ref[s]
        if latent:
            cos, sin = cos_ref[...], sin_ref[...]
            q, k = _rope(q.astype(F32), cos, sin), _rope(k.astype(F32), cos, sin)
        qs.append(q)
        ks.append(k)
    items = [(s, slice(r, r + RET_ROWS)) for s in range(n_seq)
             for r in range(0, seq_len, RET_ROWS)]
    ps = [lax.dot_general(qs[s][rows].astype(BF16), ks[s].astype(BF16), _NT,
                          preferred_element_type=F32) for s, rows in items]
    carried = []
    if has_s0:
        for s, rows in items:
            s0 = jnp.concatenate([s0_ref[s, 0], s0_ref[s, 1]], axis=0).astype(BF16)
            pos = (lax.broadcasted_iota(jnp.int32, (RET_ROWS, 1), 0) + rows.start).astype(F32)
            q_s = jnp.concatenate([qs[s][rows] * jnp.exp(lg_f * (pos + 1.0)),
                                   qs[s][rows] * jnp.exp(lg_b * (seq_len - pos))], axis=1)
            carried.append(jnp.dot(q_s.astype(BF16), s0, preferred_element_type=F32))
    pds = [p.astype(BF16) * dm_ref[rows, :] for p, (_, rows) in zip(ps, items)]
    outs = [jnp.dot(pd, v_ref[s], preferred_element_type=F32) for pd, (s, _) in zip(pds, items)]
    for i, (s, rows) in enumerate(items):
        o = outs[i] + carried[i] if has_s0 else outs[i]
        o = o * lax.rsqrt(jnp.mean(o * o, axis=-1, keepdims=True) + EPS)
        o_ref[s, rows, :] = (o * _silu(g_ref[s, rows, :].astype(F32))).astype(BF16)
    if st_ref is not None:
        if n_prev:
            st_ref[:, 0:n_prev] = prev_ref[...]
        pos = lax.broadcasted_iota(jnp.int32, (seq_len, 1), 0).astype(F32)
        w_f = K_SCALE * jnp.exp(lg_f * (seq_len - 1.0 - pos))
        w_b = K_SCALE * jnp.exp(lg_b * pos)
        for s in range(n_seq):
            st_ref[s, n_prev, 0] = lax.dot_general((ks[s] * w_f).astype(BF16), v_ref[s], _TN,
                                                   preferred_element_type=F32)
            st_ref[s, n_prev, 1] = lax.dot_general((ks[s] * w_b).astype(BF16), v_ref[s], _TN,
                                                   preferred_element_type=F32)


def _ret_log_decay():
    heads = jnp.arange(H_B, dtype=F32)
    lg_f = jnp.log1p(-jnp.exp2(-5.0 - heads))
    lg_b = jnp.log1p(-jnp.exp2(-(5.0 + RET_DECAY_OFFSET_BWD) - heads))
    return jnp.stack([lg_f, lg_b], axis=1)


def _ret_decay_mask(seq_len):
    lg = _ret_log_decay()
    d = (jnp.arange(seq_len)[:, None] - jnp.arange(seq_len)[None, :]).astype(F32)[None]
    lg_f, lg_b = lg[:, 0, None, None], lg[:, 1, None, None]
    dm = (jnp.where(d >= 0.0, jnp.exp(lg_f * jnp.maximum(d, 0.0)), 0.0)
          + jnp.where(d <= 0.0, jnp.exp(lg_b * jnp.maximum(-d, 0.0)), 0.0))
    return (K_SCALE * dm).astype(BF16)


def _ret_call(proj, state, prev, layer, latent):
    n_b, seq_len, _ = proj.shape
    has_s0 = state is not None
    n_seq = max(1, RET_TOKENS // seq_len)
    in_specs = [
        pl.BlockSpec(memory_space=pltpu.SMEM),
        pl.BlockSpec((None, seq_len, seq_len), lambda h, b: (h, 0, 0)),
        pl.BlockSpec((n_seq, seq_len, DK_B), lambda h, b: (b, 0, OFF_QB // DK_B + h)),
        pl.BlockSpec((n_seq, seq_len, DK_B), lambda h, b: (b, 0, OFF_KB // DK_B + h)),
        pl.BlockSpec((n_seq, seq_len, DV_B), lambda h, b: (b, 0, OFF_VB // DV_B + h)),
        pl.BlockSpec((n_seq, seq_len, DV_B), lambda h, b: (b, 0, OFF_GB // DV_B + h)),
    ]
    args = [_ret_log_decay(), _ret_decay_mask(seq_len), proj, proj, proj, proj]
    if latent:
        cos, sin = _rope_tables(seq_len)
        in_specs += [pl.BlockSpec((seq_len, DK_B), lambda h, b: (0, 0))] * 2
        args += [cos, sin]
    out_specs = [pl.BlockSpec((n_seq, seq_len, DV_B), lambda h, b: (b, 0, h))]
    out_shape = [jax.ShapeDtypeStruct((n_b, seq_len, H_B * DV_B), BF16)]
    if has_s0:
        in_specs.append(pl.BlockSpec((n_seq, None, 2, None, DK_B, DV_B),
                                     lambda h, b: (b, layer, 0, h, 0, 0)))
        args.append(state)
    else:
        st_spec = lambda n: pl.BlockSpec((n_seq, n, 2, None, DK_B, DV_B),
                                         lambda h, b: (b, 0, 0, h, 0, 0))
        if layer:
            in_specs.append(st_spec(layer))
            args.append(prev)
        out_specs.append(st_spec(layer + 1))
        out_shape.append(jax.ShapeDtypeStruct((n_b, layer + 1, 2, H_B, DK_B, DV_B), F32))
    res = pl.pallas_call(
        functools.partial(_ret_kernel, seq_len=seq_len, n_seq=n_seq, latent=latent,
                          has_s0=has_s0, n_prev=0 if has_s0 else layer),
        grid=(H_B, n_b // n_seq),
        in_specs=in_specs, out_specs=out_specs, out_shape=out_shape,
        compiler_params=_params(("arbitrary", "arbitrary")),
        name="retention_scan",
    )(*args)
    return (res[0], None) if has_s0 else (res[0], res[1])


def _post_kernel(oa_ref, ob_ref, gta_ref, gtb_ref, x_ref, gt1_ref, pa_ref, pb_ref, wo_ref,
                 o_ref):
    subs = [slice(r, r + POST_SUB) for r in range(0, POST_TILE, POST_SUB)]
    y_a = [jnp.dot(oa_ref[s, :], pa_ref[...], preferred_element_type=F32) for s in subs]
    y_b = [jnp.dot(ob_ref[s, :], pb_ref[...], preferred_element_type=F32) for s in subs]
    merged = [(_sigmoid(gta_ref[s, :].astype(F32)) * y_a[i]
               + _sigmoid(gtb_ref[s, :].astype(F32)) * y_b[i]).astype(BF16)
              for i, s in enumerate(subs)]
    y = [jnp.dot(m, wo_ref[...], preferred_element_type=F32) for m in merged]
    for i, s in enumerate(subs):
        o_ref[s, :] = x_ref[s, :] + gt1_ref[...] * y[i]


def _post_call(o_a, o_b, proj, x, mods, p_a, p_b, w_out, layer, row_fn):
    n_tok = x.shape[0]
    tile = lambda w: (lambda i, j: (i, w))
    pcol = lambda off: pl.BlockSpec((POST_TILE, D_MODEL), tile(off // D_MODEL))
    const = lambda a: _layer_spec(layer, a.shape[1:], lambda i, j: (0, 0))
    return pl.pallas_call(
        _post_kernel,
        grid=(n_tok // POST_TILE, 1),
        in_specs=[
            pl.BlockSpec((POST_TILE, H_A * DV_A), tile(0)),
            pl.BlockSpec((POST_TILE, H_B * DV_B), tile(0)),
            pcol(OFF_GATE_A), pcol(OFF_GATE_B),
            pl.BlockSpec((POST_TILE, D_MODEL), tile(0)),
            _mod_spec(layer, row_fn, 2, POST_TILE),
            const(p_a), const(p_b), const(w_out),
        ],
        out_specs=pl.BlockSpec((POST_TILE, D_MODEL), tile(0)),
        out_shape=jax.ShapeDtypeStruct((n_tok, D_MODEL), F32),
        compiler_params=_params(("arbitrary", "arbitrary")),
        name="mixer_output",
    )(o_a, o_b, proj, proj, x, mods, p_a, p_b, w_out)


def _conv3(u, wc, bc, seq_len):
    n = u.shape[0]
    row = lax.broadcasted_iota(jnp.int32, (SUBLANES, 1), 0)

    def zero_row(x, r0, r):
        return jnp.where(row == r, 0.0, x[r0:r0 + SUBLANES])

    prev, nxt = pltpu.roll(u, 1, axis=0), pltpu.roll(u, n - 1, axis=0)
    p_parts, n_parts = [], []
    for s0 in range(0, n, seq_len):
        s1 = s0 + seq_len
        p_parts += [zero_row(prev, s0, 0), prev[s0 + SUBLANES:s1]]
        n_parts += [nxt[s0:s1 - SUBLANES], zero_row(nxt, s1 - SUBLANES, SUBLANES - 1)]
    prev, nxt = jnp.concatenate(p_parts, axis=0), jnp.concatenate(n_parts, axis=0)
    return prev * wc[0:1] + u * wc[1:2] + nxt * wc[2:3] + bc


def _ffn_kernel(*refs, seq_len, final):
    it = iter(refs)
    (x_ref, sh_ref, sc_ref, gt_ref, n2_ref, wa_ref, wg_ref, wca_ref, wcg_ref, bca_ref, bcg_ref,
     wd_ref) = (next(it) for _ in range(12))
    fn_ref = next(it) if final else None
    o_ref, h_ref, acc_ref = next(it), next(it), next(it)
    j = pl.program_id(1)

    @pl.when(j == 0)
    def _():
        h_ref[...] = _norm_mod(x_ref[...], n2_ref[...], sc_ref[...], sh_ref[...]).astype(BF16)
        acc_ref[...] = jnp.zeros_like(acc_ref)

    h = h_ref[...]
    subs = [slice(c, min(c + FF_SUB, FF_CHUNK)) for c in range(0, FF_CHUNK, FF_SUB)]
    u_a = [jnp.dot(h, wa_ref[:, s], preferred_element_type=F32) for s in subs]
    u_g = [jnp.dot(h, wg_ref[:, s], preferred_element_type=F32) for s in subs]
    act = [(_silu(_conv3(u_g[i], wcg_ref[:, s], bcg_ref[:, s], seq_len))
            * _conv3(u_a[i], wca_ref[:, s], bca_ref[:, s], seq_len)).astype(BF16)
           for i, s in enumerate(subs)]
    acc_ref[...] += jnp.dot(jnp.concatenate(act, axis=1), wd_ref[...],
                            preferred_element_type=F32)

    @pl.when(j == pl.num_programs(1) - 1)
    def _():
        x = x_ref[...] + gt_ref[...] * acc_ref[...]
        if final:
            x = x * lax.rsqrt(jnp.mean(x * x, axis=-1, keepdims=True) + EPS) * fn_ref[...]
        o_ref[...] = x


def _ffn_call(x, mods, n2, w_up, w_conv, b_conv, w_down, final_norm, layer, row_fn, seq_len):
    n_tok = x.shape[0]
    n_ff = D_FF // FF_CHUNK
    lspec = functools.partial(_layer_spec, layer)
    in_specs = [
        pl.BlockSpec((TOKEN_TILE, D_MODEL), lambda i, j: (i, 0)),
        _mod_spec(layer, row_fn, 3), _mod_spec(layer, row_fn, 4), _mod_spec(layer, row_fn, 5),
        lspec((1, D_MODEL), lambda i, j: (0, 0)),
        lspec((D_MODEL, FF_CHUNK), lambda i, j: (0, j)),
        lspec((D_MODEL, FF_CHUNK), lambda i, j: (0, n_ff + j)),
        lspec((3, FF_CHUNK), lambda i, j: (0, j)),
        lspec((3, FF_CHUNK), lambda i, j: (0, n_ff + j)),
        lspec((1, FF_CHUNK), lambda i, j: (0, j)),
        lspec((1, FF_CHUNK), lambda i, j: (0, n_ff + j)),
        lspec((FF_CHUNK, D_MODEL), lambda i, j: (j, 0)),
    ]
    args = [x, mods, mods, mods, n2, w_up, w_up, w_conv, w_conv, b_conv, b_conv, w_down]
    final = final_norm is not None
    if final:
        in_specs.append(pl.BlockSpec((1, D_MODEL), lambda i, j: (0, 0)))
        args.append(final_norm.reshape(1, D_MODEL))
    return pl.pallas_call(
        functools.partial(_ffn_kernel, seq_len=seq_len, final=final),
        grid=(n_tok // TOKEN_TILE, n_ff),
        in_specs=in_specs,
        out_specs=pl.BlockSpec((TOKEN_TILE, D_MODEL), lambda i, j: (i, 0)),
        out_shape=jax.ShapeDtypeStruct((n_tok, D_MODEL), F32),
        scratch_shapes=[pltpu.VMEM((TOKEN_TILE, D_MODEL), BF16),
                        pltpu.VMEM((TOKEN_TILE, D_MODEL), F32)],
        compiler_params=_params(("arbitrary", "arbitrary")),
        name="conv_ffn",
    )(*args)


def kernel(x_prompt, x_sample, state_hgrn, state_ret, c, c_ctx, norm1, norm2, final_norm,
           w_mod, b_mod, w_in, hgrn_lb_raw, p_a, p_b, w_out, w_up, w_conv, b_conv, w_down):
    n_ctx, t_ctx, _ = x_prompt.shape
    n_dec, t_dec, _ = x_sample.shape
    assert t_ctx & (t_ctx - 1) == 0 and TOKEN_TILE % t_ctx == 0 and t_dec == TOKEN_TILE

    sm = jax.nn.softmax(hgrn_lb_raw.astype(F32), axis=0)
    cum = jnp.cumsum(sm, axis=0)
    lower_bounds = cum - cum[0:1]

    cvec = jnp.concatenate(
        [c_ctx[None, :], c, jnp.zeros((MOD_ROWS - 1 - n_dec, D_MODEL), F32)], axis=0)
    mods = _mod_call(cvec, w_mod, b_mod).reshape(DEPTH, MOD_ROWS, 1, 6 * D_MODEL)

    p_a_b, p_b_b, w_out_b, w_up_b, w_down_b = (
        w.astype(BF16) for w in (p_a, p_b, w_out, w_up, w_down))
    norm1_3, norm2_3 = norm1.reshape(DEPTH, 1, D_MODEL), norm2.reshape(DEPTH, 1, D_MODEL)
    b_conv_3 = b_conv.reshape(DEPTH, 1, 2 * D_FF)

    ctx_row = lambda tok: 0
    dec_row = lambda tok: tok // t_dec + 1

    def layer(x, l, n_b, seq_len, row_fn, s_hgrn, s_ret, prev_h, prev_r, latent):
        proj, la, key = _proj_call(x, mods, norm1_3, w_in, lower_bounds, l, row_fn)
        proj3 = proj.reshape(n_b, seq_len, PROJ_WIDTH)
        o_a, st_h = _hgrn_call(proj3, la.reshape(n_b, seq_len, -1), key.reshape(n_b, seq_len, -1),
                               s_hgrn, prev_h, l)
        o_b, st_r = _ret_call(proj3, s_ret, prev_r, l, latent)
        x = _post_call(o_a.reshape(-1, H_A * DV_A), o_b.reshape(-1, H_B * DV_B), proj, x,
                       mods, p_a_b, p_b_b, w_out_b, l, row_fn)
        x = _ffn_call(x, mods, norm2_3, w_up_b, w_conv, b_conv_3, w_down_b,
                      final_norm if l == DEPTH - 1 else None, l, row_fn, seq_len)
        return x, st_h, st_r

    x = x_prompt.reshape(n_ctx * t_ctx, D_MODEL)
    st_h = st_r = None
    for l in range(DEPTH):
        x, st_h, st_r = layer(x, l, n_ctx, t_ctx, ctx_row, None, None, st_h, st_r, False)
    y_prompt = x.reshape(n_ctx, t_ctx, D_MODEL)

    x = x_sample.reshape(n_dec * t_dec, D_MODEL)
    for l in range(DEPTH):
        x, _, _ = layer(x, l, n_dec, t_dec, dec_row, state_hgrn, state_ret, None, None, True)
    y_sample = x.reshape(n_dec, t_dec, D_MODEL)
    return (y_prompt, y_sample, st_h, st_r)
```

```python
import functools

import numpy as np
import jax
import jax.numpy as jnp
from jax import lax
from jax.experimental import pallas as pl
from jax.experimental.pallas import tpu as pltpu

F32 = jnp.float32
BF16 = jnp.bfloat16

D_MODEL = 1024
DEPTH = 2
GRID_W = 64
H_A, DK_A, DV_A = 8, 128, 128
H_B, DK_B, DV_B = 4, 256, 512
D_FF = 2816
ROPE_BASE = 10000.0
K_SCALE = DK_B ** -0.5
EPS = 1e-6
RET_DECAY_OFFSET_BWD = 0.5

W_OFF_ZF, W_OFF_ZB, W_OFF_IA = 1024, 2048, 3072
OFF_QA, OFF_IA, OFF_GA = 0, 1024, 2048
OFF_QB, OFF_KB, OFF_VB, OFF_GB = 3072, 4096, 5120, 7168
OFF_GATE_A, OFF_GATE_B = 9216, 10240
PROJ_WIDTH = 11264

MOD_ROWS = 8
MOD_TN = 1024
TOKEN_TILE = 1024
POST_TILE = 512
POST_SUB = 256
FF_CHUNK = 1408
FF_SUB = 256
HGRN_CHUNK = 128
HGRN_UNITS = 32
LEAF_LEVEL = 5
LEAF_MAX_DECAY = 60.0
RET_ROWS = 256
RET_TOKENS = 1024
PROJ_TN = 1024
Z_TILE0 = W_OFF_ZF // PROJ_TN
Z_TILES = (W_OFF_IA - W_OFF_ZF) // PROJ_TN
VMEM_LIMIT = 52 * 1024 * 1024
SUBLANES = 8
LOG2E = 1.4426950408889634

_NT = (((1,), (1,)), ((), ()))
_TN = (((0,), (0,)), ((), ()))


def _params(sem):
    return pltpu.CompilerParams(dimension_semantics=sem, vmem_limit_bytes=VMEM_LIMIT)


def _sigmoid(x):
    return jax.nn.sigmoid(x)


def _silu(x):
    return x * jax.nn.sigmoid(x)


def _mod_kernel(c_ref, w_ref, b_ref, o_ref):
    s = _silu(c_ref[...])
    o_ref[...] = jnp.dot(s, w_ref[...], precision=lax.Precision.HIGHEST,
                         preferred_element_type=F32) + b_ref[...]


def _mod_call(cvec, w_mod, b_mod):
    n_col = 6 * D_MODEL // MOD_TN
    return pl.pallas_call(
        _mod_kernel,
        grid=(DEPTH, n_col),
        in_specs=[
            pl.BlockSpec((MOD_ROWS, D_MODEL), lambda l, j: (0, 0)),
            pl.BlockSpec((None, D_MODEL, MOD_TN), lambda l, j: (l, 0, j)),
            pl.BlockSpec((None, 1, MOD_TN), lambda l, j: (l, 0, j)),
        ],
        out_specs=pl.BlockSpec((None, MOD_ROWS, MOD_TN), lambda l, j: (l, 0, j)),
        out_shape=jax.ShapeDtypeStruct((DEPTH, MOD_ROWS, 6 * D_MODEL), F32),
        compiler_params=_params(("arbitrary", "arbitrary")),
        name="modulation",
    )(cvec, w_mod, b_mod.reshape(DEPTH, 1, 6 * D_MODEL))


def _mod_spec(layer, row_fn, which, tile=TOKEN_TILE):
    return pl.BlockSpec((None, None, 1, D_MODEL),
                        lambda i, j: (layer, row_fn(i * tile), 0, which))


def _layer_spec(layer, block, index_map):
    return pl.BlockSpec((None,) + block, lambda *g: (layer,) + index_map(*g))


def _norm_mod(x, g, sc, sh):
    y = x * lax.rsqrt(jnp.mean(x * x, axis=-1, keepdims=True) + EPS) * g
    return y * (1.0 + sc) + sh


def _proj_kernel(x_ref, sh_ref, sc_ref, n_ref, w_ref, o_ref, hout_ref, h_ref, wb_ref):
    j, i = pl.program_id(0), pl.program_id(1)

    @pl.when(i == 0)
    def _():
        wb_ref[...] = w_ref[...].astype(BF16)

    @pl.when(j == 0)
    def _():
        h = _norm_mod(x_ref[...], n_ref[...], sc_ref[...], sh_ref[...]).astype(BF16)
        h_ref[i] = h
        hout_ref[...] = h

    o_ref[...] = jnp.dot(h_ref[i], wb_ref[...], preferred_element_type=F32).astype(BF16)


def _zproj_kernel(h_ref, w_ref, lb_ref, la_ref, key_ref, wb_ref):
    @pl.when(pl.program_id(1) == 0)
    def _():
        wb_ref[...] = w_ref[...].astype(BF16)

    z = jnp.dot(h_ref[...], wb_ref[...], preferred_element_type=F32)
    la, key = _hgrn_forget(z, lb_ref[...])
    la_ref[...] = la
    key_ref[...] = key.astype(BF16)


def _proj_call(x, mods, n1, w_in, lb, layer, row_fn):
    n_tok = x.shape[0]
    tn = PROJ_TN
    n_tile = n_tok // TOKEN_TILE
    hold = lambda j, i: jnp.where(j == 0, i, n_tile - 1)
    w_tile = lambda j: jnp.where(j >= Z_TILE0, j + Z_TILES, j)
    mod = lambda which: pl.BlockSpec(
        (None, None, 1, D_MODEL), lambda j, i: (layer, row_fn(i * TOKEN_TILE), 0, which))
    proj, h = pl.pallas_call(
        _proj_kernel,
        grid=(PROJ_WIDTH // tn, n_tile),
        in_specs=[
            pl.BlockSpec((TOKEN_TILE, D_MODEL), lambda j, i: (hold(j, i), 0)),
            mod(0), mod(1),
            _layer_spec(layer, (1, D_MODEL), lambda j, i: (0, 0)),
            _layer_spec(layer, (D_MODEL, tn), lambda j, i: (0, w_tile(j))),
        ],
        out_specs=[
            pl.BlockSpec((TOKEN_TILE, tn), lambda j, i: (i, j)),
            pl.BlockSpec((TOKEN_TILE, D_MODEL), lambda j, i: (hold(j, i), 0)),
        ],
        out_shape=[jax.ShapeDtypeStruct((n_tok, PROJ_WIDTH), BF16),
                   jax.ShapeDtypeStruct((n_tok, D_MODEL), BF16)],
        scratch_shapes=[pltpu.VMEM((n_tile, TOKEN_TILE, D_MODEL), BF16),
                        pltpu.VMEM((D_MODEL, tn), BF16)],
        compiler_params=_params(("arbitrary", "arbitrary")),
        name="in_projection",
    )(x, mods, mods, n1, w_in)
    la, key = pl.pallas_call(
        _zproj_kernel,
        grid=(Z_TILES, n_tile),
        in_specs=[
            pl.BlockSpec((TOKEN_TILE, D_MODEL), lambda j, i: (i, 0)),
            _layer_spec(layer, (D_MODEL, tn), lambda j, i: (0, Z_TILE0 + j)),
            pl.BlockSpec((None, None, 1, tn), lambda j, i: (layer, j, 0, 0)),
        ],
        out_specs=[pl.BlockSpec((TOKEN_TILE, tn), lambda j, i: (i, j))] * 2,
        out_shape=[jax.ShapeDtypeStruct((n_tok, Z_TILES * tn), F32),
                   jax.ShapeDtypeStruct((n_tok, Z_TILES * tn), BF16)],
        scratch_shapes=[pltpu.VMEM((D_MODEL, tn), BF16)],
        compiler_params=_params(("arbitrary", "arbitrary")),
        name="forget_projection",
    )(h, w_in, lb.reshape(DEPTH, Z_TILES, 1, tn))
    return proj, la, key


def _hgrn_consts(c):
    nl = int(np.log2(c))
    t = np.arange(c)
    tt, rr = t[:, None], t[None, :]
    x = tt ^ rr
    lev_of = np.where(x > 0, np.floor(np.log2(np.maximum(x, 1))) + 1, 0).astype(np.int32)
    lv_f = np.where(tt >= rr, lev_of, -1).astype(np.int32)
    lv_b = lv_f.T.copy()
    return ((rr <= tt).astype(np.float32), (rr >= tt).astype(np.float32), lv_f, lv_b, nl)


def _leaf_factors(cum, cum_row, reverse):
    c, dk = cum.shape
    leaf = 1 << LEAF_LEVEL
    pieces = []
    for base in range(0, c, leaf):
        row = base + leaf if reverse else base - 1
        if 0 <= row < c:
            pieces.append(jnp.broadcast_to(cum_row(row), (leaf, dk)))
        else:
            pieces.append(jnp.zeros((leaf, dk), F32))
    x = cum - jnp.concatenate(pieces, axis=0)
    return jnp.exp2(x * LOG2E).astype(BF16), jnp.exp2(x * (-LOG2E)).astype(BF16)


def _level_factor(la, cum, cum_row, lev, reverse):
    c, dk = la.shape
    if lev == 1:
        row = lax.broadcasted_iota(jnp.int32, (c, 1), 0)
        return jnp.exp(jnp.where((row & 1) == (0 if reverse else 1), la, 0.0)).astype(BF16)
    blk, half = 1 << lev, 1 << (lev - 1)

    def boundary(r):
        base = (r // blk) * blk
        return jnp.broadcast_to(cum_row(base + half if reverse else base + half - 1),
                                (SUBLANES, dk))

    top = lax.broadcasted_iota(jnp.int32, (SUBLANES, dk), 0) < SUBLANES // 2
    pieces = []
    for r0 in range(0, c, SUBLANES):
        if blk >= SUBLANES:
            pieces.append(boundary(r0))
        else:
            pieces.append(jnp.where(top, boundary(r0), boundary(r0 + SUBLANES // 2)))
    return jnp.exp2(jnp.abs(cum - jnp.concatenate(pieces, axis=0)) * (-LOG2E)).astype(BF16)


def _hgrn_forget(z, lb):
    e = jnp.exp(-jnp.abs(z))
    r = 1.0 / (1.0 + e)
    pos = z >= 0.0
    sig = jnp.where(pos, 1.0, e) * r
    sig_neg = jnp.where(pos, e, 1.0) * r
    oml = 1.0 - lb
    bounded = lb > 0.0
    log_f = (jnp.where(bounded, 0.0, jnp.minimum(z, 0.0))
             + jnp.log(jnp.where(bounded, lb + oml * sig, r)))
    return log_f, oml * sig_neg


def _hgrn_kernel(*refs, seq_len, chunk, n_lev, heads, has_s0, n_prev):
    it = iter(refs)
    (q_ref, laf_ref, lab_ref, kf_ref, kb_ref, i_ref, g_ref, trif_ref, trib_ref, lvf_ref, lvb_ref,
     leaf_ref) = (next(it) for _ in range(12))
    s0_ref = next(it) if has_s0 else None
    prev_ref = next(it) if n_prev else None
    o_ref = next(it)
    st_ref = None if has_s0 else next(it)
    cum_ref, qt_ref, kt_ref, qd_ref, kd_ref, ae_ref, s_ref, acc_ref = (next(it) for _ in range(8))
    n_chunks = seq_len // chunk
    dk = DK_A
    units = [(2 * hh + d, hh, d, d == 1, la_in, tri_ref, lv_ref)
             for hh in range(heads)
             for d, (la_in, tri_ref, lv_ref) in enumerate(((laf_ref, trif_ref, lvf_ref),
                                                           (lab_ref, trib_ref, lvb_ref)))]
    key_in = (kf_ref, kb_ref)
    rows_of = lambda c: slice(c * chunk, (c + 1) * chunk)
    cols_of = lambda hh: slice(hh * dk, (hh + 1) * dk)

    worst = None
    for _, hh, _, _, la_in, _, _ in units:
        leaf_sum = jnp.dot(leaf_ref[...], la_in[:, cols_of(hh)].astype(BF16),
                           preferred_element_type=F32)
        worst = leaf_sum if worst is None else jnp.minimum(worst, leaf_sum)
    leafwise_ok = jnp.min(worst) > -LEAF_MAX_DECAY

    for u, hh, _, _, la_in, tri_ref, _ in units:
        tri = tri_ref[...]
        for c in range(n_chunks):
            la = la_in[rows_of(c), cols_of(hh)]
            la_hi = la.astype(BF16)
            la_lo = (la - la_hi.astype(F32)).astype(BF16)
            x2 = jnp.dot(tri, jnp.concatenate([la_hi, la_lo], axis=1),
                         preferred_element_type=F32)
            cum_ref[u, rows_of(c), :] = x2[:, :dk] + x2[:, dk:]

    def operands_and_scores(leafwise):
        if leafwise:
            levels = list(range(LEAF_LEVEL + 1, n_lev + 1))
        else:
            levels = list(range(1, n_lev + 1))
        n_terms = len(levels) + 1
        for u, hh, d, reverse, la_in, _, _ in units:
            for c in range(n_chunks):
                rows = rows_of(c)
                cum = cum_ref[u, rows, :]
                la = la_in[rows, cols_of(hh)]
                q = q_ref[rows, cols_of(hh)]
                k = key_in[d][rows, cols_of(hh)]
                cum_row = lambda r, u=u, c=c: cum_ref[u, c * chunk + r:c * chunk + r + 1, :]
                if leafwise:
                    factors = [_leaf_factors(cum, cum_row, reverse)]
                else:
                    factors = [None]
                for lev in levels:
                    e = _level_factor(la, cum, cum_row, lev, reverse)
                    factors.append((e, e))
                for t, f in enumerate(factors):
                    cols = slice(t * dk, (t + 1) * dk)
                    qt_ref[u, rows, cols] = q if f is None else q * f[0]
                    kt_ref[u, rows, cols] = k if f is None else k * f[1]
                end_row = 0 if reverse else chunk - 1
                cum_end = cum[end_row:end_row + 1]
                qd_ref[u, rows, :] = q * jnp.exp(cum).astype(BF16)
                kd_ref[u, rows, :] = k * jnp.exp(cum_end - cum).astype(BF16)
                ae_ref[u, c:c + 1, :] = jnp.exp(cum_end)
        zero = jnp.zeros((chunk, dk), BF16)
        for u, _, _, _, _, _, lv_ref in units:
            lv = lv_ref[...]
            if leafwise:
                masks = [(lv >= 0) & (lv <= LEAF_LEVEL)]
            else:
                masks = [lv == 0]
            masks += [lv == lev for lev in levels]
            for c in range(n_chunks):
                rows = rows_of(c)
                scores = jnp.zeros((chunk, chunk), F32)
                for t in range(0, n_terms, 2):
                    if t + 1 < n_terms:
                        k_a = kt_ref[u, rows, t * dk:(t + 1) * dk]
                        k_b = kt_ref[u, rows, (t + 1) * dk:(t + 2) * dk]
                        p = lax.dot_general(
                            qt_ref[u, rows, t * dk:(t + 2) * dk],
                            jnp.concatenate([jnp.concatenate([k_a, zero], axis=1),
                                             jnp.concatenate([zero, k_b], axis=1)], axis=0),
                            _NT, preferred_element_type=F32)
                        scores = jnp.where(masks[t], p[:, :chunk], scores)
                        scores = jnp.where(masks[t + 1], p[:, chunk:], scores)
                    else:
                        p = lax.dot_general(qt_ref[u, rows, t * dk:(t + 1) * dk],
                                            kt_ref[u, rows, t * dk:(t + 1) * dk],
                                            _NT, preferred_element_type=F32)
                        scores = jnp.where(masks[t], p, scores)
                s_ref[u, rows, :] = scores.astype(BF16)

    pl.when(leafwise_ok)(lambda: operands_and_scores(True))
    pl.when(jnp.logical_not(leafwise_ok))(lambda: operands_and_scores(False))

    incs = {u: [lax.dot_general(i_ref[rows_of(c), cols_of(hh)], kd_ref[u, rows_of(c), :], _TN,
                                preferred_element_type=F32) for c in range(n_chunks)]
            for u, hh, _, _, _, _, _ in units}
    finals, befores = {}, {}
    for u, hh, d, reverse, _, _, _ in units:
        st = s0_ref[d, hh].T if has_s0 else jnp.zeros((DV_A, dk), F32)
        befores[u] = [None] * n_chunks
        for c in (range(n_chunks - 1, -1, -1) if reverse else range(n_chunks)):
            befores[u][c] = st.astype(BF16)
            st = st * ae_ref[u, c:c + 1, :] + incs[u][c]
        finals[u] = st
    for u, hh, d, reverse, _, _, _ in units:
        for c in range(n_chunks):
            rows = rows_of(c)
            o = jnp.dot(s_ref[u, rows, :], i_ref[rows, cols_of(hh)], preferred_element_type=F32)
            if has_s0 or c != (n_chunks - 1 if reverse else 0):
                o = o + lax.dot_general(qd_ref[u, rows, :], befores[u][c], _NT,
                                        preferred_element_type=F32)
            if d == 0:
                acc_ref[rows, cols_of(hh)] = o
            else:
                acc_ref[rows, cols_of(hh)] += o

    for hh in range(heads):
        o = acc_ref[:, cols_of(hh)]
        o = o * lax.rsqrt(jnp.mean(o * o, axis=-1, keepdims=True) + EPS)
        o_ref[:, cols_of(hh)] = (o * _silu(g_ref[:, cols_of(hh)].astype(F32))).astype(BF16)
    if st_ref is not None:
        if n_prev:
            st_ref[0:n_prev] = prev_ref[...]
        for u, hh, d, _, _, _, _ in units:
            st_ref[n_prev, d, hh] = finals[u].T


def _hgrn_call(proj, la, key, state, prev, layer):
    n_b, seq_len, _ = proj.shape
    tri_f, tri_b, lv_f, lv_b, n_lev = _hgrn_consts(HGRN_CHUNK)
    has_s0 = state is not None
    n_chunks = seq_len // HGRN_CHUNK
    heads = max(1, HGRN_UNITS // (2 * n_chunks))
    width = heads * DK_A
    col = lambda off: (lambda b, h: (b, 0, off // width + h))
    const = lambda a: pl.BlockSpec(a.shape, lambda b, h: (0, 0))
    bwd = W_OFF_ZB - W_OFF_ZF
    in_specs = [pl.BlockSpec((None, seq_len, width), col(o))
                for o in (OFF_QA, 0, bwd, 0, bwd, OFF_IA, OFF_GA)]
    leaf = 1 << LEAF_LEVEL
    leaf_rows = -(-(seq_len // leaf) // SUBLANES) * SUBLANES
    leaf_ind = (np.arange(seq_len)[None, :] // leaf == np.arange(leaf_rows)[:, None])
    in_specs += [const(tri_f), const(tri_b), const(lv_f), const(lv_b), const(leaf_ind)]
    args = [proj, la, la, key, key, proj, proj, jnp.asarray(tri_f, BF16), jnp.asarray(tri_b, BF16),
            jnp.asarray(lv_f), jnp.asarray(lv_b), jnp.asarray(leaf_ind, BF16)]
    out_specs = [pl.BlockSpec((None, seq_len, width), lambda b, h: (b, 0, h))]
    out_shape = [jax.ShapeDtypeStruct((n_b, seq_len, H_A * DV_A), BF16)]
    if has_s0:
        in_specs.append(pl.BlockSpec((None, None, 2, heads, DK_A, DV_A),
                                     lambda b, h: (b, layer, 0, h, 0, 0)))
        args.append(state)
    else:
        st_spec = lambda n: pl.BlockSpec((None, n, 2, heads, DK_A, DV_A),
                                         lambda b, h: (b, 0, 0, h, 0, 0))
        if layer:
            in_specs.append(st_spec(layer))
            args.append(prev)
        out_specs.append(st_spec(layer + 1))
        out_shape.append(jax.ShapeDtypeStruct((n_b, layer + 1, 2, H_A, DK_A, DV_A), F32))
    n_u = 2 * heads
    res = pl.pallas_call(
        functools.partial(_hgrn_kernel, seq_len=seq_len, chunk=HGRN_CHUNK, n_lev=n_lev,
                          heads=heads, has_s0=has_s0, n_prev=0 if has_s0 else layer),
        grid=(n_b, H_A // heads),
        in_specs=in_specs, out_specs=out_specs, out_shape=out_shape,
        scratch_shapes=[
            pltpu.VMEM((n_u, seq_len, DK_A), F32),
            pltpu.VMEM((n_u, seq_len, (n_lev + 1) * DK_A), BF16),
            pltpu.VMEM((n_u, seq_len, (n_lev + 1) * DK_A), BF16),
            pltpu.VMEM((n_u, seq_len, DK_A), BF16),
            pltpu.VMEM((n_u, seq_len, DK_A), BF16),
            pltpu.VMEM((n_u, max(SUBLANES, n_chunks), DK_A), F32),
            pltpu.VMEM((n_u, seq_len, HGRN_CHUNK), BF16),
            pltpu.VMEM((seq_len, width), F32),
        ],
        compiler_params=_params(("arbitrary", "arbitrary")),
        name="hgrn2_scan",
    )(*args)
    return (res[0], None) if has_s0 else (res[0], res[1])


def _rope_tables(seq_len):
    rows = seq_len // GRID_W
    r_idx = jnp.repeat(jnp.arange(rows), GRID_W).astype(F32)
    c_idx = jnp.tile(jnp.arange(GRID_W), rows).astype(F32)
    quarter = DK_B // 4
    inv = 1.0 / (ROPE_BASE ** (jnp.arange(quarter, dtype=F32) / quarter))
    ang_r = r_idx[:, None] * inv[None, :]
    ang_c = c_idx[:, None] * inv[None, :]
    cos = jnp.concatenate([jnp.cos(ang_r)] * 2 + [jnp.cos(ang_c)] * 2, axis=1)
    sin = jnp.concatenate([-jnp.sin(ang_r), jnp.sin(ang_r), -jnp.sin(ang_c), jnp.sin(ang_c)],
                          axis=1)
    return cos, sin


def _rope(x, cos, sin):
    half = DK_B // 2
    swapped = jnp.concatenate([pltpu.roll(x[:, :half], half // 2, axis=1),
                               pltpu.roll(x[:, half:], half // 2, axis=1)], axis=1)
    return x * cos + swapped * sin


def _ret_kernel(*refs, seq_len, n_seq, latent, has_s0, n_prev):
    it = iter(refs)
    lg_ref, dm_ref, q_ref, k_ref, v_ref, g_ref = (next(it) for _ in range(6))
    cos_ref, sin_ref = (next(it), next(it)) if latent else (None, None)
    s0_ref = next(it) if has_s0 else None
    prev_ref = next(it) if n_prev else None
    o_ref = next(it)
    st_ref = None if has_s0 else next(it)
    h = pl.program_id(0)
    lg_f = lg_ref[h, 0]
    lg_b = lg_ref[h, 1]

    qs, ks = [], []
    for s in range(n_seq):
        q, k = q_ref[s], k_ref[s]
        if latent:
            cos, sin = cos_ref[...], sin_ref[...]
            q, k = _rope(q.astype(F32), cos, sin), _rope(k.astype(F32), cos, sin)
        qs.append(q)
        ks.append(k)
    items = [(s, slice(r, r + RET_ROWS)) for s in range(n_seq)
             for r in range(0, seq_len, RET_ROWS)]
    ps = [lax.dot_general(qs[s][rows].astype(BF16), ks[s].astype(BF16), _NT,
                          preferred_element_type=F32) for s, rows in items]
    carried = []
    if has_s0:
        for s, rows in items:
            s0 = jnp.concatenate([s0_ref[s, 0], s0_ref[s, 1]], axis=0).astype(BF16)
            pos = (lax.broadcasted_iota(jnp.int32, (RET_ROWS, 1), 0) + rows.start).astype(F32)
            q_s = jnp.concatenate([qs[s][rows] * jnp.exp(lg_f * (pos + 1.0)),
                                   qs[s][rows] * jnp.exp(lg_b * (seq_len - pos))], axis=1)
            carried.append(jnp.dot(q_s.astype(BF16), s0, preferred_element_type=F32))
    pds = [p.astype(BF16) * dm_ref[rows, :] for p, (_, rows) in zip(ps, items)]
    outs = [jnp.dot(pd, v_ref[s], preferred_element_type=F32) for pd, (s, _) in zip(pds, items)]
    for i, (s, rows) in enumerate(items):
        o = outs[i] + carried[i] if has_s0 else outs[i]
        o = o * lax.rsqrt(jnp.mean(o * o, axis=-1, keepdims=True) + EPS)
        o_ref[s, rows, :] = (o * _silu(g_ref[s, rows, :].astype(F32))).astype(BF16)
    if st_ref is not None:
        if n_prev:
            st_ref[:, 0:n_prev] = prev_ref[...]
        pos = lax.broadcasted_iota(jnp.int32, (seq_len, 1), 0).astype(F32)
        w_f = K_SCALE * jnp.exp(lg_f * (seq_len - 1.0 - pos))
        w_b = K_SCALE * jnp.exp(lg_b * pos)
        for s in range(n_seq):
            st_ref[s, n_prev, 0] = lax.dot_general((ks[s] * w_f).astype(BF16), v_ref[s], _TN,
                                                   preferred_element_type=F32)
            st_ref[s, n_prev, 1] = lax.dot_general((ks[s] * w_b).astype(BF16), v_ref[s], _TN,
                                                   preferred_element_type=F32)


def _ret_log_decay():
    heads = jnp.arange(H_B, dtype=F32)
    lg_f = jnp.log1p(-jnp.exp2(-5.0 - heads))
    lg_b = jnp.log1p(-jnp.exp2(-(5.0 + RET_DECAY_OFFSET_BWD) - heads))
    return jnp.stack([lg_f, lg_b], axis=1)


def _ret_decay_mask(seq_len):
    lg = _ret_log_decay()
    d = (jnp.arange(seq_len)[:, None] - jnp.arange(seq_len)[None, :]).astype(F32)[None]
    lg_f, lg_b = lg[:, 0, None, None], lg[:, 1, None, None]
    dm = (jnp.where(d >= 0.0, jnp.exp(lg_f * jnp.maximum(d, 0.0)), 0.0)
          + jnp.where(d <= 0.0, jnp.exp(lg_b * jnp.maximum(-d, 0.0)), 0.0))
    return (K_SCALE * dm).astype(BF16)


def _ret_call(proj, state, prev, layer, latent):
    n_b, seq_len, _ = proj.shape
    has_s0 = state is not None
    n_seq = max(1, RET_TOKENS // seq_len)
    in_specs = [
        pl.BlockSpec(memory_space=pltpu.SMEM),
        pl.BlockSpec((None, seq_len, seq_len), lambda h, b: (h, 0, 0)),
        pl.BlockSpec((n_seq, seq_len, DK_B), lambda h, b: (b, 0, OFF_QB // DK_B + h)),
        pl.BlockSpec((n_seq, seq_len, DK_B), lambda h, b: (b, 0, OFF_KB // DK_B + h)),
        pl.BlockSpec((n_seq, seq_len, DV_B), lambda h, b: (b, 0, OFF_VB // DV_B + h)),
        pl.BlockSpec((n_seq, seq_len, DV_B), lambda h, b: (b, 0, OFF_GB // DV_B + h)),
    ]
    args = [_ret_log_decay(), _ret_decay_mask(seq_len), proj, proj, proj, proj]
    if latent:
        cos, sin = _rope_tables(seq_len)
        in_specs += [pl.BlockSpec((seq_len, DK_B), lambda h, b: (0, 0))] * 2
        args += [cos, sin]
    out_specs = [pl.BlockSpec((n_seq, seq_len, DV_B), lambda h, b: (b, 0, h))]
    out_shape = [jax.ShapeDtypeStruct((n_b, seq_len, H_B * DV_B), BF16)]
    if has_s0:
        in_specs.append(pl.BlockSpec((n_seq, None, 2, None, DK_B, DV_B),
                                     lambda h, b: (b, layer, 0, h, 0, 0)))
        args.append(state)
    else:
        st_spec = lambda n: pl.BlockSpec((n_seq, n, 2, None, DK_B, DV_B),
                                         lambda h, b: (b, 0, 0, h, 0, 0))
        if layer:
            in_specs.append(st_spec(layer))
            args.append(prev)
        out_specs.append(st_spec(layer + 1))
        out_shape.append(jax.ShapeDtypeStruct((n_b, layer + 1, 2, H_B, DK_B, DV_B), F32))
    res = pl.pallas_call(
        functools.partial(_ret_kernel, seq_len=seq_len, n_seq=n_seq, latent=latent,
                          has_s0=has_s0, n_prev=0 if has_s0 else layer),
        grid=(H_B, n_b // n_seq),
        in_specs=in_specs, out_specs=out_specs, out_shape=out_shape,
        compiler_params=_params(("arbitrary", "arbitrary")),
        name="retention_scan",
    )(*args)
    return (res[0], None) if has_s0 else (res[0], res[1])


def _post_kernel(oa_ref, ob_ref, gta_ref, gtb_ref, x_ref, gt1_ref, pa_ref, pb_ref, wo_ref,
                 o_ref):
    subs = [slice(r, r + POST_SUB) for r in range(0, POST_TILE, POST_SUB)]
    y_a = [jnp.dot(oa_ref[s, :], pa_ref[...], preferred_element_type=F32) for s in subs]
    y_b = [jnp.dot(ob_ref[s, :], pb_ref[...], preferred_element_type=F32) for s in subs]
    merged = [(_sigmoid(gta_ref[s, :].astype(F32)) * y_a[i]
               + _sigmoid(gtb_ref[s, :].astype(F32)) * y_b[i]).astype(BF16)
              for i, s in enumerate(subs)]
    y = [jnp.dot(m, wo_ref[...], preferred_element_type=F32) for m in merged]
    for i, s in enumerate(subs):
        o_ref[s, :] = x_ref[s, :] + gt1_ref[...] * y[i]


def _post_call(o_a, o_b, proj, x, mods, p_a, p_b, w_out, layer, row_fn):
    n_tok = x.shape[0]
    tile = lambda w: (lambda i, j: (i, w))
    pcol = lambda off: pl.BlockSpec((POST_TILE, D_MODEL), tile(off // D_MODEL))
    const = lambda a: _layer_spec(layer, a.shape[1:], lambda i, j: (0, 0))
    return pl.pallas_call(
        _post_kernel,
        grid=(n_tok // POST_TILE, 1),
        in_specs=[
            pl.BlockSpec((POST_TILE, H_A * DV_A), tile(0)),
            pl.BlockSpec((POST_TILE, H_B * DV_B), tile(0)),
            pcol(OFF_GATE_A), pcol(OFF_GATE_B),
            pl.BlockSpec((POST_TILE, D_MODEL), tile(0)),
            _mod_spec(layer, row_fn, 2, POST_TILE),
            const(p_a), const(p_b), const(w_out),
        ],
        out_specs=pl.BlockSpec((POST_TILE, D_MODEL), tile(0)),
        out_shape=jax.ShapeDtypeStruct((n_tok, D_MODEL), F32),
        compiler_params=_params(("arbitrary", "arbitrary")),
        name="mixer_output",
    )(o_a, o_b, proj, proj, x, mods, p_a, p_b, w_out)


def _conv3(u, wc, bc, seq_len):
    n = u.shape[0]
    row = lax.broadcasted_iota(jnp.int32, (SUBLANES, 1), 0)

    def zero_row(x, r0, r):
        return jnp.where(row == r, 0.0, x[r0:r0 + SUBLANES])

    prev, nxt = pltpu.roll(u, 1, axis=0), pltpu.roll(u, n - 1, axis=0)
    p_parts, n_parts = [], []
    for s0 in range(0, n, seq_len):
        s1 = s0 + seq_len
        p_parts += [zero_row(prev, s0, 0), prev[s0 + SUBLANES:s1]]
        n_parts += [nxt[s0:s1 - SUBLANES], zero_row(nxt, s1 - SUBLANES, SUBLANES - 1)]
    prev, nxt = jnp.concatenate(p_parts, axis=0), jnp.concatenate(n_parts, axis=0)
    return prev * wc[0:1] + u * wc[1:2] + nxt * wc[2:3] + bc


def _ffn_kernel(*refs, seq_len, final):
    it = iter(refs)
    (x_ref, sh_ref, sc_ref, gt_ref, n2_ref, wa_ref, wg_ref, wca_ref, wcg_ref, bca_ref, bcg_ref,
     wd_ref) = (next(it) for _ in range(12))
    fn_ref = next(it) if final else None
    o_ref, h_ref, acc_ref = next(it), next(it), next(it)
    j = pl.program_id(1)

    @pl.when(j == 0)
    def _():
        h_ref[...] = _norm_mod(x_ref[...], n2_ref[...], sc_ref[...], sh_ref[...]).astype(BF16)
        acc_ref[...] = jnp.zeros_like(acc_ref)

    h = h_ref[...]
    subs = [slice(c, min(c + FF_SUB, FF_CHUNK)) for c in range(0, FF_CHUNK, FF_SUB)]
    u_a = [jnp.dot(h, wa_ref[:, s], preferred_element_type=F32) for s in subs]
    u_g = [jnp.dot(h, wg_ref[:, s], preferred_element_type=F32) for s in subs]
    act = [(_silu(_conv3(u_g[i], wcg_ref[:, s], bcg_ref[:, s], seq_len))
            * _conv3(u_a[i], wca_ref[:, s], bca_ref[:, s], seq_len)).astype(BF16)
           for i, s in enumerate(subs)]
    acc_ref[...] += jnp.dot(jnp.concatenate(act, axis=1), wd_ref[...],
                            preferred_element_type=F32)

    @pl.when(j == pl.num_programs(1) - 1)
    def _():
        x = x_ref[...] + gt_ref[...] * acc_ref[...]
        if final:
            x = x * lax.rsqrt(jnp.mean(x * x, axis=-1, keepdims=True) + EPS) * fn_ref[...]
        o_ref[...] = x


def _ffn_call(x, mods, n2, w_up, w_conv, b_conv, w_down, final_norm, layer, row_fn, seq_len):
    n_tok = x.shape[0]
    n_ff = D_FF // FF_CHUNK
    lspec = functools.partial(_layer_spec, layer)
    in_specs = [
        pl.BlockSpec((TOKEN_TILE, D_MODEL), lambda i, j: (i, 0)),
        _mod_spec(layer, row_fn, 3), _mod_spec(layer, row_fn, 4), _mod_spec(layer, row_fn, 5),
        lspec((1, D_MODEL), lambda i, j: (0, 0)),
        lspec((D_MODEL, FF_CHUNK), lambda i, j: (0, j)),
        lspec((D_MODEL, FF_CHUNK), lambda i, j: (0, n_ff + j)),
        lspec((3, FF_CHUNK), lambda i, j: (0, j)),
        lspec((3, FF_CHUNK), lambda i, j: (0, n_ff + j)),
        lspec((1, FF_CHUNK), lambda i, j: (0, j)),
        lspec((1, FF_CHUNK), lambda i, j: (0, n_ff + j)),
        lspec((FF_CHUNK, D_MODEL), lambda i, j: (j, 0)),
    ]
    args = [x, mods, mods, mods, n2, w_up, w_up, w_conv, w_conv, b_conv, b_conv, w_down]
    final = final_norm is not None
    if final:
        in_specs.append(pl.BlockSpec((1, D_MODEL), lambda i, j: (0, 0)))
        args.append(final_norm.reshape(1, D_MODEL))
    return pl.pallas_call(
        functools.partial(_ffn_kernel, seq_len=seq_len, final=final),
        grid=(n_tok // TOKEN_TILE, n_ff),
        in_specs=in_specs,
        out_specs=pl.BlockSpec((TOKEN_TILE, D_MODEL), lambda i, j: (i, 0)),
        out_shape=jax.ShapeDtypeStruct((n_tok, D_MODEL), F32),
        scratch_shapes=[pltpu.VMEM((TOKEN_TILE, D_MODEL), BF16),
                        pltpu.VMEM((TOKEN_TILE, D_MODEL), F32)],
        compiler_params=_params(("arbitrary", "arbitrary")),
        name="conv_ffn",
    )(*args)


def kernel(x_prompt, x_sample, state_hgrn, state_ret, c, c_ctx, norm1, norm2, final_norm,
           w_mod, b_mod, w_in, hgrn_lb_raw, p_a, p_b, w_out, w_up, w_conv, b_conv, w_down):
    n_ctx, t_ctx, _ = x_prompt.shape
    n_dec, t_dec, _ = x_sample.shape
    assert t_ctx & (t_ctx - 1) == 0 and TOKEN_TILE % t_ctx == 0 and t_dec == TOKEN_TILE

    sm = jax.nn.softmax(hgrn_lb_raw.astype(F32), axis=0)
    cum = jnp.cumsum(sm, axis=0)
    lower_bounds = cum - cum[0:1]

    cvec = jnp.concatenate(
        [c_ctx[None, :], c, jnp.zeros((MOD_ROWS - 1 - n_dec, D_MODEL), F32)], axis=0)
    mods = _mod_call(cvec, w_mod, b_mod).reshape(DEPTH, MOD_ROWS, 1, 6 * D_MODEL)

    p_a_b, p_b_b, w_out_b, w_up_b, w_down_b = (
        w.astype(BF16) for w in (p_a, p_b, w_out, w_up, w_down))
    norm1_3, norm2_3 = norm1.reshape(DEPTH, 1, D_MODEL), norm2.reshape(DEPTH, 1, D_MODEL)
    b_conv_3 = b_conv.reshape(DEPTH, 1, 2 * D_FF)

    ctx_row = lambda tok: 0
    dec_row = lambda tok: tok // t_dec + 1

    def layer(x, l, n_b, seq_len, row_fn, s_hgrn, s_ret, prev_h, prev_r, latent):
        proj, la, key = _proj_call(x, mods, norm1_3, w_in, lower_bounds, l, row_fn)
        proj3 = proj.reshape(n_b, seq_len, PROJ_WIDTH)
        o_a, st_h = _hgrn_call(proj3, la.reshape(n_b, seq_len, -1), key.reshape(n_b, seq_len, -1),
                               s_hgrn, prev_h, l)
        o_b, st_r = _ret_call(proj3, s_ret, prev_r, l, latent)
        x = _post_call(o_a.reshape(-1, H_A * DV_A), o_b.reshape(-1, H_B * DV_B), proj, x,
                       mods, p_a_b, p_b_b, w_out_b, l, row_fn)
        x = _ffn_call(x, mods, norm2_3, w_up_b, w_conv, b_conv_3, w_down_b,
                      final_norm if l == DEPTH - 1 else None, l, row_fn, seq_len)
        return x, st_h, st_r

    x = x_prompt.reshape(n_ctx * t_ctx, D_MODEL)
    st_h = st_r = None
    for l in range(DEPTH):
        x, st_h, st_r = layer(x, l, n_ctx, t_ctx, ctx_row, None, None, st_h, st_r, False)
    y_prompt = x.reshape(n_ctx, t_ctx, D_MODEL)

    x = x_sample.reshape(n_dec * t_dec, D_MODEL)
    for l in range(DEPTH):
        x, _, _ = layer(x, l, n_dec, t_dec, dec_row, state_hgrn, state_ret, None, None, True)
    y_sample = x.reshape(n_dec, t_dec, D_MODEL)
    return (y_prompt, y_sample, st_h, st_r)
```

```python
import functools

import numpy as np
import jax
import jax.numpy as jnp
from jax import lax
from jax.experimental import pallas as pl
from jax.experimental.pallas import tpu as pltpu

F32 = jnp.float32
BF16 = jnp.bfloat16

D_MODEL = 1024
DEPTH = 2
GRID_W = 64
H_A, DK_A, DV_A = 8, 128, 128
H_B, DK_B, DV_B = 4, 256, 512
D_FF = 2816
ROPE_BASE = 10000.0
K_SCALE = DK_B ** -0.5
EPS = 1e-6
RET_DECAY_OFFSET_BWD = 0.5

W_OFF_ZF, W_OFF_ZB, W_OFF_IA = 1024, 2048, 3072
OFF_QA, OFF_IA, OFF_GA = 0, 1024, 2048
OFF_QB, OFF_KB, OFF_VB, OFF_GB = 3072, 4096, 5120, 7168
OFF_GATE_A, OFF_GATE_B = 9216, 10240
PROJ_WIDTH = 11264

MOD_ROWS = 8
MOD_TN = 1024
TOKEN_TILE = 1024
POST_TILE = 512
POST_SUB = 256
FF_CHUNK = 1408
FF_SUB = 256
HGRN_CHUNK = 128
HGRN_UNITS = 32
LEAF_LEVEL = 5
LEAF_MAX_DECAY = 80.0
RET_ROWS = 256
RET_TOKENS = 1024
PROJ_TN = 1024
Z_TILE0 = W_OFF_ZF // PROJ_TN
Z_TILES = (W_OFF_IA - W_OFF_ZF) // PROJ_TN
VMEM_LIMIT = 52 * 1024 * 1024
SUBLANES = 8
LOG2E = 1.4426950408889634

_NT = (((1,), (1,)), ((), ()))
_TN = (((0,), (0,)), ((), ()))


def _params(sem):
    return pltpu.CompilerParams(dimension_semantics=sem, vmem_limit_bytes=VMEM_LIMIT)


def _sigmoid(x):
    return jax.nn.sigmoid(x)


def _silu(x):
    return x * jax.nn.sigmoid(x)


def _mod_kernel(c_ref, w_ref, b_ref, o_ref):
    s = _silu(c_ref[...])
    o_ref[...] = jnp.dot(s, w_ref[...], precision=lax.Precision.HIGHEST,
                         preferred_element_type=F32) + b_ref[...]


def _mod_call(cvec, w_mod, b_mod):
    n_col = 6 * D_MODEL // MOD_TN
    return pl.pallas_call(
        _mod_kernel,
        grid=(DEPTH, n_col),
        in_specs=[
            pl.BlockSpec((MOD_ROWS, D_MODEL), lambda l, j: (0, 0)),
            pl.BlockSpec((None, D_MODEL, MOD_TN), lambda l, j: (l, 0, j)),
            pl.BlockSpec((None, 1, MOD_TN), lambda l, j: (l, 0, j)),
        ],
        out_specs=pl.BlockSpec((None, MOD_ROWS, MOD_TN), lambda l, j: (l, 0, j)),
        out_shape=jax.ShapeDtypeStruct((DEPTH, MOD_ROWS, 6 * D_MODEL), F32),
        compiler_params=_params(("arbitrary", "arbitrary")),
        name="modulation",
    )(cvec, w_mod, b_mod.reshape(DEPTH, 1, 6 * D_MODEL))


def _mod_spec(layer, row_fn, which, tile=TOKEN_TILE):
    return pl.BlockSpec((None, None, 1, D_MODEL),
                        lambda i, j: (layer, row_fn(i * tile), 0, which))


def _layer_spec(layer, block, index_map):
    return pl.BlockSpec((None,) + block, lambda *g: (layer,) + index_map(*g))


def _norm_mod(x, g, sc, sh):
    y = x * lax.rsqrt(jnp.mean(x * x, axis=-1, keepdims=True) + EPS) * g
    return y * (1.0 + sc) + sh


def _proj_kernel(x_ref, sh_ref, sc_ref, n_ref, w_ref, o_ref, hout_ref, h_ref, wb_ref):
    j, i = pl.program_id(0), pl.program_id(1)

    @pl.when(i == 0)
    def _():
        wb_ref[...] = w_ref[...].astype(BF16)

    @pl.when(j == 0)
    def _():
        h = _norm_mod(x_ref[...], n_ref[...], sc_ref[...], sh_ref[...]).astype(BF16)
        h_ref[i] = h
        hout_ref[...] = h

    o_ref[...] = jnp.dot(h_ref[i], wb_ref[...], preferred_element_type=F32).astype(BF16)


def _zproj_kernel(h_ref, w_ref, lb_ref, la_ref, key_ref, wb_ref):
    @pl.when(pl.program_id(1) == 0)
    def _():
        wb_ref[...] = w_ref[...].astype(BF16)

    z = jnp.dot(h_ref[...], wb_ref[...], preferred_element_type=F32)
    la, key = _hgrn_forget(z, lb_ref[...])
    la_ref[...] = la
    key_ref[...] = key.astype(BF16)


def _proj_call(x, mods, n1, w_in, lb, layer, row_fn):
    n_tok = x.shape[0]
    tn = PROJ_TN
    n_tile = n_tok // TOKEN_TILE
    hold = lambda j, i: jnp.where(j == 0, i, n_tile - 1)
    w_tile = lambda j: jnp.where(j >= Z_TILE0, j + Z_TILES, j)
    mod = lambda which: pl.BlockSpec(
        (None, None, 1, D_MODEL), lambda j, i: (layer, row_fn(i * TOKEN_TILE), 0, which))
    proj, h = pl.pallas_call(
        _proj_kernel,
        grid=(PROJ_WIDTH // tn, n_tile),
        in_specs=[
            pl.BlockSpec((TOKEN_TILE, D_MODEL), lambda j, i: (hold(j, i), 0)),
            mod(0), mod(1),
            _layer_spec(layer, (1, D_MODEL), lambda j, i: (0, 0)),
            _layer_spec(layer, (D_MODEL, tn), lambda j, i: (0, w_tile(j))),
        ],
        out_specs=[
            pl.BlockSpec((TOKEN_TILE, tn), lambda j, i: (i, j)),
            pl.BlockSpec((TOKEN_TILE, D_MODEL), lambda j, i: (hold(j, i), 0)),
        ],
        out_shape=[jax.ShapeDtypeStruct((n_tok, PROJ_WIDTH), BF16),
                   jax.ShapeDtypeStruct((n_tok, D_MODEL), BF16)],
        scratch_shapes=[pltpu.VMEM((n_tile, TOKEN_TILE, D_MODEL), BF16),
                        pltpu.VMEM((D_MODEL, tn), BF16)],
        compiler_params=_params(("arbitrary", "arbitrary")),
        name="in_projection",
    )(x, mods, mods, n1, w_in)
    la, key = pl.pallas_call(
        _zproj_kernel,
        grid=(Z_TILES, n_tile),
        in_specs=[
            pl.BlockSpec((TOKEN_TILE, D_MODEL), lambda j, i: (i, 0)),
            _layer_spec(layer, (D_MODEL, tn), lambda j, i: (0, Z_TILE0 + j)),
            pl.BlockSpec((None, None, 1, tn), lambda j, i: (layer, j, 0, 0)),
        ],
        out_specs=[pl.BlockSpec((TOKEN_TILE, tn), lambda j, i: (i, j))] * 2,
        out_shape=[jax.ShapeDtypeStruct((n_tok, Z_TILES * tn), F32),
                   jax.ShapeDtypeStruct((n_tok, Z_TILES * tn), BF16)],
        scratch_shapes=[pltpu.VMEM((D_MODEL, tn), BF16)],
        compiler_params=_params(("arbitrary", "arbitrary")),
        name="forget_projection",
    )(h, w_in, lb.reshape(DEPTH, Z_TILES, 1, tn))
    return proj, la, key


def _hgrn_consts(c):
    nl = int(np.log2(c))
    t = np.arange(c)
    tt, rr = t[:, None], t[None, :]
    x = tt ^ rr
    lev_of = np.where(x > 0, np.floor(np.log2(np.maximum(x, 1))) + 1, 0).astype(np.int32)
    lv_f = np.where(tt >= rr, lev_of, -1).astype(np.int32)
    lv_b = lv_f.T.copy()
    return ((rr <= tt).astype(np.float32), (rr >= tt).astype(np.float32), lv_f, lv_b, nl)


def _leaf_factors(cum, cum_row, reverse):
    c, dk = cum.shape
    leaf = 1 << LEAF_LEVEL
    pieces = []
    for base in range(0, c, leaf):
        row = base + leaf if reverse else base - 1
        if 0 <= row < c:
            pieces.append(jnp.broadcast_to(cum_row(row), (leaf, dk)))
        else:
            pieces.append(jnp.zeros((leaf, dk), F32))
    x = cum - jnp.concatenate(pieces, axis=0)
    return jnp.exp2(x * LOG2E).astype(BF16), jnp.exp2(x * (-LOG2E)).astype(BF16)


def _level_factor(la, cum, cum_row, lev, reverse):
    c, dk = la.shape
    if lev == 1:
        row = lax.broadcasted_iota(jnp.int32, (c, 1), 0)
        return jnp.exp(jnp.where((row & 1) == (0 if reverse else 1), la, 0.0)).astype(BF16)
    blk, half = 1 << lev, 1 << (lev - 1)

    def boundary(r):
        base = (r // blk) * blk
        return jnp.broadcast_to(cum_row(base + half if reverse else base + half - 1),
                                (SUBLANES, dk))

    top = lax.broadcasted_iota(jnp.int32, (SUBLANES, dk), 0) < SUBLANES // 2
    pieces = []
    for r0 in range(0, c, SUBLANES):
        if blk >= SUBLANES:
            pieces.append(boundary(r0))
        else:
            pieces.append(jnp.where(top, boundary(r0), boundary(r0 + SUBLANES // 2)))
    return jnp.exp2(jnp.abs(cum - jnp.concatenate(pieces, axis=0)) * (-LOG2E)).astype(BF16)


def _hgrn_forget(z, lb):
    e = jnp.exp(-jnp.abs(z))
    one_e = 1.0 + e
    r = 1.0 / one_e
    pos = z >= 0.0
    sig = jnp.where(pos, 1.0, e) * r
    sig_neg = jnp.where(pos, e, 1.0) * r
    oml = 1.0 - lb
    log_sig = jnp.minimum(z, 0.0) - jnp.log(one_e)
    log_f = jnp.where(lb > 0.0, jnp.log(lb + oml * sig), log_sig)
    return log_f, oml * sig_neg


def _hgrn_kernel(*refs, seq_len, chunk, n_lev, heads, has_s0, n_prev):
    it = iter(refs)
    (q_ref, laf_ref, lab_ref, kf_ref, kb_ref, i_ref, g_ref, trif_ref, trib_ref, lvf_ref, lvb_ref,
     leaf_ref) = (next(it) for _ in range(12))
    s0_ref = next(it) if has_s0 else None
    prev_ref = next(it) if n_prev else None
    o_ref = next(it)
    st_ref = None if has_s0 else next(it)
    cum_ref, qt_ref, kt_ref, qd_ref, kd_ref, ae_ref, s_ref, acc_ref = (next(it) for _ in range(8))
    n_chunks = seq_len // chunk
    dk = DK_A
    units = [(2 * hh + d, hh, d, d == 1, la_in, tri_ref, lv_ref)
             for hh in range(heads)
             for d, (la_in, tri_ref, lv_ref) in enumerate(((laf_ref, trif_ref, lvf_ref),
                                                           (lab_ref, trib_ref, lvb_ref)))]
    key_in = (kf_ref, kb_ref)
    rows_of = lambda c: slice(c * chunk, (c + 1) * chunk)
    cols_of = lambda hh: slice(hh * dk, (hh + 1) * dk)

    worst = None
    for _, hh, _, _, la_in, _, _ in units:
        leaf_sum = jnp.dot(leaf_ref[...], la_in[:, cols_of(hh)].astype(BF16),
                           preferred_element_type=F32)
        worst = leaf_sum if worst is None else jnp.minimum(worst, leaf_sum)
    leafwise_ok = jnp.min(worst) > -LEAF_MAX_DECAY

    for u, hh, _, _, la_in, tri_ref, _ in units:
        tri = tri_ref[...]
        for c in range(n_chunks):
            la = la_in[rows_of(c), cols_of(hh)]
            la_hi = la.astype(BF16)
            la_lo = (la - la_hi.astype(F32)).astype(BF16)
            x2 = jnp.dot(tri, jnp.concatenate([la_hi, la_lo], axis=1),
                         preferred_element_type=F32)
            cum_ref[u, rows_of(c), :] = x2[:, :dk] + x2[:, dk:]

    def operands_and_scores(leafwise):
        if leafwise:
            levels = list(range(LEAF_LEVEL + 1, n_lev + 1))
        else:
            levels = list(range(1, n_lev + 1))
        n_terms = len(levels) + 1
        for u, hh, d, reverse, la_in, _, _ in units:
            for c in range(n_chunks):
                rows = rows_of(c)
                cum = cum_ref[u, rows, :]
                la = la_in[rows, cols_of(hh)]
                q = q_ref[rows, cols_of(hh)]
                k = key_in[d][rows, cols_of(hh)]
                cum_row = lambda r, u=u, c=c: cum_ref[u, c * chunk + r:c * chunk + r + 1, :]
                if leafwise:
                    factors = [_leaf_factors(cum, cum_row, reverse)]
                else:
                    factors = [None]
                for lev in levels:
                    e = _level_factor(la, cum, cum_row, lev, reverse)
                    factors.append((e, e))
                for t, f in enumerate(factors):
                    cols = slice(t * dk, (t + 1) * dk)
                    qt_ref[u, rows, cols] = q if f is None else q * f[0]
                    kt_ref[u, rows, cols] = k if f is None else k * f[1]
                end_row = 0 if reverse else chunk - 1
                cum_end = cum[end_row:end_row + 1]
                qd_ref[u, rows, :] = q * jnp.exp(cum).astype(BF16)
                kd_ref[u, rows, :] = k * jnp.exp(cum_end - cum).astype(BF16)
                ae_ref[u, c:c + 1, :] = jnp.exp(cum_end)
        zero = jnp.zeros((chunk, dk), BF16)
        for u, _, _, _, _, _, lv_ref in units:
            lv = lv_ref[...]
            if leafwise:
                masks = [(lv >= 0) & (lv <= LEAF_LEVEL)]
            else:
                masks = [lv == 0]
            masks += [lv == lev for lev in levels]
            for c in range(n_chunks):
                rows = rows_of(c)
                scores = jnp.zeros((chunk, chunk), F32)
                for t in range(0, n_terms, 2):
                    if t + 1 < n_terms:
                        k_a = kt_ref[u, rows, t * dk:(t + 1) * dk]
                        k_b = kt_ref[u, rows, (t + 1) * dk:(t + 2) * dk]
                        p = lax.dot_general(
                            qt_ref[u, rows, t * dk:(t + 2) * dk],
                            jnp.concatenate([jnp.concatenate([k_a, zero], axis=1),
                                             jnp.concatenate([zero, k_b], axis=1)], axis=0),
                            _NT, preferred_element_type=F32)
                        scores = jnp.where(masks[t], p[:, :chunk], scores)
                        scores = jnp.where(masks[t + 1], p[:, chunk:], scores)
                    else:
                        p = lax.dot_general(qt_ref[u, rows, t * dk:(t + 1) * dk],
                                            kt_ref[u, rows, t * dk:(t + 1) * dk],
                                            _NT, preferred_element_type=F32)
                        scores = jnp.where(masks[t], p, scores)
                s_ref[u, rows, :] = scores.astype(BF16)

    pl.when(leafwise_ok)(lambda: operands_and_scores(True))
    pl.when(jnp.logical_not(leafwise_ok))(lambda: operands_and_scores(False))

    incs = {u: [lax.dot_general(i_ref[rows_of(c), cols_of(hh)], kd_ref[u, rows_of(c), :], _TN,
                                preferred_element_type=F32) for c in range(n_chunks)]
            for u, hh, _, _, _, _, _ in units}
    finals, befores = {}, {}
    for u, hh, d, reverse, _, _, _ in units:
        st = s0_ref[d, hh].T if has_s0 else jnp.zeros((DV_A, dk), F32)
        befores[u] = [None] * n_chunks
        for c in (range(n_chunks - 1, -1, -1) if reverse else range(n_chunks)):
            befores[u][c] = st.astype(BF16)
            st = st * ae_ref[u, c:c + 1, :] + incs[u][c]
        finals[u] = st
    for u, hh, d, reverse, _, _, _ in units:
        for c in range(n_chunks):
            rows = rows_of(c)
            o = jnp.dot(s_ref[u, rows, :], i_ref[rows, cols_of(hh)], preferred_element_type=F32)
            if has_s0 or c != (n_chunks - 1 if reverse else 0):
                o = o + lax.dot_general(qd_ref[u, rows, :], befores[u][c], _NT,
                                        preferred_element_type=F32)
            if d == 0:
                acc_ref[rows, cols_of(hh)] = o
            else:
                acc_ref[rows, cols_of(hh)] += o

    for hh in range(heads):
        o = acc_ref[:, cols_of(hh)]
        o = o * lax.rsqrt(jnp.mean(o * o, axis=-1, keepdims=True) + EPS)
        o_ref[:, cols_of(hh)] = (o * _silu(g_ref[:, cols_of(hh)].astype(F32))).astype(BF16)
    if st_ref is not None:
        if n_prev:
            st_ref[0:n_prev] = prev_ref[...]
        for u, hh, d, _, _, _, _ in units:
            st_ref[n_prev, d, hh] = finals[u].T


def _hgrn_call(proj, la, key, state, prev, layer):
    n_b, seq_len, _ = proj.shape
    tri_f, tri_b, lv_f, lv_b, n_lev = _hgrn_consts(HGRN_CHUNK)
    has_s0 = state is not None
    n_chunks = seq_len // HGRN_CHUNK
    heads = max(1, HGRN_UNITS // (2 * n_chunks))
    width = heads * DK_A
    col = lambda off: (lambda b, h: (b, 0, off // width + h))
    const = lambda a: pl.BlockSpec(a.shape, lambda b, h: (0, 0))
    bwd = W_OFF_ZB - W_OFF_ZF
    in_specs = [pl.BlockSpec((None, seq_len, width), col(o))
                for o in (OFF_QA, 0, bwd, 0, bwd, OFF_IA, OFF_GA)]
    leaf = 1 << LEAF_LEVEL
    leaf_rows = -(-(seq_len // leaf) // SUBLANES) * SUBLANES
    leaf_ind = (np.arange(seq_len)[None, :] // leaf == np.arange(leaf_rows)[:, None])
    in_specs += [const(tri_f), const(tri_b), const(lv_f), const(lv_b), const(leaf_ind)]
    args = [proj, la, la, key, key, proj, proj, jnp.asarray(tri_f, BF16), jnp.asarray(tri_b, BF16),
            jnp.asarray(lv_f), jnp.asarray(lv_b), jnp.asarray(leaf_ind, BF16)]
    out_specs = [pl.BlockSpec((None, seq_len, width), lambda b, h: (b, 0, h))]
    out_shape = [jax.ShapeDtypeStruct((n_b, seq_len, H_A * DV_A), BF16)]
    if has_s0:
        in_specs.append(pl.BlockSpec((None, None, 2, heads, DK_A, DV_A),
                                     lambda b, h: (b, layer, 0, h, 0, 0)))
        args.append(state)
    else:
        st_spec = lambda n: pl.BlockSpec((None, n, 2, heads, DK_A, DV_A),
                                         lambda b, h: (b, 0, 0, h, 0, 0))
        if layer:
            in_specs.append(st_spec(layer))
            args.append(prev)
        out_specs.append(st_spec(layer + 1))
        out_shape.append(jax.ShapeDtypeStruct((n_b, layer + 1, 2, H_A, DK_A, DV_A), F32))
    n_u = 2 * heads
    res = pl.pallas_call(
        functools.partial(_hgrn_kernel, seq_len=seq_len, chunk=HGRN_CHUNK, n_lev=n_lev,
                          heads=heads, has_s0=has_s0, n_prev=0 if has_s0 else layer),
        grid=(n_b, H_A // heads),
        in_specs=in_specs, out_specs=out_specs, out_shape=out_shape,
        scratch_shapes=[
            pltpu.VMEM((n_u, seq_len, DK_A), F32),
            pltpu.VMEM((n_u, seq_len, (n_lev + 1) * DK_A), BF16),
            pltpu.VMEM((n_u, seq_len, (n_lev + 1) * DK_A), BF16),
            pltpu.VMEM((n_u, seq_len, DK_A), BF16),
            pltpu.VMEM((n_u, seq_len, DK_A), BF16),
            pltpu.VMEM((n_u, max(SUBLANES, n_chunks), DK_A), F32),
            pltpu.VMEM((n_u, seq_len, HGRN_CHUNK), BF16),
            pltpu.VMEM((seq_len, width), F32),
        ],
        compiler_params=_params(("arbitrary", "arbitrary")),
        name="hgrn2_scan",
    )(*args)
    return (res[0], None) if has_s0 else (res[0], res[1])


def _rope_tables(seq_len):
    rows = seq_len // GRID_W
    r_idx = jnp.repeat(jnp.arange(rows), GRID_W).astype(F32)
    c_idx = jnp.tile(jnp.arange(GRID_W), rows).astype(F32)
    quarter = DK_B // 4
    inv = 1.0 / (ROPE_BASE ** (jnp.arange(quarter, dtype=F32) / quarter))
    ang_r = r_idx[:, None] * inv[None, :]
    ang_c = c_idx[:, None] * inv[None, :]
    cos = jnp.concatenate([jnp.cos(ang_r)] * 2 + [jnp.cos(ang_c)] * 2, axis=1)
    sin = jnp.concatenate([-jnp.sin(ang_r), jnp.sin(ang_r), -jnp.sin(ang_c), jnp.sin(ang_c)],
                          axis=1)
    return cos, sin


def _rope(x, cos, sin):
    half = DK_B // 2
    swapped = jnp.concatenate([pltpu.roll(x[:, :half], half // 2, axis=1),
                               pltpu.roll(x[:, half:], half // 2, axis=1)], axis=1)
    return x * cos + swapped * sin


def _ret_kernel(*refs, seq_len, n_seq, latent, has_s0, n_prev):
    it = iter(refs)
    lg_ref, dm_ref, q_ref, k_ref, v_ref, g_ref = (next(it) for _ in range(6))
    cos_ref, sin_ref = (next(it), next(it)) if latent else (None, None)
    s0_ref = next(it) if has_s0 else None
    prev_ref = next(it) if n_prev else None
    o_ref = next(it)
    st_ref = None if has_s0 else next(it)
    h = pl.program_id(0)
    lg_f = lg_ref[h, 0]
    lg_b = lg_ref[h, 1]

    qs, ks = [], []
    for s in range(n_seq):
        q, k = q_ref[s], k_ref[s]
        if latent:
            cos, sin = cos_ref[...], sin_ref[...]
            q, k = _rope(q.astype(F32), cos, sin), _rope(k.astype(F32), cos, sin)
        qs.append(q)
        ks.append(k)
    items = [(s, slice(r, r + RET_ROWS)) for s in range(n_seq)
             for r in range(0, seq_len, RET_ROWS)]
    ps = [lax.dot_general(qs[s][rows].astype(BF16), ks[s].astype(BF16), _NT,
                          preferred_element_type=F32) for s, rows in items]
    carried = []
    if has_s0:
        for s, rows in items:
            s0 = jnp.concatenate([s0_ref[s, 0], s0_ref[s, 1]], axis=0).astype(BF16)
            pos = (lax.broadcasted_iota(jnp.int32, (RET_ROWS, 1), 0) + rows.start).astype(F32)
            q_s = jnp.concatenate([qs[s][rows] * jnp.exp(lg_f * (pos + 1.0)),
                                   qs[s][rows] * jnp.exp(lg_b * (seq_len - pos))], axis=1)
            carried.append(jnp.dot(q_s.astype(BF16), s0, preferred_element_type=F32))
    pds = [p.astype(BF16) * dm_ref[rows, :] for p, (_, rows) in zip(ps, items)]
    outs = [jnp.dot(pd, v_ref[s], preferred_element_type=F32) for pd, (s, _) in zip(pds, items)]
    for i, (s, rows) in enumerate(items):
        o = outs[i] + carried[i] if has_s0 else outs[i]
        o = o * lax.rsqrt(jnp.mean(o * o, axis=-1, keepdims=True) + EPS)
        o_ref[s, rows, :] = (o * _silu(g_ref[s, rows, :].astype(F32))).astype(BF16)
    if st_ref is not None:
        if n_prev:
            st_ref[:, 0:n_prev] = prev_ref[...]
        pos = lax.broadcasted_iota(jnp.int32, (seq_len, 1), 0).astype(F32)
        w_f = K_SCALE * jnp.exp(lg_f * (seq_len - 1.0 - pos))
        w_b = K_SCALE * jnp.exp(lg_b * pos)
        for s in range(n_seq):
            st_ref[s, n_prev, 0] = lax.dot_general((ks[s] * w_f).astype(BF16), v_ref[s], _TN,
                                                   preferred_element_type=F32)
            st_ref[s, n_prev, 1] = lax.dot_general((ks[s] * w_b).astype(BF16), v_ref[s], _TN,
                                                   preferred_element_type=F32)


def _ret_log_decay():
    heads = jnp.arange(H_B, dtype=F32)
    lg_f = jnp.log1p(-jnp.exp2(-5.0 - heads))
    lg_b = jnp.log1p(-jnp.exp2(-(5.0 + RET_DECAY_OFFSET_BWD) - heads))
    return jnp.stack([lg_f, lg_b], axis=1)


def _ret_decay_mask(seq_len):
    lg = _ret_log_decay()
    d = (jnp.arange(seq_len)[:, None] - jnp.arange(seq_len)[None, :]).astype(F32)[None]
    lg_f, lg_b = lg[:, 0, None, None], lg[:, 1, None, None]
    dm = (jnp.where(d >= 0.0, jnp.exp(lg_f * jnp.maximum(d, 0.0)), 0.0)
          + jnp.where(d <= 0.0, jnp.exp(lg_b * jnp.maximum(-d, 0.0)), 0.0))
    return (K_SCALE * dm).astype(BF16)


def _ret_call(proj, state, prev, layer, latent):
    n_b, seq_len, _ = proj.shape
    has_s0 = state is not None
    n_seq = max(1, RET_TOKENS // seq_len)
    in_specs = [
        pl.BlockSpec(memory_space=pltpu.SMEM),
        pl.BlockSpec((None, seq_len, seq_len), lambda h, b: (h, 0, 0)),
        pl.BlockSpec((n_seq, seq_len, DK_B), lambda h, b: (b, 0, OFF_QB // DK_B + h)),
        pl.BlockSpec((n_seq, seq_len, DK_B), lambda h, b: (b, 0, OFF_KB // DK_B + h)),
        pl.BlockSpec((n_seq, seq_len, DV_B), lambda h, b: (b, 0, OFF_VB // DV_B + h)),
        pl.BlockSpec((n_seq, seq_len, DV_B), lambda h, b: (b, 0, OFF_GB // DV_B + h)),
    ]
    args = [_ret_log_decay(), _ret_decay_mask(seq_len), proj, proj, proj, proj]
    if latent:
        cos, sin = _rope_tables(seq_len)
        in_specs += [pl.BlockSpec((seq_len, DK_B), lambda h, b: (0, 0))] * 2
        args += [cos, sin]
    out_specs = [pl.BlockSpec((n_seq, seq_len, DV_B), lambda h, b: (b, 0, h))]
    out_shape = [jax.ShapeDtypeStruct((n_b, seq_len, H_B * DV_B), BF16)]
    if has_s0:
        in_specs.append(pl.BlockSpec((n_seq, None, 2, None, DK_B, DV_B),
                                     lambda h, b: (b, layer, 0, h, 0, 0)))
        args.append(state)
    else:
        st_spec = lambda n: pl.BlockSpec((n_seq, n, 2, None, DK_B, DV_B),
                                         lambda h, b: (b, 0, 0, h, 0, 0))
        if layer:
            in_specs.append(st_spec(layer))
            args.append(prev)
        out_specs.append(st_spec(layer + 1))
        out_shape.append(jax.ShapeDtypeStruct((n_b, layer + 1, 2, H_B, DK_B, DV_B), F32))
    res = pl.pallas_call(
        functools.partial(_ret_kernel, seq_len=seq_len, n_seq=n_seq, latent=latent,
                          has_s0=has_s0, n_prev=0 if has_s0 else layer),
        grid=(H_B, n_b // n_seq),
        in_specs=in_specs, out_specs=out_specs, out_shape=out_shape,
        compiler_params=_params(("arbitrary", "arbitrary")),
        name="retention_scan",
    )(*args)
    return (res[0], None) if has_s0 else (res[0], res[1])


def _post_kernel(oa_ref, ob_ref, gta_ref, gtb_ref, x_ref, gt1_ref, pa_ref, pb_ref, wo_ref,
                 o_ref):
    subs = [slice(r, r + POST_SUB) for r in range(0, POST_TILE, POST_SUB)]
    y_a = [jnp.dot(oa_ref[s, :], pa_ref[...], preferred_element_type=F32) for s in subs]
    y_b = [jnp.dot(ob_ref[s, :], pb_ref[...], preferred_element_type=F32) for s in subs]
    merged = [(_sigmoid(gta_ref[s, :].astype(F32)) * y_a[i]
               + _sigmoid(gtb_ref[s, :].astype(F32)) * y_b[i]).astype(BF16)
              for i, s in enumerate(subs)]
    y = [jnp.dot(m, wo_ref[...], preferred_element_type=F32) for m in merged]
    for i, s in enumerate(subs):
        o_ref[s, :] = x_ref[s, :] + gt1_ref[...] * y[i]


def _post_call(o_a, o_b, proj, x, mods, p_a, p_b, w_out, layer, row_fn):
    n_tok = x.shape[0]
    tile = lambda w: (lambda i, j: (i, w))
    pcol = lambda off: pl.BlockSpec((POST_TILE, D_MODEL), tile(off // D_MODEL))
    const = lambda a: _layer_spec(layer, a.shape[1:], lambda i, j: (0, 0))
    return pl.pallas_call(
        _post_kernel,
        grid=(n_tok // POST_TILE, 1),
        in_specs=[
            pl.BlockSpec((POST_TILE, H_A * DV_A), tile(0)),
            pl.BlockSpec((POST_TILE, H_B * DV_B), tile(0)),
            pcol(OFF_GATE_A), pcol(OFF_GATE_B),
            pl.BlockSpec((POST_TILE, D_MODEL), tile(0)),
            _mod_spec(layer, row_fn, 2, POST_TILE),
            const(p_a), const(p_b), const(w_out),
        ],
        out_specs=pl.BlockSpec((POST_TILE, D_MODEL), tile(0)),
        out_shape=jax.ShapeDtypeStruct((n_tok, D_MODEL), F32),
        compiler_params=_params(("arbitrary", "arbitrary")),
        name="mixer_output",
    )(o_a, o_b, proj, proj, x, mods, p_a, p_b, w_out)


def _conv3(u, wc, bc, seq_len):
    n = u.shape[0]
    row = lax.broadcasted_iota(jnp.int32, (SUBLANES, 1), 0)

    def zero_row(x, r0, r):
        return jnp.where(row == r, 0.0, x[r0:r0 + SUBLANES])

    prev, nxt = pltpu.roll(u, 1, axis=0), pltpu.roll(u, n - 1, axis=0)
    p_parts, n_parts = [], []
    for s0 in range(0, n, seq_len):
        s1 = s0 + seq_len
        p_parts += [zero_row(prev, s0, 0), prev[s0 + SUBLANES:s1]]
        n_parts += [nxt[s0:s1 - SUBLANES], zero_row(nxt, s1 - SUBLANES, SUBLANES - 1)]
    prev, nxt = jnp.concatenate(p_parts, axis=0), jnp.concatenate(n_parts, axis=0)
    return prev * wc[0:1] + u * wc[1:2] + nxt * wc[2:3] + bc


def _ffn_kernel(*refs, seq_len, final):
    it = iter(refs)
    (x_ref, sh_ref, sc_ref, gt_ref, n2_ref, wa_ref, wg_ref, wca_ref, wcg_ref, bca_ref, bcg_ref,
     wd_ref) = (next(it) for _ in range(12))
    fn_ref = next(it) if final else None
    o_ref, h_ref, acc_ref = next(it), next(it), next(it)
    j = pl.program_id(1)

    @pl.when(j == 0)
    def _():
        h_ref[...] = _norm_mod(x_ref[...], n2_ref[...], sc_ref[...], sh_ref[...]).astype(BF16)
        acc_ref[...] = jnp.zeros_like(acc_ref)

    h = h_ref[...]
    subs = [slice(c, min(c + FF_SUB, FF_CHUNK)) for c in range(0, FF_CHUNK, FF_SUB)]
    u_a = [jnp.dot(h, wa_ref[:, s], preferred_element_type=F32) for s in subs]
    u_g = [jnp.dot(h, wg_ref[:, s], preferred_element_type=F32) for s in subs]
    act = [(_silu(_conv3(u_g[i], wcg_ref[:, s], bcg_ref[:, s], seq_len))
            * _conv3(u_a[i], wca_ref[:, s], bca_ref[:, s], seq_len)).astype(BF16)
           for i, s in enumerate(subs)]
    acc_ref[...] += jnp.dot(jnp.concatenate(act, axis=1), wd_ref[...],
                            preferred_element_type=F32)

    @pl.when(j == pl.num_programs(1) - 1)
    def _():
        x = x_ref[...] + gt_ref[...] * acc_ref[...]
        if final:
            x = x * lax.rsqrt(jnp.mean(x * x, axis=-1, keepdims=True) + EPS) * fn_ref[...]
        o_ref[...] = x


def _ffn_call(x, mods, n2, w_up, w_conv, b_conv, w_down, final_norm, layer, row_fn, seq_len):
    n_tok = x.shape[0]
    n_ff = D_FF // FF_CHUNK
    lspec = functools.partial(_layer_spec, layer)
    in_specs = [
        pl.BlockSpec((TOKEN_TILE, D_MODEL), lambda i, j: (i, 0)),
        _mod_spec(layer, row_fn, 3), _mod_spec(layer, row_fn, 4), _mod_spec(layer, row_fn, 5),
        lspec((1, D_MODEL), lambda i, j: (0, 0)),
        lspec((D_MODEL, FF_CHUNK), lambda i, j: (0, j)),
        lspec((D_MODEL, FF_CHUNK), lambda i, j: (0, n_ff + j)),
        lspec((3, FF_CHUNK), lambda i, j: (0, j)),
        lspec((3, FF_CHUNK), lambda i, j: (0, n_ff + j)),
        lspec((1, FF_CHUNK), lambda i, j: (0, j)),
        lspec((1, FF_CHUNK), lambda i, j: (0, n_ff + j)),
        lspec((FF_CHUNK, D_MODEL), lambda i, j: (j, 0)),
    ]
    args = [x, mods, mods, mods, n2, w_up, w_up, w_conv, w_conv, b_conv, b_conv, w_down]
    final = final_norm is not None
    if final:
        in_specs.append(pl.BlockSpec((1, D_MODEL), lambda i, j: (0, 0)))
        args.append(final_norm.reshape(1, D_MODEL))
    return pl.pallas_call(
        functools.partial(_ffn_kernel, seq_len=seq_len, final=final),
        grid=(n_tok // TOKEN_TILE, n_ff),
        in_specs=in_specs,
        out_specs=pl.BlockSpec((TOKEN_TILE, D_MODEL), lambda i, j: (i, 0)),
        out_shape=jax.ShapeDtypeStruct((n_tok, D_MODEL), F32),
        scratch_shapes=[pltpu.VMEM((TOKEN_TILE, D_MODEL), BF16),
                        pltpu.VMEM((TOKEN_TILE, D_MODEL), F32)],
        compiler_params=_params(("arbitrary", "arbitrary")),
        name="conv_ffn",
    )(*args)


def kernel(x_prompt, x_sample, state_hgrn, state_ret, c, c_ctx, norm1, norm2, final_norm,
           w_mod, b_mod, w_in, hgrn_lb_raw, p_a, p_b, w_out, w_up, w_conv, b_conv, w_down):
    n_ctx, t_ctx, _ = x_prompt.shape
    n_dec, t_dec, _ = x_sample.shape
    assert t_ctx & (t_ctx - 1) == 0 and TOKEN_TILE % t_ctx == 0 and t_dec == TOKEN_TILE

    sm = jax.nn.softmax(hgrn_lb_raw.astype(F32), axis=0)
    cum = jnp.cumsum(sm, axis=0)
    lower_bounds = cum - cum[0:1]

    cvec = jnp.concatenate(
        [c_ctx[None, :], c, jnp.zeros((MOD_ROWS - 1 - n_dec, D_MODEL), F32)], axis=0)
    mods = _mod_call(cvec, w_mod, b_mod).reshape(DEPTH, MOD_ROWS, 1, 6 * D_MODEL)

    p_a_b, p_b_b, w_out_b, w_up_b, w_down_b = (
        w.astype(BF16) for w in (p_a, p_b, w_out, w_up, w_down))
    norm1_3, norm2_3 = norm1.reshape(DEPTH, 1, D_MODEL), norm2.reshape(DEPTH, 1, D_MODEL)
    b_conv_3 = b_conv.reshape(DEPTH, 1, 2 * D_FF)

    ctx_row = lambda tok: 0
    dec_row = lambda tok: tok // t_dec + 1

    def layer(x, l, n_b, seq_len, row_fn, s_hgrn, s_ret, prev_h, prev_r, latent):
        proj, la, key = _proj_call(x, mods, norm1_3, w_in, lower_bounds, l, row_fn)
        proj3 = proj.reshape(n_b, seq_len, PROJ_WIDTH)
        o_a, st_h = _hgrn_call(proj3, la.reshape(n_b, seq_len, -1), key.reshape(n_b, seq_len, -1),
                               s_hgrn, prev_h, l)
        o_b, st_r = _ret_call(proj3, s_ret, prev_r, l, latent)
        x = _post_call(o_a.reshape(-1, H_A * DV_A), o_b.reshape(-1, H_B * DV_B), proj, x,
                       mods, p_a_b, p_b_b, w_out_b, l, row_fn)
        x = _ffn_call(x, mods, norm2_3, w_up_b, w_conv, b_conv_3, w_down_b,
                      final_norm if l == DEPTH - 1 else None, l, row_fn, seq_len)
        return x, st_h, st_r

    x = x_prompt.reshape(n_ctx * t_ctx, D_MODEL)
    st_h = st_r = None
    for l in range(DEPTH):
        x, st_h, st_r = layer(x, l, n_ctx, t_ctx, ctx_row, None, None, st_h, st_r, False)
    y_prompt = x.reshape(n_ctx, t_ctx, D_MODEL)

    x = x_sample.reshape(n_dec * t_dec, D_MODEL)
    for l in range(DEPTH):
        x, _, _ = layer(x, l, n_dec, t_dec, dec_row, state_hgrn, state_ret, None, None, True)
    y_sample = x.reshape(n_dec, t_dec, D_MODEL)
    return (y_prompt, y_sample, st_h, st_r)
```

```python
import functools

import numpy as np
import jax
import jax.numpy as jnp
from jax import lax
from jax.experimental import pallas as pl
from jax.experimental.pallas import tpu as pltpu

F32 = jnp.float32
BF16 = jnp.bfloat16

D_MODEL = 1024
DEPTH = 2
GRID_W = 64
H_A, DK_A, DV_A = 8, 128, 128
H_B, DK_B, DV_B = 4, 256, 512
D_FF = 2816
ROPE_BASE = 10000.0
K_SCALE = DK_B ** -0.5
EPS = 1e-6
RET_DECAY_OFFSET_BWD = 0.5

W_OFF_ZF, W_OFF_ZB, W_OFF_IA = 1024, 2048, 3072
OFF_QA, OFF_IA, OFF_GA = 0, 1024, 2048
OFF_QB, OFF_KB, OFF_VB, OFF_GB = 3072, 4096, 5120, 7168
OFF_GATE_A, OFF_GATE_B = 9216, 10240
PROJ_WIDTH = 11264

MOD_ROWS = 8
MOD_TN = 1024
TOKEN_TILE = 1024
POST_TILE = 512
POST_SUB = 256
FF_CHUNK = 1408
FF_SUB = 256
HGRN_CHUNK = 128
HGRN_UNITS = 32
LEAF_LEVEL = 5
LEAF_MAX_DECAY = 80.0
RET_ROWS = 256
RET_TOKENS = 1024
PROJ_TN = 1024
Z_TILE0 = W_OFF_ZF // PROJ_TN
Z_TILES = (W_OFF_IA - W_OFF_ZF) // PROJ_TN
VMEM_LIMIT = 52 * 1024 * 1024
SUBLANES = 8
LOG2E = 1.4426950408889634

_NT = (((1,), (1,)), ((), ()))
_TN = (((0,), (0,)), ((), ()))


def _params(sem):
    return pltpu.CompilerParams(dimension_semantics=sem, vmem_limit_bytes=VMEM_LIMIT)


def _sigmoid(x):
    return jax.nn.sigmoid(x)


def _silu(x):
    return x * jax.nn.sigmoid(x)


def _mod_kernel(c_ref, w_ref, b_ref, o_ref):
    s = _silu(c_ref[...])
    o_ref[...] = jnp.dot(s, w_ref[...], precision=lax.Precision.HIGHEST,
                         preferred_element_type=F32) + b_ref[...]


def _mod_call(cvec, w_mod, b_mod):
    n_col = 6 * D_MODEL // MOD_TN
    return pl.pallas_call(
        _mod_kernel,
        grid=(DEPTH, n_col),
        in_specs=[
            pl.BlockSpec((MOD_ROWS, D_MODEL), lambda l, j: (0, 0)),
            pl.BlockSpec((None, D_MODEL, MOD_TN), lambda l, j: (l, 0, j)),
            pl.BlockSpec((None, 1, MOD_TN), lambda l, j: (l, 0, j)),
        ],
        out_specs=pl.BlockSpec((None, MOD_ROWS, MOD_TN), lambda l, j: (l, 0, j)),
        out_shape=jax.ShapeDtypeStruct((DEPTH, MOD_ROWS, 6 * D_MODEL), F32),
        compiler_params=_params(("arbitrary", "arbitrary")),
        name="modulation",
    )(cvec, w_mod, b_mod.reshape(DEPTH, 1, 6 * D_MODEL))


def _mod_spec(layer, row_fn, which, tile=TOKEN_TILE):
    return pl.BlockSpec((None, None, 1, D_MODEL),
                        lambda i, j: (layer, row_fn(i * tile), 0, which))


def _layer_spec(layer, block, index_map):
    return pl.BlockSpec((None,) + block, lambda *g: (layer,) + index_map(*g))


def _norm_mod(x, g, sc, sh):
    y = x * lax.rsqrt(jnp.mean(x * x, axis=-1, keepdims=True) + EPS) * g
    return y * (1.0 + sc) + sh


def _proj_kernel(x_ref, sh_ref, sc_ref, n_ref, w_ref, o_ref, hout_ref, h_ref, wb_ref):
    j, i = pl.program_id(0), pl.program_id(1)

    @pl.when(i == 0)
    def _():
        wb_ref[...] = w_ref[...].astype(BF16)

    @pl.when(j == 0)
    def _():
        h = _norm_mod(x_ref[...], n_ref[...], sc_ref[...], sh_ref[...]).astype(BF16)
        h_ref[i] = h
        hout_ref[...] = h

    o_ref[...] = jnp.dot(h_ref[i], wb_ref[...], preferred_element_type=F32).astype(BF16)


def _zproj_kernel(h_ref, w_ref, lb_ref, la_ref, key_ref, wb_ref):
    @pl.when(pl.program_id(1) == 0)
    def _():
        wb_ref[...] = w_ref[...].astype(BF16)

    z = jnp.dot(h_ref[...], wb_ref[...], preferred_element_type=F32)
    la, key = _hgrn_forget(z, lb_ref[...])
    la_ref[...] = la
    key_ref[...] = key.astype(BF16)


def _proj_call(x, mods, n1, w_in, lb, layer, row_fn):
    n_tok = x.shape[0]
    tn = PROJ_TN
    n_tile = n_tok // TOKEN_TILE
    hold = lambda j, i: jnp.where(j == 0, i, n_tile - 1)
    w_tile = lambda j: jnp.where(j >= Z_TILE0, j + Z_TILES, j)
    mod = lambda which: pl.BlockSpec(
        (None, None, 1, D_MODEL), lambda j, i: (layer, row_fn(i * TOKEN_TILE), 0, which))
    proj, h = pl.pallas_call(
        _proj_kernel,
        grid=(PROJ_WIDTH // tn, n_tile),
        in_specs=[
            pl.BlockSpec((TOKEN_TILE, D_MODEL), lambda j, i: (hold(j, i), 0)),
            mod(0), mod(1),
            _layer_spec(layer, (1, D_MODEL), lambda j, i: (0, 0)),
            _layer_spec(layer, (D_MODEL, tn), lambda j, i: (0, w_tile(j))),
        ],
        out_specs=[
            pl.BlockSpec((TOKEN_TILE, tn), lambda j, i: (i, j)),
            pl.BlockSpec((TOKEN_TILE, D_MODEL), lambda j, i: (hold(j, i), 0)),
        ],
        out_shape=[jax.ShapeDtypeStruct((n_tok, PROJ_WIDTH), BF16),
                   jax.ShapeDtypeStruct((n_tok, D_MODEL), BF16)],
        scratch_shapes=[pltpu.VMEM((n_tile, TOKEN_TILE, D_MODEL), BF16),
                        pltpu.VMEM((D_MODEL, tn), BF16)],
        compiler_params=_params(("arbitrary", "arbitrary")),
        name="in_projection",
    )(x, mods, mods, n1, w_in)
    la, key = pl.pallas_call(
        _zproj_kernel,
        grid=(Z_TILES, n_tile),
        in_specs=[
            pl.BlockSpec((TOKEN_TILE, D_MODEL), lambda j, i: (i, 0)),
            _layer_spec(layer, (D_MODEL, tn), lambda j, i: (0, Z_TILE0 + j)),
            pl.BlockSpec((None, None, 1, tn), lambda j, i: (layer, j, 0, 0)),
        ],
        out_specs=[pl.BlockSpec((TOKEN_TILE, tn), lambda j, i: (i, j))] * 2,
        out_shape=[jax.ShapeDtypeStruct((n_tok, Z_TILES * tn), F32),
                   jax.ShapeDtypeStruct((n_tok, Z_TILES * tn), BF16)],
        scratch_shapes=[pltpu.VMEM((D_MODEL, tn), BF16)],
        compiler_params=_params(("arbitrary", "arbitrary")),
        name="forget_projection",
    )(h, w_in, lb.reshape(DEPTH, Z_TILES, 1, tn))
    return proj, la, key


def _hgrn_consts(c):
    nl = int(np.log2(c))
    t = np.arange(c)
    tt, rr = t[:, None], t[None, :]
    x = tt ^ rr
    lev_of = np.where(x > 0, np.floor(np.log2(np.maximum(x, 1))) + 1, 0).astype(np.int32)
    lv_f = np.where(tt >= rr, lev_of, -1).astype(np.int32)
    lv_b = lv_f.T.copy()
    return ((rr <= tt).astype(np.float32), (rr >= tt).astype(np.float32), lv_f, lv_b, nl)


def _leaf_factors(cum, cum_row, reverse):
    c, dk = cum.shape
    leaf = 1 << LEAF_LEVEL
    pieces = []
    for base in range(0, c, leaf):
        row = base + leaf if reverse else base - 1
        if 0 <= row < c:
            pieces.append(jnp.broadcast_to(cum_row(row), (leaf, dk)))
        else:
            pieces.append(jnp.zeros((leaf, dk), F32))
    e = jnp.exp2((cum - jnp.concatenate(pieces, axis=0)) * LOG2E)
    return e.astype(BF16), (1.0 / e).astype(BF16)


def _level_factor(la, cum, cum_row, lev, reverse):
    c, dk = la.shape
    if lev == 1:
        row = lax.broadcasted_iota(jnp.int32, (c, 1), 0)
        return jnp.exp(jnp.where((row & 1) == (0 if reverse else 1), la, 0.0)).astype(BF16)
    blk, half = 1 << lev, 1 << (lev - 1)

    def boundary(r):
        base = (r // blk) * blk
        return jnp.broadcast_to(cum_row(base + half if reverse else base + half - 1),
                                (SUBLANES, dk))

    if half >= SUBLANES:
        pieces = []
        for r0 in range(0, c, SUBLANES):
            below = (r0 % blk >= half) != reverse
            pieces.append((cum[r0:r0 + SUBLANES] - boundary(r0)) * (LOG2E if below else -LOG2E))
        return jnp.exp2(jnp.concatenate(pieces, axis=0)).astype(BF16)
    top = lax.broadcasted_iota(jnp.int32, (SUBLANES, dk), 0) < SUBLANES // 2
    pieces = []
    for r0 in range(0, c, SUBLANES):
        if blk >= SUBLANES:
            pieces.append(boundary(r0))
        else:
            pieces.append(jnp.where(top, boundary(r0), boundary(r0 + SUBLANES // 2)))
    return jnp.exp2(jnp.abs(cum - jnp.concatenate(pieces, axis=0)) * (-LOG2E)).astype(BF16)


def _hgrn_forget(z, lb):
    e = jnp.exp(-jnp.abs(z))
    one_e = 1.0 + e
    r = 1.0 / one_e
    pos = z >= 0.0
    sig = jnp.where(pos, 1.0, e) * r
    sig_neg = jnp.where(pos, e, 1.0) * r
    oml = 1.0 - lb
    log_sig = jnp.minimum(z, 0.0) - jnp.log(one_e)
    log_f = jnp.where(lb > 0.0, jnp.log(lb + oml * sig), log_sig)
    return log_f, oml * sig_neg


def _hgrn_kernel(*refs, seq_len, chunk, n_lev, heads, has_s0, n_prev):
    it = iter(refs)
    (q_ref, laf_ref, lab_ref, kf_ref, kb_ref, i_ref, g_ref, trif_ref, trib_ref, lvf_ref, lvb_ref,
     leaf_ref) = (next(it) for _ in range(12))
    s0_ref = next(it) if has_s0 else None
    prev_ref = next(it) if n_prev else None
    o_ref = next(it)
    st_ref = None if has_s0 else next(it)
    cum_ref, qt_ref, kt_ref, qd_ref, kd_ref, ae_ref, s_ref, acc_ref = (next(it) for _ in range(8))
    n_chunks = seq_len // chunk
    dk = DK_A
    units = [(2 * hh + d, hh, d, d == 1, la_in, tri_ref, lv_ref)
             for hh in range(heads)
             for d, (la_in, tri_ref, lv_ref) in enumerate(((laf_ref, trif_ref, lvf_ref),
                                                           (lab_ref, trib_ref, lvb_ref)))]
    key_in = (kf_ref, kb_ref)
    rows_of = lambda c: slice(c * chunk, (c + 1) * chunk)
    cols_of = lambda hh: slice(hh * dk, (hh + 1) * dk)

    worst = None
    for _, hh, _, _, la_in, _, _ in units:
        leaf_sum = jnp.dot(leaf_ref[...], la_in[:, cols_of(hh)].astype(BF16),
                           preferred_element_type=F32)
        worst = leaf_sum if worst is None else jnp.minimum(worst, leaf_sum)
    leafwise_ok = jnp.min(worst) > -LEAF_MAX_DECAY

    for u, hh, _, _, la_in, tri_ref, _ in units:
        tri = tri_ref[...]
        for c in range(n_chunks):
            la = la_in[rows_of(c), cols_of(hh)]
            la_hi = la.astype(BF16)
            la_lo = (la - la_hi.astype(F32)).astype(BF16)
            x2 = jnp.dot(tri, jnp.concatenate([la_hi, la_lo], axis=1),
                         preferred_element_type=F32)
            cum_ref[u, rows_of(c), :] = x2[:, :dk] + x2[:, dk:]

    def operands_and_scores(leafwise):
        if leafwise:
            levels = list(range(LEAF_LEVEL + 1, n_lev + 1))
        else:
            levels = list(range(1, n_lev + 1))
        n_terms = len(levels) + 1
        for u, hh, d, reverse, la_in, _, _ in units:
            for c in range(n_chunks):
                rows = rows_of(c)
                cum = cum_ref[u, rows, :]
                la = la_in[rows, cols_of(hh)]
                q = q_ref[rows, cols_of(hh)]
                k = key_in[d][rows, cols_of(hh)]
                cum_row = lambda r, u=u, c=c: cum_ref[u, c * chunk + r:c * chunk + r + 1, :]
                if leafwise:
                    factors = [_leaf_factors(cum, cum_row, reverse)]
                else:
                    factors = [None]
                for lev in levels:
                    e = _level_factor(la, cum, cum_row, lev, reverse)
                    factors.append((e, e))
                for t, f in enumerate(factors):
                    cols = slice(t * dk, (t + 1) * dk)
                    qt_ref[u, rows, cols] = q if f is None else q * f[0]
                    kt_ref[u, rows, cols] = k if f is None else k * f[1]
                end_row = 0 if reverse else chunk - 1
                cum_end = cum[end_row:end_row + 1]
                qd_ref[u, rows, :] = q * jnp.exp(cum).astype(BF16)
                kd_ref[u, rows, :] = k * jnp.exp(cum_end - cum).astype(BF16)
                ae_ref[u, c:c + 1, :] = jnp.exp(cum_end)
        zero = jnp.zeros((chunk, dk), BF16)
        for u, _, _, _, _, _, lv_ref in units:
            lv = lv_ref[...]
            if leafwise:
                masks = [(lv >= 0) & (lv <= LEAF_LEVEL)]
            else:
                masks = [lv == 0]
            masks += [lv == lev for lev in levels]
            for c in range(n_chunks):
                rows = rows_of(c)
                scores = jnp.zeros((chunk, chunk), F32)
                for t in range(0, n_terms, 2):
                    if t + 1 < n_terms:
                        k_a = kt_ref[u, rows, t * dk:(t + 1) * dk]
                        k_b = kt_ref[u, rows, (t + 1) * dk:(t + 2) * dk]
                        p = lax.dot_general(
                            qt_ref[u, rows, t * dk:(t + 2) * dk],
                            jnp.concatenate([jnp.concatenate([k_a, zero], axis=1),
                                             jnp.concatenate([zero, k_b], axis=1)], axis=0),
                            _NT, preferred_element_type=F32)
                        scores = jnp.where(masks[t], p[:, :chunk], scores)
                        scores = jnp.where(masks[t + 1], p[:, chunk:], scores)
                    else:
                        p = lax.dot_general(qt_ref[u, rows, t * dk:(t + 1) * dk],
                                            kt_ref[u, rows, t * dk:(t + 1) * dk],
                                            _NT, preferred_element_type=F32)
                        scores = jnp.where(masks[t], p, scores)
                s_ref[u, rows, :] = scores.astype(BF16)

    pl.when(leafwise_ok)(lambda: operands_and_scores(True))
    pl.when(jnp.logical_not(leafwise_ok))(lambda: operands_and_scores(False))

    incs = {u: [lax.dot_general(i_ref[rows_of(c), cols_of(hh)], kd_ref[u, rows_of(c), :], _TN,
                                preferred_element_type=F32) for c in range(n_chunks)]
            for u, hh, _, _, _, _, _ in units}
    finals, befores = {}, {}
    for u, hh, d, reverse, _, _, _ in units:
        st = s0_ref[d, hh].T if has_s0 else jnp.zeros((DV_A, dk), F32)
        befores[u] = [None] * n_chunks
        for c in (range(n_chunks - 1, -1, -1) if reverse else range(n_chunks)):
            befores[u][c] = st.astype(BF16)
            st = st * ae_ref[u, c:c + 1, :] + incs[u][c]
        finals[u] = st
    for u, hh, d, reverse, _, _, _ in units:
        for c in range(n_chunks):
            rows = rows_of(c)
            o = jnp.dot(s_ref[u, rows, :], i_ref[rows, cols_of(hh)], preferred_element_type=F32)
            if has_s0 or c != (n_chunks - 1 if reverse else 0):
                o = o + lax.dot_general(qd_ref[u, rows, :], befores[u][c], _NT,
                                        preferred_element_type=F32)
            if d == 0:
                acc_ref[rows, cols_of(hh)] = o
            else:
                acc_ref[rows, cols_of(hh)] += o

    for hh in range(heads):
        o = acc_ref[:, cols_of(hh)]
        o = o * lax.rsqrt(jnp.mean(o * o, axis=-1, keepdims=True) + EPS)
        o_ref[:, cols_of(hh)] = (o * _silu(g_ref[:, cols_of(hh)].astype(F32))).astype(BF16)
    if st_ref is not None:
        if n_prev:
            st_ref[0:n_prev] = prev_ref[...]
        for u, hh, d, _, _, _, _ in units:
            st_ref[n_prev, d, hh] = finals[u].T


def _hgrn_call(proj, la, key, state, prev, layer):
    n_b, seq_len, _ = proj.shape
    tri_f, tri_b, lv_f, lv_b, n_lev = _hgrn_consts(HGRN_CHUNK)
    has_s0 = state is not None
    n_chunks = seq_len // HGRN_CHUNK
    heads = max(1, HGRN_UNITS // (2 * n_chunks))
    width = heads * DK_A
    col = lambda off: (lambda b, h: (b, 0, off // width + h))
    const = lambda a: pl.BlockSpec(a.shape, lambda b, h: (0, 0))
    bwd = W_OFF_ZB - W_OFF_ZF
    in_specs = [pl.BlockSpec((None, seq_len, width), col(o))
                for o in (OFF_QA, 0, bwd, 0, bwd, OFF_IA, OFF_GA)]
    leaf = 1 << LEAF_LEVEL
    leaf_rows = -(-(seq_len // leaf) // SUBLANES) * SUBLANES
    leaf_ind = (np.arange(seq_len)[None, :] // leaf == np.arange(leaf_rows)[:, None])
    in_specs += [const(tri_f), const(tri_b), const(lv_f), const(lv_b), const(leaf_ind)]
    args = [proj, la, la, key, key, proj, proj, jnp.asarray(tri_f, BF16), jnp.asarray(tri_b, BF16),
            jnp.asarray(lv_f), jnp.asarray(lv_b), jnp.asarray(leaf_ind, BF16)]
    out_specs = [pl.BlockSpec((None, seq_len, width), lambda b, h: (b, 0, h))]
    out_shape = [jax.ShapeDtypeStruct((n_b, seq_len, H_A * DV_A), BF16)]
    if has_s0:
        in_specs.append(pl.BlockSpec((None, None, 2, heads, DK_A, DV_A),
                                     lambda b, h: (b, layer, 0, h, 0, 0)))
        args.append(state)
    else:
        st_spec = lambda n: pl.BlockSpec((None, n, 2, heads, DK_A, DV_A),
                                         lambda b, h: (b, 0, 0, h, 0, 0))
        if layer:
            in_specs.append(st_spec(layer))
            args.append(prev)
        out_specs.append(st_spec(layer + 1))
        out_shape.append(jax.ShapeDtypeStruct((n_b, layer + 1, 2, H_A, DK_A, DV_A), F32))
    n_u = 2 * heads
    res = pl.pallas_call(
        functools.partial(_hgrn_kernel, seq_len=seq_len, chunk=HGRN_CHUNK, n_lev=n_lev,
                          heads=heads, has_s0=has_s0, n_prev=0 if has_s0 else layer),
        grid=(n_b, H_A // heads),
        in_specs=in_specs, out_specs=out_specs, out_shape=out_shape,
        scratch_shapes=[
            pltpu.VMEM((n_u, seq_len, DK_A), F32),
            pltpu.VMEM((n_u, seq_len, (n_lev + 1) * DK_A), BF16),
            pltpu.VMEM((n_u, seq_len, (n_lev + 1) * DK_A), BF16),
            pltpu.VMEM((n_u, seq_len, DK_A), BF16),
            pltpu.VMEM((n_u, seq_len, DK_A), BF16),
            pltpu.VMEM((n_u, max(SUBLANES, n_chunks), DK_A), F32),
            pltpu.VMEM((n_u, seq_len, HGRN_CHUNK), BF16),
            pltpu.VMEM((seq_len, width), F32),
        ],
        compiler_params=_params(("arbitrary", "arbitrary")),
        name="hgrn2_scan",
    )(*args)
    return (res[0], None) if has_s0 else (res[0], res[1])


def _rope_tables(seq_len):
    rows = seq_len // GRID_W
    r_idx = jnp.repeat(jnp.arange(rows), GRID_W).astype(F32)
    c_idx = jnp.tile(jnp.arange(GRID_W), rows).astype(F32)
    quarter = DK_B // 4
    inv = 1.0 / (ROPE_BASE ** (jnp.arange(quarter, dtype=F32) / quarter))
    ang_r = r_idx[:, None] * inv[None, :]
    ang_c = c_idx[:, None] * inv[None, :]
    cos = jnp.concatenate([jnp.cos(ang_r)] * 2 + [jnp.cos(ang_c)] * 2, axis=1)
    sin = jnp.concatenate([-jnp.sin(ang_r), jnp.sin(ang_r), -jnp.sin(ang_c), jnp.sin(ang_c)],
                          axis=1)
    return cos, sin


def _rope(x, cos, sin):
    half = DK_B // 2
    swapped = jnp.concatenate([pltpu.roll(x[:, :half], half // 2, axis=1),
                               pltpu.roll(x[:, half:], half // 2, axis=1)], axis=1)
    return x * cos + swapped * sin


def _ret_kernel(*refs, seq_len, n_seq, latent, has_s0, n_prev):
    it = iter(refs)
    lg_ref, dm_ref, q_ref, k_ref, v_ref, g_ref = (next(it) for _ in range(6))
    cos_ref, sin_ref = (next(it), next(it)) if latent else (None, None)
    s0_ref = next(it) if has_s0 else None
    prev_ref = next(it) if n_prev else None
    o_ref = next(it)
    st_ref = None if has_s0 else next(it)
    h = pl.program_id(0)
    lg_f = lg_ref[h, 0]
    lg_b = lg_ref[h, 1]

    qs, ks = [], []
    for s in range(n_seq):
        q, k = q_ref[s], k_ref[s]
        if latent:
            cos, sin = cos_ref[...], sin_ref[...]
            q, k = _rope(q.astype(F32), cos, sin), _rope(k.astype(F32), cos, sin)
        qs.append(q)
        ks.append(k)
    items = [(s, slice(r, r + RET_ROWS)) for s in range(n_seq)
             for r in range(0, seq_len, RET_ROWS)]
    ps = [lax.dot_general(qs[s][rows].astype(BF16), ks[s].astype(BF16), _NT,
                          preferred_element_type=F32) for s, rows in items]
    carried = []
    if has_s0:
        for s, rows in items:
            s0 = jnp.concatenate([s0_ref[s, 0], s0_ref[s, 1]], axis=0).astype(BF16)
            pos = (lax.broadcasted_iota(jnp.int32, (RET_ROWS, 1), 0) + rows.start).astype(F32)
            q_s = jnp.concatenate([qs[s][rows] * jnp.exp(lg_f * (pos + 1.0)),
                                   qs[s][rows] * jnp.exp(lg_b * (seq_len - pos))], axis=1)
            carried.append(jnp.dot(q_s.astype(BF16), s0, preferred_element_type=F32))
    pds = [p.astype(BF16) * dm_ref[rows, :] for p, (_, rows) in zip(ps, items)]
    outs = [jnp.dot(pd, v_ref[s], preferred_element_type=F32) for pd, (s, _) in zip(pds, items)]
    for i, (s, rows) in enumerate(items):
        o = outs[i] + carried[i] if has_s0 else outs[i]
        o = o * lax.rsqrt(jnp.mean(o * o, axis=-1, keepdims=True) + EPS)
        o_ref[s, rows, :] = (o * _silu(g_ref[s, rows, :].astype(F32))).astype(BF16)
    if st_ref is not None:
        if n_prev:
            st_ref[:, 0:n_prev] = prev_ref[...]
        pos = lax.broadcasted_iota(jnp.int32, (seq_len, 1), 0).astype(F32)
        w_f = K_SCALE * jnp.exp(lg_f * (seq_len - 1.0 - pos))
        w_b = K_SCALE * jnp.exp(lg_b * pos)
        for s in range(n_seq):
            st_ref[s, n_prev, 0] = lax.dot_general((ks[s] * w_f).astype(BF16), v_ref[s], _TN,
                                                   preferred_element_type=F32)
            st_ref[s, n_prev, 1] = lax.dot_general((ks[s] * w_b).astype(BF16), v_ref[s], _TN,
                                                   preferred_element_type=F32)


def _ret_log_decay():
    heads = jnp.arange(H_B, dtype=F32)
    lg_f = jnp.log1p(-jnp.exp2(-5.0 - heads))
    lg_b = jnp.log1p(-jnp.exp2(-(5.0 + RET_DECAY_OFFSET_BWD) - heads))
    return jnp.stack([lg_f, lg_b], axis=1)


def _ret_decay_mask(seq_len):
    lg = _ret_log_decay()
    d = (jnp.arange(seq_len)[:, None] - jnp.arange(seq_len)[None, :]).astype(F32)[None]
    lg_f, lg_b = lg[:, 0, None, None], lg[:, 1, None, None]
    dm = (jnp.where(d >= 0.0, jnp.exp(lg_f * jnp.maximum(d, 0.0)), 0.0)
          + jnp.where(d <= 0.0, jnp.exp(lg_b * jnp.maximum(-d, 0.0)), 0.0))
    return (K_SCALE * dm).astype(BF16)


def _ret_call(proj, state, prev, layer, latent):
    n_b, seq_len, _ = proj.shape
    has_s0 = state is not None
    n_seq = max(1, RET_TOKENS // seq_len)
    in_specs = [
        pl.BlockSpec(memory_space=pltpu.SMEM),
        pl.BlockSpec((None, seq_len, seq_len), lambda h, b: (h, 0, 0)),
        pl.BlockSpec((n_seq, seq_len, DK_B), lambda h, b: (b, 0, OFF_QB // DK_B + h)),
        pl.BlockSpec((n_seq, seq_len, DK_B), lambda h, b: (b, 0, OFF_KB // DK_B + h)),
        pl.BlockSpec((n_seq, seq_len, DV_B), lambda h, b: (b, 0, OFF_VB // DV_B + h)),
        pl.BlockSpec((n_seq, seq_len, DV_B), lambda h, b: (b, 0, OFF_GB // DV_B + h)),
    ]
    args = [_ret_log_decay(), _ret_decay_mask(seq_len), proj, proj, proj, proj]
    if latent:
        cos, sin = _rope_tables(seq_len)
        in_specs += [pl.BlockSpec((seq_len, DK_B), lambda h, b: (0, 0))] * 2
        args += [cos, sin]
    out_specs = [pl.BlockSpec((n_seq, seq_len, DV_B), lambda h, b: (b, 0, h))]
    out_shape = [jax.ShapeDtypeStruct((n_b, seq_len, H_B * DV_B), BF16)]
    if has_s0:
        in_specs.append(pl.BlockSpec((n_seq, None, 2, None, DK_B, DV_B),
                                     lambda h, b: (b, layer, 0, h, 0, 0)))
        args.append(state)
    else:
        st_spec = lambda n: pl.BlockSpec((n_seq, n, 2, None, DK_B, DV_B),
                                         lambda h, b: (b, 0, 0, h, 0, 0))
        if layer:
            in_specs.append(st_spec(layer))
            args.append(prev)
        out_specs.append(st_spec(layer + 1))
        out_shape.append(jax.ShapeDtypeStruct((n_b, layer + 1, 2, H_B, DK_B, DV_B), F32))
    res = pl.pallas_call(
        functools.partial(_ret_kernel, seq_len=seq_len, n_seq=n_seq, latent=latent,
                          has_s0=has_s0, n_prev=0 if has_s0 else layer),
        grid=(H_B, n_b // n_seq),
        in_specs=in_specs, out_specs=out_specs, out_shape=out_shape,
        compiler_params=_params(("arbitrary", "arbitrary")),
        name="retention_scan",
    )(*args)
    return (res[0], None) if has_s0 else (res[0], res[1])


def _post_kernel(oa_ref, ob_ref, gta_ref, gtb_ref, x_ref, gt1_ref, pa_ref, pb_ref, wo_ref,
                 o_ref):
    subs = [slice(r, r + POST_SUB) for r in range(0, POST_TILE, POST_SUB)]
    y_a = [jnp.dot(oa_ref[s, :], pa_ref[...], preferred_element_type=F32) for s in subs]
    y_b = [jnp.dot(ob_ref[s, :], pb_ref[...], preferred_element_type=F32) for s in subs]
    merged = [(_sigmoid(gta_ref[s, :].astype(F32)) * y_a[i]
               + _sigmoid(gtb_ref[s, :].astype(F32)) * y_b[i]).astype(BF16)
              for i, s in enumerate(subs)]
    y = [jnp.dot(m, wo_ref[...], preferred_element_type=F32) for m in merged]
    for i, s in enumerate(subs):
        o_ref[s, :] = x_ref[s, :] + gt1_ref[...] * y[i]


def _post_call(o_a, o_b, proj, x, mods, p_a, p_b, w_out, layer, row_fn):
    n_tok = x.shape[0]
    tile = lambda w: (lambda i, j: (i, w))
    pcol = lambda off: pl.BlockSpec((POST_TILE, D_MODEL), tile(off // D_MODEL))
    const = lambda a: _layer_spec(layer, a.shape[1:], lambda i, j: (0, 0))
    return pl.pallas_call(
        _post_kernel,
        grid=(n_tok // POST_TILE, 1),
        in_specs=[
            pl.BlockSpec((POST_TILE, H_A * DV_A), tile(0)),
            pl.BlockSpec((POST_TILE, H_B * DV_B), tile(0)),
            pcol(OFF_GATE_A), pcol(OFF_GATE_B),
            pl.BlockSpec((POST_TILE, D_MODEL), tile(0)),
            _mod_spec(layer, row_fn, 2, POST_TILE),
            const(p_a), const(p_b), const(w_out),
        ],
        out_specs=pl.BlockSpec((POST_TILE, D_MODEL), tile(0)),
        out_shape=jax.ShapeDtypeStruct((n_tok, D_MODEL), F32),
        compiler_params=_params(("arbitrary", "arbitrary")),
        name="mixer_output",
    )(o_a, o_b, proj, proj, x, mods, p_a, p_b, w_out)


def _conv3(u, wc, bc, seq_len):
    n = u.shape[0]
    row = lax.broadcasted_iota(jnp.int32, (SUBLANES, 1), 0)

    def zero_row(x, r0, r):
        return jnp.where(row == r, 0.0, x[r0:r0 + SUBLANES])

    prev, nxt = pltpu.roll(u, 1, axis=0), pltpu.roll(u, n - 1, axis=0)
    p_parts, n_parts = [], []
    for s0 in range(0, n, seq_len):
        s1 = s0 + seq_len
        p_parts += [zero_row(prev, s0, 0), prev[s0 + SUBLANES:s1]]
        n_parts += [nxt[s0:s1 - SUBLANES], zero_row(nxt, s1 - SUBLANES, SUBLANES - 1)]
    prev, nxt = jnp.concatenate(p_parts, axis=0), jnp.concatenate(n_parts, axis=0)
    return prev * wc[0:1] + u * wc[1:2] + nxt * wc[2:3] + bc


def _ffn_kernel(*refs, seq_len, final):
    it = iter(refs)
    (x_ref, sh_ref, sc_ref, gt_ref, n2_ref, wa_ref, wg_ref, wca_ref, wcg_ref, bca_ref, bcg_ref,
     wd_ref) = (next(it) for _ in range(12))
    fn_ref = next(it) if final else None
    o_ref, h_ref, acc_ref = next(it), next(it), next(it)
    j = pl.program_id(1)

    @pl.when(j == 0)
    def _():
        h_ref[...] = _norm_mod(x_ref[...], n2_ref[...], sc_ref[...], sh_ref[...]).astype(BF16)
        acc_ref[...] = jnp.zeros_like(acc_ref)

    h = h_ref[...]
    subs = [slice(c, min(c + FF_SUB, FF_CHUNK)) for c in range(0, FF_CHUNK, FF_SUB)]
    u_a = [jnp.dot(h, wa_ref[:, s], preferred_element_type=F32) for s in subs]
    u_g = [jnp.dot(h, wg_ref[:, s], preferred_element_type=F32) for s in subs]
    act = [(_silu(_conv3(u_g[i], wcg_ref[:, s], bcg_ref[:, s], seq_len))
            * _conv3(u_a[i], wca_ref[:, s], bca_ref[:, s], seq_len)).astype(BF16)
           for i, s in enumerate(subs)]
    acc_ref[...] += jnp.dot(jnp.concatenate(act, axis=1), wd_ref[...],
                            preferred_element_type=F32)

    @pl.when(j == pl.num_programs(1) - 1)
    def _():
        x = x_ref[...] + gt_ref[...] * acc_ref[...]
        if final:
            x = x * lax.rsqrt(jnp.mean(x * x, axis=-1, keepdims=True) + EPS) * fn_ref[...]
        o_ref[...] = x


def _ffn_call(x, mods, n2, w_up, w_conv, b_conv, w_down, final_norm, layer, row_fn, seq_len):
    n_tok = x.shape[0]
    n_ff = D_FF // FF_CHUNK
    lspec = functools.partial(_layer_spec, layer)
    in_specs = [
        pl.BlockSpec((TOKEN_TILE, D_MODEL), lambda i, j: (i, 0)),
        _mod_spec(layer, row_fn, 3), _mod_spec(layer, row_fn, 4), _mod_spec(layer, row_fn, 5),
        lspec((1, D_MODEL), lambda i, j: (0, 0)),
        lspec((D_MODEL, FF_CHUNK), lambda i, j: (0, j)),
        lspec((D_MODEL, FF_CHUNK), lambda i, j: (0, n_ff + j)),
        lspec((3, FF_CHUNK), lambda i, j: (0, j)),
        lspec((3, FF_CHUNK), lambda i, j: (0, n_ff + j)),
        lspec((1, FF_CHUNK), lambda i, j: (0, j)),
        lspec((1, FF_CHUNK), lambda i, j: (0, n_ff + j)),
        lspec((FF_CHUNK, D_MODEL), lambda i, j: (j, 0)),
    ]
    args = [x, mods, mods, mods, n2, w_up, w_up, w_conv, w_conv, b_conv, b_conv, w_down]
    final = final_norm is not None
    if final:
        in_specs.append(pl.BlockSpec((1, D_MODEL), lambda i, j: (0, 0)))
        args.append(final_norm.reshape(1, D_MODEL))
    return pl.pallas_call(
        functools.partial(_ffn_kernel, seq_len=seq_len, final=final),
        grid=(n_tok // TOKEN_TILE, n_ff),
        in_specs=in_specs,
        out_specs=pl.BlockSpec((TOKEN_TILE, D_MODEL), lambda i, j: (i, 0)),
        out_shape=jax.ShapeDtypeStruct((n_tok, D_MODEL), F32),
        scratch_shapes=[pltpu.VMEM((TOKEN_TILE, D_MODEL), BF16),
                        pltpu.VMEM((TOKEN_TILE, D_MODEL), F32)],
        compiler_params=_params(("arbitrary", "arbitrary")),
        name="conv_ffn",
    )(*args)


def kernel(x_prompt, x_sample, state_hgrn, state_ret, c, c_ctx, norm1, norm2, final_norm,
           w_mod, b_mod, w_in, hgrn_lb_raw, p_a, p_b, w_out, w_up, w_conv, b_conv, w_down):
    n_ctx, t_ctx, _ = x_prompt.shape
    n_dec, t_dec, _ = x_sample.shape
    assert t_ctx & (t_ctx - 1) == 0 and TOKEN_TILE % t_ctx == 0 and t_dec == TOKEN_TILE

    sm = jax.nn.softmax(hgrn_lb_raw.astype(F32), axis=0)
    cum = jnp.cumsum(sm, axis=0)
    lower_bounds = cum - cum[0:1]

    cvec = jnp.concatenate(
        [c_ctx[None, :], c, jnp.zeros((MOD_ROWS - 1 - n_dec, D_MODEL), F32)], axis=0)
    mods = _mod_call(cvec, w_mod, b_mod).reshape(DEPTH, MOD_ROWS, 1, 6 * D_MODEL)

    p_a_b, p_b_b, w_out_b, w_up_b, w_down_b = (
        w.astype(BF16) for w in (p_a, p_b, w_out, w_up, w_down))
    norm1_3, norm2_3 = norm1.reshape(DEPTH, 1, D_MODEL), norm2.reshape(DEPTH, 1, D_MODEL)
    b_conv_3 = b_conv.reshape(DEPTH, 1, 2 * D_FF)

    ctx_row = lambda tok: 0
    dec_row = lambda tok: tok // t_dec + 1

    def layer(x, l, n_b, seq_len, row_fn, s_hgrn, s_ret, prev_h, prev_r, latent):
        proj, la, key = _proj_call(x, mods, norm1_3, w_in, lower_bounds, l, row_fn)
        proj3 = proj.reshape(n_b, seq_len, PROJ_WIDTH)
        o_a, st_h = _hgrn_call(proj3, la.reshape(n_b, seq_len, -1), key.reshape(n_b, seq_len, -1),
                               s_hgrn, prev_h, l)
        o_b, st_r = _ret_call(proj3, s_ret, prev_r, l, latent)
        x = _post_call(o_a.reshape(-1, H_A * DV_A), o_b.reshape(-1, H_B * DV_B), proj, x,
                       mods, p_a_b, p_b_b, w_out_b, l, row_fn)
        x = _ffn_call(x, mods, norm2_3, w_up_b, w_conv, b_conv_3, w_down_b,
                      final_norm if l == DEPTH - 1 else None, l, row_fn, seq_len)
        return x, st_h, st_r

    x = x_prompt.reshape(n_ctx * t_ctx, D_MODEL)
    st_h = st_r = None
    for l in range(DEPTH):
        x, st_h, st_r = layer(x, l, n_ctx, t_ctx, ctx_row, None, None, st_h, st_r, False)
    y_prompt = x.reshape(n_ctx, t_ctx, D_MODEL)

    x = x_sample.reshape(n_dec * t_dec, D_MODEL)
    for l in range(DEPTH):
        x, _, _ = layer(x, l, n_dec, t_dec, dec_row, state_hgrn, state_ret, None, None, True)
    y_sample = x.reshape(n_dec, t_dec, D_MODEL)
    return (y_prompt, y_sample, st_h, st_r)
```

```python
import functools

import numpy as np
import jax
import jax.numpy as jnp
from jax import lax
from jax.experimental import pallas as pl
from jax.experimental.pallas import tpu as pltpu

F32 = jnp.float32
BF16 = jnp.bfloat16

D_MODEL = 1024
DEPTH = 2
GRID_W = 64
H_A, DK_A, DV_A = 8, 128, 128
H_B, DK_B, DV_B = 4, 256, 512
D_FF = 2816
ROPE_BASE = 10000.0
K_SCALE = DK_B ** -0.5
EPS = 1e-6
RET_DECAY_OFFSET_BWD = 0.5

W_OFF_ZF, W_OFF_ZB, W_OFF_IA = 1024, 2048, 3072
OFF_QA, OFF_IA, OFF_GA = 0, 1024, 2048
OFF_QB, OFF_KB, OFF_VB, OFF_GB = 3072, 4096, 5120, 7168
OFF_GATE_A, OFF_GATE_B = 9216, 10240
PROJ_WIDTH = 11264

MOD_ROWS = 8
MOD_TN = 1024
TOKEN_TILE = 1024
POST_TILE = 512
POST_SUB = 256
FF_CHUNK = 1408
FF_SUB = 256
HGRN_CHUNK = 128
HGRN_UNITS = 32
LEAF_LEVEL = 5
LEAF_MAX_DECAY = 80.0
RET_ROWS = 256
RET_TOKENS = 1024
PROJ_TN = 1024
Z_TILE0 = W_OFF_ZF // PROJ_TN
Z_TILES = (W_OFF_IA - W_OFF_ZF) // PROJ_TN
VMEM_LIMIT = 52 * 1024 * 1024
SUBLANES = 8
LOG2E = 1.4426950408889634

_NT = (((1,), (1,)), ((), ()))
_TN = (((0,), (0,)), ((), ()))


def _params(sem):
    return pltpu.CompilerParams(dimension_semantics=sem, vmem_limit_bytes=VMEM_LIMIT)


def _sigmoid(x):
    return jax.nn.sigmoid(x)


def _silu(x):
    return x * jax.nn.sigmoid(x)


def _mod_kernel(c_ref, w_ref, b_ref, o_ref):
    s = _silu(c_ref[...])
    o_ref[...] = jnp.dot(s, w_ref[...], precision=lax.Precision.HIGHEST,
                         preferred_element_type=F32) + b_ref[...]


def _mod_call(cvec, w_mod, b_mod):
    n_col = 6 * D_MODEL // MOD_TN
    return pl.pallas_call(
        _mod_kernel,
        grid=(DEPTH, n_col),
        in_specs=[
            pl.BlockSpec((MOD_ROWS, D_MODEL), lambda l, j: (0, 0)),
            pl.BlockSpec((None, D_MODEL, MOD_TN), lambda l, j: (l, 0, j)),
            pl.BlockSpec((None, 1, MOD_TN), lambda l, j: (l, 0, j)),
        ],
        out_specs=pl.BlockSpec((None, MOD_ROWS, MOD_TN), lambda l, j: (l, 0, j)),
        out_shape=jax.ShapeDtypeStruct((DEPTH, MOD_ROWS, 6 * D_MODEL), F32),
        compiler_params=_params(("arbitrary", "arbitrary")),
        name="modulation",
    )(cvec, w_mod, b_mod.reshape(DEPTH, 1, 6 * D_MODEL))


def _mod_spec(layer, row_fn, which, tile=TOKEN_TILE):
    return pl.BlockSpec((None, None, 1, D_MODEL),
                        lambda i, j: (layer, row_fn(i * tile), 0, which))


def _layer_spec(layer, block, index_map):
    return pl.BlockSpec((None,) + block, lambda *g: (layer,) + index_map(*g))


def _norm_mod(x, g, sc, sh):
    y = x * lax.rsqrt(jnp.mean(x * x, axis=-1, keepdims=True) + EPS) * g
    return y * (1.0 + sc) + sh


def _proj_kernel(x_ref, sh_ref, sc_ref, n_ref, w_ref, o_ref, hout_ref, h_ref, wb_ref):
    j, i = pl.program_id(0), pl.program_id(1)

    @pl.when(i == 0)
    def _():
        wb_ref[...] = w_ref[...].astype(BF16)

    @pl.when(j == 0)
    def _():
        h = _norm_mod(x_ref[...], n_ref[...], sc_ref[...], sh_ref[...]).astype(BF16)
        h_ref[i] = h
        hout_ref[...] = h

    o_ref[...] = jnp.dot(h_ref[i], wb_ref[...], preferred_element_type=F32).astype(BF16)


def _zproj_kernel(h_ref, w_ref, lb_ref, la_ref, key_ref, wb_ref):
    @pl.when(pl.program_id(1) == 0)
    def _():
        wb_ref[...] = w_ref[...].astype(BF16)

    z = jnp.dot(h_ref[...], wb_ref[...], preferred_element_type=F32)
    la, key = _hgrn_forget(z, lb_ref[...])
    la_ref[...] = la
    key_ref[...] = key.astype(BF16)


def _proj_call(x, mods, n1, w_in, lb, layer, row_fn):
    n_tok = x.shape[0]
    tn = PROJ_TN
    n_tile = n_tok // TOKEN_TILE
    hold = lambda j, i: jnp.where(j == 0, i, n_tile - 1)
    w_tile = lambda j: jnp.where(j >= Z_TILE0, j + Z_TILES, j)
    mod = lambda which: pl.BlockSpec(
        (None, None, 1, D_MODEL), lambda j, i: (layer, row_fn(i * TOKEN_TILE), 0, which))
    proj, h = pl.pallas_call(
        _proj_kernel,
        grid=(PROJ_WIDTH // tn, n_tile),
        in_specs=[
            pl.BlockSpec((TOKEN_TILE, D_MODEL), lambda j, i: (hold(j, i), 0)),
            mod(0), mod(1),
            _layer_spec(layer, (1, D_MODEL), lambda j, i: (0, 0)),
            _layer_spec(layer, (D_MODEL, tn), lambda j, i: (0, w_tile(j))),
        ],
        out_specs=[
            pl.BlockSpec((TOKEN_TILE, tn), lambda j, i: (i, j)),
            pl.BlockSpec((TOKEN_TILE, D_MODEL), lambda j, i: (hold(j, i), 0)),
        ],
        out_shape=[jax.ShapeDtypeStruct((n_tok, PROJ_WIDTH), BF16),
                   jax.ShapeDtypeStruct((n_tok, D_MODEL), BF16)],
        scratch_shapes=[pltpu.VMEM((n_tile, TOKEN_TILE, D_MODEL), BF16),
                        pltpu.VMEM((D_MODEL, tn), BF16)],
        compiler_params=_params(("arbitrary", "arbitrary")),
        name="in_projection",
    )(x, mods, mods, n1, w_in)
    la, key = pl.pallas_call(
        _zproj_kernel,
        grid=(Z_TILES, n_tile),
        in_specs=[
            pl.BlockSpec((TOKEN_TILE, D_MODEL), lambda j, i: (i, 0)),
            _layer_spec(layer, (D_MODEL, tn), lambda j, i: (0, Z_TILE0 + j)),
            pl.BlockSpec((None, None, 1, tn), lambda j, i: (layer, j, 0, 0)),
        ],
        out_specs=[pl.BlockSpec((TOKEN_TILE, tn), lambda j, i: (i, j))] * 2,
        out_shape=[jax.ShapeDtypeStruct((n_tok, Z_TILES * tn), F32),
                   jax.ShapeDtypeStruct((n_tok, Z_TILES * tn), BF16)],
        scratch_shapes=[pltpu.VMEM((D_MODEL, tn), BF16)],
        compiler_params=_params(("arbitrary", "arbitrary")),
        name="forget_projection",
    )(h, w_in, lb.reshape(DEPTH, Z_TILES, 1, tn))
    return proj, la, key


def _hgrn_consts(c):
    nl = int(np.log2(c))
    t = np.arange(c)
    tt, rr = t[:, None], t[None, :]
    x = tt ^ rr
    lev_of = np.where(x > 0, np.floor(np.log2(np.maximum(x, 1))) + 1, 0).astype(np.int32)
    lv_f = np.where(tt >= rr, lev_of, -1).astype(np.int32)
    lv_b = lv_f.T.copy()
    return ((rr <= tt).astype(np.float32), (rr >= tt).astype(np.float32), lv_f, lv_b, nl)


def _leaf_factors(cum, cum_row, reverse):
    c, dk = cum.shape
    leaf = 1 << LEAF_LEVEL
    pieces = []
    for base in range(0, c, leaf):
        row = base + leaf if reverse else base - 1
        if 0 <= row < c:
            pieces.append(jnp.broadcast_to(cum_row(row), (leaf, dk)))
        else:
            pieces.append(jnp.zeros((leaf, dk), F32))
    e = jnp.exp2((cum - jnp.concatenate(pieces, axis=0)) * LOG2E)
    return e.astype(BF16), (1.0 / e).astype(BF16)


def _level_factor(la, cum, cum_row, lev, reverse):
    c, dk = la.shape
    if lev == 1:
        row = lax.broadcasted_iota(jnp.int32, (c, 1), 0)
        return jnp.exp(jnp.where((row & 1) == (0 if reverse else 1), la, 0.0)).astype(BF16)
    blk, half = 1 << lev, 1 << (lev - 1)

    def boundary(r):
        base = (r // blk) * blk
        return jnp.broadcast_to(cum_row(base + half if reverse else base + half - 1),
                                (SUBLANES, dk))

    if half >= SUBLANES:
        pieces = []
        for r0 in range(0, c, SUBLANES):
            below = (r0 % blk >= half) != reverse
            pieces.append((cum[r0:r0 + SUBLANES] - boundary(r0)) * (LOG2E if below else -LOG2E))
        return jnp.exp2(jnp.concatenate(pieces, axis=0)).astype(BF16)
    top = lax.broadcasted_iota(jnp.int32, (SUBLANES, dk), 0) < SUBLANES // 2
    pieces = []
    for r0 in range(0, c, SUBLANES):
        if blk >= SUBLANES:
            pieces.append(boundary(r0))
        else:
            pieces.append(jnp.where(top, boundary(r0), boundary(r0 + SUBLANES // 2)))
    return jnp.exp2(jnp.abs(cum - jnp.concatenate(pieces, axis=0)) * (-LOG2E)).astype(BF16)


def _hgrn_forget(z, lb):
    e = jnp.exp(-jnp.abs(z))
    one_e = 1.0 + e
    r = 1.0 / one_e
    pos = z >= 0.0
    sig_neg = jnp.where(pos, e, 1.0) * r
    sig = 1.0 - sig_neg
    oml = 1.0 - lb
    log_sig = jnp.minimum(z, 0.0) - jnp.log(one_e)
    log_f = jnp.where(lb > 0.0, jnp.log(lb + oml * sig), log_sig)
    return log_f, oml * sig_neg


def _hgrn_kernel(*refs, seq_len, chunk, n_lev, heads, has_s0, n_prev):
    it = iter(refs)
    (q_ref, laf_ref, lab_ref, kf_ref, kb_ref, i_ref, g_ref, trif_ref, trib_ref, lvf_ref, lvb_ref,
     leaf_ref) = (next(it) for _ in range(12))
    s0_ref = next(it) if has_s0 else None
    prev_ref = next(it) if n_prev else None
    o_ref = next(it)
    st_ref = None if has_s0 else next(it)
    cum_ref, qt_ref, kt_ref, qd_ref, kd_ref, ae_ref, s_ref, acc_ref = (next(it) for _ in range(8))
    n_chunks = seq_len // chunk
    dk = DK_A
    units = [(2 * hh + d, hh, d, d == 1, la_in, tri_ref, lv_ref)
             for hh in range(heads)
             for d, (la_in, tri_ref, lv_ref) in enumerate(((laf_ref, trif_ref, lvf_ref),
                                                           (lab_ref, trib_ref, lvb_ref)))]
    key_in = (kf_ref, kb_ref)
    rows_of = lambda c: slice(c * chunk, (c + 1) * chunk)
    cols_of = lambda hh: slice(hh * dk, (hh + 1) * dk)

    worst = None
    for _, hh, _, _, la_in, _, _ in units:
        leaf_sum = jnp.dot(leaf_ref[...], la_in[:, cols_of(hh)].astype(BF16),
                           preferred_element_type=F32)
        worst = leaf_sum if worst is None else jnp.minimum(worst, leaf_sum)
    leafwise_ok = jnp.min(worst) > -LEAF_MAX_DECAY

    for u, hh, _, _, la_in, tri_ref, _ in units:
        tri = tri_ref[...]
        for c in range(n_chunks):
            la = la_in[rows_of(c), cols_of(hh)]
            la_hi = la.astype(BF16)
            la_lo = (la - la_hi.astype(F32)).astype(BF16)
            x2 = jnp.dot(tri, jnp.concatenate([la_hi, la_lo], axis=1),
                         preferred_element_type=F32)
            cum_ref[u, rows_of(c), :] = x2[:, :dk] + x2[:, dk:]

    def operands_and_scores(leafwise):
        if leafwise:
            levels = list(range(LEAF_LEVEL + 1, n_lev + 1))
        else:
            levels = list(range(1, n_lev + 1))
        n_terms = len(levels) + 1
        for u, hh, d, reverse, la_in, _, _ in units:
            for c in range(n_chunks):
                rows = rows_of(c)
                cum = cum_ref[u, rows, :]
                la = la_in[rows, cols_of(hh)]
                q = q_ref[rows, cols_of(hh)]
                k = key_in[d][rows, cols_of(hh)]
                cum_row = lambda r, u=u, c=c: cum_ref[u, c * chunk + r:c * chunk + r + 1, :]
                if leafwise:
                    factors = [_leaf_factors(cum, cum_row, reverse)]
                else:
                    factors = [None]
                for lev in levels:
                    e = _level_factor(la, cum, cum_row, lev, reverse)
                    factors.append((e, e))
                for t, f in enumerate(factors):
                    cols = slice(t * dk, (t + 1) * dk)
                    qt_ref[u, rows, cols] = q if f is None else q * f[0]
                    kt_ref[u, rows, cols] = k if f is None else k * f[1]
                end_row = 0 if reverse else chunk - 1
                cum_end = cum[end_row:end_row + 1]
                qd_ref[u, rows, :] = q * jnp.exp(cum).astype(BF16)
                kd_ref[u, rows, :] = k * jnp.exp(cum_end - cum).astype(BF16)
                ae_ref[u, c:c + 1, :] = jnp.exp(cum_end)
        zero = jnp.zeros((chunk, dk), BF16)
        for u, _, _, _, _, _, lv_ref in units:
            lv = lv_ref[...]
            if leafwise:
                masks = [(lv >= 0) & (lv <= LEAF_LEVEL)]
            else:
                masks = [lv == 0]
            masks += [lv == lev for lev in levels]
            for c in range(n_chunks):
                rows = rows_of(c)
                scores = jnp.zeros((chunk, chunk), F32)
                for t in range(0, n_terms, 2):
                    if t + 1 < n_terms:
                        k_a = kt_ref[u, rows, t * dk:(t + 1) * dk]
                        k_b = kt_ref[u, rows, (t + 1) * dk:(t + 2) * dk]
                        p = lax.dot_general(
                            qt_ref[u, rows, t * dk:(t + 2) * dk],
                            jnp.concatenate([jnp.concatenate([k_a, zero], axis=1),
                                             jnp.concatenate([zero, k_b], axis=1)], axis=0),
                            _NT, preferred_element_type=F32)
                        scores = jnp.where(masks[t], p[:, :chunk], scores)
                        scores = jnp.where(masks[t + 1], p[:, chunk:], scores)
                    else:
                        p = lax.dot_general(qt_ref[u, rows, t * dk:(t + 1) * dk],
                                            kt_ref[u, rows, t * dk:(t + 1) * dk],
                                            _NT, preferred_element_type=F32)
                        scores = jnp.where(masks[t], p, scores)
                s_ref[u, rows, :] = scores.astype(BF16)

    pl.when(leafwise_ok)(lambda: operands_and_scores(True))
    pl.when(jnp.logical_not(leafwise_ok))(lambda: operands_and_scores(False))

    incs = {u: [lax.dot_general(i_ref[rows_of(c), cols_of(hh)], kd_ref[u, rows_of(c), :], _TN,
                                preferred_element_type=F32) for c in range(n_chunks)]
            for u, hh, _, _, _, _, _ in units}
    finals, befores = {}, {}
    for u, hh, d, reverse, _, _, _ in units:
        st = s0_ref[d, hh].T if has_s0 else jnp.zeros((DV_A, dk), F32)
        befores[u] = [None] * n_chunks
        for c in (range(n_chunks - 1, -1, -1) if reverse else range(n_chunks)):
            befores[u][c] = st.astype(BF16)
            st = st * ae_ref[u, c:c + 1, :] + incs[u][c]
        finals[u] = st
    for u, hh, d, reverse, _, _, _ in units:
        for c in range(n_chunks):
            rows = rows_of(c)
            o = jnp.dot(s_ref[u, rows, :], i_ref[rows, cols_of(hh)], preferred_element_type=F32)
            if has_s0 or c != (n_chunks - 1 if reverse else 0):
                o = o + lax.dot_general(qd_ref[u, rows, :], befores[u][c], _NT,
                                        preferred_element_type=F32)
            if d == 0:
                acc_ref[rows, cols_of(hh)] = o
            else:
                acc_ref[rows, cols_of(hh)] += o

    for hh in range(heads):
        o = acc_ref[:, cols_of(hh)]
        o = o * lax.rsqrt(jnp.mean(o * o, axis=-1, keepdims=True) + EPS)
        o_ref[:, cols_of(hh)] = (o * _silu(g_ref[:, cols_of(hh)].astype(F32))).astype(BF16)
    if st_ref is not None:
        if n_prev:
            st_ref[0:n_prev] = prev_ref[...]
        for u, hh, d, _, _, _, _ in units:
            st_ref[n_prev, d, hh] = finals[u].T


def _hgrn_call(proj, la, key, state, prev, layer):
    n_b, seq_len, _ = proj.shape
    tri_f, tri_b, lv_f, lv_b, n_lev = _hgrn_consts(HGRN_CHUNK)
    has_s0 = state is not None
    n_chunks = seq_len // HGRN_CHUNK
    heads = max(1, HGRN_UNITS // (2 * n_chunks))
    width = heads * DK_A
    col = lambda off: (lambda b, h: (b, 0, off // width + h))
    const = lambda a: pl.BlockSpec(a.shape, lambda b, h: (0, 0))
    bwd = W_OFF_ZB - W_OFF_ZF
    in_specs = [pl.BlockSpec((None, seq_len, width), col(o))
                for o in (OFF_QA, 0, bwd, 0, bwd, OFF_IA, OFF_GA)]
    leaf = 1 << LEAF_LEVEL
    leaf_rows = -(-(seq_len // leaf) // SUBLANES) * SUBLANES
    leaf_ind = (np.arange(seq_len)[None, :] // leaf == np.arange(leaf_rows)[:, None])
    in_specs += [const(tri_f), const(tri_b), const(lv_f), const(lv_b), const(leaf_ind)]
    args = [proj, la, la, key, key, proj, proj, jnp.asarray(tri_f, BF16), jnp.asarray(tri_b, BF16),
            jnp.asarray(lv_f), jnp.asarray(lv_b), jnp.asarray(leaf_ind, BF16)]
    out_specs = [pl.BlockSpec((None, seq_len, width), lambda b, h: (b, 0, h))]
    out_shape = [jax.ShapeDtypeStruct((n_b, seq_len, H_A * DV_A), BF16)]
    if has_s0:
        in_specs.append(pl.BlockSpec((None, None, 2, heads, DK_A, DV_A),
                                     lambda b, h: (b, layer, 0, h, 0, 0)))
        args.append(state)
    else:
        st_spec = lambda n: pl.BlockSpec((None, n, 2, heads, DK_A, DV_A),
                                         lambda b, h: (b, 0, 0, h, 0, 0))
        if layer:
            in_specs.append(st_spec(layer))
            args.append(prev)
        out_specs.append(st_spec(layer + 1))
        out_shape.append(jax.ShapeDtypeStruct((n_b, layer + 1, 2, H_A, DK_A, DV_A), F32))
    n_u = 2 * heads
    res = pl.pallas_call(
        functools.partial(_hgrn_kernel, seq_len=seq_len, chunk=HGRN_CHUNK, n_lev=n_lev,
                          heads=heads, has_s0=has_s0, n_prev=0 if has_s0 else layer),
        grid=(n_b, H_A // heads),
        in_specs=in_specs, out_specs=out_specs, out_shape=out_shape,
        scratch_shapes=[
            pltpu.VMEM((n_u, seq_len, DK_A), F32),
            pltpu.VMEM((n_u, seq_len, (n_lev + 1) * DK_A), BF16),
            pltpu.VMEM((n_u, seq_len, (n_lev + 1) * DK_A), BF16),
            pltpu.VMEM((n_u, seq_len, DK_A), BF16),
            pltpu.VMEM((n_u, seq_len, DK_A), BF16),
            pltpu.VMEM((n_u, max(SUBLANES, n_chunks), DK_A), F32),
            pltpu.VMEM((n_u, seq_len, HGRN_CHUNK), BF16),
            pltpu.VMEM((seq_len, width), F32),
        ],
        compiler_params=_params(("arbitrary", "arbitrary")),
        name="hgrn2_scan",
    )(*args)
    return (res[0], None) if has_s0 else (res[0], res[1])


def _rope_tables(seq_len):
    rows = seq_len // GRID_W
    r_idx = jnp.repeat(jnp.arange(rows), GRID_W).astype(F32)
    c_idx = jnp.tile(jnp.arange(GRID_W), rows).astype(F32)
    quarter = DK_B // 4
    inv = 1.0 / (ROPE_BASE ** (jnp.arange(quarter, dtype=F32) / quarter))
    ang_r = r_idx[:, None] * inv[None, :]
    ang_c = c_idx[:, None] * inv[None, :]
    cos = jnp.concatenate([jnp.cos(ang_r)] * 2 + [jnp.cos(ang_c)] * 2, axis=1)
    sin = jnp.concatenate([-jnp.sin(ang_r), jnp.sin(ang_r), -jnp.sin(ang_c), jnp.sin(ang_c)],
                          axis=1)
    return cos, sin


def _rope(x, cos, sin):
    half = DK_B // 2
    swapped = jnp.concatenate([pltpu.roll(x[:, :half], half // 2, axis=1),
                               pltpu.roll(x[:, half:], half // 2, axis=1)], axis=1)
    return x * cos + swapped * sin


def _ret_kernel(*refs, seq_len, n_seq, latent, has_s0, n_prev):
    it = iter(refs)
    lg_ref, dm_ref, q_ref, k_ref, v_ref, g_ref = (next(it) for _ in range(6))
    cos_ref, sin_ref = (next(it), next(it)) if latent else (None, None)
    s0_ref = next(it) if has_s0 else None
    prev_ref = next(it) if n_prev else None
    o_ref = next(it)
    st_ref = None if has_s0 else next(it)
    h = pl.program_id(0)
    lg_f = lg_ref[h, 0]
    lg_b = lg_ref[h, 1]

    qs, ks = [], []
    for s in range(n_seq):
        q, k = q_ref[s], k_ref[s]
        if latent:
            cos, sin = cos_ref[...], sin_ref[...]
            q, k = _rope(q.astype(F32), cos, sin), _rope(k.astype(F32), cos, sin)
        qs.append(q)
        ks.append(k)
    items = [(s, slice(r, r + RET_ROWS)) for s in range(n_seq)
             for r in range(0, seq_len, RET_ROWS)]
    ps = [lax.dot_general(qs[s][rows].astype(BF16), ks[s].astype(BF16), _NT,
                          preferred_element_type=F32) for s, rows in items]
    carried = []
    if has_s0:
        for s, rows in items:
            s0 = jnp.concatenate([s0_ref[s, 0], s0_ref[s, 1]], axis=0).astype(BF16)
            pos = (lax.broadcasted_iota(jnp.int32, (RET_ROWS, 1), 0) + rows.start).astype(F32)
            q_s = jnp.concatenate([qs[s][rows] * jnp.exp(lg_f * (pos + 1.0)),
                                   qs[s][rows] * jnp.exp(lg_b * (seq_len - pos))], axis=1)
            carried.append(jnp.dot(q_s.astype(BF16), s0, preferred_element_type=F32))
    pds = [p.astype(BF16) * dm_ref[rows, :] for p, (_, rows) in zip(ps, items)]
    outs = [jnp.dot(pd, v_ref[s], preferred_element_type=F32) for pd, (s, _) in zip(pds, items)]
    for i, (s, rows) in enumerate(items):
        o = outs[i] + carried[i] if has_s0 else outs[i]
        o = o * lax.rsqrt(jnp.mean(o * o, axis=-1, keepdims=True) + EPS)
        o_ref[s, rows, :] = (o * _silu(g_ref[s, rows, :].astype(F32))).astype(BF16)
    if st_ref is not None:
        if n_prev:
            st_ref[:, 0:n_prev] = prev_ref[...]
        pos = lax.broadcasted_iota(jnp.int32, (seq_len, 1), 0).astype(F32)
        w_f = K_SCALE * jnp.exp(lg_f * (seq_len - 1.0 - pos))
        w_b = K_SCALE * jnp.exp(lg_b * pos)
        for s in range(n_seq):
            st_ref[s, n_prev, 0] = lax.dot_general((ks[s] * w_f).astype(BF16), v_ref[s], _TN,
                                                   preferred_element_type=F32)
            st_ref[s, n_prev, 1] = lax.dot_general((ks[s] * w_b).astype(BF16), v_ref[s], _TN,
                                                   preferred_element_type=F32)


def _ret_log_decay():
    heads = jnp.arange(H_B, dtype=F32)
    lg_f = jnp.log1p(-jnp.exp2(-5.0 - heads))
    lg_b = jnp.log1p(-jnp.exp2(-(5.0 + RET_DECAY_OFFSET_BWD) - heads))
    return jnp.stack([lg_f, lg_b], axis=1)


def _ret_decay_mask(seq_len):
    lg = _ret_log_decay()
    d = (jnp.arange(seq_len)[:, None] - jnp.arange(seq_len)[None, :]).astype(F32)[None]
    lg_f, lg_b = lg[:, 0, None, None], lg[:, 1, None, None]
    dm = (jnp.where(d >= 0.0, jnp.exp(lg_f * jnp.maximum(d, 0.0)), 0.0)
          + jnp.where(d <= 0.0, jnp.exp(lg_b * jnp.maximum(-d, 0.0)), 0.0))
    return (K_SCALE * dm).astype(BF16)


def _ret_call(proj, state, prev, layer, latent):
    n_b, seq_len, _ = proj.shape
    has_s0 = state is not None
    n_seq = max(1, RET_TOKENS // seq_len)
    in_specs = [
        pl.BlockSpec(memory_space=pltpu.SMEM),
        pl.BlockSpec((None, seq_len, seq_len), lambda h, b: (h, 0, 0)),
        pl.BlockSpec((n_seq, seq_len, DK_B), lambda h, b: (b, 0, OFF_QB // DK_B + h)),
        pl.BlockSpec((n_seq, seq_len, DK_B), lambda h, b: (b, 0, OFF_KB // DK_B + h)),
        pl.BlockSpec((n_seq, seq_len, DV_B), lambda h, b: (b, 0, OFF_VB // DV_B + h)),
        pl.BlockSpec((n_seq, seq_len, DV_B), lambda h, b: (b, 0, OFF_GB // DV_B + h)),
    ]
    args = [_ret_log_decay(), _ret_decay_mask(seq_len), proj, proj, proj, proj]
    if latent:
        cos, sin = _rope_tables(seq_len)
        in_specs += [pl.BlockSpec((seq_len, DK_B), lambda h, b: (0, 0))] * 2
        args += [cos, sin]
    out_specs = [pl.BlockSpec((n_seq, seq_len, DV_B), lambda h, b: (b, 0, h))]
    out_shape = [jax.ShapeDtypeStruct((n_b, seq_len, H_B * DV_B), BF16)]
    if has_s0:
        in_specs.append(pl.BlockSpec((n_seq, None, 2, None, DK_B, DV_B),
                                     lambda h, b: (b, layer, 0, h, 0, 0)))
        args.append(state)
    else:
        st_spec = lambda n: pl.BlockSpec((n_seq, n, 2, None, DK_B, DV_B),
                                         lambda h, b: (b, 0, 0, h, 0, 0))
        if layer:
            in_specs.append(st_spec(layer))
            args.append(prev)
        out_specs.append(st_spec(layer + 1))
        out_shape.append(jax.ShapeDtypeStruct((n_b, layer + 1, 2, H_B, DK_B, DV_B), F32))
    res = pl.pallas_call(
        functools.partial(_ret_kernel, seq_len=seq_len, n_seq=n_seq, latent=latent,
                          has_s0=has_s0, n_prev=0 if has_s0 else layer),
        grid=(H_B, n_b // n_seq),
        in_specs=in_specs, out_specs=out_specs, out_shape=out_shape,
        compiler_params=_params(("arbitrary", "arbitrary")),
        name="retention_scan",
    )(*args)
    return (res[0], None) if has_s0 else (res[0], res[1])


def _post_kernel(oa_ref, ob_ref, gta_ref, gtb_ref, x_ref, gt1_ref, pa_ref, pb_ref, wo_ref,
                 o_ref):
    subs = [slice(r, r + POST_SUB) for r in range(0, POST_TILE, POST_SUB)]
    y_a = [jnp.dot(oa_ref[s, :], pa_ref[...], preferred_element_type=F32) for s in subs]
    y_b = [jnp.dot(ob_ref[s, :], pb_ref[...], preferred_element_type=F32) for s in subs]
    merged = [(_sigmoid(gta_ref[s, :].astype(F32)) * y_a[i]
               + _sigmoid(gtb_ref[s, :].astype(F32)) * y_b[i]).astype(BF16)
              for i, s in enumerate(subs)]
    y = [jnp.dot(m, wo_ref[...], preferred_element_type=F32) for m in merged]
    for i, s in enumerate(subs):
        o_ref[s, :] = x_ref[s, :] + gt1_ref[...] * y[i]


def _post_call(o_a, o_b, proj, x, mods, p_a, p_b, w_out, layer, row_fn):
    n_tok = x.shape[0]
    tile = lambda w: (lambda i, j: (i, w))
    pcol = lambda off: pl.BlockSpec((POST_TILE, D_MODEL), tile(off // D_MODEL))
    const = lambda a: _layer_spec(layer, a.shape[1:], lambda i, j: (0, 0))
    return pl.pallas_call(
        _post_kernel,
        grid=(n_tok // POST_TILE, 1),
        in_specs=[
            pl.BlockSpec((POST_TILE, H_A * DV_A), tile(0)),
            pl.BlockSpec((POST_TILE, H_B * DV_B), tile(0)),
            pcol(OFF_GATE_A), pcol(OFF_GATE_B),
            pl.BlockSpec((POST_TILE, D_MODEL), tile(0)),
            _mod_spec(layer, row_fn, 2, POST_TILE),
            const(p_a), const(p_b), const(w_out),
        ],
        out_specs=pl.BlockSpec((POST_TILE, D_MODEL), tile(0)),
        out_shape=jax.ShapeDtypeStruct((n_tok, D_MODEL), F32),
        compiler_params=_params(("arbitrary", "arbitrary")),
        name="mixer_output",
    )(o_a, o_b, proj, proj, x, mods, p_a, p_b, w_out)


def _conv3(u, wc, bc, seq_len):
    n = u.shape[0]
    row = lax.broadcasted_iota(jnp.int32, (SUBLANES, 1), 0)

    def zero_row(x, r0, r):
        return jnp.where(row == r, 0.0, x[r0:r0 + SUBLANES])

    prev, nxt = pltpu.roll(u, 1, axis=0), pltpu.roll(u, n - 1, axis=0)
    p_parts, n_parts = [], []
    for s0 in range(0, n, seq_len):
        s1 = s0 + seq_len
        p_parts += [zero_row(prev, s0, 0), prev[s0 + SUBLANES:s1]]
        n_parts += [nxt[s0:s1 - SUBLANES], zero_row(nxt, s1 - SUBLANES, SUBLANES - 1)]
    prev, nxt = jnp.concatenate(p_parts, axis=0), jnp.concatenate(n_parts, axis=0)
    return prev * wc[0:1] + u * wc[1:2] + nxt * wc[2:3] + bc


def _ffn_kernel(*refs, seq_len, final):
    it = iter(refs)
    (x_ref, sh_ref, sc_ref, gt_ref, n2_ref, wa_ref, wg_ref, wca_ref, wcg_ref, bca_ref, bcg_ref,
     wd_ref) = (next(it) for _ in range(12))
    fn_ref = next(it) if final else None
    o_ref, h_ref, acc_ref = next(it), next(it), next(it)
    j = pl.program_id(1)

    @pl.when(j == 0)
    def _():
        h_ref[...] = _norm_mod(x_ref[...], n2_ref[...], sc_ref[...], sh_ref[...]).astype(BF16)
        acc_ref[...] = jnp.zeros_like(acc_ref)

    h = h_ref[...]
    subs = [slice(c, min(c + FF_SUB, FF_CHUNK)) for c in range(0, FF_CHUNK, FF_SUB)]
    u_a = [jnp.dot(h, wa_ref[:, s], preferred_element_type=F32) for s in subs]
    u_g = [jnp.dot(h, wg_ref[:, s], preferred_element_type=F32) for s in subs]
    act = [(_silu(_conv3(u_g[i], wcg_ref[:, s], bcg_ref[:, s], seq_len))
            * _conv3(u_a[i], wca_ref[:, s], bca_ref[:, s], seq_len)).astype(BF16)
           for i, s in enumerate(subs)]
    acc_ref[...] += jnp.dot(jnp.concatenate(act, axis=1), wd_ref[...],
                            preferred_element_type=F32)

    @pl.when(j == pl.num_programs(1) - 1)
    def _():
        x = x_ref[...] + gt_ref[...] * acc_ref[...]
        if final:
            x = x * lax.rsqrt(jnp.mean(x * x, axis=-1, keepdims=True) + EPS) * fn_ref[...]
        o_ref[...] = x


def _ffn_call(x, mods, n2, w_up, w_conv, b_conv, w_down, final_norm, layer, row_fn, seq_len):
    n_tok = x.shape[0]
    n_ff = D_FF // FF_CHUNK
    lspec = functools.partial(_layer_spec, layer)
    in_specs = [
        pl.BlockSpec((TOKEN_TILE, D_MODEL), lambda i, j: (i, 0)),
        _mod_spec(layer, row_fn, 3), _mod_spec(layer, row_fn, 4), _mod_spec(layer, row_fn, 5),
        lspec((1, D_MODEL), lambda i, j: (0, 0)),
        lspec((D_MODEL, FF_CHUNK), lambda i, j: (0, j)),
        lspec((D_MODEL, FF_CHUNK), lambda i, j: (0, n_ff + j)),
        lspec((3, FF_CHUNK), lambda i, j: (0, j)),
        lspec((3, FF_CHUNK), lambda i, j: (0, n_ff + j)),
        lspec((1, FF_CHUNK), lambda i, j: (0, j)),
        lspec((1, FF_CHUNK), lambda i, j: (0, n_ff + j)),
        lspec((FF_CHUNK, D_MODEL), lambda i, j: (j, 0)),
    ]
    args = [x, mods, mods, mods, n2, w_up, w_up, w_conv, w_conv, b_conv, b_conv, w_down]
    final = final_norm is not None
    if final:
        in_specs.append(pl.BlockSpec((1, D_MODEL), lambda i, j: (0, 0)))
        args.append(final_norm.reshape(1, D_MODEL))
    return pl.pallas_call(
        functools.partial(_ffn_kernel, seq_len=seq_len, final=final),
        grid=(n_tok // TOKEN_TILE, n_ff),
        in_specs=in_specs,
        out_specs=pl.BlockSpec((TOKEN_TILE, D_MODEL), lambda i, j: (i, 0)),
        out_shape=jax.ShapeDtypeStruct((n_tok, D_MODEL), F32),
        scratch_shapes=[pltpu.VMEM((TOKEN_TILE, D_MODEL), BF16),
                        pltpu.VMEM((TOKEN_TILE, D_MODEL), F32)],
        compiler_params=_params(("arbitrary", "arbitrary")),
        name="conv_ffn",
    )(*args)


def kernel(x_prompt, x_sample, state_hgrn, state_ret, c, c_ctx, norm1, norm2, final_norm,
           w_mod, b_mod, w_in, hgrn_lb_raw, p_a, p_b, w_out, w_up, w_conv, b_conv, w_down):
    n_ctx, t_ctx, _ = x_prompt.shape
    n_dec, t_dec, _ = x_sample.shape
    assert t_ctx & (t_ctx - 1) == 0 and TOKEN_TILE % t_ctx == 0 and t_dec == TOKEN_TILE

    sm = jax.nn.softmax(hgrn_lb_raw.astype(F32), axis=0)
    cum = jnp.cumsum(sm, axis=0)
    lower_bounds = cum - cum[0:1]

    cvec = jnp.concatenate(
        [c_ctx[None, :], c, jnp.zeros((MOD_ROWS - 1 - n_dec, D_MODEL), F32)], axis=0)
    mods = _mod_call(cvec, w_mod, b_mod).reshape(DEPTH, MOD_ROWS, 1, 6 * D_MODEL)

    p_a_b, p_b_b, w_out_b, w_up_b, w_down_b = (
        w.astype(BF16) for w in (p_a, p_b, w_out, w_up, w_down))
    norm1_3, norm2_3 = norm1.reshape(DEPTH, 1, D_MODEL), norm2.reshape(DEPTH, 1, D_MODEL)
    b_conv_3 = b_conv.reshape(DEPTH, 1, 2 * D_FF)

    ctx_row = lambda tok: 0
    dec_row = lambda tok: tok // t_dec + 1

    def layer(x, l, n_b, seq_len, row_fn, s_hgrn, s_ret, prev_h, prev_r, latent):
        proj, la, key = _proj_call(x, mods, norm1_3, w_in, lower_bounds, l, row_fn)
        proj3 = proj.reshape(n_b, seq_len, PROJ_WIDTH)
        o_a, st_h = _hgrn_call(proj3, la.reshape(n_b, seq_len, -1), key.reshape(n_b, seq_len, -1),
                               s_hgrn, prev_h, l)
        o_b, st_r = _ret_call(proj3, s_ret, prev_r, l, latent)
        x = _post_call(o_a.reshape(-1, H_A * DV_A), o_b.reshape(-1, H_B * DV_B), proj, x,
                       mods, p_a_b, p_b_b, w_out_b, l, row_fn)
        x = _ffn_call(x, mods, norm2_3, w_up_b, w_conv, b_conv_3, w_down_b,
                      final_norm if l == DEPTH - 1 else None, l, row_fn, seq_len)
        return x, st_h, st_r

    x = x_prompt.reshape(n_ctx * t_ctx, D_MODEL)
    st_h = st_r = None
    for l in range(DEPTH):
        x, st_h, st_r = layer(x, l, n_ctx, t_ctx, ctx_row, None, None, st_h, st_r, False)
    y_prompt = x.reshape(n_ctx, t_ctx, D_MODEL)

    x = x_sample.reshape(n_dec * t_dec, D_MODEL)
    for l in range(DEPTH):
        x, _, _ = layer(x, l, n_dec, t_dec, dec_row, state_hgrn, state_ret, None, None, True)
    y_sample = x.reshape(n_dec, t_dec, D_MODEL)
    return (y_prompt, y_sample, st_h, st_r)
```

```python
import functools

import numpy as np
import jax
import jax.numpy as jnp
from jax import lax
from jax.experimental import pallas as pl
from jax.experimental.pallas import tpu as pltpu

F32 = jnp.float32
BF16 = jnp.bfloat16

D_MODEL = 1024
DEPTH = 2
GRID_W = 64
H_A, DK_A, DV_A = 8, 128, 128
H_B, DK_B, DV_B = 4, 256, 512
D_FF = 2816
ROPE_BASE = 10000.0
K_SCALE = DK_B ** -0.5
EPS = 1e-6
RET_DECAY_OFFSET_BWD = 0.5

W_OFF_ZF, W_OFF_ZB, W_OFF_IA = 1024, 2048, 3072
OFF_QA, OFF_IA, OFF_GA = 0, 1024, 2048
OFF_QB, OFF_KB, OFF_VB, OFF_GB = 3072, 4096, 5120, 7168
OFF_GATE_A, OFF_GATE_B = 9216, 10240
PROJ_WIDTH = 11264

MOD_ROWS = 8
MOD_TN = 1024
TOKEN_TILE = 1024
POST_TILE = 512
POST_SUB = 256
FF_CHUNK = 1408
FF_SUB = 256
HGRN_CHUNK = 128
HGRN_UNITS = 32
LEAF_LEVEL = 5
LEAF_MAX_DECAY = 80.0
RET_ROWS = 256
RET_TOKENS = 1024
PROJ_TN = 1024
Z_TILE0 = W_OFF_ZF // PROJ_TN
Z_TILES = (W_OFF_IA - W_OFF_ZF) // PROJ_TN
VMEM_LIMIT = 56 * 1024 * 1024
SUBLANES = 8
LOG2E = 1.4426950408889634

_NT = (((1,), (1,)), ((), ()))
_TN = (((0,), (0,)), ((), ()))


def _params(sem):
    return pltpu.CompilerParams(dimension_semantics=sem, vmem_limit_bytes=VMEM_LIMIT)


def _sigmoid(x):
    return jax.nn.sigmoid(x)


def _silu(x):
    return x * jax.nn.sigmoid(x)


def _mod_kernel(c_ref, w_ref, b_ref, o_ref):
    s = _silu(c_ref[...])
    o_ref[...] = jnp.dot(s, w_ref[...], precision=lax.Precision.HIGHEST,
                         preferred_element_type=F32) + b_ref[...]


def _mod_call(cvec, w_mod, b_mod):
    n_col = 6 * D_MODEL // MOD_TN
    return pl.pallas_call(
        _mod_kernel,
        grid=(DEPTH, n_col),
        in_specs=[
            pl.BlockSpec((MOD_ROWS, D_MODEL), lambda l, j: (0, 0)),
            pl.BlockSpec((None, D_MODEL, MOD_TN), lambda l, j: (l, 0, j)),
            pl.BlockSpec((None, 1, MOD_TN), lambda l, j: (l, 0, j)),
        ],
        out_specs=pl.BlockSpec((None, MOD_ROWS, MOD_TN), lambda l, j: (l, 0, j)),
        out_shape=jax.ShapeDtypeStruct((DEPTH, MOD_ROWS, 6 * D_MODEL), F32),
        compiler_params=_params(("arbitrary", "arbitrary")),
        name="modulation",
    )(cvec, w_mod, b_mod.reshape(DEPTH, 1, 6 * D_MODEL))


def _mod_spec(layer, row_fn, which, tile=TOKEN_TILE):
    return pl.BlockSpec((None, None, 1, D_MODEL),
                        lambda i, j: (layer, row_fn(i * tile), 0, which))


def _layer_spec(layer, block, index_map):
    return pl.BlockSpec((None,) + block, lambda *g: (layer,) + index_map(*g))


def _norm_mod(x, g, sc, sh):
    y = x * lax.rsqrt(jnp.mean(x * x, axis=-1, keepdims=True) + EPS) * g
    return y * (1.0 + sc) + sh


def _proj_kernel(x_ref, sh_ref, sc_ref, n_ref, w_ref, o_ref, hout_ref, h_ref, wb_ref):
    j, i = pl.program_id(0), pl.program_id(1)

    @pl.when(i == 0)
    def _():
        wb_ref[...] = w_ref[...].astype(BF16)

    @pl.when(j == 0)
    def _():
        h = _norm_mod(x_ref[...], n_ref[...], sc_ref[...], sh_ref[...]).astype(BF16)
        h_ref[i] = h
        hout_ref[...] = h

    o_ref[...] = jnp.dot(h_ref[i], wb_ref[...], preferred_element_type=F32).astype(BF16)


def _zproj_kernel(h_ref, w_ref, lb_ref, la_ref, key_ref, wb_ref):
    @pl.when(pl.program_id(1) == 0)
    def _():
        wb_ref[...] = w_ref[...].astype(BF16)

    z = jnp.dot(h_ref[...], wb_ref[...], preferred_element_type=F32)
    la, key = _hgrn_forget(z, lb_ref[...])
    la_ref[...] = la
    key_ref[...] = key.astype(BF16)


def _proj_call(x, mods, n1, w_in, lb, layer, row_fn):
    n_tok = x.shape[0]
    tn = PROJ_TN
    n_tile = n_tok // TOKEN_TILE
    hold = lambda j, i: jnp.where(j == 0, i, n_tile - 1)
    w_tile = lambda j: jnp.where(j >= Z_TILE0, j + Z_TILES, j)
    mod = lambda which: pl.BlockSpec(
        (None, None, 1, D_MODEL), lambda j, i: (layer, row_fn(i * TOKEN_TILE), 0, which))
    proj, h = pl.pallas_call(
        _proj_kernel,
        grid=(PROJ_WIDTH // tn, n_tile),
        in_specs=[
            pl.BlockSpec((TOKEN_TILE, D_MODEL), lambda j, i: (hold(j, i), 0)),
            mod(0), mod(1),
            _layer_spec(layer, (1, D_MODEL), lambda j, i: (0, 0)),
            _layer_spec(layer, (D_MODEL, tn), lambda j, i: (0, w_tile(j))),
        ],
        out_specs=[
            pl.BlockSpec((TOKEN_TILE, tn), lambda j, i: (i, j)),
            pl.BlockSpec((TOKEN_TILE, D_MODEL), lambda j, i: (hold(j, i), 0)),
        ],
        out_shape=[jax.ShapeDtypeStruct((n_tok, PROJ_WIDTH), BF16),
                   jax.ShapeDtypeStruct((n_tok, D_MODEL), BF16)],
        scratch_shapes=[pltpu.VMEM((n_tile, TOKEN_TILE, D_MODEL), BF16),
                        pltpu.VMEM((D_MODEL, tn), BF16)],
        compiler_params=_params(("arbitrary", "arbitrary")),
        name="in_projection",
    )(x, mods, mods, n1, w_in)
    la, key = pl.pallas_call(
        _zproj_kernel,
        grid=(Z_TILES, n_tile),
        in_specs=[
            pl.BlockSpec((TOKEN_TILE, D_MODEL), lambda j, i: (i, 0)),
            _layer_spec(layer, (D_MODEL, tn), lambda j, i: (0, Z_TILE0 + j)),
            pl.BlockSpec((None, None, 1, tn), lambda j, i: (layer, j, 0, 0)),
        ],
        out_specs=[pl.BlockSpec((TOKEN_TILE, tn), lambda j, i: (i, j))] * 2,
        out_shape=[jax.ShapeDtypeStruct((n_tok, Z_TILES * tn), F32),
                   jax.ShapeDtypeStruct((n_tok, Z_TILES * tn), BF16)],
        scratch_shapes=[pltpu.VMEM((D_MODEL, tn), BF16)],
        compiler_params=_params(("arbitrary", "arbitrary")),
        name="forget_projection",
    )(h, w_in, lb.reshape(DEPTH, Z_TILES, 1, tn))
    return proj, la, key


def _hgrn_consts(c):
    nl = int(np.log2(c))
    t = np.arange(c)
    tt, rr = t[:, None], t[None, :]
    x = tt ^ rr
    lev_of = np.where(x > 0, np.floor(np.log2(np.maximum(x, 1))) + 1, 0).astype(np.int32)
    lv_f = np.where(tt >= rr, lev_of, -1).astype(np.int32)
    lv_b = lv_f.T.copy()
    return ((rr <= tt).astype(np.float32), (rr >= tt).astype(np.float32), lv_f, lv_b, nl)


def _leaf_factors(cum, cum_row, reverse):
    c, dk = cum.shape
    leaf = 1 << LEAF_LEVEL
    pieces = []
    for base in range(0, c, leaf):
        row = base + leaf if reverse else base - 1
        if 0 <= row < c:
            pieces.append(jnp.broadcast_to(cum_row(row), (leaf, dk)))
        else:
            pieces.append(jnp.zeros((leaf, dk), F32))
    e = jnp.exp2((cum - jnp.concatenate(pieces, axis=0)) * LOG2E)
    return e.astype(BF16), (1.0 / e).astype(BF16)


def _level_factor(la, cum, cum_row, lev, reverse):
    c, dk = la.shape
    if lev == 1:
        row = lax.broadcasted_iota(jnp.int32, (c, 1), 0)
        return jnp.exp(jnp.where((row & 1) == (0 if reverse else 1), la, 0.0)).astype(BF16)
    blk, half = 1 << lev, 1 << (lev - 1)

    def boundary(r):
        base = (r // blk) * blk
        return jnp.broadcast_to(cum_row(base + half if reverse else base + half - 1),
                                (SUBLANES, dk))

    if half >= SUBLANES:
        pieces = []
        for r0 in range(0, c, SUBLANES):
            below = (r0 % blk >= half) != reverse
            pieces.append((cum[r0:r0 + SUBLANES] - boundary(r0)) * (LOG2E if below else -LOG2E))
        return jnp.exp2(jnp.concatenate(pieces, axis=0)).astype(BF16)
    top = lax.broadcasted_iota(jnp.int32, (SUBLANES, dk), 0) < SUBLANES // 2
    pieces = []
    for r0 in range(0, c, SUBLANES):
        if blk >= SUBLANES:
            pieces.append(boundary(r0))
        else:
            pieces.append(jnp.where(top, boundary(r0), boundary(r0 + SUBLANES // 2)))
    return jnp.exp2(jnp.abs(cum - jnp.concatenate(pieces, axis=0)) * (-LOG2E)).astype(BF16)


def _hgrn_forget(z, lb):
    e = jnp.exp(-jnp.abs(z))
    one_e = 1.0 + e
    r = 1.0 / one_e
    pos = z >= 0.0
    sig_neg = jnp.where(pos, e, 1.0) * r
    sig = 1.0 - sig_neg
    oml = 1.0 - lb
    log_sig = jnp.minimum(z, 0.0) - jnp.log(one_e)
    log_f = jnp.where(lb > 0.0, jnp.log(lb + oml * sig), log_sig)
    return log_f, oml * sig_neg


def _hgrn_kernel(*refs, seq_len, chunk, n_lev, heads, has_s0, n_prev):
    it = iter(refs)
    (q_ref, laf_ref, lab_ref, kf_ref, kb_ref, i_ref, g_ref, trif_ref, trib_ref, lvf_ref, lvb_ref,
     leaf_ref) = (next(it) for _ in range(12))
    s0_ref = next(it) if has_s0 else None
    prev_ref = next(it) if n_prev else None
    o_ref = next(it)
    st_ref = None if has_s0 else next(it)
    cum_ref, qt_ref, kt_ref, qd_ref, kd_ref, ae_ref, s_ref, acc_ref = (next(it) for _ in range(8))
    n_chunks = seq_len // chunk
    dk = DK_A
    units = [(2 * hh + d, hh, d, d == 1, la_in, tri_ref, lv_ref)
             for hh in range(heads)
             for d, (la_in, tri_ref, lv_ref) in enumerate(((laf_ref, trif_ref, lvf_ref),
                                                           (lab_ref, trib_ref, lvb_ref)))]
    key_in = (kf_ref, kb_ref)
    rows_of = lambda c: slice(c * chunk, (c + 1) * chunk)
    cols_of = lambda hh: slice(hh * dk, (hh + 1) * dk)

    worst = None
    for _, hh, _, _, la_in, _, _ in units:
        leaf_sum = jnp.dot(leaf_ref[...], la_in[:, cols_of(hh)].astype(BF16),
                           preferred_element_type=F32)
        worst = leaf_sum if worst is None else jnp.minimum(worst, leaf_sum)
    leafwise_ok = jnp.min(worst) > -LEAF_MAX_DECAY

    for u, hh, _, _, la_in, tri_ref, _ in units:
        tri = tri_ref[...]
        for c in range(n_chunks):
            la = la_in[rows_of(c), cols_of(hh)]
            la_hi = la.astype(BF16)
            la_lo = (la - la_hi.astype(F32)).astype(BF16)
            x2 = jnp.dot(tri, jnp.concatenate([la_hi, la_lo], axis=1),
                         preferred_element_type=F32)
            cum_ref[u, rows_of(c), :] = x2[:, :dk] + x2[:, dk:]

    def operands_and_scores(leafwise):
        if leafwise:
            levels = list(range(LEAF_LEVEL + 1, n_lev + 1))
        else:
            levels = list(range(1, n_lev + 1))
        n_terms = len(levels) + 1
        for u, hh, d, reverse, la_in, _, _ in units:
            for c in range(n_chunks):
                rows = rows_of(c)
                cum = cum_ref[u, rows, :]
                la = la_in[rows, cols_of(hh)]
                q = q_ref[rows, cols_of(hh)]
                k = key_in[d][rows, cols_of(hh)]
                cum_row = lambda r, u=u, c=c: cum_ref[u, c * chunk + r:c * chunk + r + 1, :]
                if leafwise:
                    factors = [_leaf_factors(cum, cum_row, reverse)]
                else:
                    factors = [None]
                for lev in levels:
                    e = _level_factor(la, cum, cum_row, lev, reverse)
                    factors.append((e, e))
                for t, f in enumerate(factors):
                    cols = slice(t * dk, (t + 1) * dk)
                    qt_ref[u, rows, cols] = q if f is None else q * f[0]
                    kt_ref[u, rows, cols] = k if f is None else k * f[1]
                end_row = 0 if reverse else chunk - 1
                cum_end = cum[end_row:end_row + 1]
                qd_ref[u, rows, :] = q * jnp.exp(cum).astype(BF16)
                kd_ref[u, rows, :] = k * jnp.exp(cum_end - cum).astype(BF16)
                ae_ref[u, c:c + 1, :] = jnp.exp(cum_end)
        zero = jnp.zeros((chunk, dk), BF16)
        for u, _, _, _, _, _, lv_ref in units:
            lv = lv_ref[...]
            if leafwise:
                masks = [(lv >= 0) & (lv <= LEAF_LEVEL)]
            else:
                masks = [lv == 0]
            masks += [lv == lev for lev in levels]
            for c in range(n_chunks):
                rows = rows_of(c)
                scores = jnp.zeros((chunk, chunk), F32)
                for t in range(0, n_terms, 2):
                    if t + 1 < n_terms:
                        k_a = kt_ref[u, rows, t * dk:(t + 1) * dk]
                        k_b = kt_ref[u, rows, (t + 1) * dk:(t + 2) * dk]
                        p = lax.dot_general(
                            qt_ref[u, rows, t * dk:(t + 2) * dk],
                            jnp.concatenate([jnp.concatenate([k_a, zero], axis=1),
                                             jnp.concatenate([zero, k_b], axis=1)], axis=0),
                            _NT, preferred_element_type=F32)
                        scores = jnp.where(masks[t], p[:, :chunk], scores)
                        scores = jnp.where(masks[t + 1], p[:, chunk:], scores)
                    else:
                        p = lax.dot_general(qt_ref[u, rows, t * dk:(t + 1) * dk],
                                            kt_ref[u, rows, t * dk:(t + 1) * dk],
                                            _NT, preferred_element_type=F32)
                        scores = jnp.where(masks[t], p, scores)
                s_ref[u, rows, :] = scores.astype(BF16)

    pl.when(leafwise_ok)(lambda: operands_and_scores(True))
    pl.when(jnp.logical_not(leafwise_ok))(lambda: operands_and_scores(False))

    incs = {u: [lax.dot_general(i_ref[rows_of(c), cols_of(hh)], kd_ref[u, rows_of(c), :], _TN,
                                preferred_element_type=F32) for c in range(n_chunks)]
            for u, hh, _, _, _, _, _ in units}
    finals, befores = {}, {}
    for u, hh, d, reverse, _, _, _ in units:
        st = s0_ref[d, hh].T if has_s0 else jnp.zeros((DV_A, dk), F32)
        befores[u] = [None] * n_chunks
        for c in (range(n_chunks - 1, -1, -1) if reverse else range(n_chunks)):
            befores[u][c] = st.astype(BF16)
            st = st * ae_ref[u, c:c + 1, :] + incs[u][c]
        finals[u] = st
    for u, hh, d, reverse, _, _, _ in units:
        for c in range(n_chunks):
            rows = rows_of(c)
            o = jnp.dot(s_ref[u, rows, :], i_ref[rows, cols_of(hh)], preferred_element_type=F32)
            if has_s0 or c != (n_chunks - 1 if reverse else 0):
                o = o + lax.dot_general(qd_ref[u, rows, :], befores[u][c], _NT,
                                        preferred_element_type=F32)
            if d == 0:
                acc_ref[rows, cols_of(hh)] = o
            else:
                acc_ref[rows, cols_of(hh)] += o

    for hh in range(heads):
        o = acc_ref[:, cols_of(hh)]
        o = o * lax.rsqrt(jnp.mean(o * o, axis=-1, keepdims=True) + EPS)
        o_ref[:, cols_of(hh)] = (o * _silu(g_ref[:, cols_of(hh)].astype(F32))).astype(BF16)
    if st_ref is not None:
        if n_prev:
            st_ref[0:n_prev] = prev_ref[...]
        for u, hh, d, _, _, _, _ in units:
            st_ref[n_prev, d, hh] = finals[u].T


def _hgrn_call(proj, la, key, state, prev, layer):
    n_b, seq_len, _ = proj.shape
    tri_f, tri_b, lv_f, lv_b, n_lev = _hgrn_consts(HGRN_CHUNK)
    has_s0 = state is not None
    n_chunks = seq_len // HGRN_CHUNK
    heads = max(1, HGRN_UNITS // (2 * n_chunks))
    width = heads * DK_A
    col = lambda off: (lambda b, h: (b, 0, off // width + h))
    const = lambda a: pl.BlockSpec(a.shape, lambda b, h: (0, 0))
    bwd = W_OFF_ZB - W_OFF_ZF
    in_specs = [pl.BlockSpec((None, seq_len, width), col(o))
                for o in (OFF_QA, 0, bwd, 0, bwd, OFF_IA, OFF_GA)]
    leaf = 1 << LEAF_LEVEL
    leaf_rows = -(-(seq_len // leaf) // SUBLANES) * SUBLANES
    leaf_ind = (np.arange(seq_len)[None, :] // leaf == np.arange(leaf_rows)[:, None])
    in_specs += [const(tri_f), const(tri_b), const(lv_f), const(lv_b), const(leaf_ind)]
    args = [proj, la, la, key, key, proj, proj, jnp.asarray(tri_f, BF16), jnp.asarray(tri_b, BF16),
            jnp.asarray(lv_f), jnp.asarray(lv_b), jnp.asarray(leaf_ind, BF16)]
    out_specs = [pl.BlockSpec((None, seq_len, width), lambda b, h: (b, 0, h))]
    out_shape = [jax.ShapeDtypeStruct((n_b, seq_len, H_A * DV_A), BF16)]
    if has_s0:
        in_specs.append(pl.BlockSpec((None, None, 2, heads, DK_A, DV_A),
                                     lambda b, h: (b, layer, 0, h, 0, 0)))
        args.append(state)
    else:
        st_spec = lambda n: pl.BlockSpec((None, n, 2, heads, DK_A, DV_A),
                                         lambda b, h: (b, 0, 0, h, 0, 0))
        if layer:
            in_specs.append(st_spec(layer))
            args.append(prev)
        out_specs.append(st_spec(layer + 1))
        out_shape.append(jax.ShapeDtypeStruct((n_b, layer + 1, 2, H_A, DK_A, DV_A), F32))
    n_u = 2 * heads
    res = pl.pallas_call(
        functools.partial(_hgrn_kernel, seq_len=seq_len, chunk=HGRN_CHUNK, n_lev=n_lev,
                          heads=heads, has_s0=has_s0, n_prev=0 if has_s0 else layer),
        grid=(n_b, H_A // heads),
        in_specs=in_specs, out_specs=out_specs, out_shape=out_shape,
        scratch_shapes=[
            pltpu.VMEM((n_u, seq_len, DK_A), F32),
            pltpu.VMEM((n_u, seq_len, (n_lev + 1) * DK_A), BF16),
            pltpu.VMEM((n_u, seq_len, (n_lev + 1) * DK_A), BF16),
            pltpu.VMEM((n_u, seq_len, DK_A), BF16),
            pltpu.VMEM((n_u, seq_len, DK_A), BF16),
            pltpu.VMEM((n_u, max(SUBLANES, n_chunks), DK_A), F32),
            pltpu.VMEM((n_u, seq_len, HGRN_CHUNK), BF16),
            pltpu.VMEM((seq_len, width), F32),
        ],
        compiler_params=_params(("arbitrary", "arbitrary")),
        name="hgrn2_scan",
    )(*args)
    return (res[0], None) if has_s0 else (res[0], res[1])


def _rope_tables(seq_len):
    rows = seq_len // GRID_W
    r_idx = jnp.repeat(jnp.arange(rows), GRID_W).astype(F32)
    c_idx = jnp.tile(jnp.arange(GRID_W), rows).astype(F32)
    quarter = DK_B // 4
    inv = 1.0 / (ROPE_BASE ** (jnp.arange(quarter, dtype=F32) / quarter))
    ang_r = r_idx[:, None] * inv[None, :]
    ang_c = c_idx[:, None] * inv[None, :]
    cos = jnp.concatenate([jnp.cos(ang_r)] * 2 + [jnp.cos(ang_c)] * 2, axis=1)
    sin = jnp.concatenate([-jnp.sin(ang_r), jnp.sin(ang_r), -jnp.sin(ang_c), jnp.sin(ang_c)],
                          axis=1)
    return cos, sin


def _rope(x, cos, sin):
    half = DK_B // 2
    swapped = jnp.concatenate([pltpu.roll(x[:, :half], half // 2, axis=1),
                               pltpu.roll(x[:, half:], half // 2, axis=1)], axis=1)
    return x * cos + swapped * sin


def _ret_kernel(*refs, seq_len, n_seq, latent, has_s0, n_prev):
    it = iter(refs)
    lg_ref, dm_ref, q_ref, k_ref, v_ref, g_ref = (next(it) for _ in range(6))
    cos_ref, sin_ref = (next(it), next(it)) if latent else (None, None)
    s0_ref = next(it) if has_s0 else None
    prev_ref = next(it) if n_prev else None
    o_ref = next(it)
    st_ref = None if has_s0 else next(it)
    h = pl.program_id(0)
    lg_f = lg_ref[h, 0]
    lg_b = lg_ref[h, 1]

    qs, ks = [], []
    for s in range(n_seq):
        q, k = q_ref[s], k_ref[s]
        if latent:
            cos, sin = cos_ref[...], sin_ref[...]
            q, k = _rope(q.astype(F32), cos, sin), _rope(k.astype(F32), cos, sin)
        qs.append(q)
        ks.append(k)
    items = [(s, slice(r, r + RET_ROWS)) for s in range(n_seq)
             for r in range(0, seq_len, RET_ROWS)]
    ps = [lax.dot_general(qs[s][rows].astype(BF16), ks[s].astype(BF16), _NT,
                          preferred_element_type=F32) for s, rows in items]
    carried = []
    if has_s0:
        for s, rows in items:
            s0 = jnp.concatenate([s0_ref[s, 0], s0_ref[s, 1]], axis=0).astype(BF16)
            pos = (lax.broadcasted_iota(jnp.int32, (RET_ROWS, 1), 0) + rows.start).astype(F32)
            q_s = jnp.concatenate([qs[s][rows] * jnp.exp(lg_f * (pos + 1.0)),
                                   qs[s][rows] * jnp.exp(lg_b * (seq_len - pos))], axis=1)
            carried.append(jnp.dot(q_s.astype(BF16), s0, preferred_element_type=F32))
    pds = [p.astype(BF16) * dm_ref[rows, :] for p, (_, rows) in zip(ps, items)]
    outs = [jnp.dot(pd, v_ref[s], preferred_element_type=F32) for pd, (s, _) in zip(pds, items)]
    for i, (s, rows) in enumerate(items):
        o = outs[i] + carried[i] if has_s0 else outs[i]
        o = o * lax.rsqrt(jnp.mean(o * o, axis=-1, keepdims=True) + EPS)
        o_ref[s, rows, :] = (o * _silu(g_ref[s, rows, :].astype(F32))).astype(BF16)
    if st_ref is not None:
        if n_prev:
            st_ref[:, 0:n_prev] = prev_ref[...]
        pos = lax.broadcasted_iota(jnp.int32, (seq_len, 1), 0).astype(F32)
        w_f = K_SCALE * jnp.exp(lg_f * (seq_len - 1.0 - pos))
        w_b = K_SCALE * jnp.exp(lg_b * pos)
        for s in range(n_seq):
            st_ref[s, n_prev, 0] = lax.dot_general((ks[s] * w_f).astype(BF16), v_ref[s], _TN,
                                                   preferred_element_type=F32)
            st_ref[s, n_prev, 1] = lax.dot_general((ks[s] * w_b).astype(BF16), v_ref[s], _TN,
                                                   preferred_element_type=F32)


def _ret_log_decay():
    heads = jnp.arange(H_B, dtype=F32)
    lg_f = jnp.log1p(-jnp.exp2(-5.0 - heads))
    lg_b = jnp.log1p(-jnp.exp2(-(5.0 + RET_DECAY_OFFSET_BWD) - heads))
    return jnp.stack([lg_f, lg_b], axis=1)


def _ret_decay_mask(seq_len):
    lg = _ret_log_decay()
    d = (jnp.arange(seq_len)[:, None] - jnp.arange(seq_len)[None, :]).astype(F32)[None]
    lg_f, lg_b = lg[:, 0, None, None], lg[:, 1, None, None]
    dm = (jnp.where(d >= 0.0, jnp.exp(lg_f * jnp.maximum(d, 0.0)), 0.0)
          + jnp.where(d <= 0.0, jnp.exp(lg_b * jnp.maximum(-d, 0.0)), 0.0))
    return (K_SCALE * dm).astype(BF16)


def _ret_call(proj, state, prev, layer, latent):
    n_b, seq_len, _ = proj.shape
    has_s0 = state is not None
    n_seq = max(1, RET_TOKENS // seq_len)
    in_specs = [
        pl.BlockSpec(memory_space=pltpu.SMEM),
        pl.BlockSpec((None, seq_len, seq_len), lambda h, b: (h, 0, 0)),
        pl.BlockSpec((n_seq, seq_len, DK_B), lambda h, b: (b, 0, OFF_QB // DK_B + h)),
        pl.BlockSpec((n_seq, seq_len, DK_B), lambda h, b: (b, 0, OFF_KB // DK_B + h)),
        pl.BlockSpec((n_seq, seq_len, DV_B), lambda h, b: (b, 0, OFF_VB // DV_B + h)),
        pl.BlockSpec((n_seq, seq_len, DV_B), lambda h, b: (b, 0, OFF_GB // DV_B + h)),
    ]
    args = [_ret_log_decay(), _ret_decay_mask(seq_len), proj, proj, proj, proj]
    if latent:
        cos, sin = _rope_tables(seq_len)
        in_specs += [pl.BlockSpec((seq_len, DK_B), lambda h, b: (0, 0))] * 2
        args += [cos, sin]
    out_specs = [pl.BlockSpec((n_seq, seq_len, DV_B), lambda h, b: (b, 0, h))]
    out_shape = [jax.ShapeDtypeStruct((n_b, seq_len, H_B * DV_B), BF16)]
    if has_s0:
        in_specs.append(pl.BlockSpec((n_seq, None, 2, None, DK_B, DV_B),
                                     lambda h, b: (b, layer, 0, h, 0, 0)))
        args.append(state)
    else:
        st_spec = lambda n: pl.BlockSpec((n_seq, n, 2, None, DK_B, DV_B),
                                         lambda h, b: (b, 0, 0, h, 0, 0))
        if layer:
            in_specs.append(st_spec(layer))
            args.append(prev)
        out_specs.append(st_spec(layer + 1))
        out_shape.append(jax.ShapeDtypeStruct((n_b, layer + 1, 2, H_B, DK_B, DV_B), F32))
    res = pl.pallas_call(
        functools.partial(_ret_kernel, seq_len=seq_len, n_seq=n_seq, latent=latent,
                          has_s0=has_s0, n_prev=0 if has_s0 else layer),
        grid=(H_B, n_b // n_seq),
        in_specs=in_specs, out_specs=out_specs, out_shape=out_shape,
        compiler_params=_params(("arbitrary", "arbitrary")),
        name="retention_scan",
    )(*args)
    return (res[0], None) if has_s0 else (res[0], res[1])


def _post_kernel(oa_ref, ob_ref, gta_ref, gtb_ref, x_ref, gt1_ref, sh2_ref, sc2_ref, n2_ref,
                 pa_ref, pb_ref, wo_ref, o_ref, h_ref):
    subs = [slice(r, r + POST_SUB) for r in range(0, POST_TILE, POST_SUB)]
    y_a = [jnp.dot(oa_ref[s, :], pa_ref[...], preferred_element_type=F32) for s in subs]
    y_b = [jnp.dot(ob_ref[s, :], pb_ref[...], preferred_element_type=F32) for s in subs]
    merged = [(_sigmoid(gta_ref[s, :].astype(F32)) * y_a[i]
               + _sigmoid(gtb_ref[s, :].astype(F32)) * y_b[i]).astype(BF16)
              for i, s in enumerate(subs)]
    y = [jnp.dot(m, wo_ref[...], preferred_element_type=F32) for m in merged]
    for i, s in enumerate(subs):
        x1 = x_ref[s, :] + gt1_ref[...] * y[i]
        o_ref[s, :] = x1
        h_ref[s, :] = _norm_mod(x1, n2_ref[...], sc2_ref[...], sh2_ref[...]).astype(BF16)


def _post_call(o_a, o_b, proj, x, mods, n2, p_a, p_b, w_out, layer, row_fn):
    n_tok = x.shape[0]
    tile = lambda w: (lambda i, j: (i, w))
    pcol = lambda off: pl.BlockSpec((POST_TILE, D_MODEL), tile(off // D_MODEL))
    const = lambda a: _layer_spec(layer, a.shape[1:], lambda i, j: (0, 0))
    return pl.pallas_call(
        _post_kernel,
        grid=(n_tok // POST_TILE, 1),
        in_specs=[
            pl.BlockSpec((POST_TILE, H_A * DV_A), tile(0)),
            pl.BlockSpec((POST_TILE, H_B * DV_B), tile(0)),
            pcol(OFF_GATE_A), pcol(OFF_GATE_B),
            pl.BlockSpec((POST_TILE, D_MODEL), tile(0)),
            _mod_spec(layer, row_fn, 2, POST_TILE),
            _mod_spec(layer, row_fn, 3, POST_TILE),
            _mod_spec(layer, row_fn, 4, POST_TILE),
            _layer_spec(layer, (1, D_MODEL), lambda i, j: (0, 0)),
            const(p_a), const(p_b), const(w_out),
        ],
        out_specs=[pl.BlockSpec((POST_TILE, D_MODEL), tile(0))] * 2,
        out_shape=[jax.ShapeDtypeStruct((n_tok, D_MODEL), F32),
                   jax.ShapeDtypeStruct((n_tok, D_MODEL), BF16)],
        compiler_params=_params(("arbitrary", "arbitrary")),
        name="mixer_output",
    )(o_a, o_b, proj, proj, x, mods, mods, mods, n2, p_a, p_b, w_out)


def _conv3(u, wc, bc, seq_len):
    n = u.shape[0]
    row = lax.broadcasted_iota(jnp.int32, (SUBLANES, 1), 0)

    def zero_row(x, r0, r):
        return jnp.where(row == r, 0.0, x[r0:r0 + SUBLANES])

    prev, nxt = pltpu.roll(u, 1, axis=0), pltpu.roll(u, n - 1, axis=0)
    p_parts, n_parts = [], []
    for s0 in range(0, n, seq_len):
        s1 = s0 + seq_len
        p_parts += [zero_row(prev, s0, 0), prev[s0 + SUBLANES:s1]]
        n_parts += [nxt[s0:s1 - SUBLANES], zero_row(nxt, s1 - SUBLANES, SUBLANES - 1)]
    prev, nxt = jnp.concatenate(p_parts, axis=0), jnp.concatenate(n_parts, axis=0)
    return prev * wc[0:1] + u * wc[1:2] + nxt * wc[2:3] + bc


def _ffn_kernel(*refs, seq_len, final):
    it = iter(refs)
    (x_ref, h_ref, gt_ref, wa_ref, wg_ref, wca_ref, wcg_ref, bca_ref, bcg_ref,
     wd_ref) = (next(it) for _ in range(10))
    fn_ref = next(it) if final else None
    o_ref, acc_ref = next(it), next(it)
    j = pl.program_id(1)

    @pl.when(j == 0)
    def _():
        acc_ref[...] = jnp.zeros_like(acc_ref)

    h = h_ref[...]
    subs = [slice(c, min(c + FF_SUB, FF_CHUNK)) for c in range(0, FF_CHUNK, FF_SUB)]
    u_a = [jnp.dot(h, wa_ref[:, s], preferred_element_type=F32) for s in subs]
    u_g = [jnp.dot(h, wg_ref[:, s], preferred_element_type=F32) for s in subs]
    act = [(_silu(_conv3(u_g[i], wcg_ref[:, s], bcg_ref[:, s], seq_len))
            * _conv3(u_a[i], wca_ref[:, s], bca_ref[:, s], seq_len)).astype(BF16)
           for i, s in enumerate(subs)]
    acc_ref[...] += jnp.dot(jnp.concatenate(act, axis=1), wd_ref[...],
                            preferred_element_type=F32)

    @pl.when(j == pl.num_programs(1) - 1)
    def _():
        x = x_ref[...] + gt_ref[...] * acc_ref[...]
        if final:
            x = x * lax.rsqrt(jnp.mean(x * x, axis=-1, keepdims=True) + EPS) * fn_ref[...]
        o_ref[...] = x


def _ffn_call(x, h, mods, w_up, w_conv, b_conv, w_down, final_norm, layer, row_fn, seq_len):
    n_tok = x.shape[0]
    n_ff = D_FF // FF_CHUNK
    lspec = functools.partial(_layer_spec, layer)
    in_specs = [
        pl.BlockSpec((TOKEN_TILE, D_MODEL), lambda i, j: (i, 0)),
        pl.BlockSpec((TOKEN_TILE, D_MODEL), lambda i, j: (i, 0)),
        _mod_spec(layer, row_fn, 5),
        lspec((D_MODEL, FF_CHUNK), lambda i, j: (0, j)),
        lspec((D_MODEL, FF_CHUNK), lambda i, j: (0, n_ff + j)),
        lspec((3, FF_CHUNK), lambda i, j: (0, j)),
        lspec((3, FF_CHUNK), lambda i, j: (0, n_ff + j)),
        lspec((1, FF_CHUNK), lambda i, j: (0, j)),
        lspec((1, FF_CHUNK), lambda i, j: (0, n_ff + j)),
        lspec((FF_CHUNK, D_MODEL), lambda i, j: (j, 0)),
    ]
    args = [x, h, mods, w_up, w_up, w_conv, w_conv, b_conv, b_conv, w_down]
    final = final_norm is not None
    if final:
        in_specs.append(pl.BlockSpec((1, D_MODEL), lambda i, j: (0, 0)))
        args.append(final_norm.reshape(1, D_MODEL))
    return pl.pallas_call(
        functools.partial(_ffn_kernel, seq_len=seq_len, final=final),
        grid=(n_tok // TOKEN_TILE, n_ff),
        in_specs=in_specs,
        out_specs=pl.BlockSpec((TOKEN_TILE, D_MODEL), lambda i, j: (i, 0)),
        out_shape=jax.ShapeDtypeStruct((n_tok, D_MODEL), F32),
        scratch_shapes=[pltpu.VMEM((TOKEN_TILE, D_MODEL), F32)],
        compiler_params=_params(("arbitrary", "arbitrary")),
        name="conv_ffn",
    )(*args)


def kernel(x_prompt, x_sample, state_hgrn, state_ret, c, c_ctx, norm1, norm2, final_norm,
           w_mod, b_mod, w_in, hgrn_lb_raw, p_a, p_b, w_out, w_up, w_conv, b_conv, w_down):
    n_ctx, t_ctx, _ = x_prompt.shape
    n_dec, t_dec, _ = x_sample.shape
    assert t_ctx & (t_ctx - 1) == 0 and TOKEN_TILE % t_ctx == 0 and t_dec == TOKEN_TILE

    sm = jax.nn.softmax(hgrn_lb_raw.astype(F32), axis=0)
    cum = jnp.cumsum(sm, axis=0)
    lower_bounds = cum - cum[0:1]

    cvec = jnp.concatenate(
        [c_ctx[None, :], c, jnp.zeros((MOD_ROWS - 1 - n_dec, D_MODEL), F32)], axis=0)
    mods = _mod_call(cvec, w_mod, b_mod).reshape(DEPTH, MOD_ROWS, 1, 6 * D_MODEL)

    p_a_b, p_b_b, w_out_b, w_up_b, w_down_b = (
        w.astype(BF16) for w in (p_a, p_b, w_out, w_up, w_down))
    norm1_3, norm2_3 = norm1.reshape(DEPTH, 1, D_MODEL), norm2.reshape(DEPTH, 1, D_MODEL)
    b_conv_3 = b_conv.reshape(DEPTH, 1, 2 * D_FF)

    ctx_row = lambda tok: 0
    dec_row = lambda tok: tok // t_dec + 1

    def layer(x, l, n_b, seq_len, row_fn, s_hgrn, s_ret, prev_h, prev_r, latent):
        proj, la, key = _proj_call(x, mods, norm1_3, w_in, lower_bounds, l, row_fn)
        proj3 = proj.reshape(n_b, seq_len, PROJ_WIDTH)
        o_a, st_h = _hgrn_call(proj3, la.reshape(n_b, seq_len, -1), key.reshape(n_b, seq_len, -1),
                               s_hgrn, prev_h, l)
        o_b, st_r = _ret_call(proj3, s_ret, prev_r, l, latent)
        x, h2 = _post_call(o_a.reshape(-1, H_A * DV_A), o_b.reshape(-1, H_B * DV_B), proj, x,
                           mods, norm2_3, p_a_b, p_b_b, w_out_b, l, row_fn)
        x = _ffn_call(x, h2, mods, w_up_b, w_conv, b_conv_3, w_down_b,
                      final_norm if l == DEPTH - 1 else None, l, row_fn, seq_len)
        return x, st_h, st_r

    x = x_prompt.reshape(n_ctx * t_ctx, D_MODEL)
    st_h = st_r = None
    for l in range(DEPTH):
        x, st_h, st_r = layer(x, l, n_ctx, t_ctx, ctx_row, None, None, st_h, st_r, False)
    y_prompt = x.reshape(n_ctx, t_ctx, D_MODEL)

    x = x_sample.reshape(n_dec * t_dec, D_MODEL)
    for l in range(DEPTH):
        x, _, _ = layer(x, l, n_dec, t_dec, dec_row, state_hgrn, state_ret, None, None, True)
    y_sample = x.reshape(n_dec, t_dec, D_MODEL)
    return (y_prompt, y_sample, st_h, st_r)
```
